```python
import math
import jax, jax.numpy as jnp
from jax import lax
import numpy as np

D_MODEL = 1024
BATCH = 8
SEQ = 2048
DEPTH = 1

N_META = 16
N_HEADS = 8
HEAD_DIM = 64
ATTN_WIDTH = N_HEADS * HEAD_DIM
IDX_HEADS = 8
IDX_DIM = 64
TOPK_MAX = 256
Q_BLOCK = 64
CONV_CH = 512
CONV_WIDTH = 31
N_BUCKETS = 32
MAX_DISTANCE = 128
N_EXPERTS = 32
TOP_K_EXPERTS = 4
D_FF = 1024
SWIGLU_LIMIT = 7.0
SWIGLU_ALPHA = 1.702
EPS = 1e-6
IDX_SCALE = (IDX_DIM ** -0.5) * (IDX_HEADS ** -0.5)

COL_Q = ATTN_WIDTH
COL_K = ATTN_WIDTH
COL_V = ATTN_WIDTH
COL_IQ = IDX_HEADS * IDX_DIM
COL_IK = IDX_DIM
COL_IW = IDX_HEADS
COL_GLU = 2 * CONV_CH
COL_GATE = 2 * D_MODEL
SPLIT_POINTS = (COL_Q,
                COL_Q + COL_K,
                COL_Q + COL_K + COL_V,
                COL_Q + COL_K + COL_V + COL_IQ,
                COL_Q + COL_K + COL_V + COL_IQ + COL_IK,
                COL_Q + COL_K + COL_V + COL_IQ + COL_IK + COL_IW,
                COL_Q + COL_K + COL_V + COL_IQ + COL_IK + COL_IW + COL_GLU)
IN_COLS = COL_Q + COL_K + COL_V + COL_IQ + COL_IK + COL_IW + COL_GLU + COL_GATE

kernel_name = "hybrid_dsa_conformer_moe_block"


def rmsnorm(x, g):
    xf = x.astype(jnp.float32)
    y = xf * lax.rsqrt(jnp.mean(xf * xf, axis=-1, keepdims=True) + EPS)
    return (y * g.astype(jnp.float32)).astype(x.dtype)


def t5_bucket(n):
    max_exact = N_BUCKETS // 2
    nf = jnp.maximum(n, 1).astype(jnp.float32)
    large = max_exact + (jnp.log(nf / max_exact) / math.log(MAX_DISTANCE / max_exact)
                         * (N_BUCKETS - max_exact)).astype(jnp.int32)
    large = jnp.minimum(large, N_BUCKETS - 1)
    return jnp.where(n < max_exact, n, large)


def sparse_attention(q, k, v, iq, ik, iw, rel_bias, topk):
    B, T = q.shape[0], q.shape[1]
    nb = -(-T // Q_BLOCK)
    Tp = nb * Q_BLOCK

    def to_blocks(a):
        a = jnp.pad(a, [(0, 0), (0, Tp - T)] + [(0, 0)] * (a.ndim - 2))
        return jnp.moveaxis(a.reshape((B, nb, Q_BLOCK) + a.shape[2:]), 1, 0)

    key_pos = jnp.arange(T, dtype=jnp.int32)
    starts = jnp.arange(nb, dtype=jnp.int32) * Q_BLOCK
    gather = jax.vmap(lambda kb, ib: kb[ib])

    def block(args):
        qb, iqb, iwb, start = args
        qpos = start + jnp.arange(Q_BLOCK, dtype=jnp.int32)
        s = jnp.einsum('bqhd,bsd->bqhs', iqb, ik).astype(jnp.float32)
        score = jnp.einsum('bqh,bqhs->bqs', iwb.astype(jnp.float32), jax.nn.relu(s)) * IDX_SCALE
        causal = key_pos[None, :] <= qpos[:, None]
        score = jnp.where(causal[None], score, -jnp.inf)
        _, sel = lax.top_k(score, topk)
        k_sel = gather(k, sel)
        v_sel = gather(v, sel)
        logits = jnp.einsum('bqhd,bqkhd->bhqk', qb, k_sel).astype(jnp.float32) * (HEAD_DIM ** -0.5)
        dist = qpos[None, :, None] - sel
        bias = rel_bias[t5_bucket(jnp.maximum(dist, 0))]
        logits = logits + jnp.moveaxis(bias, 3, 1).astype(jnp.float32)
        logits = jnp.where((dist >= 0)[:, None], logits, -jnp.inf)
        p = jax.nn.softmax(logits, axis=-1).astype(v.dtype)
        return jnp.einsum('bhqk,bqkhd->bqhd', p, v_sel)

    out = lax.map(block, (to_blocks(q), to_blocks(iq), to_blocks(iw), starts))
    out = jnp.moveaxis(out, 0, 1).reshape(B, Tp, N_HEADS * HEAD_DIM)
    return out[:, :T]


def conformer_conv(u, conv_w, conv_b, ln_g, ln_b, w_conv_out):
    a, g = jnp.split(u, 2, axis=-1)
    y = a * jax.nn.sigmoid(g)
    y = lax.conv_general_dilated(y, conv_w[:, None, :].astype(y.dtype), window_strides=(1,),
                                 padding=[(CONV_WIDTH - 1, 0)],
                                 dimension_numbers=('NWC', 'WIO', 'NWC'),
                                 feature_group_count=CONV_CH) + conv_b
    yf = y.astype(jnp.float32)
    mu = jnp.mean(yf, axis=-1, keepdims=True)
    var = jnp.mean(jnp.square(yf - mu), axis=-1, keepdims=True)
    yf = (yf - mu) * lax.rsqrt(var + EPS) * ln_g.astype(jnp.float32) + ln_b.astype(jnp.float32)
    y = jax.nn.silu(yf).astype(u.dtype)
    return y @ w_conv_out


def moe_ffn(h, w_router, b_router, w_ug, b_ug, w_down, b_down):
    B, T, D = h.shape
    hf = h.reshape(B * T, D)
    logits = (hf @ w_router + b_router).astype(jnp.float32)
    vals, idx = lax.top_k(logits, TOP_K_EXPERTS)
    gates = jax.nn.softmax(vals, axis=-1)
    combine = jnp.sum(jax.nn.one_hot(idx, N_EXPERTS, dtype=jnp.float32) * gates[..., None], axis=1)

    def expert(acc, params):
        wug, bug, wd, bd, c = params
        ug = hf @ wug + bug
        gate = jnp.minimum(ug[:, :D_FF], SWIGLU_LIMIT)
        up = jnp.clip(ug[:, D_FF:], -SWIGLU_LIMIT, SWIGLU_LIMIT)
        act = (up + 1.0) * gate * jax.nn.sigmoid(SWIGLU_ALPHA * gate)
        y = act @ wd + bd
        return acc + c[:, None].astype(y.dtype) * y, None

    acc, _ = lax.scan(expert, jnp.zeros_like(hf), (w_ug, b_ug, w_down, b_down, combine.T))
    return acc.reshape(B, T, D)


def setup_inputs(seed: int = 0) -> dict:
    key = jax.random.key(seed)
    ks = jax.random.split(key, 24)
    f32 = jnp.float32
    nrm = lambda k, shape, s: jax.random.normal(k, shape, f32) * s
    return {
        "x": nrm(ks[0], (BATCH, SEQ, D_MODEL), 1.0),
        "meta_tokens": nrm(ks[1], (N_META, D_MODEL), 1.0),
        "rel_bias": nrm(ks[2], (N_BUCKETS, N_HEADS), 0.5),
        "norm_mix": 1.0 + nrm(ks[3], (DEPTH, D_MODEL), 0.02),
        "w_in": nrm(ks[4], (DEPTH, D_MODEL, IN_COLS), D_MODEL ** -0.5),
        "b_gate": nrm(ks[5], (DEPTH, 2 * D_MODEL), 0.1),
        "w_attn_out": nrm(ks[6], (DEPTH, ATTN_WIDTH, D_MODEL), ATTN_WIDTH ** -0.5),
        "conv_w": nrm(ks[7], (DEPTH, CONV_WIDTH, CONV_CH), CONV_WIDTH ** -0.5),
        "conv_b": nrm(ks[8], (DEPTH, CONV_CH), 0.01),
        "conv_ln_g": 1.0 + nrm(ks[9], (DEPTH, CONV_CH), 0.02),
        "conv_ln_b": nrm(ks[10], (DEPTH, CONV_CH), 0.01),
        "w_conv_out": nrm(ks[11], (DEPTH, CONV_CH, D_MODEL), CONV_CH ** -0.5),
        "w_out": nrm(ks[12], (DEPTH, D_MODEL, D_MODEL), D_MODEL ** -0.5),
        "norm_ffn": 1.0 + nrm(ks[13], (DEPTH, D_MODEL), 0.02),
        "w_router": nrm(ks[14], (DEPTH, D_MODEL, N_EXPERTS), D_MODEL ** -0.5),
        "b_router": nrm(ks[15], (DEPTH, N_EXPERTS), 0.01),
        "w_up_gate": nrm(ks[16], (DEPTH, N_EXPERTS, D_MODEL, 2 * D_FF), D_MODEL ** -0.5),
        "b_up_gate": nrm(ks[17], (DEPTH, N_EXPERTS, 2 * D_FF), 0.01),
        "w_down": nrm(ks[18], (DEPTH, N_EXPERTS, D_FF, D_MODEL), D_FF ** -0.5),
        "b_down": nrm(ks[19], (DEPTH, N_EXPERTS, D_MODEL), 0.01),
        "norm_final": 1.0 + nrm(ks[20], (D_MODEL,), 0.02),
    }


def reference(x, meta_tokens, rel_bias, norm_mix, w_in, b_gate, w_attn_out, conv_w, conv_b,
              conv_ln_g, conv_ln_b, w_conv_out, w_out, norm_ffn, w_router, b_router,
              w_up_gate, b_up_gate, w_down, b_down, norm_final):
    B, S, D = x.shape
    topk = min(TOPK_MAX, S // 4)
    meta = jnp.broadcast_to(meta_tokens[None].astype(x.dtype), (B, N_META, D))
    h = jnp.concatenate([meta, x], axis=1)
    T = h.shape[1]
    for l in range(DEPTH):
        hn = rmsnorm(h, norm_mix[l])
        proj = hn @ w_in[l]
        q, k, v, iq, ik, iw, glu_in, gate_in = jnp.split(proj, SPLIT_POINTS, axis=-1)
        q = q.reshape(B, T, N_HEADS, HEAD_DIM)
        k = k.reshape(B, T, N_HEADS, HEAD_DIM)
        v = v.reshape(B, T, N_HEADS, HEAD_DIM)
        iq = iq.reshape(B, T, IDX_HEADS, IDX_DIM)
        y_a = sparse_attention(q, k, v, iq, ik, iw, rel_bias, topk) @ w_attn_out[l]
        y_b = conformer_conv(glu_in, conv_w[l], conv_b[l], conv_ln_g[l], conv_ln_b[l], w_conv_out[l])
        g_a, g_b = jnp.split(jax.nn.sigmoid(gate_in + b_gate[l]), 2, axis=-1)
        h = h + (g_a * y_a + g_b * y_b) @ w_out[l]
        h = h + moe_ffn(rmsnorm(h, norm_ffn[l]), w_router[l], b_router[l], w_up_gate[l],
                        b_up_gate[l], w_down[l], b_down[l])
    return rmsnorm(h, norm_final)[:, N_META:]
```

```python
import functools
import math

import jax
import jax.numpy as jnp
from jax import lax
from jax.experimental import pallas as pl
from jax.experimental.pallas import tpu as pltpu

F32 = jnp.float32
BF16 = jnp.bfloat16
I32 = jnp.int32

D_MODEL = 1024
N_META = 16
N_HEADS = 8
HEAD_DIM = 64
ATTN_WIDTH = N_HEADS * HEAD_DIM
IDX_HEADS = 8
IDX_DIM = 64
TOPK_MAX = 256
CONV_CH = 512
CONV_WIDTH = 31
N_BUCKETS = 32
MAX_DISTANCE = 128
N_EXPERTS = 32
TOP_K_EXPERTS = 4
D_FF = 1024
SWIGLU_LIMIT = 7.0
SWIGLU_ALPHA = 1.702
EPS = 1e-6
IDX_SCALE = (IDX_DIM ** -0.5) * (IDX_HEADS ** -0.5)

ROW_TILE = 256
Q_TILE = 256
K_CHUNK = 256
HALO = 32
NEG = -1e30
VMEM_LIMIT = 56 * 1024 * 1024

NT_DIMS = (((1,), (1,)), ((), ()))


def _sigmoid(x):
    return 1.0 / (1.0 + jnp.exp(-x))


def _proj_kernel(x_ref, g_ref, wq_ref, wk_ref, wiq_ref, wik_ref, wvt_ref, wiwt_ref, wglu_ref,
                 wgate_ref, bgate_ref,
                 q_ref, k_ref, iq_ref, ik_ref, vt_ref, iwt_ref, y_ref, gate_ref):
    x = x_ref[...]
    ms = jnp.mean(x * x, axis=-1, keepdims=True)
    xn = (x * lax.rsqrt(ms + EPS) * g_ref[...]).astype(BF16)

    q = jnp.dot(xn, wq_ref[...], preferred_element_type=F32) * (HEAD_DIM ** -0.5)
    k = jnp.dot(xn, wk_ref[...], preferred_element_type=F32)
    iq = jnp.dot(xn, wiq_ref[...], preferred_element_type=F32)
    for h in range(N_HEADS):
        sl = slice(h * HEAD_DIM, (h + 1) * HEAD_DIM)
        q_ref[h] = q[:, sl].astype(BF16)
        k_ref[h] = k[:, sl].astype(BF16)
        iq_ref[h] = iq[:, sl].astype(BF16)
    ik_ref[...] = jnp.dot(xn, wik_ref[...], preferred_element_type=F32).astype(BF16)
    vt_ref[...] = lax.dot_general(wvt_ref[...], xn, NT_DIMS, preferred_element_type=F32).astype(BF16)
    iwt = lax.dot_general(wiwt_ref[...], xn, NT_DIMS, preferred_element_type=F32)
    iwt_ref[...] = iwt[:IDX_HEADS] * IDX_SCALE
    glu = jnp.dot(xn, wglu_ref[...], preferred_element_type=F32)
    y_ref[...] = glu[:, :CONV_CH] * _sigmoid(glu[:, CONV_CH:])
    gate = jnp.dot(xn, wgate_ref[...], preferred_element_type=F32) + bgate_ref[...]
    gate_ref[...] = _sigmoid(gate)


def _project(x2d, g, wts, tm):
    n = x2d.shape[0]
    wq, wk, wiq, wik, wvt, wiwt, wglu, wgate, bgate = wts
    full = lambda a: pl.BlockSpec(a.shape, lambda i: (0,) * a.ndim)
    out_shape = (
        jax.ShapeDtypeStruct((N_HEADS, n, HEAD_DIM), BF16),
        jax.ShapeDtypeStruct((N_HEADS, n, HEAD_DIM), BF16),
        jax.ShapeDtypeStruct((IDX_HEADS, n, IDX_DIM), BF16),
        jax.ShapeDtypeStruct((n, IDX_DIM), BF16),
        jax.ShapeDtypeStruct((ATTN_WIDTH, n), BF16),
        jax.ShapeDtypeStruct((IDX_HEADS, n), F32),
        jax.ShapeDtypeStruct((n, CONV_CH), F32),
        jax.ShapeDtypeStruct((n, 2 * D_MODEL), F32),
    )
    out_specs = (
        pl.BlockSpec((N_HEADS, tm, HEAD_DIM), lambda i: (0, i, 0)),
        pl.BlockSpec((N_HEADS, tm, HEAD_DIM), lambda i: (0, i, 0)),
        pl.BlockSpec((IDX_HEADS, tm, IDX_DIM), lambda i: (0, i, 0)),
        pl.BlockSpec((tm, IDX_DIM), lambda i: (i, 0)),
        pl.BlockSpec((ATTN_WIDTH, tm), lambda i: (0, i)),
        pl.BlockSpec((IDX_HEADS, tm), lambda i: (0, i)),
        pl.BlockSpec((tm, CONV_CH), lambda i: (i, 0)),
        pl.BlockSpec((tm, 2 * D_MODEL), lambda i: (i, 0)),
    )
    return pl.pallas_call(
        _proj_kernel,
        grid=(n // tm,),
        in_specs=[pl.BlockSpec((tm, D_MODEL), lambda i: (i, 0)), full(g), full(wq), full(wk), full(wiq),
                  full(wik), full(wvt), full(wiwt), full(wglu), full(wgate), full(bgate)],
        out_specs=out_specs,
        out_shape=out_shape,
        compiler_params=pltpu.CompilerParams(dimension_semantics=("arbitrary",),
                                             vmem_limit_bytes=VMEM_LIMIT),
        name="proj",
    )(x2d, g, wq, wk, wiq, wik, wvt, wiwt, wglu, wgate, bgate)


def _t5_bucket(n):
    max_exact = N_BUCKETS // 2
    nf = jnp.maximum(n, 1).astype(F32)
    large = max_exact + (jnp.log(nf / max_exact) / math.log(MAX_DISTANCE / max_exact)
                         * (N_BUCKETS - max_exact)).astype(I32)
    large = jnp.minimum(large, N_BUCKETS - 1)
    return jnp.where(n < max_exact, n, large)


def _bias_lookup(rb_ref, dist, h):
    bucket = _t5_bucket(jnp.maximum(dist, 0))
    out = jnp.full(dist.shape, NEG, F32)
    for b in range(N_BUCKETS):
        out = jnp.where(bucket == b, rb_ref[b, h], out)
    return jnp.where(dist >= 0, out, NEG)


def _bias_kernel(rb_ref, tab_ref, tabm_ref):
    kind = pl.program_id(0)
    r = pl.program_id(1)
    rows = tab_ref.shape[2]
    s = lax.broadcasted_iota(I32, (rows, Q_TILE), 0) + r * rows
    t = lax.broadcasted_iota(I32, (rows, Q_TILE), 1)
    dist = jnp.where(kind == 2, 2 * K_CHUNK, t - s + kind * K_CHUNK)
    for h in range(N_HEADS):
        tab_ref[0, h] = _bias_lookup(rb_ref, dist, h)
    m = lax.broadcasted_iota(I32, (N_META, Q_TILE), 0)
    tm_ = lax.broadcasted_iota(I32, (N_META, Q_TILE), 1)
    distm = jnp.where(kind == 0, N_META + tm_ - m, 2 * K_CHUNK)
    for h in range(N_HEADS):
        tabm_ref[0, h] = _bias_lookup(rb_ref, distm, h)


def _bias_tables(rel_bias):
    rows = 64
    return pl.pallas_call(
        _bias_kernel,
        grid=(3, K_CHUNK // rows),
        in_specs=[pl.BlockSpec(memory_space=pltpu.SMEM)],
        out_specs=(pl.BlockSpec((1, N_HEADS, rows, Q_TILE), lambda kd, r: (kd, 0, r, 0)),
                   pl.BlockSpec((1, N_HEADS, N_META, Q_TILE), lambda kd, r: (kd, 0, 0, 0))),
        out_shape=(jax.ShapeDtypeStruct((3, N_HEADS, K_CHUNK, Q_TILE), F32),
                   jax.ShapeDtypeStruct((3, N_HEADS, N_META, Q_TILE), F32)),
        compiler_params=pltpu.CompilerParams(dimension_semantics=("arbitrary", "arbitrary")),
        name="bias_tables",
    )(rel_bias)


def _order_bits_to_float(u):
    bits = jnp.where(u < 0, u ^ jnp.int32(-2 ** 31), ~u)
    return lax.bitcast_convert_type(bits, F32)


def _fold_rows(x, op):
    r, l = x.shape
    x3 = x.reshape(r // 8, 8, l)
    return op(x3, axis=0)


def _attn_kernel(q_ref, iq_ref, iwt_ref, k_ref, ik_ref, vt_ref, km_ref, ikm_ref, vtm_ref,
                 tab_ref, tabm_ref, o_ref, sc_ref, scm_ref, l_ref, lm_ref, ot_ref, cst_ref):
    j = pl.program_id(1)
    nchunks = j + 1
    iw = iwt_ref[...]

    def chunk_rows(c):
        return pl.ds(pl.multiple_of(c * K_CHUNK, K_CHUNK), K_CHUNK)

    def idx_scores(ikc):
        acc = None
        for h in range(IDX_HEADS):
            s = lax.dot_general(ikc, iq_ref[h], NT_DIMS, preferred_element_type=F32)
            term = jnp.maximum(s, 0.0) * iw[h:h + 1, :]
            acc = term if acc is None else acc + term
        return acc

    scm_ref[...] = idx_scores(ikm_ref[...])

    def p1(c, carry):
        rows = chunk_rows(c)
        sc_ref[rows, :] = idx_scores(ik_ref[rows, :])
        return carry

    lax.fori_loop(0, j, p1, 0)
    row_i = lax.broadcasted_iota(I32, (K_CHUNK, Q_TILE), 0)
    col_i = lax.broadcasted_iota(I32, (K_CHUNK, Q_TILE), 1)
    drows = chunk_rows(j)
    sc_ref[drows, :] = jnp.where(row_i <= col_i, idx_scores(ik_ref[drows, :]), -jnp.inf)

    def count(pred):
        def body(c, acc):
            blk = sc_ref[chunk_rows(c), :]
            return acc + _fold_rows(jnp.where(pred(blk, c), 1, 0).astype(I32), jnp.sum)
        acc = lax.fori_loop(0, nchunks, body, jnp.zeros((8, Q_TILE), I32))
        acc = acc + _fold_rows(jnp.where(pred(scm_ref[...], -1), 1, 0).astype(I32), jnp.sum)
        return jnp.sum(acc, axis=0, keepdims=True)

    def bisect(i, u):
        cand = u | lax.shift_left(jnp.int32(1), 31 - i)
        cf = _order_bits_to_float(cand)
        cnt = count(lambda blk, c: blk >= cf)
        return jnp.where(cnt >= TOPK_MAX, cand, u)

    u = lax.fori_loop(0, 32, bisect, jnp.zeros((1, Q_TILE), I32))
    thr = jnp.where((u >= 0) & (u < 0x00800000), -jnp.inf, _order_bits_to_float(u))

    cnt_ge = count(lambda blk, c: blk >= thr)
    tie = (cnt_ge > TOPK_MAX) & (thr > -jnp.inf)
    any_tie = jnp.max(tie.astype(I32))
    cst_ref[...] = jnp.full((8, Q_TILE), 2 ** 30, I32)

    def pos_of(c, shape):
        r = lax.broadcasted_iota(I32, shape, 0)
        return jnp.where(c < 0, r, r + N_META + c * K_CHUNK)

    @pl.when(any_tie > 0)
    def _():
        cnt_gt = count(lambda blk, c: blk > thr)
        need = TOPK_MAX - cnt_gt

        def bis_pos(i, cs):
            cand = cs | lax.shift_left(jnp.int32(1), 11 - i)
            f = count(lambda blk, c: (blk == thr) & (pos_of(c, blk.shape) < cand))
            return jnp.where(f <= need, cand, cs)

        cs = lax.fori_loop(0, 12, bis_pos, jnp.zeros((1, Q_TILE), I32))
        cs = jnp.where(tie, cs, 2 ** 30)
        cst_ref[...] = jnp.broadcast_to(cs, (8, Q_TILE))

    def to_mask(blk, c, with_ties):
        if with_ties:
            cs = cst_ref[0:1, :]
            sel = (blk > thr) | ((blk == thr) & (pos_of(c, blk.shape) < cs))
        else:
            sel = blk >= thr
        return jnp.where(sel, 0.0, NEG)

    for with_ties in (False, True):
        @pl.when((any_tie > 0) == with_ties)
        def _():
            def body(c, carry):
                rows = chunk_rows(c)
                sc_ref[rows, :] = to_mask(sc_ref[rows, :], c, with_ties)
                return carry
            lax.fori_loop(0, nchunks, body, 0)
            scm_ref[...] = to_mask(scm_ref[...], -1, with_ties)

    kindm = jnp.minimum(j, 1)
    for h in range(N_HEADS):
        qh = q_ref[h]
        hs = slice(h * HEAD_DIM, (h + 1) * HEAD_DIM)

        lm = (lax.dot_general(km_ref[h], qh, NT_DIMS, preferred_element_type=F32)
              + tabm_ref[kindm, h] + scm_ref[...])
        lm_ref[...] = lm
        mx0 = _fold_rows(lm, jnp.max)

        def pa(c, mx):
            rows = chunk_rows(c)
            kind = jnp.minimum(j - c, 2)
            l = (lax.dot_general(k_ref[h, rows, :], qh, NT_DIMS, preferred_element_type=F32)
                 + tab_ref[kind, h] + sc_ref[rows, :])
            l_ref[rows, :] = l
            return jnp.maximum(mx, _fold_rows(l, jnp.max))

        mx = lax.fori_loop(0, nchunks, pa, mx0)
        m = jnp.max(mx, axis=0, keepdims=True)

        pm = jnp.exp(lm_ref[...] - m)
        den0 = _fold_rows(pm, jnp.sum)
        acc0 = jnp.dot(vtm_ref[hs, :], pm.astype(BF16), preferred_element_type=F32)

        def pb(c, carry):
            den, acc = carry
            rows = chunk_rows(c)
            p = jnp.exp(l_ref[rows, :] - m)
            den = den + _fold_rows(p, jnp.sum)
            acc = acc + jnp.dot(vt_ref[hs, rows], p.astype(BF16), preferred_element_type=F32)
            return den, acc

        den, acc = lax.fori_loop(0, nchunks, pb, (den0, acc0))
        ot_ref[hs, :] = acc / jnp.sum(den, axis=0, keepdims=True)

    o_ref[...] = ot_ref[...].T.astype(BF16)


def _attention(q3, iq3, iwt, k3, ik, vt, km3, ikm, vtm, tab, tabm, batch, seq):
    n = batch * seq
    tiles = seq // Q_TILE
    full = lambda a: pl.BlockSpec(a.shape, lambda b, j: (0,) * a.ndim)
    return pl.pallas_call(
        _attn_kernel,
        grid=(batch, tiles),
        in_specs=[
            pl.BlockSpec((N_HEADS, Q_TILE, HEAD_DIM), lambda b, j: (0, b * tiles + j, 0)),
            pl.BlockSpec((IDX_HEADS, Q_TILE, IDX_DIM), lambda b, j: (0, b * tiles + j, 0)),
            pl.BlockSpec((IDX_HEADS, Q_TILE), lambda b, j: (0, b * tiles + j)),
            pl.BlockSpec((N_HEADS, seq, HEAD_DIM), lambda b, j: (0, b, 0)),
            pl.BlockSpec((seq, IDX_DIM), lambda b, j: (b, 0)),
            pl.BlockSpec((ATTN_WIDTH, seq), lambda b, j: (0, b)),
            full(km3), full(ikm), full(vtm), full(tab), full(tabm),
        ],
        out_specs=pl.BlockSpec((Q_TILE, ATTN_WIDTH), lambda b, j: (b * tiles + j, 0)),
        out_shape=jax.ShapeDtypeStruct((n, ATTN_WIDTH), BF16),
        scratch_shapes=[
            pltpu.VMEM((seq, Q_TILE), F32),
            pltpu.VMEM((N_META, Q_TILE), F32),
            pltpu.VMEM((seq, Q_TILE), F32),
            pltpu.VMEM((N_META, Q_TILE), F32),
            pltpu.VMEM((ATTN_WIDTH, Q_TILE), F32),
            pltpu.VMEM((8, Q_TILE), I32),
        ],
        compiler_params=pltpu.CompilerParams(dimension_semantics=("arbitrary", "arbitrary"),
                                             vmem_limit_bytes=VMEM_LIMIT),
        name="attn",
    )(q3, iq3, iwt, k3, ik, vt, km3, ikm, vtm, tab, tabm)


def _mix_kernel(x_ref, attn_ref, y_ref, yprev_ref, ymeta_ref, gate_ref,
                cw_ref, cb_ref, lng_ref, lnb_ref, wco_ref, wao_ref, wout_ref, nf_ref, wrt_ref, br_ref,
                h2_ref, hn2_ref, eid_ref, rank_ref, gcol_ref, cnt_ref,
                win_ref, base_ref, *, tiles_per_seq):
    i = pl.program_id(0)

    @pl.when(i == 0)
    def _():
        base_ref[...] = jnp.zeros_like(base_ref)

    first = (i % tiles_per_seq) == 0
    win_ref[0:HALO, :] = jnp.where(first, ymeta_ref[...], yprev_ref[...])
    win_ref[HALO:, :] = y_ref[...]
    acc = jnp.zeros((ROW_TILE, CONV_CH), F32)
    lead = HALO - (CONV_WIDTH - 1)
    for b in range(8):
        taps = [w for w in range(b, CONV_WIDTH, 8)]
        span = ROW_TILE + 8 * (len(taps) - 1)
        yb = win_ref[pl.ds(lead + b, span), :]
        for a, w in enumerate(taps):
            acc = acc + cw_ref[w:w + 1, :] * yb[8 * a:8 * a + ROW_TILE, :]
    yc = acc + cb_ref[...]
    mu = jnp.mean(yc, axis=-1, keepdims=True)
    var = jnp.mean(jnp.square(yc - mu), axis=-1, keepdims=True)
    yn = (yc - mu) * lax.rsqrt(var + EPS) * lng_ref[...] + lnb_ref[...]
    ys = yn * _sigmoid(yn)
    y_b = jnp.dot(ys.astype(BF16), wco_ref[...], preferred_element_type=F32)

    y_a = jnp.dot(attn_ref[...], wao_ref[...], preferred_element_type=F32)
    merged = gate_ref[:, :D_MODEL] * y_a + gate_ref[:, D_MODEL:] * y_b
    h2 = x_ref[...] + jnp.dot(merged.astype(BF16), wout_ref[...], preferred_element_type=F32)
    h2_ref[...] = h2
    ms = jnp.mean(h2 * h2, axis=-1, keepdims=True)
    hn2 = h2 * lax.rsqrt(ms + EPS) * nf_ref[...]
    hn2_ref[...] = hn2.astype(BF16)

    logits = lax.dot_general(wrt_ref[...], hn2, NT_DIMS, preferred_element_type=F32,
                             precision=lax.Precision.HIGHEST) + br_ref[...]
    erow = lax.broadcasted_iota(I32, (N_EXPERTS, ROW_TILE), 0)
    vals, ids = [], []
    l = logits
    for _ in range(TOP_K_EXPERTS):
        m = jnp.max(l, axis=0, keepdims=True)
        idx = jnp.min(jnp.where(l == m, erow, N_EXPERTS), axis=0, keepdims=True)
        vals.append(m)
        ids.append(idx)
        l = jnp.where(erow == idx, -jnp.inf, l)
    ex = [jnp.exp(v - vals[0]) for v in vals]
    den = ex[0] + ex[1] + ex[2] + ex[3]
    gates = [e / den for e in ex]

    onehot = [(erow == idx) for idx in ids]
    oh = jnp.concatenate([jnp.where(o, 1.0, 0.0) for o in onehot], axis=0)
    tr = lax.broadcasted_iota(I32, (ROW_TILE, ROW_TILE), 0)
    tc = lax.broadcasted_iota(I32, (ROW_TILE, ROW_TILE), 1)
    upper = jnp.where(tr <= tc, 1.0, 0.0).astype(BF16)
    pref = jnp.dot(oh.astype(BF16), upper, preferred_element_type=F32)
    offs = base_ref[:, 0:1]
    ranks = []
    for kk in range(TOP_K_EXPERTS):
        pk = pref[kk * N_EXPERTS:(kk + 1) * N_EXPERTS, :]
        r = jnp.sum(jnp.where(onehot[kk], offs + pk - 1.0, 0.0), axis=0, keepdims=True)
        ranks.append(r.astype(I32))
        offs = offs + pk[:, ROW_TILE - 1:ROW_TILE]
    base_ref[...] = jnp.broadcast_to(offs, base_ref.shape)
    cnt_ref[...] = jnp.broadcast_to(offs, cnt_ref.shape)

    zi = jnp.zeros((8 - TOP_K_EXPERTS, ROW_TILE), I32)
    eid_ref[...] = jnp.concatenate(ids + [zi], axis=0)
    rank_ref[...] = jnp.concatenate(ranks + [zi], axis=0)
    g8 = jnp.concatenate(gates + [jnp.zeros((128 - TOP_K_EXPERTS, ROW_TILE), F32)], axis=0)
    gcol_ref[...] = g8.T


def _mix(x2d, attn, y, ymeta, gate, wts, seq):
    n = x2d.shape[0]
    tiles_per_seq = seq // ROW_TILE
    halo_per_tile = ROW_TILE // HALO
    full = lambda a: pl.BlockSpec(a.shape, lambda i: (0,) * a.ndim)
    row = lambda w: pl.BlockSpec((ROW_TILE, w), lambda i: (i, 0))
    lane = lambda r: pl.BlockSpec((r, ROW_TILE), lambda i: (0, i))
    out_shape = (
        jax.ShapeDtypeStruct((n, D_MODEL), F32),
        jax.ShapeDtypeStruct((n, D_MODEL), BF16),
        jax.ShapeDtypeStruct((8, n), I32),
        jax.ShapeDtypeStruct((8, n), I32),
        jax.ShapeDtypeStruct((n, 128), F32),
        jax.ShapeDtypeStruct((N_EXPERTS, 128), F32),
    )
    out_specs = (row(D_MODEL), row(D_MODEL), lane(8), lane(8), row(128), full(out_shape[5]))
    return pl.pallas_call(
        functools.partial(_mix_kernel, tiles_per_seq=tiles_per_seq),
        grid=(n // ROW_TILE,),
        in_specs=[row(D_MODEL), row(ATTN_WIDTH), row(CONV_CH),
                  pl.BlockSpec((HALO, CONV_CH), lambda i: (jnp.maximum(i * halo_per_tile - 1, 0), 0)),
                  full(ymeta), row(2 * D_MODEL)] + [full(w) for w in wts],
        out_specs=out_specs,
        out_shape=out_shape,
        scratch_shapes=[pltpu.VMEM((HALO + ROW_TILE, CONV_CH), F32),
                        pltpu.VMEM((N_EXPERTS, 128), F32)],
        compiler_params=pltpu.CompilerParams(dimension_semantics=("arbitrary",),
                                             vmem_limit_bytes=VMEM_LIMIT),
        name="mix",
    )(x2d, attn, y, y, ymeta, gate, *wts)


def _expert_kernel(te_ref, nused_ref, xg_ref, wug_ref, bug_ref, wd_ref, bd_ref, y_ref, wug_bf, wd_bf):
    i = pl.program_id(0)
    used = i < nused_ref[0]
    prev = te_ref[jnp.maximum(i - 1, 0)]
    fresh = (i == 0) | (te_ref[i] != prev)

    @pl.when(used & fresh)
    def _():
        wug_bf[...] = wug_ref[0].astype(BF16)
        wd_bf[...] = wd_ref[0].astype(BF16)

    @pl.when(used)
    def _():
        ug = jnp.dot(xg_ref[...], wug_bf[...], preferred_element_type=F32) + bug_ref[0]
        gate = jnp.minimum(ug[:, :D_FF], SWIGLU_LIMIT)
        up = jnp.clip(ug[:, D_FF:], -SWIGLU_LIMIT, SWIGLU_LIMIT)
        act = (up + 1.0) * gate * _sigmoid(SWIGLU_ALPHA * gate)
        y_ref[...] = jnp.dot(act.astype(BF16), wd_bf[...], preferred_element_type=F32) + bd_ref[0]

    @pl.when(jnp.logical_not(used))
    def _():
        y_ref[...] = jnp.zeros_like(y_ref)


def _experts(tile_expert, nused, xg, w_ug, b_ug, w_down, b_down):
    s = xg.shape[0]
    ntiles = s // ROW_TILE
    clamp = lambda i, nu: jnp.minimum(i, nu[0] - 1)
    grid_spec = pltpu.PrefetchScalarGridSpec(
        num_scalar_prefetch=2,
        grid=(ntiles,),
        in_specs=[
            pl.BlockSpec((ROW_TILE, D_MODEL), lambda i, te, nu: (clamp(i, nu), 0)),
            pl.BlockSpec((1, D_MODEL, 2 * D_FF), lambda i, te, nu: (te[i], 0, 0)),
            pl.BlockSpec((1, 1, 2 * D_FF), lambda i, te, nu: (te[i], 0, 0)),
            pl.BlockSpec((1, D_FF, D_MODEL), lambda i, te, nu: (te[i], 0, 0)),
            pl.BlockSpec((1, 1, D_MODEL), lambda i, te, nu: (te[i], 0, 0)),
        ],
        out_specs=pl.BlockSpec((ROW_TILE, D_MODEL), lambda i, te, nu: (i, 0)),
        scratch_shapes=[pltpu.VMEM((D_MODEL, 2 * D_FF), BF16), pltpu.VMEM((D_FF, D_MODEL), BF16)],
    )
    return pl.pallas_call(
        _expert_kernel,
        grid_spec=grid_spec,
        out_shape=jax.ShapeDtypeStruct((s, D_MODEL), F32),
        compiler_params=pltpu.CompilerParams(dimension_semantics=("arbitrary",),
                                             vmem_limit_bytes=VMEM_LIMIT),
        name="experts",
    )(tile_expert, nused, xg, w_ug, b_ug[:, None, :], w_down, b_down[:, None, :])


def _final_kernel(h2_ref, yg_ref, gcol_ref, nf_ref, o_ref):
    h = h2_ref[...]
    for kk in range(TOP_K_EXPERTS):
        h = h + gcol_ref[:, kk:kk + 1] * yg_ref[kk]
    ms = jnp.mean(h * h, axis=-1, keepdims=True)
    o_ref[...] = h * lax.rsqrt(ms + EPS) * nf_ref[...]


def _final(h2, yg, gcol, nf):
    n = h2.shape[0]
    return pl.pallas_call(
        _final_kernel,
        grid=(n // ROW_TILE,),
        in_specs=[pl.BlockSpec((ROW_TILE, D_MODEL), lambda i: (i, 0)),
                  pl.BlockSpec((TOP_K_EXPERTS, ROW_TILE, D_MODEL), lambda i: (0, i, 0)),
                  pl.BlockSpec((ROW_TILE, 128), lambda i: (i, 0)),
                  pl.BlockSpec((1, D_MODEL), lambda i: (0, 0))],
        out_specs=pl.BlockSpec((ROW_TILE, D_MODEL), lambda i: (i, 0)),
        out_shape=jax.ShapeDtypeStruct((n, D_MODEL), F32),
        compiler_params=pltpu.CompilerParams(dimension_semantics=("arbitrary",),
                                             vmem_limit_bytes=VMEM_LIMIT),
        name="final",
    )(h2, yg, gcol, nf)


def _split_w_in(w_in, b_gate):
    c = ATTN_WIDTH
    o = 0
    wq = w_in[:, o:o + c]; o += c
    wk = w_in[:, o:o + c]; o += c
    wv = w_in[:, o:o + c]; o += c
    wiq = w_in[:, o:o + IDX_HEADS * IDX_DIM]; o += IDX_HEADS * IDX_DIM
    wik = w_in[:, o:o + IDX_DIM]; o += IDX_DIM
    wiw = w_in[:, o:o + IDX_HEADS]; o += IDX_HEADS
    wglu = w_in[:, o:o + 2 * CONV_CH]; o += 2 * CONV_CH
    wgate = w_in[:, o:]
    wiwt = jnp.concatenate([wiw.T, jnp.zeros((16 - IDX_HEADS, D_MODEL), w_in.dtype)], axis=0)
    bf = lambda a: a.astype(BF16)
    return (bf(wq), bf(wk), bf(wiq), bf(wik), bf(wv.T), bf(wiwt), bf(wglu), bf(wgate),
            b_gate[None, :].astype(F32))


def kernel(x, meta_tokens, rel_bias, norm_mix, w_in, b_gate, w_attn_out, conv_w, conv_b, conv_ln_g, conv_ln_b, w_conv_out, w_out, norm_ffn, w_router, b_router, w_up_gate, b_up_gate, w_down, b_down, norm_final):
    batch, seq, d = x.shape
    n = batch * seq
    x2d = x.reshape(n, d)

    wts = _split_w_in(w_in[0], b_gate[0])
    g_mix = norm_mix[0][None, :]
    q3, k3, iq3, ik, vt, iwt, y, gate = _project(x2d, g_mix, wts, ROW_TILE)
    _, km3, _, ikm, vtm, _, ym, _ = _project(meta_tokens.astype(F32), g_mix, wts, N_META)

    tab, tabm = _bias_tables(rel_bias.astype(F32))
    attn = _attention(q3, iq3, iwt, k3, ik, vt, km3, ikm, vtm, tab, tabm, batch, seq)

    ymeta = jnp.concatenate([jnp.zeros((HALO - N_META, CONV_CH), F32), ym], axis=0)
    cw = jnp.concatenate([conv_w[0], jnp.zeros((32 - CONV_WIDTH, CONV_CH), F32)], axis=0)
    mix_w = (cw, conv_b[0][None, :], conv_ln_g[0][None, :], conv_ln_b[0][None, :],
             w_conv_out[0].astype(BF16), w_attn_out[0].astype(BF16), w_out[0].astype(BF16),
             norm_ffn[0][None, :], w_router[0].T, b_router[0][:, None])
    h2, hn2, eid8, rank8, gcol, cnt = _mix(x2d, attn, y, ymeta, gate, mix_w, seq)

    counts = cnt[:, 0].astype(I32)
    padded = ((counts + ROW_TILE - 1) // ROW_TILE) * ROW_TILE
    ends = jnp.cumsum(padded)
    starts = ends - padded
    eid = eid8[:TOP_K_EXPERTS]
    slot = starts[eid] + rank8[:TOP_K_EXPERTS]
    nslots = n * TOP_K_EXPERTS + N_EXPERTS * ROW_TILE
    ntiles = nslots // ROW_TILE
    tile_start = jnp.arange(ntiles, dtype=I32) * ROW_TILE
    nused = (ends[-1] // ROW_TILE).astype(I32)
    tile_expert = jnp.minimum(jnp.searchsorted(ends, tile_start, side="right"), N_EXPERTS - 1).astype(I32)
    tile_expert = jnp.where(tile_start < ends[-1], tile_expert, tile_expert[jnp.maximum(nused - 1, 0)])
    token = jnp.broadcast_to(jnp.arange(n, dtype=I32)[None, :], slot.shape)
    token_of_slot = jnp.zeros((nslots,), I32).at[slot.reshape(-1)].set(token.reshape(-1))
    xg = jnp.take(hn2, token_of_slot, axis=0)

    yslot = _experts(tile_expert, nused[None], xg, w_up_gate[0], b_up_gate[0], w_down[0], b_down[0])
    yg = jnp.take(yslot, slot.reshape(-1), axis=0).reshape(TOP_K_EXPERTS, n, d)

    out = _final(h2, yg, gcol, norm_final[None, :])
    return out.reshape(batch, seq, d)
```

```python
import functools
import math

import jax
import jax.numpy as jnp
from jax import lax
from jax.experimental import pallas as pl
from jax.experimental.pallas import tpu as pltpu

F32 = jnp.float32
BF16 = jnp.bfloat16
I32 = jnp.int32

D_MODEL = 1024
N_META = 16
N_HEADS = 8
HEAD_DIM = 64
ATTN_WIDTH = N_HEADS * HEAD_DIM
IDX_HEADS = 8
IDX_DIM = 64
TOPK_MAX = 256
CONV_CH = 512
CONV_WIDTH = 31
N_BUCKETS = 32
MAX_DISTANCE = 128
N_EXPERTS = 32
TOP_K_EXPERTS = 4
D_FF = 1024
SWIGLU_LIMIT = 7.0
SWIGLU_ALPHA = 1.702
EPS = 1e-6
IDX_SCALE = (IDX_DIM ** -0.5) * (IDX_HEADS ** -0.5)

ROW_TILE = 256
Q_TILE = 256
K_CHUNK = 256
HALO = 32
NEG = -1e30
VMEM_LIMIT = 56 * 1024 * 1024

NT_DIMS = (((1,), (1,)), ((), ()))


def _sigmoid(x):
    return 1.0 / (1.0 + jnp.exp(-x))


def _pack_bf16_pairs(x):
    w = x.shape[1] // 2
    hi = lax.bitcast_convert_type(x[:, :w].astype(BF16).astype(F32), jnp.uint32)
    lo = lax.bitcast_convert_type(x[:, w:].astype(BF16).astype(F32), jnp.uint32)
    return lax.bitcast_convert_type(hi | (lo >> 16), I32)


def _unpack_bf16_pairs(p):
    u = lax.bitcast_convert_type(p, jnp.uint32)
    hi = lax.bitcast_convert_type(u & jnp.uint32(0xFFFF0000), F32)
    lo = lax.bitcast_convert_type(u << 16, F32)
    return jnp.concatenate([hi, lo], axis=1).astype(BF16)


def _proj_kernel(x_ref, g_ref, wq_ref, wk_ref, wiq_ref, wik_ref, wvt_ref, wiwt_ref, wglu_ref,
                 wgate_ref, bgate_ref,
                 q_ref, k_ref, iq_ref, ik_ref, vt_ref, iwt_ref, y_ref, gate_ref):
    x = x_ref[...]
    ms = jnp.mean(x * x, axis=-1, keepdims=True)
    xn = (x * lax.rsqrt(ms + EPS) * g_ref[...]).astype(BF16)

    q = jnp.dot(xn, wq_ref[...], preferred_element_type=F32) * (HEAD_DIM ** -0.5)
    k = jnp.dot(xn, wk_ref[...], preferred_element_type=F32)
    iq = jnp.dot(xn, wiq_ref[...], preferred_element_type=F32)
    for h in range(N_HEADS):
        sl = slice(h * HEAD_DIM, (h + 1) * HEAD_DIM)
        q_ref[h] = q[:, sl].astype(BF16)
        k_ref[h] = k[:, sl].astype(BF16)
        iq_ref[h] = iq[:, sl].astype(BF16)
    ik_ref[...] = jnp.dot(xn, wik_ref[...], preferred_element_type=F32).astype(BF16)
    vt_ref[...] = lax.dot_general(wvt_ref[...], xn, NT_DIMS, preferred_element_type=F32).astype(BF16)
    iwt = lax.dot_general(wiwt_ref[...], xn, NT_DIMS, preferred_element_type=F32)
    iwt_ref[...] = iwt[:IDX_HEADS] * IDX_SCALE
    glu = jnp.dot(xn, wglu_ref[...], preferred_element_type=F32)
    y_ref[...] = glu[:, :CONV_CH] * _sigmoid(glu[:, CONV_CH:])
    gate = jnp.dot(xn, wgate_ref[...], preferred_element_type=F32) + bgate_ref[...]
    gate_ref[...] = _sigmoid(gate)


def _project(x2d, g, wts, tm):
    n = x2d.shape[0]
    wq, wk, wiq, wik, wvt, wiwt, wglu, wgate, bgate = wts
    full = lambda a: pl.BlockSpec(a.shape, lambda i: (0,) * a.ndim)
    out_shape = (
        jax.ShapeDtypeStruct((N_HEADS, n, HEAD_DIM), BF16),
        jax.ShapeDtypeStruct((N_HEADS, n, HEAD_DIM), BF16),
        jax.ShapeDtypeStruct((IDX_HEADS, n, IDX_DIM), BF16),
        jax.ShapeDtypeStruct((n, IDX_DIM), BF16),
        jax.ShapeDtypeStruct((ATTN_WIDTH, n), BF16),
        jax.ShapeDtypeStruct((IDX_HEADS, n), F32),
        jax.ShapeDtypeStruct((n, CONV_CH), F32),
        jax.ShapeDtypeStruct((n, 2 * D_MODEL), F32),
    )
    out_specs = (
        pl.BlockSpec((N_HEADS, tm, HEAD_DIM), lambda i: (0, i, 0)),
        pl.BlockSpec((N_HEADS, tm, HEAD_DIM), lambda i: (0, i, 0)),
        pl.BlockSpec((IDX_HEADS, tm, IDX_DIM), lambda i: (0, i, 0)),
        pl.BlockSpec((tm, IDX_DIM), lambda i: (i, 0)),
        pl.BlockSpec((ATTN_WIDTH, tm), lambda i: (0, i)),
        pl.BlockSpec((IDX_HEADS, tm), lambda i: (0, i)),
        pl.BlockSpec((tm, CONV_CH), lambda i: (i, 0)),
        pl.BlockSpec((tm, 2 * D_MODEL), lambda i: (i, 0)),
    )
    return pl.pallas_call(
        _proj_kernel,
        grid=(n // tm,),
        in_specs=[pl.BlockSpec((tm, D_MODEL), lambda i: (i, 0)), full(g), full(wq), full(wk), full(wiq),
                  full(wik), full(wvt), full(wiwt), full(wglu), full(wgate), full(bgate)],
        out_specs=out_specs,
        out_shape=out_shape,
        compiler_params=pltpu.CompilerParams(dimension_semantics=("arbitrary",),
                                             vmem_limit_bytes=VMEM_LIMIT),
        name="proj",
    )(x2d, g, wq, wk, wiq, wik, wvt, wiwt, wglu, wgate, bgate)


def _t5_bucket(n):
    max_exact = N_BUCKETS // 2
    nf = jnp.maximum(n, 1).astype(F32)
    large = max_exact + (jnp.log(nf / max_exact) / math.log(MAX_DISTANCE / max_exact)
                         * (N_BUCKETS - max_exact)).astype(I32)
    large = jnp.minimum(large, N_BUCKETS - 1)
    return jnp.where(n < max_exact, n, large)


def _bias_lookup(rb_ref, dist, h):
    bucket = _t5_bucket(jnp.maximum(dist, 0))
    out = jnp.full(dist.shape, NEG, F32)
    for b in range(N_BUCKETS):
        out = jnp.where(bucket == b, rb_ref[b, h], out)
    return jnp.where(dist >= 0, out, NEG)


def _bias_kernel(rb_ref, tab_ref, tabm_ref):
    kind = pl.program_id(0)
    r = pl.program_id(1)
    rows = tab_ref.shape[2]
    s = lax.broadcasted_iota(I32, (rows, Q_TILE), 0) + r * rows
    t = lax.broadcasted_iota(I32, (rows, Q_TILE), 1)
    dist = jnp.where(kind == 2, 2 * K_CHUNK, t - s + kind * K_CHUNK)
    for h in range(N_HEADS):
        tab_ref[0, h] = _bias_lookup(rb_ref, dist, h)
    m = lax.broadcasted_iota(I32, (N_META, Q_TILE), 0)
    tm_ = lax.broadcasted_iota(I32, (N_META, Q_TILE), 1)
    distm = jnp.where(kind == 0, N_META + tm_ - m, 2 * K_CHUNK)
    for h in range(N_HEADS):
        tabm_ref[0, h] = _bias_lookup(rb_ref, distm, h)


def _bias_tables(rel_bias):
    rows = 64
    return pl.pallas_call(
        _bias_kernel,
        grid=(3, K_CHUNK // rows),
        in_specs=[pl.BlockSpec(memory_space=pltpu.SMEM)],
        out_specs=(pl.BlockSpec((1, N_HEADS, rows, Q_TILE), lambda kd, r: (kd, 0, r, 0)),
                   pl.BlockSpec((1, N_HEADS, N_META, Q_TILE), lambda kd, r: (kd, 0, 0, 0))),
        out_shape=(jax.ShapeDtypeStruct((3, N_HEADS, K_CHUNK, Q_TILE), F32),
                   jax.ShapeDtypeStruct((3, N_HEADS, N_META, Q_TILE), F32)),
        compiler_params=pltpu.CompilerParams(dimension_semantics=("arbitrary", "arbitrary")),
        name="bias_tables",
    )(rel_bias)


def _order_bits_to_float(u):
    bits = jnp.where(u < 0, u ^ jnp.int32(-2 ** 31), ~u)
    return lax.bitcast_convert_type(bits, F32)


def _fold_rows(x, op):
    r, l = x.shape
    x3 = x.reshape(r // 8, 8, l)
    return op(x3, axis=0)


def _attn_kernel(q_ref, iq_ref, iwt_ref, k_ref, ik_ref, vt_ref, km_ref, ikm_ref, vtm_ref,
                 tab_ref, tabm_ref, o_ref, sc_ref, scm_ref, l_ref, lm_ref, ot_ref, cst_ref):
    j = pl.program_id(1)
    nchunks = j + 1
    iw = iwt_ref[...]

    def chunk_rows(c):
        return pl.ds(pl.multiple_of(c * K_CHUNK, K_CHUNK), K_CHUNK)

    def idx_scores(ikc):
        acc = None
        for h in range(IDX_HEADS):
            s = lax.dot_general(ikc, iq_ref[h], NT_DIMS, preferred_element_type=F32)
            term = jnp.maximum(s, 0.0) * iw[h:h + 1, :]
            acc = term if acc is None else acc + term
        return acc

    scm_ref[...] = idx_scores(ikm_ref[...])

    def p1(c, carry):
        rows = chunk_rows(c)
        sc_ref[rows, :] = idx_scores(ik_ref[rows, :])
        return carry

    lax.fori_loop(0, j, p1, 0)
    row_i = lax.broadcasted_iota(I32, (K_CHUNK, Q_TILE), 0)
    col_i = lax.broadcasted_iota(I32, (K_CHUNK, Q_TILE), 1)
    drows = chunk_rows(j)
    sc_ref[drows, :] = jnp.where(row_i <= col_i, idx_scores(ik_ref[drows, :]), -jnp.inf)

    def count(pred):
        def body(c, acc):
            blk = sc_ref[chunk_rows(c), :]
            return acc + _fold_rows(jnp.where(pred(blk, c), 1, 0).astype(I32), jnp.sum)
        acc = lax.fori_loop(0, nchunks, body, jnp.zeros((8, Q_TILE), I32))
        acc = acc + _fold_rows(jnp.where(pred(scm_ref[...], -1), 1, 0).astype(I32), jnp.sum)
        return jnp.sum(acc, axis=0, keepdims=True)

    def bisect(i, u):
        cand = u | lax.shift_left(jnp.int32(1), 31 - i)
        cf = _order_bits_to_float(cand)
        cnt = count(lambda blk, c: blk >= cf)
        return jnp.where(cnt >= TOPK_MAX, cand, u)

    u = lax.fori_loop(0, 32, bisect, jnp.zeros((1, Q_TILE), I32))
    thr = jnp.where((u >= 0) & (u < 0x00800000), -jnp.inf, _order_bits_to_float(u))

    cnt_ge = count(lambda blk, c: blk >= thr)
    tie = (cnt_ge > TOPK_MAX) & (thr > -jnp.inf)
    any_tie = jnp.max(tie.astype(I32))
    cst_ref[...] = jnp.full((8, Q_TILE), 2 ** 30, I32)

    def pos_of(c, shape):
        r = lax.broadcasted_iota(I32, shape, 0)
        return jnp.where(c < 0, r, r + N_META + c * K_CHUNK)

    @pl.when(any_tie > 0)
    def _():
        cnt_gt = count(lambda blk, c: blk > thr)
        need = TOPK_MAX - cnt_gt

        def bis_pos(i, cs):
            cand = cs | lax.shift_left(jnp.int32(1), 11 - i)
            f = count(lambda blk, c: (blk == thr) & (pos_of(c, blk.shape) < cand))
            return jnp.where(f <= need, cand, cs)

        cs = lax.fori_loop(0, 12, bis_pos, jnp.zeros((1, Q_TILE), I32))
        cs = jnp.where(tie, cs, 2 ** 30)
        cst_ref[...] = jnp.broadcast_to(cs, (8, Q_TILE))

    def to_mask(blk, c, with_ties):
        if with_ties:
            cs = cst_ref[0:1, :]
            sel = (blk > thr) | ((blk == thr) & (pos_of(c, blk.shape) < cs))
        else:
            sel = blk >= thr
        return jnp.where(sel, 0.0, NEG)

    for with_ties in (False, True):
        @pl.when((any_tie > 0) == with_ties)
        def _():
            def body(c, carry):
                rows = chunk_rows(c)
                sc_ref[rows, :] = to_mask(sc_ref[rows, :], c, with_ties)
                return carry
            lax.fori_loop(0, nchunks, body, 0)
            scm_ref[...] = to_mask(scm_ref[...], -1, with_ties)

    kindm = jnp.minimum(j, 1)
    for h in range(N_HEADS):
        qh = q_ref[h]
        hs = slice(h * HEAD_DIM, (h + 1) * HEAD_DIM)

        lm = (lax.dot_general(km_ref[h], qh, NT_DIMS, preferred_element_type=F32)
              + tabm_ref[kindm, h] + scm_ref[...])
        lm_ref[...] = lm
        mx0 = _fold_rows(lm, jnp.max)

        def pa(c, mx):
            rows = chunk_rows(c)
            kind = jnp.minimum(j - c, 2)
            l = (lax.dot_general(k_ref[h, rows, :], qh, NT_DIMS, preferred_element_type=F32)
                 + tab_ref[kind, h] + sc_ref[rows, :])
            l_ref[rows, :] = l
            return jnp.maximum(mx, _fold_rows(l, jnp.max))

        mx = lax.fori_loop(0, nchunks, pa, mx0)
        m = jnp.max(mx, axis=0, keepdims=True)

        pm = jnp.exp(lm_ref[...] - m)
        den0 = _fold_rows(pm, jnp.sum)
        acc0 = jnp.dot(vtm_ref[hs, :], pm.astype(BF16), preferred_element_type=F32)

        def pb(c, carry):
            den, acc = carry
            rows = chunk_rows(c)
            p = jnp.exp(l_ref[rows, :] - m)
            den = den + _fold_rows(p, jnp.sum)
            acc = acc + jnp.dot(vt_ref[hs, rows], p.astype(BF16), preferred_element_type=F32)
            return den, acc

        den, acc = lax.fori_loop(0, nchunks, pb, (den0, acc0))
        ot_ref[hs, :] = acc / jnp.sum(den, axis=0, keepdims=True)

    o_ref[...] = ot_ref[...].T.astype(BF16)


def _attention(q3, iq3, iwt, k3, ik, vt, km3, ikm, vtm, tab, tabm, batch, seq):
    n = batch * seq
    tiles = seq // Q_TILE
    full = lambda a: pl.BlockSpec(a.shape, lambda b, j: (0,) * a.ndim)
    return pl.pallas_call(
        _attn_kernel,
        grid=(batch, tiles),
        in_specs=[
            pl.BlockSpec((N_HEADS, Q_TILE, HEAD_DIM), lambda b, j: (0, b * tiles + j, 0)),
            pl.BlockSpec((IDX_HEADS, Q_TILE, IDX_DIM), lambda b, j: (0, b * tiles + j, 0)),
            pl.BlockSpec((IDX_HEADS, Q_TILE), lambda b, j: (0, b * tiles + j)),
            pl.BlockSpec((N_HEADS, seq, HEAD_DIM), lambda b, j: (0, b, 0)),
            pl.BlockSpec((seq, IDX_DIM), lambda b, j: (b, 0)),
            pl.BlockSpec((ATTN_WIDTH, seq), lambda b, j: (0, b)),
            full(km3), full(ikm), full(vtm), full(tab), full(tabm),
        ],
        out_specs=pl.BlockSpec((Q_TILE, ATTN_WIDTH), lambda b, j: (b * tiles + j, 0)),
        out_shape=jax.ShapeDtypeStruct((n, ATTN_WIDTH), BF16),
        scratch_shapes=[
            pltpu.VMEM((seq, Q_TILE), F32),
            pltpu.VMEM((N_META, Q_TILE), F32),
            pltpu.VMEM((seq, Q_TILE), F32),
            pltpu.VMEM((N_META, Q_TILE), F32),
            pltpu.VMEM((ATTN_WIDTH, Q_TILE), F32),
            pltpu.VMEM((8, Q_TILE), I32),
        ],
        compiler_params=pltpu.CompilerParams(dimension_semantics=("arbitrary", "arbitrary"),
                                             vmem_limit_bytes=VMEM_LIMIT),
        name="attn",
    )(q3, iq3, iwt, k3, ik, vt, km3, ikm, vtm, tab, tabm)


def _mix_kernel(x_ref, attn_ref, y_ref, yprev_ref, ymeta_ref, gate_ref,
                cw_ref, cb_ref, lng_ref, lnb_ref, wco_ref, wao_ref, wout_ref, nf_ref, wrt_ref, br_ref,
                h2_ref, hn2_ref, eid_ref, rank_ref, gcol_ref, cnt_ref,
                win_ref, base_ref, *, tiles_per_seq):
    i = pl.program_id(0)

    @pl.when(i == 0)
    def _():
        base_ref[...] = jnp.zeros_like(base_ref)

    first = (i % tiles_per_seq) == 0
    win_ref[0:HALO, :] = jnp.where(first, ymeta_ref[...], yprev_ref[...])
    win_ref[HALO:, :] = y_ref[...]
    acc = jnp.zeros((ROW_TILE, CONV_CH), F32)
    lead = HALO - (CONV_WIDTH - 1)
    for b in range(8):
        taps = [w for w in range(b, CONV_WIDTH, 8)]
        span = ROW_TILE + 8 * (len(taps) - 1)
        yb = win_ref[pl.ds(lead + b, span), :]
        for a, w in enumerate(taps):
            acc = acc + cw_ref[w:w + 1, :] * yb[8 * a:8 * a + ROW_TILE, :]
    yc = acc + cb_ref[...]
    mu = jnp.mean(yc, axis=-1, keepdims=True)
    var = jnp.mean(jnp.square(yc - mu), axis=-1, keepdims=True)
    yn = (yc - mu) * lax.rsqrt(var + EPS) * lng_ref[...] + lnb_ref[...]
    ys = yn * _sigmoid(yn)
    y_b = jnp.dot(ys.astype(BF16), wco_ref[...], preferred_element_type=F32)

    y_a = jnp.dot(attn_ref[...], wao_ref[...], preferred_element_type=F32)
    merged = gate_ref[:, :D_MODEL] * y_a + gate_ref[:, D_MODEL:] * y_b
    h2 = x_ref[...] + jnp.dot(merged.astype(BF16), wout_ref[...], preferred_element_type=F32)
    h2_ref[...] = h2
    ms = jnp.mean(h2 * h2, axis=-1, keepdims=True)
    hn2 = h2 * lax.rsqrt(ms + EPS) * nf_ref[...]
    hn2_ref[...] = _pack_bf16_pairs(hn2)

    logits = lax.dot_general(wrt_ref[...], hn2, NT_DIMS, preferred_element_type=F32,
                             precision=lax.Precision.HIGHEST) + br_ref[...]
    erow = lax.broadcasted_iota(I32, (N_EXPERTS, ROW_TILE), 0)
    vals, ids = [], []
    l = logits
    for _ in range(TOP_K_EXPERTS):
        m = jnp.max(l, axis=0, keepdims=True)
        idx = jnp.min(jnp.where(l == m, erow, N_EXPERTS), axis=0, keepdims=True)
        vals.append(m)
        ids.append(idx)
        l = jnp.where(erow == idx, -jnp.inf, l)
    ex = [jnp.exp(v - vals[0]) for v in vals]
    den = ex[0] + ex[1] + ex[2] + ex[3]
    gates = [e / den for e in ex]

    onehot = [(erow == idx) for idx in ids]
    oh = jnp.concatenate([jnp.where(o, 1.0, 0.0) for o in onehot], axis=0)
    tr = lax.broadcasted_iota(I32, (ROW_TILE, ROW_TILE), 0)
    tc = lax.broadcasted_iota(I32, (ROW_TILE, ROW_TILE), 1)
    upper = jnp.where(tr <= tc, 1.0, 0.0).astype(BF16)
    pref = jnp.dot(oh.astype(BF16), upper, preferred_element_type=F32)
    offs = base_ref[:, 0:1]
    ranks = []
    for kk in range(TOP_K_EXPERTS):
        pk = pref[kk * N_EXPERTS:(kk + 1) * N_EXPERTS, :]
        r = jnp.sum(jnp.where(onehot[kk], offs + pk - 1.0, 0.0), axis=0, keepdims=True)
        ranks.append(r.astype(I32))
        offs = offs + pk[:, ROW_TILE - 1:ROW_TILE]
    base_ref[...] = jnp.broadcast_to(offs, base_ref.shape)
    cnt_ref[...] = jnp.broadcast_to(offs, cnt_ref.shape)

    zi = jnp.zeros((8 - TOP_K_EXPERTS, ROW_TILE), I32)
    eid_ref[...] = jnp.concatenate(ids + [zi], axis=0)
    rank_ref[...] = jnp.concatenate(ranks + [zi], axis=0)
    g8 = jnp.concatenate(gates + [jnp.zeros((128 - TOP_K_EXPERTS, ROW_TILE), F32)], axis=0)
    gcol_ref[...] = g8.T


def _mix(x2d, attn, y, ymeta, gate, wts, seq):
    n = x2d.shape[0]
    tiles_per_seq = seq // ROW_TILE
    halo_per_tile = ROW_TILE // HALO
    full = lambda a: pl.BlockSpec(a.shape, lambda i: (0,) * a.ndim)
    row = lambda w: pl.BlockSpec((ROW_TILE, w), lambda i: (i, 0))
    lane = lambda r: pl.BlockSpec((r, ROW_TILE), lambda i: (0, i))
    out_shape = (
        jax.ShapeDtypeStruct((n, D_MODEL), F32),
        jax.ShapeDtypeStruct((n, D_MODEL // 2), I32),
        jax.ShapeDtypeStruct((8, n), I32),
        jax.ShapeDtypeStruct((8, n), I32),
        jax.ShapeDtypeStruct((n, 128), F32),
        jax.ShapeDtypeStruct((N_EXPERTS, 128), F32),
    )
    out_specs = (row(D_MODEL), row(D_MODEL // 2), lane(8), lane(8), row(128), full(out_shape[5]))
    return pl.pallas_call(
        functools.partial(_mix_kernel, tiles_per_seq=tiles_per_seq),
        grid=(n // ROW_TILE,),
        in_specs=[row(D_MODEL), row(ATTN_WIDTH), row(CONV_CH),
                  pl.BlockSpec((HALO, CONV_CH), lambda i: (jnp.maximum(i * halo_per_tile - 1, 0), 0)),
                  full(ymeta), row(2 * D_MODEL)] + [full(w) for w in wts],
        out_specs=out_specs,
        out_shape=out_shape,
        scratch_shapes=[pltpu.VMEM((HALO + ROW_TILE, CONV_CH), F32),
                        pltpu.VMEM((N_EXPERTS, 128), F32)],
        compiler_params=pltpu.CompilerParams(dimension_semantics=("arbitrary",),
                                             vmem_limit_bytes=VMEM_LIMIT),
        name="mix",
    )(x2d, attn, y, y, ymeta, gate, *wts)


def _slots_kernel(starts_ref, eid_ref, rank_ref, slot_ref):
    eid = eid_ref[...]
    base = jnp.zeros(eid.shape, I32)
    for e in range(N_EXPERTS):
        base = jnp.where(eid == e, starts_ref[e], base)
    slot_ref[...] = base + rank_ref[...]


def _slots(starts, eid8, rank8):
    n = eid8.shape[1]
    cols = 2048
    spec = pl.BlockSpec((8, cols), lambda i: (0, i))
    return pl.pallas_call(
        _slots_kernel,
        grid=(n // cols,),
        in_specs=[pl.BlockSpec(memory_space=pltpu.SMEM), spec, spec],
        out_specs=spec,
        out_shape=jax.ShapeDtypeStruct((8, n), I32),
        compiler_params=pltpu.CompilerParams(dimension_semantics=("arbitrary",)),
        name="slots",
    )(starts, eid8, rank8)


COPIES_PER_TILE = TOP_K_EXPERTS * ROW_TILE
ISSUE_UNROLL = 8


def _row_copy_loop(make_copy):
    def body(g, carry):
        for u in range(ISSUE_UNROLL):
            make_copy(g * ISSUE_UNROLL + u).start()
        return carry
    lax.fori_loop(0, COPIES_PER_TILE // ISSUE_UNROLL, body, 0)


def _dispatch_kernel(slot_ref, hn_ref, xg_in_ref, xg_ref, sem):
    del xg_in_ref

    def copy(idx):
        t = idx & (ROW_TILE - 1)
        s = slot_ref[0, 0, idx]
        return pltpu.make_async_copy(hn_ref.at[pl.ds(t, 1), :], xg_ref.at[pl.ds(s, 1), :], sem)

    _row_copy_loop(copy)
    for _ in range(TOP_K_EXPERTS):
        pltpu.make_async_copy(hn_ref, xg_ref.at[pl.ds(0, ROW_TILE), :], sem).wait()


def _dispatch(slot_tiles, hn2p, xg_init):
    n = hn2p.shape[0]
    return pl.pallas_call(
        _dispatch_kernel,
        grid=(n // ROW_TILE,),
        in_specs=[pl.BlockSpec((1, 1, COPIES_PER_TILE), lambda i: (i, 0, 0), memory_space=pltpu.SMEM),
                  pl.BlockSpec((ROW_TILE, D_MODEL // 2), lambda i: (i, 0)),
                  pl.BlockSpec(memory_space=pl.ANY)],
        out_specs=pl.BlockSpec(memory_space=pl.ANY),
        out_shape=jax.ShapeDtypeStruct(xg_init.shape, xg_init.dtype),
        scratch_shapes=[pltpu.SemaphoreType.DMA(())],
        input_output_aliases={2: 0},
        compiler_params=pltpu.CompilerParams(dimension_semantics=("arbitrary",)),
        name="dispatch",
    )(slot_tiles, hn2p, xg_init)


def _expert_kernel(te_ref, nused_ref, xg_ref, wug_ref, bug_ref, wd_ref, bd_ref, y_ref, wug_bf, wd_bf):
    i = pl.program_id(0)
    used = i < nused_ref[0]
    prev = te_ref[jnp.maximum(i - 1, 0)]
    fresh = (i == 0) | (te_ref[i] != prev)

    @pl.when(used & fresh)
    def _():
        wug_bf[...] = wug_ref[0].astype(BF16)
        wd_bf[...] = wd_ref[0].astype(BF16)

    @pl.when(used)
    def _():
        xb = _unpack_bf16_pairs(xg_ref[...])
        ug = jnp.dot(xb, wug_bf[...], preferred_element_type=F32) + bug_ref[0]
        gate = jnp.minimum(ug[:, :D_FF], SWIGLU_LIMIT)
        up = jnp.clip(ug[:, D_FF:], -SWIGLU_LIMIT, SWIGLU_LIMIT)
        act = (up + 1.0) * gate * _sigmoid(SWIGLU_ALPHA * gate)
        y_ref[...] = jnp.dot(act.astype(BF16), wd_bf[...], preferred_element_type=F32) + bd_ref[0]

    @pl.when(jnp.logical_not(used))
    def _():
        y_ref[...] = jnp.zeros_like(y_ref)


def _experts(tile_expert, nused, xg, w_ug, b_ug, w_down, b_down):
    s = xg.shape[0]
    ntiles = s // ROW_TILE
    clamp = lambda i, nu: jnp.minimum(i, nu[0] - 1)
    grid_spec = pltpu.PrefetchScalarGridSpec(
        num_scalar_prefetch=2,
        grid=(ntiles,),
        in_specs=[
            pl.BlockSpec((ROW_TILE, D_MODEL // 2), lambda i, te, nu: (clamp(i, nu), 0)),
            pl.BlockSpec((1, D_MODEL, 2 * D_FF), lambda i, te, nu: (te[i], 0, 0)),
            pl.BlockSpec((1, 1, 2 * D_FF), lambda i, te, nu: (te[i], 0, 0)),
            pl.BlockSpec((1, D_FF, D_MODEL), lambda i, te, nu: (te[i], 0, 0)),
            pl.BlockSpec((1, 1, D_MODEL), lambda i, te, nu: (te[i], 0, 0)),
        ],
        out_specs=pl.BlockSpec((ROW_TILE, D_MODEL), lambda i, te, nu: (i, 0)),
        scratch_shapes=[pltpu.VMEM((D_MODEL, 2 * D_FF), BF16), pltpu.VMEM((D_FF, D_MODEL), BF16)],
    )
    return pl.pallas_call(
        _expert_kernel,
        grid_spec=grid_spec,
        out_shape=jax.ShapeDtypeStruct((s, D_MODEL), F32),
        compiler_params=pltpu.CompilerParams(dimension_semantics=("arbitrary",),
                                             vmem_limit_bytes=VMEM_LIMIT),
        name="experts",
    )(tile_expert, nused, xg, w_ug, b_ug[:, None, :], w_down, b_down[:, None, :])


def _final_kernel(slot_ref, slot_next_ref, h2_ref, gcol_ref, nf_ref, y_hbm, o_ref, buf, sem):
    i = pl.program_id(0)
    last = pl.num_programs(0) - 1

    def gather(slots, b):
        def copy(idx):
            kk = lax.shift_right_logical(idx, ROW_TILE.bit_length() - 1)
            t = idx & (ROW_TILE - 1)
            s = slots[0, 0, idx]
            return pltpu.make_async_copy(y_hbm.at[pl.ds(s, 1), :], buf.at[b, kk, pl.ds(t, 1), :], sem.at[b])
        _row_copy_loop(copy)

    @pl.when(i == 0)
    def _():
        gather(slot_ref, 0)

    @pl.when(i < last)
    def _():
        gather(slot_next_ref, (i + 1) % 2)

    b = i % 2
    for kk in range(TOP_K_EXPERTS):
        pltpu.make_async_copy(y_hbm.at[pl.ds(0, ROW_TILE), :], buf.at[b, kk], sem.at[b]).wait()
    h = h2_ref[...]
    for kk in range(TOP_K_EXPERTS):
        h = h + gcol_ref[:, kk:kk + 1] * buf[b, kk]
    ms = jnp.mean(h * h, axis=-1, keepdims=True)
    o_ref[...] = h * lax.rsqrt(ms + EPS) * nf_ref[...]


def _final(slot_tiles, h2, gcol, nf, yslot):
    n = h2.shape[0]
    ntiles = n // ROW_TILE
    smem_tile = lambda f: pl.BlockSpec((1, 1, COPIES_PER_TILE), f, memory_space=pltpu.SMEM)
    return pl.pallas_call(
        _final_kernel,
        grid=(ntiles,),
        in_specs=[smem_tile(lambda i: (i, 0, 0)),
                  smem_tile(lambda i: (jnp.minimum(i + 1, ntiles - 1), 0, 0)),
                  pl.BlockSpec((ROW_TILE, D_MODEL), lambda i: (i, 0)),
                  pl.BlockSpec((ROW_TILE, 128), lambda i: (i, 0)),
                  pl.BlockSpec((1, D_MODEL), lambda i: (0, 0)),
                  pl.BlockSpec(memory_space=pl.ANY)],
        out_specs=pl.BlockSpec((ROW_TILE, D_MODEL), lambda i: (i, 0)),
        out_shape=jax.ShapeDtypeStruct((n, D_MODEL), F32),
        scratch_shapes=[pltpu.VMEM((2, TOP_K_EXPERTS, ROW_TILE, D_MODEL), F32),
                        pltpu.SemaphoreType.DMA((2,))],
        compiler_params=pltpu.CompilerParams(dimension_semantics=("arbitrary",),
                                             vmem_limit_bytes=VMEM_LIMIT),
        name="final",
    )(slot_tiles, slot_tiles, h2, gcol, nf, yslot)


def _split_w_in(w_in, b_gate):
    c = ATTN_WIDTH
    o = 0
    wq = w_in[:, o:o + c]; o += c
    wk = w_in[:, o:o + c]; o += c
    wv = w_in[:, o:o + c]; o += c
    wiq = w_in[:, o:o + IDX_HEADS * IDX_DIM]; o += IDX_HEADS * IDX_DIM
    wik = w_in[:, o:o + IDX_DIM]; o += IDX_DIM
    wiw = w_in[:, o:o + IDX_HEADS]; o += IDX_HEADS
    wglu = w_in[:, o:o + 2 * CONV_CH]; o += 2 * CONV_CH
    wgate = w_in[:, o:]
    wiwt = jnp.concatenate([wiw.T, jnp.zeros((16 - IDX_HEADS, D_MODEL), w_in.dtype)], axis=0)
    bf = lambda a: a.astype(BF16)
    return (bf(wq), bf(wk), bf(wiq), bf(wik), bf(wv.T), bf(wiwt), bf(wglu), bf(wgate),
            b_gate[None, :].astype(F32))


def kernel(x, meta_tokens, rel_bias, norm_mix, w_in, b_gate, w_attn_out, conv_w, conv_b, conv_ln_g, conv_ln_b, w_conv_out, w_out, norm_ffn, w_router, b_router, w_up_gate, b_up_gate, w_down, b_down, norm_final):
    batch, seq, d = x.shape
    n = batch * seq
    x2d = x.reshape(n, d)

    wts = _split_w_in(w_in[0], b_gate[0])
    g_mix = norm_mix[0][None, :]
    q3, k3, iq3, ik, vt, iwt, y, gate = _project(x2d, g_mix, wts, ROW_TILE)
    _, km3, _, ikm, vtm, _, ym, _ = _project(meta_tokens.astype(F32), g_mix, wts, N_META)

    tab, tabm = _bias_tables(rel_bias.astype(F32))
    attn = _attention(q3, iq3, iwt, k3, ik, vt, km3, ikm, vtm, tab, tabm, batch, seq)

    ymeta = jnp.concatenate([jnp.zeros((HALO - N_META, CONV_CH), F32), ym], axis=0)
    cw = jnp.concatenate([conv_w[0], jnp.zeros((32 - CONV_WIDTH, CONV_CH), F32)], axis=0)
    mix_w = (cw, conv_b[0][None, :], conv_ln_g[0][None, :], conv_ln_b[0][None, :],
             w_conv_out[0].astype(BF16), w_attn_out[0].astype(BF16), w_out[0].astype(BF16),
             norm_ffn[0][None, :], w_router[0].T, b_router[0][:, None])
    h2, hn2, eid8, rank8, gcol, cnt = _mix(x2d, attn, y, ymeta, gate, mix_w, seq)

    counts = cnt[:, 0].astype(I32)
    padded = ((counts + ROW_TILE - 1) // ROW_TILE) * ROW_TILE
    ends = jnp.cumsum(padded)
    starts = ends - padded
    slot8 = _slots(starts, eid8, rank8)
    slot_tiles = (slot8[:TOP_K_EXPERTS].reshape(TOP_K_EXPERTS, n // ROW_TILE, ROW_TILE)
                  .transpose(1, 0, 2).reshape(n // ROW_TILE, 1, COPIES_PER_TILE))
    nslots = n * TOP_K_EXPERTS + N_EXPERTS * ROW_TILE
    ntiles = nslots // ROW_TILE
    tile_start = jnp.arange(ntiles, dtype=I32) * ROW_TILE
    nused = (ends[-1] // ROW_TILE).astype(I32)
    last_start = jnp.maximum(ends[-1] - ROW_TILE, 0)
    tile_expert = jnp.sum((jnp.minimum(tile_start, last_start)[:, None] >= ends[None, :]).astype(I32), axis=1)

    xg = _dispatch(slot_tiles, hn2, jnp.zeros((nslots, d // 2), I32))
    yslot = _experts(tile_expert, nused[None], xg, w_up_gate[0], b_up_gate[0], w_down[0], b_down[0])
    out = _final(slot_tiles, h2, gcol, norm_final[None, :], yslot)
    return out.reshape(batch, seq, d)
```

```python
import functools
import math

import jax
import jax.numpy as jnp
from jax import lax
from jax.experimental import pallas as pl
from jax.experimental.pallas import tpu as pltpu

F32 = jnp.float32
BF16 = jnp.bfloat16
I32 = jnp.int32

D_MODEL = 1024
N_META = 16
N_HEADS = 8
HEAD_DIM = 64
ATTN_WIDTH = N_HEADS * HEAD_DIM
IDX_HEADS = 8
IDX_DIM = 64
TOPK_MAX = 256
CONV_CH = 512
CONV_WIDTH = 31
N_BUCKETS = 32
MAX_DISTANCE = 128
N_EXPERTS = 32
TOP_K_EXPERTS = 4
D_FF = 1024
SWIGLU_LIMIT = 7.0
SWIGLU_ALPHA = 1.702
EPS = 1e-6
IDX_SCALE = (IDX_DIM ** -0.5) * (IDX_HEADS ** -0.5)

ROW_TILE = 256
Q_TILE = 256
K_CHUNK = 256
HALO = 32
NEG = -1e30
VMEM_LIMIT = 56 * 1024 * 1024

NT_DIMS = (((1,), (1,)), ((), ()))


def _sigmoid(x):
    return 1.0 / (1.0 + jnp.exp(-x))


def _pack_bf16_pairs(x):
    w = x.shape[1] // 2
    hi = lax.bitcast_convert_type(x[:, :w].astype(BF16).astype(F32), jnp.uint32)
    lo = lax.bitcast_convert_type(x[:, w:].astype(BF16).astype(F32), jnp.uint32)
    return lax.bitcast_convert_type(hi | (lo >> 16), I32)


def _unpack_bf16_pairs(p):
    u = lax.bitcast_convert_type(p, jnp.uint32)
    hi = lax.bitcast_convert_type(u & jnp.uint32(0xFFFF0000), F32)
    lo = lax.bitcast_convert_type(u << 16, F32)
    return jnp.concatenate([hi, lo], axis=1).astype(BF16)


def _proj_kernel(x_ref, g_ref, wq_ref, wk_ref, wiq_ref, wik_ref, wvt_ref, wiwt_ref, wglu_ref,
                 wgate_ref, bgate_ref,
                 q_ref, k_ref, iq_ref, ik_ref, vt_ref, iwt_ref, y_ref, gate_ref):
    x = x_ref[...]
    ms = jnp.mean(x * x, axis=-1, keepdims=True)
    xn = (x * lax.rsqrt(ms + EPS) * g_ref[...]).astype(BF16)

    q = jnp.dot(xn, wq_ref[...], preferred_element_type=F32) * (HEAD_DIM ** -0.5)
    k = jnp.dot(xn, wk_ref[...], preferred_element_type=F32)
    iq = jnp.dot(xn, wiq_ref[...], preferred_element_type=F32)
    for h in range(N_HEADS):
        sl = slice(h * HEAD_DIM, (h + 1) * HEAD_DIM)
        q_ref[h] = q[:, sl].astype(BF16)
        k_ref[h] = k[:, sl].astype(BF16)
        iq_ref[h] = iq[:, sl].astype(BF16)
    ik_ref[...] = jnp.dot(xn, wik_ref[...], preferred_element_type=F32).astype(BF16)
    vt_ref[...] = lax.dot_general(wvt_ref[...], xn, NT_DIMS, preferred_element_type=F32).astype(BF16)
    iwt = lax.dot_general(wiwt_ref[...], xn, NT_DIMS, preferred_element_type=F32)
    iwt_ref[...] = iwt[:IDX_HEADS] * IDX_SCALE
    glu = jnp.dot(xn, wglu_ref[...], preferred_element_type=F32)
    y_ref[...] = glu[:, :CONV_CH] * _sigmoid(glu[:, CONV_CH:])
    gate = jnp.dot(xn, wgate_ref[...], preferred_element_type=F32) + bgate_ref[...]
    gate_ref[...] = _sigmoid(gate)


def _project(x2d, g, wts, tm):
    n = x2d.shape[0]
    wq, wk, wiq, wik, wvt, wiwt, wglu, wgate, bgate = wts
    full = lambda a: pl.BlockSpec(a.shape, lambda i: (0,) * a.ndim)
    out_shape = (
        jax.ShapeDtypeStruct((N_HEADS, n, HEAD_DIM), BF16),
        jax.ShapeDtypeStruct((N_HEADS, n, HEAD_DIM), BF16),
        jax.ShapeDtypeStruct((IDX_HEADS, n, IDX_DIM), BF16),
        jax.ShapeDtypeStruct((n, IDX_DIM), BF16),
        jax.ShapeDtypeStruct((ATTN_WIDTH, n), BF16),
        jax.ShapeDtypeStruct((IDX_HEADS, n), F32),
        jax.ShapeDtypeStruct((n, CONV_CH), F32),
        jax.ShapeDtypeStruct((n, 2 * D_MODEL), F32),
    )
    out_specs = (
        pl.BlockSpec((N_HEADS, tm, HEAD_DIM), lambda i: (0, i, 0)),
        pl.BlockSpec((N_HEADS, tm, HEAD_DIM), lambda i: (0, i, 0)),
        pl.BlockSpec((IDX_HEADS, tm, IDX_DIM), lambda i: (0, i, 0)),
        pl.BlockSpec((tm, IDX_DIM), lambda i: (i, 0)),
        pl.BlockSpec((ATTN_WIDTH, tm), lambda i: (0, i)),
        pl.BlockSpec((IDX_HEADS, tm), lambda i: (0, i)),
        pl.BlockSpec((tm, CONV_CH), lambda i: (i, 0)),
        pl.BlockSpec((tm, 2 * D_MODEL), lambda i: (i, 0)),
    )
    return pl.pallas_call(
        _proj_kernel,
        grid=(n // tm,),
        in_specs=[pl.BlockSpec((tm, D_MODEL), lambda i: (i, 0)), full(g), full(wq), full(wk), full(wiq),
                  full(wik), full(wvt), full(wiwt), full(wglu), full(wgate), full(bgate)],
        out_specs=out_specs,
        out_shape=out_shape,
        compiler_params=pltpu.CompilerParams(dimension_semantics=("arbitrary",),
                                             vmem_limit_bytes=VMEM_LIMIT),
        name="proj",
    )(x2d, g, wq, wk, wiq, wik, wvt, wiwt, wglu, wgate, bgate)


def _t5_bucket(n):
    max_exact = N_BUCKETS // 2
    nf = jnp.maximum(n, 1).astype(F32)
    large = max_exact + (jnp.log(nf / max_exact) / math.log(MAX_DISTANCE / max_exact)
                         * (N_BUCKETS - max_exact)).astype(I32)
    large = jnp.minimum(large, N_BUCKETS - 1)
    return jnp.where(n < max_exact, n, large)


def _bias_lookup(rb_ref, dist, h):
    bucket = _t5_bucket(jnp.maximum(dist, 0))
    out = jnp.full(dist.shape, NEG, F32)
    for b in range(N_BUCKETS):
        out = jnp.where(bucket == b, rb_ref[b, h], out)
    return jnp.where(dist >= 0, out, NEG)


def _bias_kernel(rb_ref, tab_ref, tabm_ref):
    kind = pl.program_id(0)
    r = pl.program_id(1)
    rows = tab_ref.shape[2]
    s = lax.broadcasted_iota(I32, (rows, Q_TILE), 0) + r * rows
    t = lax.broadcasted_iota(I32, (rows, Q_TILE), 1)
    dist = jnp.where(kind == 2, 2 * K_CHUNK, t - s + kind * K_CHUNK)
    for h in range(N_HEADS):
        tab_ref[0, h] = _bias_lookup(rb_ref, dist, h)
    m = lax.broadcasted_iota(I32, (N_META, Q_TILE), 0)
    tm_ = lax.broadcasted_iota(I32, (N_META, Q_TILE), 1)
    distm = jnp.where(kind == 0, N_META + tm_ - m, 2 * K_CHUNK)
    for h in range(N_HEADS):
        tabm_ref[0, h] = _bias_lookup(rb_ref, distm, h)


def _bias_tables(rel_bias):
    rows = 64
    return pl.pallas_call(
        _bias_kernel,
        grid=(3, K_CHUNK // rows),
        in_specs=[pl.BlockSpec(memory_space=pltpu.SMEM)],
        out_specs=(pl.BlockSpec((1, N_HEADS, rows, Q_TILE), lambda kd, r: (kd, 0, r, 0)),
                   pl.BlockSpec((1, N_HEADS, N_META, Q_TILE), lambda kd, r: (kd, 0, 0, 0))),
        out_shape=(jax.ShapeDtypeStruct((3, N_HEADS, K_CHUNK, Q_TILE), F32),
                   jax.ShapeDtypeStruct((3, N_HEADS, N_META, Q_TILE), F32)),
        compiler_params=pltpu.CompilerParams(dimension_semantics=("arbitrary", "arbitrary")),
        name="bias_tables",
    )(rel_bias)


def _order_bits_to_float(u):
    bits = jnp.where(u < 0, u ^ jnp.int32(-2 ** 31), ~u)
    return lax.bitcast_convert_type(bits, F32)


def _fold_rows(x, op):
    r, l = x.shape
    x3 = x.reshape(r // 8, 8, l)
    return op(x3, axis=0)


def _attn_kernel(q_ref, iq_ref, iwt_ref, k_ref, ik_ref, vt_ref, km_ref, ikm_ref, vtm_ref,
                 tab_ref, tabm_ref, o_ref, sc_ref, scm_ref, l_ref, lm_ref, ot_ref, cst_ref):
    j = pl.program_id(1)
    nchunks = j + 1
    iw = iwt_ref[...]

    def chunk_rows(c):
        return pl.ds(pl.multiple_of(c * K_CHUNK, K_CHUNK), K_CHUNK)

    def idx_scores(ikc):
        acc = None
        for h in range(IDX_HEADS):
            s = lax.dot_general(ikc, iq_ref[h], NT_DIMS, preferred_element_type=F32)
            term = jnp.maximum(s, 0.0) * iw[h:h + 1, :]
            acc = term if acc is None else acc + term
        return acc

    scm_ref[...] = idx_scores(ikm_ref[...])

    row_minus_col = (lax.broadcasted_iota(I32, (K_CHUNK, Q_TILE), 0)
                     - lax.broadcasted_iota(I32, (K_CHUNK, Q_TILE), 1))

    def p1(c, carry):
        rows = chunk_rows(c)
        future = row_minus_col > jnp.where(c == j, 0, K_CHUNK)
        sc_ref[rows, :] = jnp.where(future, -jnp.inf, idx_scores(ik_ref[rows, :]))
        return carry

    lax.fori_loop(0, nchunks, p1, 0)

    def count(pred):
        def body(c, acc):
            blk = sc_ref[chunk_rows(c), :]
            return acc + _fold_rows(jnp.where(pred(blk, c), 1, 0).astype(I32), jnp.sum)
        acc = lax.fori_loop(0, nchunks, body, jnp.zeros((8, Q_TILE), I32))
        acc = acc + _fold_rows(jnp.where(pred(scm_ref[...], -1), 1, 0).astype(I32), jnp.sum)
        return jnp.sum(acc, axis=0, keepdims=True)

    def bisect(i, u):
        cand = u | lax.shift_left(jnp.int32(1), 31 - i)
        cf = _order_bits_to_float(cand)
        cnt = count(lambda blk, c: blk >= cf)
        return jnp.where(cnt >= TOPK_MAX, cand, u)

    u = lax.fori_loop(0, 32, bisect, jnp.zeros((1, Q_TILE), I32))
    thr = jnp.where((u >= 0) & (u < 0x00800000), -jnp.inf, _order_bits_to_float(u))

    cnt_ge = count(lambda blk, c: blk >= thr)
    tie = (cnt_ge > TOPK_MAX) & (thr > -jnp.inf)
    any_tie = jnp.max(tie.astype(I32))
    cst_ref[...] = jnp.full((8, Q_TILE), 2 ** 30, I32)

    def pos_of(c, shape):
        r = lax.broadcasted_iota(I32, shape, 0)
        return jnp.where(c < 0, r, r + N_META + c * K_CHUNK)

    @pl.when(any_tie > 0)
    def _():
        cnt_gt = count(lambda blk, c: blk > thr)
        need = TOPK_MAX - cnt_gt

        def bis_pos(i, cs):
            cand = cs | lax.shift_left(jnp.int32(1), 11 - i)
            f = count(lambda blk, c: (blk == thr) & (pos_of(c, blk.shape) < cand))
            return jnp.where(f <= need, cand, cs)

        cs = lax.fori_loop(0, 12, bis_pos, jnp.zeros((1, Q_TILE), I32))
        cs = jnp.where(tie, cs, 2 ** 30)
        cst_ref[...] = jnp.broadcast_to(cs, (8, Q_TILE))

    def to_mask(blk, c, with_ties):
        if with_ties:
            cs = cst_ref[0:1, :]
            sel = (blk > thr) | ((blk == thr) & (pos_of(c, blk.shape) < cs))
        else:
            sel = blk >= thr
        return jnp.where(sel, 0.0, NEG)

    for with_ties in (False, True):
        @pl.when((any_tie > 0) == with_ties)
        def _():
            def body(c, carry):
                rows = chunk_rows(c)
                sc_ref[rows, :] = to_mask(sc_ref[rows, :], c, with_ties)
                return carry
            lax.fori_loop(0, nchunks, body, 0)
            scm_ref[...] = to_mask(scm_ref[...], -1, with_ties)

    kindm = jnp.minimum(j, 1)
    heads = [slice(h * HEAD_DIM, (h + 1) * HEAD_DIM) for h in range(N_HEADS)]

    mx0 = []
    for h in range(N_HEADS):
        lm = (lax.dot_general(km_ref[h], q_ref[h], NT_DIMS, preferred_element_type=F32)
              + tabm_ref[kindm, h] + scm_ref[...])
        lm_ref[h] = lm
        mx0.append(_fold_rows(lm, jnp.max))

    def pass_a(c, mx):
        rows = chunk_rows(c)
        kind = jnp.minimum(j - c, 2)
        mask = sc_ref[rows, :]
        out = []
        for h in range(N_HEADS):
            l = (lax.dot_general(k_ref[h, rows, :], q_ref[h], NT_DIMS, preferred_element_type=F32)
                 + tab_ref[kind, h] + mask)
            l_ref[h, rows, :] = l
            out.append(jnp.maximum(mx[h], _fold_rows(l, jnp.max)))
        return tuple(out)

    mx = lax.fori_loop(0, nchunks, pass_a, tuple(mx0))
    m = [jnp.max(mx[h], axis=0, keepdims=True) for h in range(N_HEADS)]

    den0 = []
    for h in range(N_HEADS):
        pm = jnp.exp(lm_ref[h] - m[h])
        den0.append(_fold_rows(pm, jnp.sum))
        ot_ref[heads[h], :] = jnp.dot(vtm_ref[heads[h], :], pm.astype(BF16), preferred_element_type=F32)

    def pass_b(c, den):
        rows = chunk_rows(c)
        out = []
        for h in range(N_HEADS):
            p = jnp.exp(l_ref[h, rows, :] - m[h])
            out.append(den[h] + _fold_rows(p, jnp.sum))
            ot_ref[heads[h], :] += jnp.dot(vt_ref[heads[h], rows], p.astype(BF16),
                                           preferred_element_type=F32)
        return tuple(out)

    den = lax.fori_loop(0, nchunks, pass_b, tuple(den0))
    for h in range(N_HEADS):
        ot_ref[heads[h], :] = ot_ref[heads[h], :] / jnp.sum(den[h], axis=0, keepdims=True)
    o_ref[...] = ot_ref[...].T.astype(BF16)


def _attention(q3, iq3, iwt, k3, ik, vt, km3, ikm, vtm, tab, tabm, batch, seq):
    n = batch * seq
    tiles = seq // Q_TILE
    full = lambda a: pl.BlockSpec(a.shape, lambda b, j: (0,) * a.ndim, pipeline_mode=pl.Buffered(1))
    return pl.pallas_call(
        _attn_kernel,
        grid=(batch, tiles),
        in_specs=[
            pl.BlockSpec((N_HEADS, Q_TILE, HEAD_DIM), lambda b, j: (0, b * tiles + j, 0)),
            pl.BlockSpec((IDX_HEADS, Q_TILE, IDX_DIM), lambda b, j: (0, b * tiles + j, 0)),
            pl.BlockSpec((IDX_HEADS, Q_TILE), lambda b, j: (0, b * tiles + j)),
            pl.BlockSpec((N_HEADS, seq, HEAD_DIM), lambda b, j: (0, b, 0)),
            pl.BlockSpec((seq, IDX_DIM), lambda b, j: (b, 0)),
            pl.BlockSpec((ATTN_WIDTH, seq), lambda b, j: (0, b)),
            full(km3), full(ikm), full(vtm), full(tab), full(tabm),
        ],
        out_specs=pl.BlockSpec((Q_TILE, ATTN_WIDTH), lambda b, j: (b * tiles + j, 0)),
        out_shape=jax.ShapeDtypeStruct((n, ATTN_WIDTH), BF16),
        scratch_shapes=[
            pltpu.VMEM((seq, Q_TILE), F32),
            pltpu.VMEM((N_META, Q_TILE), F32),
            pltpu.VMEM((N_HEADS, seq, Q_TILE), F32),
            pltpu.VMEM((N_HEADS, N_META, Q_TILE), F32),
            pltpu.VMEM((ATTN_WIDTH, Q_TILE), F32),
            pltpu.VMEM((8, Q_TILE), I32),
        ],
        compiler_params=pltpu.CompilerParams(dimension_semantics=("arbitrary", "arbitrary"),
                                             vmem_limit_bytes=VMEM_LIMIT),
        name="attn",
    )(q3, iq3, iwt, k3, ik, vt, km3, ikm, vtm, tab, tabm)


def _mix_kernel(x_ref, attn_ref, y_ref, yprev_ref, ymeta_ref, gate_ref,
                cw_ref, cb_ref, lng_ref, lnb_ref, wco_ref, wao_ref, wout_ref, nf_ref, wrt_ref, br_ref,
                h2_ref, hn2_ref, eid_ref, rank_ref, gcol_ref, cnt_ref,
                win_ref, base_ref, *, tiles_per_seq):
    i = pl.program_id(0)

    @pl.when(i == 0)
    def _():
        base_ref[...] = jnp.zeros_like(base_ref)

    first = (i % tiles_per_seq) == 0
    win_ref[0:HALO, :] = jnp.where(first, ymeta_ref[...], yprev_ref[...])
    win_ref[HALO:, :] = y_ref[...]
    acc = jnp.zeros((ROW_TILE, CONV_CH), F32)
    lead = HALO - (CONV_WIDTH - 1)
    for b in range(8):
        taps = [w for w in range(b, CONV_WIDTH, 8)]
        span = ROW_TILE + 8 * (len(taps) - 1)
        yb = win_ref[pl.ds(lead + b, span), :]
        for a, w in enumerate(taps):
            acc = acc + cw_ref[w:w + 1, :] * yb[8 * a:8 * a + ROW_TILE, :]
    yc = acc + cb_ref[...]
    mu = jnp.mean(yc, axis=-1, keepdims=True)
    var = jnp.mean(jnp.square(yc - mu), axis=-1, keepdims=True)
    yn = (yc - mu) * lax.rsqrt(var + EPS) * lng_ref[...] + lnb_ref[...]
    ys = yn * _sigmoid(yn)
    y_b = jnp.dot(ys.astype(BF16), wco_ref[...], preferred_element_type=F32)

    y_a = jnp.dot(attn_ref[...], wao_ref[...], preferred_element_type=F32)
    merged = gate_ref[:, :D_MODEL] * y_a + gate_ref[:, D_MODEL:] * y_b
    h2 = x_ref[...] + jnp.dot(merged.astype(BF16), wout_ref[...], preferred_element_type=F32)
    h2_ref[...] = h2
    ms = jnp.mean(h2 * h2, axis=-1, keepdims=True)
    hn2 = h2 * lax.rsqrt(ms + EPS) * nf_ref[...]
    hn2_ref[...] = _pack_bf16_pairs(hn2)

    logits = lax.dot_general(wrt_ref[...], hn2, NT_DIMS, preferred_element_type=F32,
                             precision=lax.Precision.HIGHEST) + br_ref[...]
    erow = lax.broadcasted_iota(I32, (N_EXPERTS, ROW_TILE), 0)
    vals, ids = [], []
    l = logits
    for _ in range(TOP_K_EXPERTS):
        m = jnp.max(l, axis=0, keepdims=True)
        idx = jnp.min(jnp.where(l == m, erow, N_EXPERTS), axis=0, keepdims=True)
        vals.append(m)
        ids.append(idx)
        l = jnp.where(erow == idx, -jnp.inf, l)
    ex = [jnp.exp(v - vals[0]) for v in vals]
    den = ex[0] + ex[1] + ex[2] + ex[3]
    gates = [e / den for e in ex]

    onehot = [(erow == idx) for idx in ids]
    oh = jnp.concatenate([jnp.where(o, 1.0, 0.0) for o in onehot], axis=0)
    tr = lax.broadcasted_iota(I32, (ROW_TILE, ROW_TILE), 0)
    tc = lax.broadcasted_iota(I32, (ROW_TILE, ROW_TILE), 1)
    upper = jnp.where(tr <= tc, 1.0, 0.0).astype(BF16)
    pref = jnp.dot(oh.astype(BF16), upper, preferred_element_type=F32)
    offs = base_ref[:, 0:1]
    ranks = []
    for kk in range(TOP_K_EXPERTS):
        pk = pref[kk * N_EXPERTS:(kk + 1) * N_EXPERTS, :]
        r = jnp.sum(jnp.where(onehot[kk], offs + pk - 1.0, 0.0), axis=0, keepdims=True)
        ranks.append(r.astype(I32))
        offs = offs + pk[:, ROW_TILE - 1:ROW_TILE]
    base_ref[...] = jnp.broadcast_to(offs, base_ref.shape)
    cnt_ref[...] = jnp.broadcast_to(offs, cnt_ref.shape)

    zi = jnp.zeros((8 - TOP_K_EXPERTS, ROW_TILE), I32)
    eid_ref[...] = jnp.concatenate(ids + [zi], axis=0)
    rank_ref[...] = jnp.concatenate(ranks + [zi], axis=0)
    g8 = jnp.concatenate(gates + [jnp.zeros((128 - TOP_K_EXPERTS, ROW_TILE), F32)], axis=0)
    gcol_ref[...] = g8.T


def _mix(x2d, attn, y, ymeta, gate, wts, seq):
    n = x2d.shape[0]
    tiles_per_seq = seq // ROW_TILE
    halo_per_tile = ROW_TILE // HALO
    full = lambda a: pl.BlockSpec(a.shape, lambda i: (0,) * a.ndim)
    row = lambda w: pl.BlockSpec((ROW_TILE, w), lambda i: (i, 0))
    lane = lambda r: pl.BlockSpec((r, ROW_TILE), lambda i: (0, i))
    out_shape = (
        jax.ShapeDtypeStruct((n, D_MODEL), F32),
        jax.ShapeDtypeStruct((n, D_MODEL // 2), I32),
        jax.ShapeDtypeStruct((8, n), I32),
        jax.ShapeDtypeStruct((8, n), I32),
        jax.ShapeDtypeStruct((n, 128), F32),
        jax.ShapeDtypeStruct((N_EXPERTS, 128), F32),
    )
    out_specs = (row(D_MODEL), row(D_MODEL // 2), lane(8), lane(8), row(128), full(out_shape[5]))
    return pl.pallas_call(
        functools.partial(_mix_kernel, tiles_per_seq=tiles_per_seq),
        grid=(n // ROW_TILE,),
        in_specs=[row(D_MODEL), row(ATTN_WIDTH), row(CONV_CH),
                  pl.BlockSpec((HALO, CONV_CH), lambda i: (jnp.maximum(i * halo_per_tile - 1, 0), 0)),
                  full(ymeta), row(2 * D_MODEL)] + [full(w) for w in wts],
        out_specs=out_specs,
        out_shape=out_shape,
        scratch_shapes=[pltpu.VMEM((HALO + ROW_TILE, CONV_CH), F32),
                        pltpu.VMEM((N_EXPERTS, 128), F32)],
        compiler_params=pltpu.CompilerParams(dimension_semantics=("arbitrary",),
                                             vmem_limit_bytes=VMEM_LIMIT),
        name="mix",
    )(x2d, attn, y, y, ymeta, gate, *wts)


def _slots_kernel(starts_ref, eid_ref, rank_ref, slot_ref):
    eid = eid_ref[...]
    base = jnp.zeros(eid.shape, I32)
    for e in range(N_EXPERTS):
        base = jnp.where(eid == e, starts_ref[e], base)
    slot_ref[...] = base + rank_ref[...]


def _slots(starts, eid8, rank8):
    n = eid8.shape[1]
    cols = 2048
    spec = pl.BlockSpec((8, cols), lambda i: (0, i))
    return pl.pallas_call(
        _slots_kernel,
        grid=(n // cols,),
        in_specs=[pl.BlockSpec(memory_space=pltpu.SMEM), spec, spec],
        out_specs=spec,
        out_shape=jax.ShapeDtypeStruct((8, n), I32),
        compiler_params=pltpu.CompilerParams(dimension_semantics=("arbitrary",)),
        name="slots",
    )(starts, eid8, rank8)


COPIES_PER_TILE = TOP_K_EXPERTS * ROW_TILE
ISSUE_UNROLL = 8


def _row_copy_loop(make_copy):
    def body(g, carry):
        for u in range(ISSUE_UNROLL):
            make_copy(g * ISSUE_UNROLL + u).start()
        return carry
    lax.fori_loop(0, COPIES_PER_TILE // ISSUE_UNROLL, body, 0)


def _dispatch_kernel(slot_ref, hn_ref, xg_in_ref, xg_ref, sem):
    del xg_in_ref

    def copy(idx):
        t = idx & (ROW_TILE - 1)
        s = slot_ref[0, 0, idx]
        return pltpu.make_async_copy(hn_ref.at[pl.ds(t, 1), :], xg_ref.at[pl.ds(s, 1), :], sem)

    _row_copy_loop(copy)
    for _ in range(TOP_K_EXPERTS):
        pltpu.make_async_copy(hn_ref, xg_ref.at[pl.ds(0, ROW_TILE), :], sem).wait()


def _dispatch(slot_tiles, hn2p, xg_init):
    n = hn2p.shape[0]
    return pl.pallas_call(
        _dispatch_kernel,
        grid=(n // ROW_TILE,),
        in_specs=[pl.BlockSpec((1, 1, COPIES_PER_TILE), lambda i: (i, 0, 0), memory_space=pltpu.SMEM),
                  pl.BlockSpec((ROW_TILE, D_MODEL // 2), lambda i: (i, 0)),
                  pl.BlockSpec(memory_space=pl.ANY)],
        out_specs=pl.BlockSpec(memory_space=pl.ANY),
        out_shape=jax.ShapeDtypeStruct(xg_init.shape, xg_init.dtype),
        scratch_shapes=[pltpu.SemaphoreType.DMA(())],
        input_output_aliases={2: 0},
        compiler_params=pltpu.CompilerParams(dimension_semantics=("arbitrary",)),
        name="dispatch",
    )(slot_tiles, hn2p, xg_init)


def _expert_kernel(te_ref, nused_ref, xg_ref, wug_ref, bug_ref, wd_ref, bd_ref, y_ref, wug_bf, wd_bf):
    i = pl.program_id(0)
    used = i < nused_ref[0]
    prev = te_ref[jnp.maximum(i - 1, 0)]
    fresh = (i == 0) | (te_ref[i] != prev)

    @pl.when(used & fresh)
    def _():
        wug_bf[...] = wug_ref[0].astype(BF16)
        wd_bf[...] = wd_ref[0].astype(BF16)

    @pl.when(used)
    def _():
        xb = _unpack_bf16_pairs(xg_ref[...])
        ug = jnp.dot(xb, wug_bf[...], preferred_element_type=F32) + bug_ref[0]
        gate = jnp.minimum(ug[:, :D_FF], SWIGLU_LIMIT)
        up = jnp.clip(ug[:, D_FF:], -SWIGLU_LIMIT, SWIGLU_LIMIT)
        act = (up + 1.0) * gate * _sigmoid(SWIGLU_ALPHA * gate)
        y_ref[...] = jnp.dot(act.astype(BF16), wd_bf[...], preferred_element_type=F32) + bd_ref[0]

    @pl.when(jnp.logical_not(used))
    def _():
        y_ref[...] = jnp.zeros_like(y_ref)


def _experts(tile_expert, nused, xg, w_ug, b_ug, w_down, b_down):
    s = xg.shape[0]
    ntiles = s // ROW_TILE
    clamp = lambda i, nu: jnp.minimum(i, nu[0] - 1)
    grid_spec = pltpu.PrefetchScalarGridSpec(
        num_scalar_prefetch=2,
        grid=(ntiles,),
        in_specs=[
            pl.BlockSpec((ROW_TILE, D_MODEL // 2), lambda i, te, nu: (clamp(i, nu), 0)),
            pl.BlockSpec((1, D_MODEL, 2 * D_FF), lambda i, te, nu: (te[i], 0, 0)),
            pl.BlockSpec((1, 1, 2 * D_FF), lambda i, te, nu: (te[i], 0, 0)),
            pl.BlockSpec((1, D_FF, D_MODEL), lambda i, te, nu: (te[i], 0, 0)),
            pl.BlockSpec((1, 1, D_MODEL), lambda i, te, nu: (te[i], 0, 0)),
        ],
        out_specs=pl.BlockSpec((ROW_TILE, D_MODEL), lambda i, te, nu: (i, 0)),
        scratch_shapes=[pltpu.VMEM((D_MODEL, 2 * D_FF), BF16), pltpu.VMEM((D_FF, D_MODEL), BF16)],
    )
    return pl.pallas_call(
        _expert_kernel,
        grid_spec=grid_spec,
        out_shape=jax.ShapeDtypeStruct((s, D_MODEL), F32),
        compiler_params=pltpu.CompilerParams(dimension_semantics=("arbitrary",),
                                             vmem_limit_bytes=VMEM_LIMIT),
        name="experts",
    )(tile_expert, nused, xg, w_ug, b_ug[:, None, :], w_down, b_down[:, None, :])


def _final_kernel(slot_ref, slot_next_ref, h2_ref, gcol_ref, nf_ref, y_hbm, o_ref, buf, sem):
    i = pl.program_id(0)
    last = pl.num_programs(0) - 1

    def gather(slots, b):
        def copy(idx):
            kk = lax.shift_right_logical(idx, ROW_TILE.bit_length() - 1)
            t = idx & (ROW_TILE - 1)
            s = slots[0, 0, idx]
            return pltpu.make_async_copy(y_hbm.at[pl.ds(s, 1), :], buf.at[b, kk, pl.ds(t, 1), :], sem.at[b])
        _row_copy_loop(copy)

    @pl.when(i == 0)
    def _():
        gather(slot_ref, 0)

    @pl.when(i < last)
    def _():
        gather(slot_next_ref, (i + 1) % 2)

    b = i % 2
    for kk in range(TOP_K_EXPERTS):
        pltpu.make_async_copy(y_hbm.at[pl.ds(0, ROW_TILE), :], buf.at[b, kk], sem.at[b]).wait()
    h = h2_ref[...]
    for kk in range(TOP_K_EXPERTS):
        h = h + gcol_ref[:, kk:kk + 1] * buf[b, kk]
    ms = jnp.mean(h * h, axis=-1, keepdims=True)
    o_ref[...] = h * lax.rsqrt(ms + EPS) * nf_ref[...]


def _final(slot_tiles, h2, gcol, nf, yslot):
    n = h2.shape[0]
    ntiles = n // ROW_TILE
    smem_tile = lambda f: pl.BlockSpec((1, 1, COPIES_PER_TILE), f, memory_space=pltpu.SMEM)
    return pl.pallas_call(
        _final_kernel,
        grid=(ntiles,),
        in_specs=[smem_tile(lambda i: (i, 0, 0)),
                  smem_tile(lambda i: (jnp.minimum(i + 1, ntiles - 1), 0, 0)),
                  pl.BlockSpec((ROW_TILE, D_MODEL), lambda i: (i, 0)),
                  pl.BlockSpec((ROW_TILE, 128), lambda i: (i, 0)),
                  pl.BlockSpec((1, D_MODEL), lambda i: (0, 0)),
                  pl.BlockSpec(memory_space=pl.ANY)],
        out_specs=pl.BlockSpec((ROW_TILE, D_MODEL), lambda i: (i, 0)),
        out_shape=jax.ShapeDtypeStruct((n, D_MODEL), F32),
        scratch_shapes=[pltpu.VMEM((2, TOP_K_EXPERTS, ROW_TILE, D_MODEL), F32),
                        pltpu.SemaphoreType.DMA((2,))],
        compiler_params=pltpu.CompilerParams(dimension_semantics=("arbitrary",),
                                             vmem_limit_bytes=VMEM_LIMIT),
        name="final",
    )(slot_tiles, slot_tiles, h2, gcol, nf, yslot)


def _split_w_in(w_in, b_gate):
    c = ATTN_WIDTH
    o = 0
    wq = w_in[:, o:o + c]; o += c
    wk = w_in[:, o:o + c]; o += c
    wv = w_in[:, o:o + c]; o += c
    wiq = w_in[:, o:o + IDX_HEADS * IDX_DIM]; o += IDX_HEADS * IDX_DIM
    wik = w_in[:, o:o + IDX_DIM]; o += IDX_DIM
    wiw = w_in[:, o:o + IDX_HEADS]; o += IDX_HEADS
    wglu = w_in[:, o:o + 2 * CONV_CH]; o += 2 * CONV_CH
    wgate = w_in[:, o:]
    wiwt = jnp.concatenate([wiw.T, jnp.zeros((16 - IDX_HEADS, D_MODEL), w_in.dtype)], axis=0)
    bf = lambda a: a.astype(BF16)
    return (bf(wq), bf(wk), bf(wiq), bf(wik), bf(wv.T), bf(wiwt), bf(wglu), bf(wgate),
            b_gate[None, :].astype(F32))


def kernel(x, meta_tokens, rel_bias, norm_mix, w_in, b_gate, w_attn_out, conv_w, conv_b, conv_ln_g, conv_ln_b, w_conv_out, w_out, norm_ffn, w_router, b_router, w_up_gate, b_up_gate, w_down, b_down, norm_final):
    batch, seq, d = x.shape
    n = batch * seq
    x2d = x.reshape(n, d)

    wts = _split_w_in(w_in[0], b_gate[0])
    g_mix = norm_mix[0][None, :]
    q3, k3, iq3, ik, vt, iwt, y, gate = _project(x2d, g_mix, wts, ROW_TILE)
    _, km3, _, ikm, vtm, _, ym, _ = _project(meta_tokens.astype(F32), g_mix, wts, N_META)

    tab, tabm = _bias_tables(rel_bias.astype(F32))
    attn = _attention(q3, iq3, iwt, k3, ik, vt, km3, ikm, vtm, tab, tabm, batch, seq)

    ymeta = jnp.concatenate([jnp.zeros((HALO - N_META, CONV_CH), F32), ym], axis=0)
    cw = jnp.concatenate([conv_w[0], jnp.zeros((32 - CONV_WIDTH, CONV_CH), F32)], axis=0)
    mix_w = (cw, conv_b[0][None, :], conv_ln_g[0][None, :], conv_ln_b[0][None, :],
             w_conv_out[0].astype(BF16), w_attn_out[0].astype(BF16), w_out[0].astype(BF16),
             norm_ffn[0][None, :], w_router[0].T, b_router[0][:, None])
    h2, hn2, eid8, rank8, gcol, cnt = _mix(x2d, attn, y, ymeta, gate, mix_w, seq)

    counts = cnt[:, 0].astype(I32)
    padded = ((counts + ROW_TILE - 1) // ROW_TILE) * ROW_TILE
    ends = jnp.cumsum(padded)
    starts = ends - padded
    slot8 = _slots(starts, eid8, rank8)
    slot_tiles = (slot8[:TOP_K_EXPERTS].reshape(TOP_K_EXPERTS, n // ROW_TILE, ROW_TILE)
                  .transpose(1, 0, 2).reshape(n // ROW_TILE, 1, COPIES_PER_TILE))
    nslots = n * TOP_K_EXPERTS + N_EXPERTS * ROW_TILE
    ntiles = nslots // ROW_TILE
    tile_start = jnp.arange(ntiles, dtype=I32) * ROW_TILE
    nused = (ends[-1] // ROW_TILE).astype(I32)
    last_start = jnp.maximum(ends[-1] - ROW_TILE, 0)
    tile_expert = jnp.sum((jnp.minimum(tile_start, last_start)[:, None] >= ends[None, :]).astype(I32), axis=1)

    xg = _dispatch(slot_tiles, hn2, jnp.zeros((nslots, d // 2), I32))
    yslot = _experts(tile_expert, nused[None], xg, w_up_gate[0], b_up_gate[0], w_down[0], b_down[0])
    out = _final(slot_tiles, h2, gcol, norm_final[None, :], yslot)
    return out.reshape(batch, seq, d)
```

```python
import functools
import math

import jax
import jax.numpy as jnp
from jax import lax
from jax.experimental import pallas as pl
from jax.experimental.pallas import tpu as pltpu

F32 = jnp.float32
BF16 = jnp.bfloat16
I32 = jnp.int32

D_MODEL = 1024
N_META = 16
N_HEADS = 8
HEAD_DIM = 64
ATTN_WIDTH = N_HEADS * HEAD_DIM
IDX_HEADS = 8
IDX_DIM = 64
TOPK_MAX = 256
CONV_CH = 512
CONV_WIDTH = 31
N_BUCKETS = 32
MAX_DISTANCE = 128
N_EXPERTS = 32
TOP_K_EXPERTS = 4
D_FF = 1024
SWIGLU_LIMIT = 7.0
SWIGLU_ALPHA = 1.702
EPS = 1e-6
IDX_SCALE = (IDX_DIM ** -0.5) * (IDX_HEADS ** -0.5)

ROW_TILE = 256
Q_TILE = 256
K_CHUNK = 256
HALO = 32
NEG = -1e30
VMEM_LIMIT = 56 * 1024 * 1024

NT_DIMS = (((1,), (1,)), ((), ()))


def _sigmoid(x):
    return 1.0 / (1.0 + jnp.exp(-x))


def _pack_bf16_pairs(x):
    w = x.shape[1] // 2
    hi = lax.bitcast_convert_type(x[:, :w].astype(BF16).astype(F32), jnp.uint32)
    lo = lax.bitcast_convert_type(x[:, w:].astype(BF16).astype(F32), jnp.uint32)
    return lax.bitcast_convert_type(hi | (lo >> 16), I32)


def _unpack_bf16_pairs(p):
    u = lax.bitcast_convert_type(p, jnp.uint32)
    hi = lax.bitcast_convert_type(u & jnp.uint32(0xFFFF0000), F32)
    lo = lax.bitcast_convert_type(u << 16, F32)
    return jnp.concatenate([hi, lo], axis=1).astype(BF16)


def _proj_kernel(x_ref, g_ref, wq_ref, wk_ref, wiq_ref, wik_ref, wvt_ref, wiwt_ref, wglu_ref,
                 wgate_ref, bgate_ref,
                 q_ref, k_ref, iq_ref, ik_ref, vt_ref, iwt_ref, y_ref, gate_ref):
    x = x_ref[...]
    ms = jnp.mean(x * x, axis=-1, keepdims=True)
    xn = (x * lax.rsqrt(ms + EPS) * g_ref[...]).astype(BF16)

    q = jnp.dot(xn, wq_ref[...], preferred_element_type=F32) * (HEAD_DIM ** -0.5)
    k = jnp.dot(xn, wk_ref[...], preferred_element_type=F32)
    iq = jnp.dot(xn, wiq_ref[...], preferred_element_type=F32)
    for h in range(N_HEADS):
        sl = slice(h * HEAD_DIM, (h + 1) * HEAD_DIM)
        q_ref[h] = q[:, sl].astype(BF16)
        k_ref[h] = k[:, sl].astype(BF16)
        iq_ref[h] = iq[:, sl].astype(BF16)
    ik_ref[...] = jnp.dot(xn, wik_ref[...], preferred_element_type=F32).astype(BF16)
    vt_ref[...] = lax.dot_general(wvt_ref[...], xn, NT_DIMS, preferred_element_type=F32).astype(BF16)
    iwt = lax.dot_general(wiwt_ref[...], xn, NT_DIMS, preferred_element_type=F32)
    iwt_ref[...] = iwt[:IDX_HEADS] * IDX_SCALE
    glu = jnp.dot(xn, wglu_ref[...], preferred_element_type=F32)
    y_ref[...] = glu[:, :CONV_CH] * _sigmoid(glu[:, CONV_CH:])
    gate = jnp.dot(xn, wgate_ref[...], preferred_element_type=F32) + bgate_ref[...]
    gate_ref[...] = _sigmoid(gate)


def _project(x2d, g, wts, tm):
    n = x2d.shape[0]
    wq, wk, wiq, wik, wvt, wiwt, wglu, wgate, bgate = wts
    full = lambda a: pl.BlockSpec(a.shape, lambda i: (0,) * a.ndim)
    out_shape = (
        jax.ShapeDtypeStruct((N_HEADS, n, HEAD_DIM), BF16),
        jax.ShapeDtypeStruct((N_HEADS, n, HEAD_DIM), BF16),
        jax.ShapeDtypeStruct((IDX_HEADS, n, IDX_DIM), BF16),
        jax.ShapeDtypeStruct((n, IDX_DIM), BF16),
        jax.ShapeDtypeStruct((ATTN_WIDTH, n), BF16),
        jax.ShapeDtypeStruct((IDX_HEADS, n), F32),
        jax.ShapeDtypeStruct((n, CONV_CH), F32),
        jax.ShapeDtypeStruct((n, 2 * D_MODEL), F32),
    )
    out_specs = (
        pl.BlockSpec((N_HEADS, tm, HEAD_DIM), lambda i: (0, i, 0)),
        pl.BlockSpec((N_HEADS, tm, HEAD_DIM), lambda i: (0, i, 0)),
        pl.BlockSpec((IDX_HEADS, tm, IDX_DIM), lambda i: (0, i, 0)),
        pl.BlockSpec((tm, IDX_DIM), lambda i: (i, 0)),
        pl.BlockSpec((ATTN_WIDTH, tm), lambda i: (0, i)),
        pl.BlockSpec((IDX_HEADS, tm), lambda i: (0, i)),
        pl.BlockSpec((tm, CONV_CH), lambda i: (i, 0)),
        pl.BlockSpec((tm, 2 * D_MODEL), lambda i: (i, 0)),
    )
    return pl.pallas_call(
        _proj_kernel,
        grid=(n // tm,),
        in_specs=[pl.BlockSpec((tm, D_MODEL), lambda i: (i, 0)), full(g), full(wq), full(wk), full(wiq),
                  full(wik), full(wvt), full(wiwt), full(wglu), full(wgate), full(bgate)],
        out_specs=out_specs,
        out_shape=out_shape,
        compiler_params=pltpu.CompilerParams(dimension_semantics=("arbitrary",),
                                             vmem_limit_bytes=VMEM_LIMIT),
        name="proj",
    )(x2d, g, wq, wk, wiq, wik, wvt, wiwt, wglu, wgate, bgate)


def _t5_bucket(n):
    max_exact = N_BUCKETS // 2
    nf = jnp.maximum(n, 1).astype(F32)
    large = max_exact + (jnp.log(nf / max_exact) / math.log(MAX_DISTANCE / max_exact)
                         * (N_BUCKETS - max_exact)).astype(I32)
    large = jnp.minimum(large, N_BUCKETS - 1)
    return jnp.where(n < max_exact, n, large)


def _bias_lookup(rb_ref, dist, h):
    bucket = _t5_bucket(jnp.maximum(dist, 0))
    out = jnp.full(dist.shape, NEG, F32)
    for b in range(N_BUCKETS):
        out = jnp.where(bucket == b, rb_ref[b, h], out)
    return jnp.where(dist >= 0, out, NEG)


def _bias_kernel(rb_ref, tab_ref, tabm_ref):
    kind = pl.program_id(0)
    r = pl.program_id(1)
    rows = tab_ref.shape[2]
    s = lax.broadcasted_iota(I32, (rows, Q_TILE), 0) + r * rows
    t = lax.broadcasted_iota(I32, (rows, Q_TILE), 1)
    dist = jnp.where(kind == 2, 2 * K_CHUNK, t - s + kind * K_CHUNK)
    for h in range(N_HEADS):
        tab_ref[0, h] = _bias_lookup(rb_ref, dist, h)
    m = lax.broadcasted_iota(I32, (N_META, Q_TILE), 0)
    tm_ = lax.broadcasted_iota(I32, (N_META, Q_TILE), 1)
    distm = jnp.where(kind == 0, N_META + tm_ - m, 2 * K_CHUNK)
    for h in range(N_HEADS):
        tabm_ref[0, h] = _bias_lookup(rb_ref, distm, h)


def _bias_tables(rel_bias):
    rows = 64
    return pl.pallas_call(
        _bias_kernel,
        grid=(3, K_CHUNK // rows),
        in_specs=[pl.BlockSpec(memory_space=pltpu.SMEM)],
        out_specs=(pl.BlockSpec((1, N_HEADS, rows, Q_TILE), lambda kd, r: (kd, 0, r, 0)),
                   pl.BlockSpec((1, N_HEADS, N_META, Q_TILE), lambda kd, r: (kd, 0, 0, 0))),
        out_shape=(jax.ShapeDtypeStruct((3, N_HEADS, K_CHUNK, Q_TILE), F32),
                   jax.ShapeDtypeStruct((3, N_HEADS, N_META, Q_TILE), F32)),
        compiler_params=pltpu.CompilerParams(dimension_semantics=("arbitrary", "arbitrary")),
        name="bias_tables",
    )(rel_bias)


def _order_bits_to_float(u):
    bits = jnp.where(u < 0, u ^ jnp.int32(-2 ** 31), ~u)
    return lax.bitcast_convert_type(bits, F32)


def _fold_rows(x, op):
    r, l = x.shape
    x3 = x.reshape(r // 8, 8, l)
    return op(x3, axis=0)


def _attn_kernel(q_ref, iq_ref, iwt_ref, k_ref, ik_ref, vt_ref, km_ref, ikm_ref, vtm_ref,
                 tab_ref, tabm_ref, o_ref, sc_ref, scm_ref, l_ref, lm_ref, ot_ref, cst_ref):
    j = pl.program_id(1)
    nchunks = j + 1
    iw = iwt_ref[...]

    def chunk_rows(c):
        return pl.ds(pl.multiple_of(c * K_CHUNK, K_CHUNK), K_CHUNK)

    def idx_scores(ikc):
        acc = None
        for h in range(IDX_HEADS):
            s = lax.dot_general(ikc, iq_ref[h], NT_DIMS, preferred_element_type=F32)
            term = jnp.maximum(s, 0.0) * iw[h:h + 1, :]
            acc = term if acc is None else acc + term
        return acc

    scm_ref[...] = idx_scores(ikm_ref[...])

    row_minus_col = (lax.broadcasted_iota(I32, (K_CHUNK, Q_TILE), 0)
                     - lax.broadcasted_iota(I32, (K_CHUNK, Q_TILE), 1))

    def p1(c, carry):
        rows = chunk_rows(c)
        future = row_minus_col > jnp.where(c == j, 0, K_CHUNK)
        sc_ref[rows, :] = jnp.where(future, -jnp.inf, idx_scores(ik_ref[rows, :]))
        return carry

    lax.fori_loop(0, nchunks, p1, 0)

    def count(pred):
        def body(c, acc):
            blk = sc_ref[chunk_rows(c), :]
            return acc + _fold_rows(jnp.where(pred(blk, c), 1, 0).astype(I32), jnp.sum)
        acc = lax.fori_loop(0, nchunks, body, jnp.zeros((8, Q_TILE), I32))
        acc = acc + _fold_rows(jnp.where(pred(scm_ref[...], -1), 1, 0).astype(I32), jnp.sum)
        return jnp.sum(acc, axis=0, keepdims=True)

    def bisect(i, u):
        cand = u | lax.shift_left(jnp.int32(1), 31 - i)
        cf = _order_bits_to_float(cand)
        cnt = count(lambda blk, c: blk >= cf)
        return jnp.where(cnt >= TOPK_MAX, cand, u)

    u = lax.fori_loop(0, 32, bisect, jnp.zeros((1, Q_TILE), I32))
    thr = jnp.where((u >= 0) & (u < 0x00800000), -jnp.inf, _order_bits_to_float(u))

    cnt_ge = count(lambda blk, c: blk >= thr)
    tie = (cnt_ge > TOPK_MAX) & (thr > -jnp.inf)
    any_tie = jnp.max(tie.astype(I32))
    cst_ref[...] = jnp.full((8, Q_TILE), 2 ** 30, I32)

    def pos_of(c, shape):
        r = lax.broadcasted_iota(I32, shape, 0)
        return jnp.where(c < 0, r, r + N_META + c * K_CHUNK)

    @pl.when(any_tie > 0)
    def _():
        cnt_gt = count(lambda blk, c: blk > thr)
        need = TOPK_MAX - cnt_gt

        def bis_pos(i, cs):
            cand = cs | lax.shift_left(jnp.int32(1), 11 - i)
            f = count(lambda blk, c: (blk == thr) & (pos_of(c, blk.shape) < cand))
            return jnp.where(f <= need, cand, cs)

        cs = lax.fori_loop(0, 12, bis_pos, jnp.zeros((1, Q_TILE), I32))
        cs = jnp.where(tie, cs, 2 ** 30)
        cst_ref[...] = jnp.broadcast_to(cs, (8, Q_TILE))

    def to_mask(blk, c, with_ties):
        if with_ties:
            cs = cst_ref[0:1, :]
            sel = (blk > thr) | ((blk == thr) & (pos_of(c, blk.shape) < cs))
        else:
            sel = blk >= thr
        return jnp.where(sel, 0.0, NEG)

    for with_ties in (False, True):
        @pl.when((any_tie > 0) == with_ties)
        def _():
            def body(c, carry):
                rows = chunk_rows(c)
                sc_ref[rows, :] = to_mask(sc_ref[rows, :], c, with_ties)
                return carry
            lax.fori_loop(0, nchunks, body, 0)
            scm_ref[...] = to_mask(scm_ref[...], -1, with_ties)

    kindm = jnp.minimum(j, 1)
    heads = [slice(h * HEAD_DIM, (h + 1) * HEAD_DIM) for h in range(N_HEADS)]

    mx0 = []
    for h in range(N_HEADS):
        lm = (lax.dot_general(km_ref[h], q_ref[h], NT_DIMS, preferred_element_type=F32)
              + tabm_ref[kindm, h] + scm_ref[...])
        lm_ref[h] = lm
        mx0.append(_fold_rows(lm, jnp.max))

    def pass_a(c, mx):
        rows = chunk_rows(c)
        kind = jnp.minimum(j - c, 2)
        mask = sc_ref[rows, :]
        out = []
        for h in range(N_HEADS):
            l = (lax.dot_general(k_ref[h, rows, :], q_ref[h], NT_DIMS, preferred_element_type=F32)
                 + tab_ref[kind, h] + mask)
            l_ref[h, rows, :] = l
            out.append(jnp.maximum(mx[h], _fold_rows(l, jnp.max)))
        return tuple(out)

    mx = lax.fori_loop(0, nchunks, pass_a, tuple(mx0))
    m = [jnp.max(mx[h], axis=0, keepdims=True) for h in range(N_HEADS)]

    den0 = []
    for h in range(N_HEADS):
        pm = jnp.exp(lm_ref[h] - m[h])
        den0.append(_fold_rows(pm, jnp.sum))
        ot_ref[heads[h], :] = jnp.dot(vtm_ref[heads[h], :], pm.astype(BF16), preferred_element_type=F32)

    def pass_b(c, den):
        rows = chunk_rows(c)
        out = []
        for h in range(N_HEADS):
            p = jnp.exp(l_ref[h, rows, :] - m[h])
            out.append(den[h] + _fold_rows(p, jnp.sum))
            ot_ref[heads[h], :] += jnp.dot(vt_ref[heads[h], rows], p.astype(BF16),
                                           preferred_element_type=F32)
        return tuple(out)

    den = lax.fori_loop(0, nchunks, pass_b, tuple(den0))
    for h in range(N_HEADS):
        ot_ref[heads[h], :] = ot_ref[heads[h], :] / jnp.sum(den[h], axis=0, keepdims=True)
    o_ref[...] = ot_ref[...].T.astype(BF16)


def _attention(q3, iq3, iwt, k3, ik, vt, km3, ikm, vtm, tab, tabm, batch, seq):
    n = batch * seq
    tiles = seq // Q_TILE
    full = lambda a: pl.BlockSpec(a.shape, lambda b, j: (0,) * a.ndim, pipeline_mode=pl.Buffered(1))
    return pl.pallas_call(
        _attn_kernel,
        grid=(batch, tiles),
        in_specs=[
            pl.BlockSpec((N_HEADS, Q_TILE, HEAD_DIM), lambda b, j: (0, b * tiles + j, 0)),
            pl.BlockSpec((IDX_HEADS, Q_TILE, IDX_DIM), lambda b, j: (0, b * tiles + j, 0)),
            pl.BlockSpec((IDX_HEADS, Q_TILE), lambda b, j: (0, b * tiles + j)),
            pl.BlockSpec((N_HEADS, seq, HEAD_DIM), lambda b, j: (0, b, 0)),
            pl.BlockSpec((seq, IDX_DIM), lambda b, j: (b, 0)),
            pl.BlockSpec((ATTN_WIDTH, seq), lambda b, j: (0, b)),
            full(km3), full(ikm), full(vtm), full(tab), full(tabm),
        ],
        out_specs=pl.BlockSpec((Q_TILE, ATTN_WIDTH), lambda b, j: (b * tiles + j, 0)),
        out_shape=jax.ShapeDtypeStruct((n, ATTN_WIDTH), BF16),
        scratch_shapes=[
            pltpu.VMEM((seq, Q_TILE), F32),
            pltpu.VMEM((N_META, Q_TILE), F32),
            pltpu.VMEM((N_HEADS, seq, Q_TILE), F32),
            pltpu.VMEM((N_HEADS, N_META, Q_TILE), F32),
            pltpu.VMEM((ATTN_WIDTH, Q_TILE), F32),
            pltpu.VMEM((8, Q_TILE), I32),
        ],
        compiler_params=pltpu.CompilerParams(dimension_semantics=("arbitrary", "arbitrary"),
                                             vmem_limit_bytes=VMEM_LIMIT),
        name="attn",
    )(q3, iq3, iwt, k3, ik, vt, km3, ikm, vtm, tab, tabm)


def _mix_kernel(x_ref, attn_ref, y_ref, yprev_ref, ymeta_ref, gate_ref,
                cw_ref, cb_ref, lng_ref, lnb_ref, wco_ref, wao_ref, wout_ref, nf_ref, wrt_ref, br_ref,
                h2_ref, hn2_ref, eid_ref, rank_ref, gcol_ref, cnt_ref,
                win_ref, base_ref, *, tiles_per_seq):
    i = pl.program_id(0)

    @pl.when(i == 0)
    def _():
        base_ref[...] = jnp.zeros_like(base_ref)

    first = (i % tiles_per_seq) == 0
    win_ref[0:HALO, :] = jnp.where(first, ymeta_ref[...], yprev_ref[...])
    win_ref[HALO:, :] = y_ref[...]
    acc = jnp.zeros((ROW_TILE, CONV_CH), F32)
    lead = HALO - (CONV_WIDTH - 1)
    for b in range(8):
        taps = [w for w in range(b, CONV_WIDTH, 8)]
        span = ROW_TILE + 8 * (len(taps) - 1)
        yb = win_ref[pl.ds(lead + b, span), :]
        for a, w in enumerate(taps):
            acc = acc + cw_ref[w:w + 1, :] * yb[8 * a:8 * a + ROW_TILE, :]
    yc = acc + cb_ref[...]
    mu = jnp.mean(yc, axis=-1, keepdims=True)
    var = jnp.mean(jnp.square(yc - mu), axis=-1, keepdims=True)
    yn = (yc - mu) * lax.rsqrt(var + EPS) * lng_ref[...] + lnb_ref[...]
    ys = yn * _sigmoid(yn)
    y_b = jnp.dot(ys.astype(BF16), wco_ref[...], preferred_element_type=F32)

    y_a = jnp.dot(attn_ref[...], wao_ref[...], preferred_element_type=F32)
    merged = gate_ref[:, :D_MODEL] * y_a + gate_ref[:, D_MODEL:] * y_b
    h2 = x_ref[...] + jnp.dot(merged.astype(BF16), wout_ref[...], preferred_element_type=F32)
    h2_ref[...] = h2
    ms = jnp.mean(h2 * h2, axis=-1, keepdims=True)
    hn2 = h2 * lax.rsqrt(ms + EPS) * nf_ref[...]
    hn2_ref[...] = _pack_bf16_pairs(hn2)

    logits = lax.dot_general(wrt_ref[...], hn2, NT_DIMS, preferred_element_type=F32,
                             precision=lax.Precision.HIGHEST) + br_ref[...]
    erow = lax.broadcasted_iota(I32, (N_EXPERTS, ROW_TILE), 0)
    vals, ids = [], []
    l = logits
    for _ in range(TOP_K_EXPERTS):
        m = jnp.max(l, axis=0, keepdims=True)
        idx = jnp.min(jnp.where(l == m, erow, N_EXPERTS), axis=0, keepdims=True)
        vals.append(m)
        ids.append(idx)
        l = jnp.where(erow == idx, -jnp.inf, l)
    ex = [jnp.exp(v - vals[0]) for v in vals]
    den = ex[0] + ex[1] + ex[2] + ex[3]
    gates = [e / den for e in ex]

    onehot = [(erow == idx) for idx in ids]
    oh = jnp.concatenate([jnp.where(o, 1.0, 0.0) for o in onehot], axis=0)
    tr = lax.broadcasted_iota(I32, (ROW_TILE, ROW_TILE), 0)
    tc = lax.broadcasted_iota(I32, (ROW_TILE, ROW_TILE), 1)
    upper = jnp.where(tr <= tc, 1.0, 0.0).astype(BF16)
    pref = jnp.dot(oh.astype(BF16), upper, preferred_element_type=F32)
    offs = base_ref[:, 0:1]
    ranks = []
    for kk in range(TOP_K_EXPERTS):
        pk = pref[kk * N_EXPERTS:(kk + 1) * N_EXPERTS, :]
        r = jnp.sum(jnp.where(onehot[kk], offs + pk - 1.0, 0.0), axis=0, keepdims=True)
        ranks.append(r.astype(I32))
        offs = offs + pk[:, ROW_TILE - 1:ROW_TILE]
    base_ref[...] = jnp.broadcast_to(offs, base_ref.shape)
    cnt_ref[...] = jnp.broadcast_to(offs, cnt_ref.shape)

    zi = jnp.zeros((8 - TOP_K_EXPERTS, ROW_TILE), I32)
    eid_ref[...] = jnp.concatenate(ids + [zi], axis=0)
    rank_ref[...] = jnp.concatenate(ranks + [zi], axis=0)
    g8 = jnp.concatenate(gates + [jnp.zeros((128 - TOP_K_EXPERTS, ROW_TILE), F32)], axis=0)
    gcol_ref[...] = g8.T


def _mix(x2d, attn, y, ymeta, gate, wts, seq):
    n = x2d.shape[0]
    tiles_per_seq = seq // ROW_TILE
    halo_per_tile = ROW_TILE // HALO
    full = lambda a: pl.BlockSpec(a.shape, lambda i: (0,) * a.ndim)
    row = lambda w: pl.BlockSpec((ROW_TILE, w), lambda i: (i, 0))
    lane = lambda r: pl.BlockSpec((r, ROW_TILE), lambda i: (0, i))
    out_shape = (
        jax.ShapeDtypeStruct((n, D_MODEL), F32),
        jax.ShapeDtypeStruct((n, D_MODEL // 2), I32),
        jax.ShapeDtypeStruct((8, n), I32),
        jax.ShapeDtypeStruct((8, n), I32),
        jax.ShapeDtypeStruct((n, 128), F32),
        jax.ShapeDtypeStruct((N_EXPERTS, 128), F32),
    )
    out_specs = (row(D_MODEL), row(D_MODEL // 2), lane(8), lane(8), row(128), full(out_shape[5]))
    return pl.pallas_call(
        functools.partial(_mix_kernel, tiles_per_seq=tiles_per_seq),
        grid=(n // ROW_TILE,),
        in_specs=[row(D_MODEL), row(ATTN_WIDTH), row(CONV_CH),
                  pl.BlockSpec((HALO, CONV_CH), lambda i: (jnp.maximum(i * halo_per_tile - 1, 0), 0)),
                  full(ymeta), row(2 * D_MODEL)] + [full(w) for w in wts],
        out_specs=out_specs,
        out_shape=out_shape,
        scratch_shapes=[pltpu.VMEM((HALO + ROW_TILE, CONV_CH), F32),
                        pltpu.VMEM((N_EXPERTS, 128), F32)],
        compiler_params=pltpu.CompilerParams(dimension_semantics=("arbitrary",),
                                             vmem_limit_bytes=VMEM_LIMIT),
        name="mix",
    )(x2d, attn, y, y, ymeta, gate, *wts)


def _slots_kernel(starts_ref, eid_ref, rank_ref, slot_ref):
    eid = eid_ref[...]
    base = jnp.zeros(eid.shape, I32)
    for e in range(N_EXPERTS):
        base = jnp.where(eid == e, starts_ref[e], base)
    slot_ref[...] = base + rank_ref[...]


def _slots(starts, eid8, rank8):
    n = eid8.shape[1]
    cols = 2048
    spec = pl.BlockSpec((8, cols), lambda i: (0, i))
    return pl.pallas_call(
        _slots_kernel,
        grid=(n // cols,),
        in_specs=[pl.BlockSpec(memory_space=pltpu.SMEM), spec, spec],
        out_specs=spec,
        out_shape=jax.ShapeDtypeStruct((8, n), I32),
        compiler_params=pltpu.CompilerParams(dimension_semantics=("arbitrary",)),
        name="slots",
    )(starts, eid8, rank8)


COPIES_PER_TILE = TOP_K_EXPERTS * ROW_TILE
SLOT_UNROLL = 16


def _slot_source_kernel(slot_ref, init_ref, src_ref, sem, *, n):
    i = pl.program_id(0)

    @pl.when(i == 0)
    def _():
        cp = pltpu.make_async_copy(init_ref, src_ref, sem)
        cp.start()
        cp.wait()

    for kk in range(TOP_K_EXPERTS):
        def body(g, carry):
            for u in range(SLOT_UNROLL):
                t = g * SLOT_UNROLL + u
                src_ref[slot_ref[0, 0, kk * ROW_TILE + t]] = kk * n + i * ROW_TILE + t
            return carry
        lax.fori_loop(0, ROW_TILE // SLOT_UNROLL, body, 0)


def _slot_sources(slot_tiles, init, n):
    return pl.pallas_call(
        functools.partial(_slot_source_kernel, n=n),
        grid=(slot_tiles.shape[0],),
        in_specs=[pl.BlockSpec((1, 1, COPIES_PER_TILE), lambda i: (i, 0, 0), memory_space=pltpu.SMEM),
                  pl.BlockSpec(memory_space=pl.ANY)],
        out_specs=pl.BlockSpec(memory_space=pltpu.SMEM),
        out_shape=jax.ShapeDtypeStruct(init.shape, I32),
        scratch_shapes=[pltpu.SemaphoreType.DMA(())],
        compiler_params=pltpu.CompilerParams(dimension_semantics=("arbitrary",)),
        name="slot_sources",
    )(slot_tiles, init)


def _expert_kernel(te_ref, nused_ref, src_ref, src_next_ref, src_prev_ref, hn_hbm,
                   wug_ref, bug_ref, wd_ref, bd_ref, yk_hbm, xbuf, ybuf, wug_bf, wd_bf, gsem, ssem, *, n):
    i = pl.program_id(0)
    nused = nused_ref[0]

    def gather_copy(src, r, b):
        tok = src[0, 0, r] & (n - 1)
        return pltpu.make_async_copy(hn_hbm.at[pl.ds(tok, 1), :], xbuf.at[b, pl.ds(r, 1), :], gsem.at[b])

    def scatter_copy(src, r, b):
        return pltpu.make_async_copy(ybuf.at[b, pl.ds(r, 1), :], yk_hbm.at[pl.ds(src[0, 0, r], 1), :],
                                     ssem.at[b])

    @pl.when(i <= nused)
    def _():
        cur = i % 2
        oth = 1 - cur

        @pl.when(i == 0)
        def _():
            for r in range(ROW_TILE):
                gather_copy(src_ref, r, 0).start()
            ybuf[1] = jnp.zeros((ROW_TILE, D_MODEL), F32)
            spare = pltpu.make_async_copy(ybuf.at[1], yk_hbm.at[pl.ds(TOP_K_EXPERTS * n, ROW_TILE), :],
                                          ssem.at[1])
            spare.start()
            spare.wait()

        pltpu.make_async_copy(hn_hbm.at[pl.ds(0, ROW_TILE), :], xbuf.at[cur], gsem.at[cur]).wait()

        prev = te_ref[jnp.maximum(i - 1, 0)]
        fresh = (i == 0) | (te_ref[jnp.minimum(i, nused - 1)] != prev)

        @pl.when(fresh)
        def _():
            wug_bf[...] = wug_ref[0].astype(BF16)
            wd_bf[...] = wd_ref[0].astype(BF16)

        for r in range(ROW_TILE):
            gather_copy(src_next_ref, r, oth).start()
            scatter_copy(src_prev_ref, r, oth).start()
        xb = _unpack_bf16_pairs(xbuf[cur])
        ug = jnp.dot(xb, wug_bf[...], preferred_element_type=F32) + bug_ref[0]
        gate = jnp.minimum(ug[:, :D_FF], SWIGLU_LIMIT)
        up = jnp.clip(ug[:, D_FF:], -SWIGLU_LIMIT, SWIGLU_LIMIT)
        act = (up + 1.0) * gate * _sigmoid(SWIGLU_ALPHA * gate)
        ybuf[cur] = jnp.dot(act.astype(BF16), wd_bf[...], preferred_element_type=F32) + bd_ref[0]
        pltpu.make_async_copy(ybuf.at[oth], yk_hbm.at[pl.ds(0, ROW_TILE), :], ssem.at[oth]).wait()

        @pl.when(i == nused)
        def _():
            pltpu.make_async_copy(hn_hbm.at[pl.ds(0, ROW_TILE), :], xbuf.at[oth], gsem.at[oth]).wait()


def _experts(tile_expert, nused, src_tiles, hn2p, w_ug, b_ug, w_down, b_down, n):
    ntiles = src_tiles.shape[0]
    clamp = lambda i, nu: jnp.maximum(jnp.minimum(i, nu[0] - 1), 0)
    smem = lambda f: pl.BlockSpec((1, 1, ROW_TILE), f, memory_space=pltpu.SMEM)
    expert = lambda i, te, nu: (te[clamp(i, nu)], 0, 0)
    grid_spec = pltpu.PrefetchScalarGridSpec(
        num_scalar_prefetch=2,
        grid=(ntiles + 1,),
        in_specs=[
            smem(lambda i, te, nu: (clamp(i, nu), 0, 0)),
            smem(lambda i, te, nu: (clamp(i + 1, nu), 0, 0)),
            smem(lambda i, te, nu: (clamp(i - 1, nu), 0, 0)),
            pl.BlockSpec(memory_space=pl.ANY),
            pl.BlockSpec((1, D_MODEL, 2 * D_FF), expert),
            pl.BlockSpec((1, 1, 2 * D_FF), expert),
            pl.BlockSpec((1, D_FF, D_MODEL), expert),
            pl.BlockSpec((1, 1, D_MODEL), expert),
        ],
        out_specs=pl.BlockSpec(memory_space=pl.ANY),
        scratch_shapes=[pltpu.VMEM((2, ROW_TILE, D_MODEL // 2), I32),
                        pltpu.VMEM((2, ROW_TILE, D_MODEL), F32),
                        pltpu.VMEM((D_MODEL, 2 * D_FF), BF16), pltpu.VMEM((D_FF, D_MODEL), BF16),
                        pltpu.SemaphoreType.DMA((2,)), pltpu.SemaphoreType.DMA((2,))],
    )
    return pl.pallas_call(
        functools.partial(_expert_kernel, n=n),
        grid_spec=grid_spec,
        out_shape=jax.ShapeDtypeStruct((TOP_K_EXPERTS * n + ROW_TILE, D_MODEL), F32),
        compiler_params=pltpu.CompilerParams(dimension_semantics=("arbitrary",),
                                             vmem_limit_bytes=VMEM_LIMIT),
        name="experts",
    )(tile_expert, nused, src_tiles, src_tiles, src_tiles, hn2p,
      w_ug, b_ug[:, None, :], w_down, b_down[:, None, :])


def _final_kernel(h2_ref, gcol_ref, nf_ref, *rest):
    y_refs, o_ref = rest[:TOP_K_EXPERTS], rest[TOP_K_EXPERTS]
    h = h2_ref[...]
    for kk in range(TOP_K_EXPERTS):
        h = h + gcol_ref[:, kk:kk + 1] * y_refs[kk][...]
    ms = jnp.mean(h * h, axis=-1, keepdims=True)
    o_ref[...] = h * lax.rsqrt(ms + EPS) * nf_ref[...]


def _final(h2, gcol, nf, yk):
    n = h2.shape[0]
    ntiles = n // ROW_TILE
    choice = lambda kk: pl.BlockSpec((ROW_TILE, D_MODEL), lambda i: (kk * ntiles + i, 0))
    return pl.pallas_call(
        _final_kernel,
        grid=(ntiles,),
        in_specs=[pl.BlockSpec((ROW_TILE, D_MODEL), lambda i: (i, 0)),
                  pl.BlockSpec((ROW_TILE, 128), lambda i: (i, 0)),
                  pl.BlockSpec((1, D_MODEL), lambda i: (0, 0))] + [choice(kk) for kk in range(TOP_K_EXPERTS)],
        out_specs=pl.BlockSpec((ROW_TILE, D_MODEL), lambda i: (i, 0)),
        out_shape=jax.ShapeDtypeStruct((n, D_MODEL), F32),
        compiler_params=pltpu.CompilerParams(dimension_semantics=("arbitrary",),
                                             vmem_limit_bytes=VMEM_LIMIT),
        name="final",
    )(h2, gcol, nf, *([yk] * TOP_K_EXPERTS))


def _split_w_in(w_in, b_gate):
    c = ATTN_WIDTH
    o = 0
    wq = w_in[:, o:o + c]; o += c
    wk = w_in[:, o:o + c]; o += c
    wv = w_in[:, o:o + c]; o += c
    wiq = w_in[:, o:o + IDX_HEADS * IDX_DIM]; o += IDX_HEADS * IDX_DIM
    wik = w_in[:, o:o + IDX_DIM]; o += IDX_DIM
    wiw = w_in[:, o:o + IDX_HEADS]; o += IDX_HEADS
    wglu = w_in[:, o:o + 2 * CONV_CH]; o += 2 * CONV_CH
    wgate = w_in[:, o:]
    wiwt = jnp.concatenate([wiw.T, jnp.zeros((16 - IDX_HEADS, D_MODEL), w_in.dtype)], axis=0)
    bf = lambda a: a.astype(BF16)
    return (bf(wq), bf(wk), bf(wiq), bf(wik), bf(wv.T), bf(wiwt), bf(wglu), bf(wgate),
            b_gate[None, :].astype(F32))


def kernel(x, meta_tokens, rel_bias, norm_mix, w_in, b_gate, w_attn_out, conv_w, conv_b, conv_ln_g, conv_ln_b, w_conv_out, w_out, norm_ffn, w_router, b_router, w_up_gate, b_up_gate, w_down, b_down, norm_final):
    batch, seq, d = x.shape
    n = batch * seq
    x2d = x.reshape(n, d)

    wts = _split_w_in(w_in[0], b_gate[0])
    g_mix = norm_mix[0][None, :]
    q3, k3, iq3, ik, vt, iwt, y, gate = _project(x2d, g_mix, wts, ROW_TILE)
    _, km3, _, ikm, vtm, _, ym, _ = _project(meta_tokens.astype(F32), g_mix, wts, N_META)

    tab, tabm = _bias_tables(rel_bias.astype(F32))
    attn = _attention(q3, iq3, iwt, k3, ik, vt, km3, ikm, vtm, tab, tabm, batch, seq)

    ymeta = jnp.concatenate([jnp.zeros((HALO - N_META, CONV_CH), F32), ym], axis=0)
    cw = jnp.concatenate([conv_w[0], jnp.zeros((32 - CONV_WIDTH, CONV_CH), F32)], axis=0)
    mix_w = (cw, conv_b[0][None, :], conv_ln_g[0][None, :], conv_ln_b[0][None, :],
             w_conv_out[0].astype(BF16), w_attn_out[0].astype(BF16), w_out[0].astype(BF16),
             norm_ffn[0][None, :], w_router[0].T, b_router[0][:, None])
    h2, hn2, eid8, rank8, gcol, cnt = _mix(x2d, attn, y, ymeta, gate, mix_w, seq)

    counts = cnt[:, 0].astype(I32)
    padded = ((counts + ROW_TILE - 1) // ROW_TILE) * ROW_TILE
    ends = jnp.cumsum(padded)
    starts = ends - padded
    slot8 = _slots(starts, eid8, rank8)
    slot_tiles = (slot8[:TOP_K_EXPERTS].reshape(TOP_K_EXPERTS, n // ROW_TILE, ROW_TILE)
                  .transpose(1, 0, 2).reshape(n // ROW_TILE, 1, COPIES_PER_TILE))
    nslots = n * TOP_K_EXPERTS + N_EXPERTS * ROW_TILE
    ntiles = nslots // ROW_TILE
    tile_start = jnp.arange(ntiles, dtype=I32) * ROW_TILE
    nused = (ends[-1] // ROW_TILE).astype(I32)
    last_start = jnp.maximum(ends[-1] - ROW_TILE, 0)
    tile_expert = jnp.sum((jnp.minimum(tile_start, last_start)[:, None] >= ends[None, :]).astype(I32), axis=1)

    spare = TOP_K_EXPERTS * n + jnp.arange(nslots, dtype=I32) % ROW_TILE
    src_tiles = _slot_sources(slot_tiles, spare, n).reshape(ntiles, 1, ROW_TILE)
    yk = _experts(tile_expert, nused[None], src_tiles, hn2, w_up_gate[0], b_up_gate[0], w_down[0], b_down[0], n)
    out = _final(h2, gcol, norm_final[None, :], yk)
    return out.reshape(batch, seq, d)
```

```python
import functools
import math

import jax
import jax.numpy as jnp
from jax import lax
from jax.experimental import pallas as pl
from jax.experimental.pallas import tpu as pltpu

F32 = jnp.float32
BF16 = jnp.bfloat16
I32 = jnp.int32

D_MODEL = 1024
N_META = 16
N_HEADS = 8
HEAD_DIM = 64
ATTN_WIDTH = N_HEADS * HEAD_DIM
IDX_HEADS = 8
IDX_DIM = 64
TOPK_MAX = 256
CONV_CH = 512
CONV_WIDTH = 31
N_BUCKETS = 32
MAX_DISTANCE = 128
N_EXPERTS = 32
TOP_K_EXPERTS = 4
D_FF = 1024
SWIGLU_LIMIT = 7.0
SWIGLU_ALPHA = 1.702
EPS = 1e-6
IDX_SCALE = (IDX_DIM ** -0.5) * (IDX_HEADS ** -0.5)

ROW_TILE = 256
Q_TILE = 256
K_CHUNK = 256
HALO = 32
NEG = -1e30
VMEM_LIMIT = 56 * 1024 * 1024

NT_DIMS = (((1,), (1,)), ((), ()))


def _sigmoid(x):
    return 1.0 / (1.0 + jnp.exp(-x))


def _pack_bf16_pairs(x):
    w = x.shape[1] // 2
    hi = lax.bitcast_convert_type(x[:, :w].astype(BF16).astype(F32), jnp.uint32)
    lo = lax.bitcast_convert_type(x[:, w:].astype(BF16).astype(F32), jnp.uint32)
    return lax.bitcast_convert_type(hi | (lo >> 16), I32)


def _unpack_bf16_pairs(p):
    u = lax.bitcast_convert_type(p, jnp.uint32)
    hi = lax.bitcast_convert_type(u & jnp.uint32(0xFFFF0000), F32)
    lo = lax.bitcast_convert_type(u << 16, F32)
    return jnp.concatenate([hi, lo], axis=1).astype(BF16)


def _proj_kernel(x_ref, g_ref, wq_ref, wk_ref, wiq_ref, wik_ref, wvt_ref, wiwt_ref, wglu_ref,
                 wgate_ref, bgate_ref,
                 q_ref, k_ref, iq_ref, ik_ref, vt_ref, iwt_ref, y_ref, gate_ref):
    x = x_ref[...]
    ms = jnp.mean(x * x, axis=-1, keepdims=True)
    xn = (x * lax.rsqrt(ms + EPS) * g_ref[...]).astype(BF16)

    q = jnp.dot(xn, wq_ref[...], preferred_element_type=F32) * (HEAD_DIM ** -0.5)
    k = jnp.dot(xn, wk_ref[...], preferred_element_type=F32)
    iq = jnp.dot(xn, wiq_ref[...], preferred_element_type=F32)
    for h in range(N_HEADS):
        sl = slice(h * HEAD_DIM, (h + 1) * HEAD_DIM)
        q_ref[h] = q[:, sl].astype(BF16)
        k_ref[h] = k[:, sl].astype(BF16)
        iq_ref[h] = iq[:, sl].astype(BF16)
    ik_ref[...] = jnp.dot(xn, wik_ref[...], preferred_element_type=F32).astype(BF16)
    vt_ref[...] = lax.dot_general(wvt_ref[...], xn, NT_DIMS, preferred_element_type=F32).astype(BF16)
    iwt = lax.dot_general(wiwt_ref[...], xn, NT_DIMS, preferred_element_type=F32)
    iwt_ref[...] = iwt[:IDX_HEADS] * IDX_SCALE
    glu = jnp.dot(xn, wglu_ref[...], preferred_element_type=F32)
    y_ref[...] = glu[:, :CONV_CH] * _sigmoid(glu[:, CONV_CH:])
    gate = jnp.dot(xn, wgate_ref[...], preferred_element_type=F32) + bgate_ref[...]
    gate_ref[...] = _sigmoid(gate)


def _project(x2d, g, wts, tm):
    n = x2d.shape[0]
    wq, wk, wiq, wik, wvt, wiwt, wglu, wgate, bgate = wts
    full = lambda a: pl.BlockSpec(a.shape, lambda i: (0,) * a.ndim)
    out_shape = (
        jax.ShapeDtypeStruct((N_HEADS, n, HEAD_DIM), BF16),
        jax.ShapeDtypeStruct((N_HEADS, n, HEAD_DIM), BF16),
        jax.ShapeDtypeStruct((IDX_HEADS, n, IDX_DIM), BF16),
        jax.ShapeDtypeStruct((n, IDX_DIM), BF16),
        jax.ShapeDtypeStruct((ATTN_WIDTH, n), BF16),
        jax.ShapeDtypeStruct((IDX_HEADS, n), F32),
        jax.ShapeDtypeStruct((n, CONV_CH), F32),
        jax.ShapeDtypeStruct((n, 2 * D_MODEL), F32),
    )
    out_specs = (
        pl.BlockSpec((N_HEADS, tm, HEAD_DIM), lambda i: (0, i, 0)),
        pl.BlockSpec((N_HEADS, tm, HEAD_DIM), lambda i: (0, i, 0)),
        pl.BlockSpec((IDX_HEADS, tm, IDX_DIM), lambda i: (0, i, 0)),
        pl.BlockSpec((tm, IDX_DIM), lambda i: (i, 0)),
        pl.BlockSpec((ATTN_WIDTH, tm), lambda i: (0, i)),
        pl.BlockSpec((IDX_HEADS, tm), lambda i: (0, i)),
        pl.BlockSpec((tm, CONV_CH), lambda i: (i, 0)),
        pl.BlockSpec((tm, 2 * D_MODEL), lambda i: (i, 0)),
    )
    return pl.pallas_call(
        _proj_kernel,
        grid=(n // tm,),
        in_specs=[pl.BlockSpec((tm, D_MODEL), lambda i: (i, 0)), full(g), full(wq), full(wk), full(wiq),
                  full(wik), full(wvt), full(wiwt), full(wglu), full(wgate), full(bgate)],
        out_specs=out_specs,
        out_shape=out_shape,
        compiler_params=pltpu.CompilerParams(dimension_semantics=("arbitrary",),
                                             vmem_limit_bytes=VMEM_LIMIT),
        name="proj",
    )(x2d, g, wq, wk, wiq, wik, wvt, wiwt, wglu, wgate, bgate)


def _t5_bucket(n):
    max_exact = N_BUCKETS // 2
    nf = jnp.maximum(n, 1).astype(F32)
    large = max_exact + (jnp.log(nf / max_exact) / math.log(MAX_DISTANCE / max_exact)
                         * (N_BUCKETS - max_exact)).astype(I32)
    large = jnp.minimum(large, N_BUCKETS - 1)
    return jnp.where(n < max_exact, n, large)


def _bias_lookup(rb_ref, dist, h):
    bucket = _t5_bucket(jnp.maximum(dist, 0))
    out = jnp.full(dist.shape, NEG, F32)
    for b in range(N_BUCKETS):
        out = jnp.where(bucket == b, rb_ref[b, h], out)
    return jnp.where(dist >= 0, out, NEG)


def _bias_kernel(rb_ref, tab_ref, tabm_ref):
    kind = pl.program_id(0)
    r = pl.program_id(1)
    rows = tab_ref.shape[2]
    s = lax.broadcasted_iota(I32, (rows, Q_TILE), 0) + r * rows
    t = lax.broadcasted_iota(I32, (rows, Q_TILE), 1)
    dist = jnp.where(kind == 2, 2 * K_CHUNK, t - s + kind * K_CHUNK)
    for h in range(N_HEADS):
        tab_ref[0, h] = _bias_lookup(rb_ref, dist, h)
    m = lax.broadcasted_iota(I32, (N_META, Q_TILE), 0)
    tm_ = lax.broadcasted_iota(I32, (N_META, Q_TILE), 1)
    distm = jnp.where(kind == 0, N_META + tm_ - m, 2 * K_CHUNK)
    for h in range(N_HEADS):
        tabm_ref[0, h] = _bias_lookup(rb_ref, distm, h)


def _bias_tables(rel_bias):
    rows = 64
    return pl.pallas_call(
        _bias_kernel,
        grid=(3, K_CHUNK // rows),
        in_specs=[pl.BlockSpec(memory_space=pltpu.SMEM)],
        out_specs=(pl.BlockSpec((1, N_HEADS, rows, Q_TILE), lambda kd, r: (kd, 0, r, 0)),
                   pl.BlockSpec((1, N_HEADS, N_META, Q_TILE), lambda kd, r: (kd, 0, 0, 0))),
        out_shape=(jax.ShapeDtypeStruct((3, N_HEADS, K_CHUNK, Q_TILE), F32),
                   jax.ShapeDtypeStruct((3, N_HEADS, N_META, Q_TILE), F32)),
        compiler_params=pltpu.CompilerParams(dimension_semantics=("arbitrary", "arbitrary")),
        name="bias_tables",
    )(rel_bias)


def _order_bits_to_float(u):
    bits = jnp.where(u < 0, u ^ jnp.int32(-2 ** 31), ~u)
    return lax.bitcast_convert_type(bits, F32)


def _fold_rows(x, op):
    r, l = x.shape
    x3 = x.reshape(r // 8, 8, l)
    return op(x3, axis=0)


def _attn_kernel(q_ref, iq_ref, iwt_ref, k_ref, ik_ref, vt_ref, km_ref, ikm_ref, vtm_ref,
                 tab_ref, tabm_ref, o_ref, sc_ref, scm_ref, l_ref, lm_ref, ot_ref, cst_ref):
    j = pl.program_id(1)
    nchunks = j + 1
    iw = iwt_ref[...]

    def chunk_rows(c):
        return pl.ds(pl.multiple_of(c * K_CHUNK, K_CHUNK), K_CHUNK)

    def idx_scores(ikc):
        acc = None
        for h in range(IDX_HEADS):
            s = lax.dot_general(ikc, iq_ref[h], NT_DIMS, preferred_element_type=F32)
            term = jnp.maximum(s, 0.0) * iw[h:h + 1, :]
            acc = term if acc is None else acc + term
        return acc

    scm_ref[...] = idx_scores(ikm_ref[...])

    row_minus_col = (lax.broadcasted_iota(I32, (K_CHUNK, Q_TILE), 0)
                     - lax.broadcasted_iota(I32, (K_CHUNK, Q_TILE), 1))

    def p1(c, carry):
        rows = chunk_rows(c)
        future = row_minus_col > jnp.where(c == j, 0, K_CHUNK)
        sc_ref[rows, :] = jnp.where(future, -jnp.inf, idx_scores(ik_ref[rows, :]))
        return carry

    lax.fori_loop(0, nchunks, p1, 0)

    def count(pred):
        def body(c, acc):
            blk = sc_ref[chunk_rows(c), :]
            return acc + _fold_rows(jnp.where(pred(blk, c), 1, 0).astype(I32), jnp.sum)
        acc = lax.fori_loop(0, nchunks, body, jnp.zeros((8, Q_TILE), I32))
        acc = acc + _fold_rows(jnp.where(pred(scm_ref[...], -1), 1, 0).astype(I32), jnp.sum)
        return jnp.sum(acc, axis=0, keepdims=True)

    def bisect(i, u):
        cand = u | lax.shift_left(jnp.int32(1), 31 - i)
        cf = _order_bits_to_float(cand)
        cnt = count(lambda blk, c: blk >= cf)
        return jnp.where(cnt >= TOPK_MAX, cand, u)

    u = lax.fori_loop(0, 32, bisect, jnp.zeros((1, Q_TILE), I32))
    thr = jnp.where((u >= 0) & (u < 0x00800000), -jnp.inf, _order_bits_to_float(u))

    cnt_ge = count(lambda blk, c: blk >= thr)
    tie = (cnt_ge > TOPK_MAX) & (thr > -jnp.inf)
    any_tie = jnp.max(tie.astype(I32))
    cst_ref[...] = jnp.full((8, Q_TILE), 2 ** 30, I32)

    def pos_of(c, shape):
        r = lax.broadcasted_iota(I32, shape, 0)
        return jnp.where(c < 0, r, r + N_META + c * K_CHUNK)

    @pl.when(any_tie > 0)
    def _():
        cnt_gt = count(lambda blk, c: blk > thr)
        need = TOPK_MAX - cnt_gt

        def bis_pos(i, cs):
            cand = cs | lax.shift_left(jnp.int32(1), 11 - i)
            f = count(lambda blk, c: (blk == thr) & (pos_of(c, blk.shape) < cand))
            return jnp.where(f <= need, cand, cs)

        cs = lax.fori_loop(0, 12, bis_pos, jnp.zeros((1, Q_TILE), I32))
        cs = jnp.where(tie, cs, 2 ** 30)
        cst_ref[...] = jnp.broadcast_to(cs, (8, Q_TILE))

    def to_mask(blk, c, with_ties):
        if with_ties:
            cs = cst_ref[0:1, :]
            sel = (blk > thr) | ((blk == thr) & (pos_of(c, blk.shape) < cs))
        else:
            sel = blk >= thr
        return jnp.where(sel, 0.0, NEG)

    for with_ties in (False, True):
        @pl.when((any_tie > 0) == with_ties)
        def _():
            def body(c, carry):
                rows = chunk_rows(c)
                sc_ref[rows, :] = to_mask(sc_ref[rows, :], c, with_ties)
                return carry
            lax.fori_loop(0, nchunks, body, 0)
            scm_ref[...] = to_mask(scm_ref[...], -1, with_ties)

    kindm = jnp.minimum(j, 1)
    heads = [slice(h * HEAD_DIM, (h + 1) * HEAD_DIM) for h in range(N_HEADS)]

    mx0 = []
    for h in range(N_HEADS):
        lm = (lax.dot_general(km_ref[h], q_ref[h], NT_DIMS, preferred_element_type=F32)
              + tabm_ref[kindm, h] + scm_ref[...])
        lm_ref[h] = lm
        mx0.append(_fold_rows(lm, jnp.max))

    def pass_a(c, mx):
        rows = chunk_rows(c)
        kind = jnp.minimum(j - c, 2)
        mask = sc_ref[rows, :]
        out = []
        for h in range(N_HEADS):
            l = (lax.dot_general(k_ref[h, rows, :], q_ref[h], NT_DIMS, preferred_element_type=F32)
                 + tab_ref[kind, h] + mask)
            l_ref[h, rows, :] = l
            out.append(jnp.maximum(mx[h], _fold_rows(l, jnp.max)))
        return tuple(out)

    mx = lax.fori_loop(0, nchunks, pass_a, tuple(mx0))
    m = [jnp.max(mx[h], axis=0, keepdims=True) for h in range(N_HEADS)]

    den0 = []
    for h in range(N_HEADS):
        pm = jnp.exp(lm_ref[h] - m[h])
        den0.append(_fold_rows(pm, jnp.sum))
        ot_ref[heads[h], :] = jnp.dot(vtm_ref[heads[h], :], pm.astype(BF16), preferred_element_type=F32)

    def pass_b(c, den):
        rows = chunk_rows(c)
        out = []
        for h in range(N_HEADS):
            p = jnp.exp(l_ref[h, rows, :] - m[h])
            out.append(den[h] + _fold_rows(p, jnp.sum))
            ot_ref[heads[h], :] += jnp.dot(vt_ref[heads[h], rows], p.astype(BF16),
                                           preferred_element_type=F32)
        return tuple(out)

    den = lax.fori_loop(0, nchunks, pass_b, tuple(den0))
    for h in range(N_HEADS):
        ot_ref[heads[h], :] = ot_ref[heads[h], :] / jnp.sum(den[h], axis=0, keepdims=True)
    o_ref[...] = ot_ref[...].T.astype(BF16)


def _attention(q3, iq3, iwt, k3, ik, vt, km3, ikm, vtm, tab, tabm, batch, seq):
    n = batch * seq
    tiles = seq // Q_TILE
    full = lambda a: pl.BlockSpec(a.shape, lambda b, j: (0,) * a.ndim, pipeline_mode=pl.Buffered(1))
    return pl.pallas_call(
        _attn_kernel,
        grid=(batch, tiles),
        in_specs=[
            pl.BlockSpec((N_HEADS, Q_TILE, HEAD_DIM), lambda b, j: (0, b * tiles + j, 0)),
            pl.BlockSpec((IDX_HEADS, Q_TILE, IDX_DIM), lambda b, j: (0, b * tiles + j, 0)),
            pl.BlockSpec((IDX_HEADS, Q_TILE), lambda b, j: (0, b * tiles + j)),
            pl.BlockSpec((N_HEADS, seq, HEAD_DIM), lambda b, j: (0, b, 0)),
            pl.BlockSpec((seq, IDX_DIM), lambda b, j: (b, 0)),
            pl.BlockSpec((ATTN_WIDTH, seq), lambda b, j: (0, b)),
            full(km3), full(ikm), full(vtm), full(tab), full(tabm),
        ],
        out_specs=pl.BlockSpec((Q_TILE, ATTN_WIDTH), lambda b, j: (b * tiles + j, 0)),
        out_shape=jax.ShapeDtypeStruct((n, ATTN_WIDTH), BF16),
        scratch_shapes=[
            pltpu.VMEM((seq, Q_TILE), F32),
            pltpu.VMEM((N_META, Q_TILE), F32),
            pltpu.VMEM((N_HEADS, seq, Q_TILE), F32),
            pltpu.VMEM((N_HEADS, N_META, Q_TILE), F32),
            pltpu.VMEM((ATTN_WIDTH, Q_TILE), F32),
            pltpu.VMEM((8, Q_TILE), I32),
        ],
        compiler_params=pltpu.CompilerParams(dimension_semantics=("arbitrary", "arbitrary"),
                                             vmem_limit_bytes=VMEM_LIMIT),
        name="attn",
    )(q3, iq3, iwt, k3, ik, vt, km3, ikm, vtm, tab, tabm)


def _mix_kernel(x_ref, attn_ref, y_ref, yprev_ref, ymeta_ref, gate_ref,
                cw_ref, cb_ref, lng_ref, lnb_ref, wco_ref, wao_ref, wout_ref, nf_ref, wrt_ref, br_ref,
                h2_ref, hn2_ref, eid_ref, rank_ref, gcol_ref, cnt_ref,
                win_ref, base_ref, *, tiles_per_seq):
    i = pl.program_id(0)

    @pl.when(i == 0)
    def _():
        base_ref[...] = jnp.zeros_like(base_ref)

    first = (i % tiles_per_seq) == 0
    win_ref[0:HALO, :] = jnp.where(first, ymeta_ref[...], yprev_ref[...])
    win_ref[HALO:, :] = y_ref[...]
    acc = jnp.zeros((ROW_TILE, CONV_CH), F32)
    lead = HALO - (CONV_WIDTH - 1)
    for b in range(8):
        taps = [w for w in range(b, CONV_WIDTH, 8)]
        span = ROW_TILE + 8 * (len(taps) - 1)
        yb = win_ref[pl.ds(lead + b, span), :]
        for a, w in enumerate(taps):
            acc = acc + cw_ref[w:w + 1, :] * yb[8 * a:8 * a + ROW_TILE, :]
    yc = acc + cb_ref[...]
    mu = jnp.mean(yc, axis=-1, keepdims=True)
    var = jnp.mean(jnp.square(yc - mu), axis=-1, keepdims=True)
    yn = (yc - mu) * lax.rsqrt(var + EPS) * lng_ref[...] + lnb_ref[...]
    ys = yn * _sigmoid(yn)
    y_b = jnp.dot(ys.astype(BF16), wco_ref[...], preferred_element_type=F32)

    y_a = jnp.dot(attn_ref[...], wao_ref[...], preferred_element_type=F32)
    merged = gate_ref[:, :D_MODEL] * y_a + gate_ref[:, D_MODEL:] * y_b
    h2 = x_ref[...] + jnp.dot(merged.astype(BF16), wout_ref[...], preferred_element_type=F32)
    h2_ref[...] = h2
    ms = jnp.mean(h2 * h2, axis=-1, keepdims=True)
    hn2 = h2 * lax.rsqrt(ms + EPS) * nf_ref[...]
    hn2_ref[...] = _pack_bf16_pairs(hn2)

    logits = lax.dot_general(wrt_ref[...], hn2, NT_DIMS, preferred_element_type=F32,
                             precision=lax.Precision.HIGHEST) + br_ref[...]
    erow = lax.broadcasted_iota(I32, (N_EXPERTS, ROW_TILE), 0)
    vals, ids = [], []
    l = logits
    for _ in range(TOP_K_EXPERTS):
        m = jnp.max(l, axis=0, keepdims=True)
        idx = jnp.min(jnp.where(l == m, erow, N_EXPERTS), axis=0, keepdims=True)
        vals.append(m)
        ids.append(idx)
        l = jnp.where(erow == idx, -jnp.inf, l)
    ex = [jnp.exp(v - vals[0]) for v in vals]
    den = ex[0] + ex[1] + ex[2] + ex[3]
    gates = [e / den for e in ex]

    onehot = [(erow == idx) for idx in ids]
    oh = jnp.concatenate([jnp.where(o, 1.0, 0.0) for o in onehot], axis=0)
    tr = lax.broadcasted_iota(I32, (ROW_TILE, ROW_TILE), 0)
    tc = lax.broadcasted_iota(I32, (ROW_TILE, ROW_TILE), 1)
    upper = jnp.where(tr <= tc, 1.0, 0.0).astype(BF16)
    pref = jnp.dot(oh.astype(BF16), upper, preferred_element_type=F32)
    offs = base_ref[:, 0:1]
    ranks = []
    for kk in range(TOP_K_EXPERTS):
        pk = pref[kk * N_EXPERTS:(kk + 1) * N_EXPERTS, :]
        r = jnp.sum(jnp.where(onehot[kk], offs + pk - 1.0, 0.0), axis=0, keepdims=True)
        ranks.append(r.astype(I32))
        offs = offs + pk[:, ROW_TILE - 1:ROW_TILE]
    base_ref[...] = jnp.broadcast_to(offs, base_ref.shape)
    cnt_ref[...] = jnp.broadcast_to(offs, cnt_ref.shape)

    zi = jnp.zeros((8 - TOP_K_EXPERTS, ROW_TILE), I32)
    eid_ref[...] = jnp.concatenate(ids + [zi], axis=0)
    rank_ref[...] = jnp.concatenate(ranks + [zi], axis=0)
    g8 = jnp.concatenate(gates + [jnp.zeros((128 - TOP_K_EXPERTS, ROW_TILE), F32)], axis=0)
    gcol_ref[...] = g8.T


def _mix(x2d, attn, y, ymeta, gate, wts, seq):
    n = x2d.shape[0]
    tiles_per_seq = seq // ROW_TILE
    halo_per_tile = ROW_TILE // HALO
    full = lambda a: pl.BlockSpec(a.shape, lambda i: (0,) * a.ndim)
    row = lambda w: pl.BlockSpec((ROW_TILE, w), lambda i: (i, 0))
    lane = lambda r: pl.BlockSpec((r, ROW_TILE), lambda i: (0, i))
    out_shape = (
        jax.ShapeDtypeStruct((n, D_MODEL), F32),
        jax.ShapeDtypeStruct((n, D_MODEL // 2), I32),
        jax.ShapeDtypeStruct((8, n), I32),
        jax.ShapeDtypeStruct((8, n), I32),
        jax.ShapeDtypeStruct((n, 128), F32),
        jax.ShapeDtypeStruct((N_EXPERTS, 128), F32),
    )
    out_specs = (row(D_MODEL), row(D_MODEL // 2), lane(8), lane(8), row(128), full(out_shape[5]))
    return pl.pallas_call(
        functools.partial(_mix_kernel, tiles_per_seq=tiles_per_seq),
        grid=(n // ROW_TILE,),
        in_specs=[row(D_MODEL), row(ATTN_WIDTH), row(CONV_CH),
                  pl.BlockSpec((HALO, CONV_CH), lambda i: (jnp.maximum(i * halo_per_tile - 1, 0), 0)),
                  full(ymeta), row(2 * D_MODEL)] + [full(w) for w in wts],
        out_specs=out_specs,
        out_shape=out_shape,
        scratch_shapes=[pltpu.VMEM((HALO + ROW_TILE, CONV_CH), F32),
                        pltpu.VMEM((N_EXPERTS, 128), F32)],
        compiler_params=pltpu.CompilerParams(dimension_semantics=("arbitrary",),
                                             vmem_limit_bytes=VMEM_LIMIT),
        name="mix",
    )(x2d, attn, y, y, ymeta, gate, *wts)


def _slots_kernel(starts_ref, eid_ref, rank_ref, slot_ref):
    eid = eid_ref[...]
    base = jnp.zeros(eid.shape, I32)
    for e in range(N_EXPERTS):
        base = jnp.where(eid == e, starts_ref[e], base)
    slot_ref[...] = base + rank_ref[...]


def _slots(starts, eid8, rank8):
    n = eid8.shape[1]
    cols = 2048
    spec = pl.BlockSpec((8, cols), lambda i: (0, i))
    return pl.pallas_call(
        _slots_kernel,
        grid=(n // cols,),
        in_specs=[pl.BlockSpec(memory_space=pltpu.SMEM), spec, spec],
        out_specs=spec,
        out_shape=jax.ShapeDtypeStruct((8, n), I32),
        compiler_params=pltpu.CompilerParams(dimension_semantics=("arbitrary",)),
        name="slots",
    )(starts, eid8, rank8)


COPIES_PER_TILE = TOP_K_EXPERTS * ROW_TILE
SLOT_UNROLL = 16


def _slot_source_kernel(slot_ref, init_ref, src_ref, sem, *, n):
    i = pl.program_id(0)

    @pl.when(i == 0)
    def _():
        cp = pltpu.make_async_copy(init_ref, src_ref, sem)
        cp.start()
        cp.wait()

    for kk in range(TOP_K_EXPERTS):
        def body(g, carry):
            for u in range(SLOT_UNROLL):
                t = g * SLOT_UNROLL + u
                src_ref[slot_ref[0, 0, kk * ROW_TILE + t]] = kk * n + i * ROW_TILE + t
            return carry
        lax.fori_loop(0, ROW_TILE // SLOT_UNROLL, body, 0)


def _slot_sources(slot_tiles, init, n):
    return pl.pallas_call(
        functools.partial(_slot_source_kernel, n=n),
        grid=(slot_tiles.shape[0],),
        in_specs=[pl.BlockSpec((1, 1, COPIES_PER_TILE), lambda i: (i, 0, 0), memory_space=pltpu.SMEM),
                  pl.BlockSpec(memory_space=pl.ANY)],
        out_specs=pl.BlockSpec(memory_space=pltpu.SMEM),
        out_shape=jax.ShapeDtypeStruct(init.shape, I32),
        scratch_shapes=[pltpu.SemaphoreType.DMA(())],
        compiler_params=pltpu.CompilerParams(dimension_semantics=("arbitrary",)),
        name="slot_sources",
    )(slot_tiles, init)


def _expert_kernel(te_ref, nused_ref, src_ref, src_next_ref, src_prev_ref, hn_hbm,
                   wug_ref, bug_ref, wd_ref, bd_ref, yk_hbm, xbuf, ybuf, wug_bf, wd_bf, gsem, ssem, *, n):
    i = pl.program_id(0)
    nused = nused_ref[0]

    def gather_copy(src, r, b):
        tok = src[0, 0, r] & (n - 1)
        return pltpu.make_async_copy(hn_hbm.at[pl.ds(tok, 1), :], xbuf.at[b, pl.ds(r, 1), :], gsem.at[b])

    def scatter_copy(src, r, b):
        return pltpu.make_async_copy(ybuf.at[b, pl.ds(r, 1), :], yk_hbm.at[pl.ds(src[0, 0, r], 1), :],
                                     ssem.at[b])

    @pl.when(i <= nused)
    def _():
        cur = i % 2
        oth = 1 - cur

        def wait_scatter(b):
            pltpu.make_async_copy(ybuf.at[b], yk_hbm.at[pl.ds(0, ROW_TILE), :], ssem.at[b]).wait()

        @pl.when(i == 0)
        def _():
            for r in range(ROW_TILE):
                gather_copy(src_ref, r, 0).start()
            ybuf[1] = jnp.zeros((ROW_TILE, D_MODEL), F32)
            spare = pltpu.make_async_copy(ybuf.at[1], yk_hbm.at[pl.ds(TOP_K_EXPERTS * n, ROW_TILE), :],
                                          ssem.at[1])
            spare.start()
            spare.wait()

        @pl.when(i < nused)
        def _():
            for r in range(ROW_TILE):
                gather_copy(src_next_ref, r, oth).start()

        @pl.when(i > 0)
        def _():
            for r in range(ROW_TILE):
                scatter_copy(src_prev_ref, r, oth).start()

        pltpu.make_async_copy(hn_hbm.at[pl.ds(0, ROW_TILE), :], xbuf.at[cur], gsem.at[cur]).wait()

        prev = te_ref[jnp.maximum(i - 1, 0)]
        fresh = (i == 0) | (te_ref[jnp.minimum(i, nused - 1)] != prev)

        @pl.when(fresh)
        def _():
            wug_bf[...] = wug_ref[0].astype(BF16)
            wd_bf[...] = wd_ref[0].astype(BF16)

        xb = _unpack_bf16_pairs(xbuf[cur])
        ug = jnp.dot(xb, wug_bf[...], preferred_element_type=F32) + bug_ref[0]
        gate = jnp.minimum(ug[:, :D_FF], SWIGLU_LIMIT)
        up = jnp.clip(ug[:, D_FF:], -SWIGLU_LIMIT, SWIGLU_LIMIT)
        act = (up + 1.0) * gate * _sigmoid(SWIGLU_ALPHA * gate)
        y = jnp.dot(act.astype(BF16), wd_bf[...], preferred_element_type=F32) + bd_ref[0]

        @pl.when(i > 1)
        def _():
            wait_scatter(cur)

        ybuf[cur] = y

        @pl.when((i == nused) & (i > 0))
        def _():
            wait_scatter(oth)


def _experts(tile_expert, nused, src_tiles, hn2p, w_ug, b_ug, w_down, b_down, n):
    ntiles = src_tiles.shape[0]
    clamp = lambda i, nu: jnp.maximum(jnp.minimum(i, nu[0] - 1), 0)
    smem = lambda f: pl.BlockSpec((1, 1, ROW_TILE), f, memory_space=pltpu.SMEM)
    expert = lambda i, te, nu: (te[clamp(i, nu)], 0, 0)
    grid_spec = pltpu.PrefetchScalarGridSpec(
        num_scalar_prefetch=2,
        grid=(ntiles + 1,),
        in_specs=[
            smem(lambda i, te, nu: (clamp(i, nu), 0, 0)),
            smem(lambda i, te, nu: (clamp(i + 1, nu), 0, 0)),
            smem(lambda i, te, nu: (clamp(i - 1, nu), 0, 0)),
            pl.BlockSpec(memory_space=pl.ANY),
            pl.BlockSpec((1, D_MODEL, 2 * D_FF), expert),
            pl.BlockSpec((1, 1, 2 * D_FF), expert),
            pl.BlockSpec((1, D_FF, D_MODEL), expert),
            pl.BlockSpec((1, 1, D_MODEL), expert),
        ],
        out_specs=pl.BlockSpec(memory_space=pl.ANY),
        scratch_shapes=[pltpu.VMEM((2, ROW_TILE, D_MODEL // 2), I32),
                        pltpu.VMEM((2, ROW_TILE, D_MODEL), F32),
                        pltpu.VMEM((D_MODEL, 2 * D_FF), BF16), pltpu.VMEM((D_FF, D_MODEL), BF16),
                        pltpu.SemaphoreType.DMA((2,)), pltpu.SemaphoreType.DMA((2,))],
    )
    return pl.pallas_call(
        functools.partial(_expert_kernel, n=n),
        grid_spec=grid_spec,
        out_shape=jax.ShapeDtypeStruct((TOP_K_EXPERTS * n + ROW_TILE, D_MODEL), F32),
        compiler_params=pltpu.CompilerParams(dimension_semantics=("arbitrary",),
                                             vmem_limit_bytes=VMEM_LIMIT),
        name="experts",
    )(tile_expert, nused, src_tiles, src_tiles, src_tiles, hn2p,
      w_ug, b_ug[:, None, :], w_down, b_down[:, None, :])


def _final_kernel(h2_ref, gcol_ref, nf_ref, *rest):
    y_refs, o_ref = rest[:TOP_K_EXPERTS], rest[TOP_K_EXPERTS]
    h = h2_ref[...]
    for kk in range(TOP_K_EXPERTS):
        h = h + gcol_ref[:, kk:kk + 1] * y_refs[kk][...]
    ms = jnp.mean(h * h, axis=-1, keepdims=True)
    o_ref[...] = h * lax.rsqrt(ms + EPS) * nf_ref[...]


def _final(h2, gcol, nf, yk):
    n = h2.shape[0]
    ntiles = n // ROW_TILE
    choice = lambda kk: pl.BlockSpec((ROW_TILE, D_MODEL), lambda i: (kk * ntiles + i, 0))
    return pl.pallas_call(
        _final_kernel,
        grid=(ntiles,),
        in_specs=[pl.BlockSpec((ROW_TILE, D_MODEL), lambda i: (i, 0)),
                  pl.BlockSpec((ROW_TILE, 128), lambda i: (i, 0)),
                  pl.BlockSpec((1, D_MODEL), lambda i: (0, 0))] + [choice(kk) for kk in range(TOP_K_EXPERTS)],
        out_specs=pl.BlockSpec((ROW_TILE, D_MODEL), lambda i: (i, 0)),
        out_shape=jax.ShapeDtypeStruct((n, D_MODEL), F32),
        compiler_params=pltpu.CompilerParams(dimension_semantics=("arbitrary",),
                                             vmem_limit_bytes=VMEM_LIMIT),
        name="final",
    )(h2, gcol, nf, *([yk] * TOP_K_EXPERTS))


def _split_w_in(w_in, b_gate):
    c = ATTN_WIDTH
    o = 0
    wq = w_in[:, o:o + c]; o += c
    wk = w_in[:, o:o + c]; o += c
    wv = w_in[:, o:o + c]; o += c
    wiq = w_in[:, o:o + IDX_HEADS * IDX_DIM]; o += IDX_HEADS * IDX_DIM
    wik = w_in[:, o:o + IDX_DIM]; o += IDX_DIM
    wiw = w_in[:, o:o + IDX_HEADS]; o += IDX_HEADS
    wglu = w_in[:, o:o + 2 * CONV_CH]; o += 2 * CONV_CH
    wgate = w_in[:, o:]
    wiwt = jnp.concatenate([wiw.T, jnp.zeros((16 - IDX_HEADS, D_MODEL), w_in.dtype)], axis=0)
    bf = lambda a: a.astype(BF16)
    return (bf(wq), bf(wk), bf(wiq), bf(wik), bf(wv.T), bf(wiwt), bf(wglu), bf(wgate),
            b_gate[None, :].astype(F32))


def kernel(x, meta_tokens, rel_bias, norm_mix, w_in, b_gate, w_attn_out, conv_w, conv_b, conv_ln_g, conv_ln_b, w_conv_out, w_out, norm_ffn, w_router, b_router, w_up_gate, b_up_gate, w_down, b_down, norm_final):
    batch, seq, d = x.shape
    n = batch * seq
    x2d = x.reshape(n, d)

    wts = _split_w_in(w_in[0], b_gate[0])
    g_mix = norm_mix[0][None, :]
    q3, k3, iq3, ik, vt, iwt, y, gate = _project(x2d, g_mix, wts, ROW_TILE)
    _, km3, _, ikm, vtm, _, ym, _ = _project(meta_tokens.astype(F32), g_mix, wts, N_META)

    tab, tabm = _bias_tables(rel_bias.astype(F32))
    attn = _attention(q3, iq3, iwt, k3, ik, vt, km3, ikm, vtm, tab, tabm, batch, seq)

    ymeta = jnp.concatenate([jnp.zeros((HALO - N_META, CONV_CH), F32), ym], axis=0)
    cw = jnp.concatenate([conv_w[0], jnp.zeros((32 - CONV_WIDTH, CONV_CH), F32)], axis=0)
    mix_w = (cw, conv_b[0][None, :], conv_ln_g[0][None, :], conv_ln_b[0][None, :],
             w_conv_out[0].astype(BF16), w_attn_out[0].astype(BF16), w_out[0].astype(BF16),
             norm_ffn[0][None, :], w_router[0].T, b_router[0][:, None])
    h2, hn2, eid8, rank8, gcol, cnt = _mix(x2d, attn, y, ymeta, gate, mix_w, seq)

    counts = cnt[:, 0].astype(I32)
    padded = ((counts + ROW_TILE - 1) // ROW_TILE) * ROW_TILE
    ends = jnp.cumsum(padded)
    starts = ends - padded
    slot8 = _slots(starts, eid8, rank8)
    slot_tiles = (slot8[:TOP_K_EXPERTS].reshape(TOP_K_EXPERTS, n // ROW_TILE, ROW_TILE)
                  .transpose(1, 0, 2).reshape(n // ROW_TILE, 1, COPIES_PER_TILE))
    nslots = n * TOP_K_EXPERTS + N_EXPERTS * ROW_TILE
    ntiles = nslots // ROW_TILE
    tile_start = jnp.arange(ntiles, dtype=I32) * ROW_TILE
    nused = (ends[-1] // ROW_TILE).astype(I32)
    last_start = jnp.maximum(ends[-1] - ROW_TILE, 0)
    tile_expert = jnp.sum((jnp.minimum(tile_start, last_start)[:, None] >= ends[None, :]).astype(I32), axis=1)

    spare = TOP_K_EXPERTS * n + jnp.arange(nslots, dtype=I32) % ROW_TILE
    src_tiles = _slot_sources(slot_tiles, spare, n).reshape(ntiles, 1, ROW_TILE)
    yk = _experts(tile_expert, nused[None], src_tiles, hn2, w_up_gate[0], b_up_gate[0], w_down[0], b_down[0], n)
    out = _final(h2, gcol, norm_final[None, :], yk)
    return out.reshape(batch, seq, d)
```

```python
import functools
import math

import jax
import jax.numpy as jnp
from jax import lax
from jax.experimental import pallas as pl
from jax.experimental.pallas import tpu as pltpu

F32 = jnp.float32
BF16 = jnp.bfloat16
I32 = jnp.int32

D_MODEL = 1024
N_META = 16
N_HEADS = 8
HEAD_DIM = 64
ATTN_WIDTH = N_HEADS * HEAD_DIM
IDX_HEADS = 8
IDX_DIM = 64
TOPK_MAX = 256
CONV_CH = 512
CONV_WIDTH = 31
N_BUCKETS = 32
MAX_DISTANCE = 128
N_EXPERTS = 32
TOP_K_EXPERTS = 4
D_FF = 1024
SWIGLU_LIMIT = 7.0
SWIGLU_ALPHA = 1.702
EPS = 1e-6
IDX_SCALE = (IDX_DIM ** -0.5) * (IDX_HEADS ** -0.5)

ROW_TILE = 256
Q_TILE = 256
K_CHUNK = 256
HALO = 32
NEG = -1e30
VMEM_LIMIT = 56 * 1024 * 1024

NT_DIMS = (((1,), (1,)), ((), ()))


def _sigmoid(x):
    return 1.0 / (1.0 + jnp.exp(-x))


def _pack_bf16_pairs(x):
    w = x.shape[1] // 2
    hi = lax.bitcast_convert_type(x[:, :w].astype(BF16).astype(F32), jnp.uint32)
    lo = lax.bitcast_convert_type(x[:, w:].astype(BF16).astype(F32), jnp.uint32)
    return lax.bitcast_convert_type(hi | (lo >> 16), I32)


def _unpack_bf16_pairs(p):
    u = lax.bitcast_convert_type(p, jnp.uint32)
    hi = lax.bitcast_convert_type(u & jnp.uint32(0xFFFF0000), F32)
    lo = lax.bitcast_convert_type(u << 16, F32)
    return jnp.concatenate([hi, lo], axis=1).astype(BF16)


def _proj_kernel(x_ref, g_ref, wq_ref, wk_ref, wiq_ref, wik_ref, wvt_ref, wiwt_ref, wglu_ref,
                 wgate_ref, bgate_ref,
                 q_ref, k_ref, iq_ref, ik_ref, vt_ref, iwt_ref, y_ref, gate_ref):
    x = x_ref[...]
    ms = jnp.mean(x * x, axis=-1, keepdims=True)
    xn = (x * lax.rsqrt(ms + EPS) * g_ref[...]).astype(BF16)

    q = jnp.dot(xn, wq_ref[...], preferred_element_type=F32) * (HEAD_DIM ** -0.5)
    k = jnp.dot(xn, wk_ref[...], preferred_element_type=F32)
    iq = jnp.dot(xn, wiq_ref[...], preferred_element_type=F32)
    for h in range(N_HEADS):
        sl = slice(h * HEAD_DIM, (h + 1) * HEAD_DIM)
        q_ref[h] = q[:, sl].astype(BF16)
        k_ref[h] = k[:, sl].astype(BF16)
        iq_ref[h] = iq[:, sl].astype(BF16)
    ik_ref[...] = jnp.dot(xn, wik_ref[...], preferred_element_type=F32).astype(BF16)
    vt_ref[...] = lax.dot_general(wvt_ref[...], xn, NT_DIMS, preferred_element_type=F32).astype(BF16)
    iwt = lax.dot_general(wiwt_ref[...], xn, NT_DIMS, preferred_element_type=F32)
    iwt_ref[...] = iwt[:IDX_HEADS] * IDX_SCALE
    glu = jnp.dot(xn, wglu_ref[...], preferred_element_type=F32)
    y_ref[...] = glu[:, :CONV_CH] * _sigmoid(glu[:, CONV_CH:])
    gate = jnp.dot(xn, wgate_ref[...], preferred_element_type=F32) + bgate_ref[...]
    gate_ref[...] = _sigmoid(gate)


def _project(x2d, g, wts, tm):
    n = x2d.shape[0]
    wq, wk, wiq, wik, wvt, wiwt, wglu, wgate, bgate = wts
    full = lambda a: pl.BlockSpec(a.shape, lambda i: (0,) * a.ndim)
    out_shape = (
        jax.ShapeDtypeStruct((N_HEADS, n, HEAD_DIM), BF16),
        jax.ShapeDtypeStruct((N_HEADS, n, HEAD_DIM), BF16),
        jax.ShapeDtypeStruct((IDX_HEADS, n, IDX_DIM), BF16),
        jax.ShapeDtypeStruct((n, IDX_DIM), BF16),
        jax.ShapeDtypeStruct((ATTN_WIDTH, n), BF16),
        jax.ShapeDtypeStruct((IDX_HEADS, n), F32),
        jax.ShapeDtypeStruct((n, CONV_CH), F32),
        jax.ShapeDtypeStruct((n, 2 * D_MODEL), F32),
    )
    out_specs = (
        pl.BlockSpec((N_HEADS, tm, HEAD_DIM), lambda i: (0, i, 0)),
        pl.BlockSpec((N_HEADS, tm, HEAD_DIM), lambda i: (0, i, 0)),
        pl.BlockSpec((IDX_HEADS, tm, IDX_DIM), lambda i: (0, i, 0)),
        pl.BlockSpec((tm, IDX_DIM), lambda i: (i, 0)),
        pl.BlockSpec((ATTN_WIDTH, tm), lambda i: (0, i)),
        pl.BlockSpec((IDX_HEADS, tm), lambda i: (0, i)),
        pl.BlockSpec((tm, CONV_CH), lambda i: (i, 0)),
        pl.BlockSpec((tm, 2 * D_MODEL), lambda i: (i, 0)),
    )
    return pl.pallas_call(
        _proj_kernel,
        grid=(n // tm,),
        in_specs=[pl.BlockSpec((tm, D_MODEL), lambda i: (i, 0)), full(g), full(wq), full(wk), full(wiq),
                  full(wik), full(wvt), full(wiwt), full(wglu), full(wgate), full(bgate)],
        out_specs=out_specs,
        out_shape=out_shape,
        compiler_params=pltpu.CompilerParams(dimension_semantics=("arbitrary",),
                                             vmem_limit_bytes=VMEM_LIMIT),
        name="proj",
    )(x2d, g, wq, wk, wiq, wik, wvt, wiwt, wglu, wgate, bgate)


def _t5_bucket(n):
    max_exact = N_BUCKETS // 2
    nf = jnp.maximum(n, 1).astype(F32)
    large = max_exact + (jnp.log(nf / max_exact) / math.log(MAX_DISTANCE / max_exact)
                         * (N_BUCKETS - max_exact)).astype(I32)
    large = jnp.minimum(large, N_BUCKETS - 1)
    return jnp.where(n < max_exact, n, large)


def _bias_lookup(rb_ref, dist, h):
    bucket = _t5_bucket(jnp.maximum(dist, 0))
    out = jnp.full(dist.shape, NEG, F32)
    for b in range(N_BUCKETS):
        out = jnp.where(bucket == b, rb_ref[b, h], out)
    return jnp.where(dist >= 0, out, NEG)


def _bias_kernel(rb_ref, tab_ref, tabm_ref):
    kind = pl.program_id(0)
    r = pl.program_id(1)
    rows = tab_ref.shape[2]
    s = lax.broadcasted_iota(I32, (rows, Q_TILE), 0) + r * rows
    t = lax.broadcasted_iota(I32, (rows, Q_TILE), 1)
    dist = jnp.where(kind == 2, 2 * K_CHUNK, t - s + kind * K_CHUNK)
    for h in range(N_HEADS):
        tab_ref[0, h] = _bias_lookup(rb_ref, dist, h)
    m = lax.broadcasted_iota(I32, (N_META, Q_TILE), 0)
    tm_ = lax.broadcasted_iota(I32, (N_META, Q_TILE), 1)
    distm = jnp.where(kind == 0, N_META + tm_ - m, 2 * K_CHUNK)
    for h in range(N_HEADS):
        tabm_ref[0, h] = _bias_lookup(rb_ref, distm, h)


def _bias_tables(rel_bias):
    rows = 64
    return pl.pallas_call(
        _bias_kernel,
        grid=(3, K_CHUNK // rows),
        in_specs=[pl.BlockSpec(memory_space=pltpu.SMEM)],
        out_specs=(pl.BlockSpec((1, N_HEADS, rows, Q_TILE), lambda kd, r: (kd, 0, r, 0)),
                   pl.BlockSpec((1, N_HEADS, N_META, Q_TILE), lambda kd, r: (kd, 0, 0, 0))),
        out_shape=(jax.ShapeDtypeStruct((3, N_HEADS, K_CHUNK, Q_TILE), F32),
                   jax.ShapeDtypeStruct((3, N_HEADS, N_META, Q_TILE), F32)),
        compiler_params=pltpu.CompilerParams(dimension_semantics=("arbitrary", "arbitrary")),
        name="bias_tables",
    )(rel_bias)


def _order_bits_to_float(u):
    bits = jnp.where(u < 0, u ^ jnp.int32(-2 ** 31), ~u)
    return lax.bitcast_convert_type(bits, F32)


def _fold_rows(x, op):
    r, l = x.shape
    x3 = x.reshape(r // 8, 8, l)
    return op(x3, axis=0)


def _attn_kernel(q_ref, iq_ref, iwt_ref, k_ref, ik_ref, vt_ref, km_ref, ikm_ref, vtm_ref,
                 tab_ref, tabm_ref, o_ref, sc_ref, scm_ref, l_ref, lm_ref, ot_ref, cst_ref):
    j = pl.program_id(1)
    nchunks = j + 1
    iw = iwt_ref[...]

    def chunk_rows(c):
        return pl.ds(pl.multiple_of(c * K_CHUNK, K_CHUNK), K_CHUNK)

    def idx_scores(ikc):
        acc = None
        for h in range(IDX_HEADS):
            s = lax.dot_general(ikc, iq_ref[h], NT_DIMS, preferred_element_type=F32)
            term = jnp.maximum(s, 0.0) * iw[h:h + 1, :]
            acc = term if acc is None else acc + term
        return acc

    scm_ref[...] = idx_scores(ikm_ref[...])

    row_minus_col = (lax.broadcasted_iota(I32, (K_CHUNK, Q_TILE), 0)
                     - lax.broadcasted_iota(I32, (K_CHUNK, Q_TILE), 1))

    def p1(c, carry):
        rows = chunk_rows(c)
        future = row_minus_col > jnp.where(c == j, 0, K_CHUNK)
        sc_ref[rows, :] = jnp.where(future, -jnp.inf, idx_scores(ik_ref[rows, :]))
        return carry

    lax.fori_loop(0, nchunks, p1, 0)

    def count(pred):
        def body(c, acc):
            blk = sc_ref[chunk_rows(c), :]
            return acc + _fold_rows(jnp.where(pred(blk, c), 1, 0).astype(I32), jnp.sum)
        acc = lax.fori_loop(0, nchunks, body, jnp.zeros((8, Q_TILE), I32))
        acc = acc + _fold_rows(jnp.where(pred(scm_ref[...], -1), 1, 0).astype(I32), jnp.sum)
        return jnp.sum(acc, axis=0, keepdims=True)

    def bisect(i, u):
        cand = u | lax.shift_left(jnp.int32(1), 31 - i)
        cf = _order_bits_to_float(cand)
        cnt = count(lambda blk, c: blk >= cf)
        return jnp.where(cnt >= TOPK_MAX, cand, u)

    u = lax.fori_loop(0, 32, bisect, jnp.zeros((1, Q_TILE), I32))
    thr = jnp.where((u >= 0) & (u < 0x00800000), -jnp.inf, _order_bits_to_float(u))

    cnt_ge = count(lambda blk, c: blk >= thr)
    tie = (cnt_ge > TOPK_MAX) & (thr > -jnp.inf)
    any_tie = jnp.max(tie.astype(I32))
    cst_ref[...] = jnp.full((8, Q_TILE), 2 ** 30, I32)

    def pos_of(c, shape):
        r = lax.broadcasted_iota(I32, shape, 0)
        return jnp.where(c < 0, r, r + N_META + c * K_CHUNK)

    @pl.when(any_tie > 0)
    def _():
        cnt_gt = count(lambda blk, c: blk > thr)
        need = TOPK_MAX - cnt_gt

        def bis_pos(i, cs):
            cand = cs | lax.shift_left(jnp.int32(1), 11 - i)
            f = count(lambda blk, c: (blk == thr) & (pos_of(c, blk.shape) < cand))
            return jnp.where(f <= need, cand, cs)

        cs = lax.fori_loop(0, 12, bis_pos, jnp.zeros((1, Q_TILE), I32))
        cs = jnp.where(tie, cs, 2 ** 30)
        cst_ref[...] = jnp.broadcast_to(cs, (8, Q_TILE))

    def to_mask(blk, c, with_ties):
        if with_ties:
            cs = cst_ref[0:1, :]
            sel = (blk > thr) | ((blk == thr) & (pos_of(c, blk.shape) < cs))
        else:
            sel = blk >= thr
        return jnp.where(sel, 0.0, NEG)

    for with_ties in (False, True):
        @pl.when((any_tie > 0) == with_ties)
        def _():
            def body(c, carry):
                rows = chunk_rows(c)
                sc_ref[rows, :] = to_mask(sc_ref[rows, :], c, with_ties)
                return carry
            lax.fori_loop(0, nchunks, body, 0)
            scm_ref[...] = to_mask(scm_ref[...], -1, with_ties)

    kindm = jnp.minimum(j, 1)
    heads = [slice(h * HEAD_DIM, (h + 1) * HEAD_DIM) for h in range(N_HEADS)]

    mx0 = []
    for h in range(N_HEADS):
        lm = (lax.dot_general(km_ref[h], q_ref[h], NT_DIMS, preferred_element_type=F32)
              + tabm_ref[kindm, h] + scm_ref[...])
        lm_ref[h] = lm
        mx0.append(_fold_rows(lm, jnp.max))

    def pass_a(c, mx):
        rows = chunk_rows(c)
        kind = jnp.minimum(j - c, 2)
        mask = sc_ref[rows, :]
        out = []
        for h in range(N_HEADS):
            l = (lax.dot_general(k_ref[h, rows, :], q_ref[h], NT_DIMS, preferred_element_type=F32)
                 + tab_ref[kind, h] + mask)
            l_ref[h, rows, :] = l
            out.append(jnp.maximum(mx[h], _fold_rows(l, jnp.max)))
        return tuple(out)

    mx = lax.fori_loop(0, nchunks, pass_a, tuple(mx0))
    m = [jnp.max(mx[h], axis=0, keepdims=True) for h in range(N_HEADS)]

    den0 = []
    for h in range(N_HEADS):
        pm = jnp.exp(lm_ref[h] - m[h])
        den0.append(_fold_rows(pm, jnp.sum))
        ot_ref[heads[h], :] = jnp.dot(vtm_ref[heads[h], :], pm.astype(BF16), preferred_element_type=F32)

    def pass_b(c, den):
        rows = chunk_rows(c)
        out = []
        for h in range(N_HEADS):
            p = jnp.exp(l_ref[h, rows, :] - m[h])
            out.append(den[h] + _fold_rows(p, jnp.sum))
            ot_ref[heads[h], :] += jnp.dot(vt_ref[heads[h], rows], p.astype(BF16),
                                           preferred_element_type=F32)
        return tuple(out)

    den = lax.fori_loop(0, nchunks, pass_b, tuple(den0))
    for h in range(N_HEADS):
        ot_ref[heads[h], :] = ot_ref[heads[h], :] / jnp.sum(den[h], axis=0, keepdims=True)
    o_ref[...] = ot_ref[...].T.astype(BF16)


def _attention(q3, iq3, iwt, k3, ik, vt, km3, ikm, vtm, tab, tabm, batch, seq):
    n = batch * seq
    tiles = seq // Q_TILE
    full = lambda a: pl.BlockSpec(a.shape, lambda b, j: (0,) * a.ndim, pipeline_mode=pl.Buffered(1))
    return pl.pallas_call(
        _attn_kernel,
        grid=(batch, tiles),
        in_specs=[
            pl.BlockSpec((N_HEADS, Q_TILE, HEAD_DIM), lambda b, j: (0, b * tiles + j, 0)),
            pl.BlockSpec((IDX_HEADS, Q_TILE, IDX_DIM), lambda b, j: (0, b * tiles + j, 0)),
            pl.BlockSpec((IDX_HEADS, Q_TILE), lambda b, j: (0, b * tiles + j)),
            pl.BlockSpec((N_HEADS, seq, HEAD_DIM), lambda b, j: (0, b, 0)),
            pl.BlockSpec((seq, IDX_DIM), lambda b, j: (b, 0)),
            pl.BlockSpec((ATTN_WIDTH, seq), lambda b, j: (0, b)),
            full(km3), full(ikm), full(vtm), full(tab), full(tabm),
        ],
        out_specs=pl.BlockSpec((Q_TILE, ATTN_WIDTH), lambda b, j: (b * tiles + j, 0)),
        out_shape=jax.ShapeDtypeStruct((n, ATTN_WIDTH), BF16),
        scratch_shapes=[
            pltpu.VMEM((seq, Q_TILE), F32),
            pltpu.VMEM((N_META, Q_TILE), F32),
            pltpu.VMEM((N_HEADS, seq, Q_TILE), F32),
            pltpu.VMEM((N_HEADS, N_META, Q_TILE), F32),
            pltpu.VMEM((ATTN_WIDTH, Q_TILE), F32),
            pltpu.VMEM((8, Q_TILE), I32),
        ],
        compiler_params=pltpu.CompilerParams(dimension_semantics=("arbitrary", "arbitrary"),
                                             vmem_limit_bytes=VMEM_LIMIT),
        name="attn",
    )(q3, iq3, iwt, k3, ik, vt, km3, ikm, vtm, tab, tabm)


def _mix_kernel(x_ref, attn_ref, y_ref, yprev_ref, ymeta_ref, gate_ref,
                cw_ref, cb_ref, lng_ref, lnb_ref, wco_ref, wao_ref, wout_ref, nf_ref, wrt_ref, br_ref,
                h2_ref, hn2_ref, eid_ref, rank_ref, gcol_ref, cnt_ref,
                win_ref, base_ref, *, tiles_per_seq):
    i = pl.program_id(0)

    @pl.when(i == 0)
    def _():
        base_ref[...] = jnp.zeros_like(base_ref)

    first = (i % tiles_per_seq) == 0
    win_ref[0:HALO, :] = jnp.where(first, ymeta_ref[...], yprev_ref[...])
    win_ref[HALO:, :] = y_ref[...]
    acc = jnp.zeros((ROW_TILE, CONV_CH), F32)
    lead = HALO - (CONV_WIDTH - 1)
    for b in range(8):
        taps = [w for w in range(b, CONV_WIDTH, 8)]
        span = ROW_TILE + 8 * (len(taps) - 1)
        yb = win_ref[pl.ds(lead + b, span), :]
        for a, w in enumerate(taps):
            acc = acc + cw_ref[w:w + 1, :] * yb[8 * a:8 * a + ROW_TILE, :]
    yc = acc + cb_ref[...]
    mu = jnp.mean(yc, axis=-1, keepdims=True)
    var = jnp.mean(jnp.square(yc - mu), axis=-1, keepdims=True)
    yn = (yc - mu) * lax.rsqrt(var + EPS) * lng_ref[...] + lnb_ref[...]
    ys = yn * _sigmoid(yn)
    y_b = jnp.dot(ys.astype(BF16), wco_ref[...], preferred_element_type=F32)

    y_a = jnp.dot(attn_ref[...], wao_ref[...], preferred_element_type=F32)
    merged = gate_ref[:, :D_MODEL] * y_a + gate_ref[:, D_MODEL:] * y_b
    h2 = x_ref[...] + jnp.dot(merged.astype(BF16), wout_ref[...], preferred_element_type=F32)
    h2_ref[...] = h2
    ms = jnp.mean(h2 * h2, axis=-1, keepdims=True)
    hn2 = h2 * lax.rsqrt(ms + EPS) * nf_ref[...]
    hn2_ref[...] = _pack_bf16_pairs(hn2)

    logits = lax.dot_general(wrt_ref[...], hn2, NT_DIMS, preferred_element_type=F32,
                             precision=lax.Precision.HIGHEST) + br_ref[...]
    erow = lax.broadcasted_iota(I32, (N_EXPERTS, ROW_TILE), 0)
    vals, ids = [], []
    l = logits
    for _ in range(TOP_K_EXPERTS):
        m = jnp.max(l, axis=0, keepdims=True)
        idx = jnp.min(jnp.where(l == m, erow, N_EXPERTS), axis=0, keepdims=True)
        vals.append(m)
        ids.append(idx)
        l = jnp.where(erow == idx, -jnp.inf, l)
    ex = [jnp.exp(v - vals[0]) for v in vals]
    den = ex[0] + ex[1] + ex[2] + ex[3]
    gates = [e / den for e in ex]

    onehot = [(erow == idx) for idx in ids]
    oh = jnp.concatenate([jnp.where(o, 1.0, 0.0) for o in onehot], axis=0)
    tr = lax.broadcasted_iota(I32, (ROW_TILE, ROW_TILE), 0)
    tc = lax.broadcasted_iota(I32, (ROW_TILE, ROW_TILE), 1)
    upper = jnp.where(tr <= tc, 1.0, 0.0).astype(BF16)
    pref = jnp.dot(oh.astype(BF16), upper, preferred_element_type=F32)
    offs = base_ref[:, 0:1]
    ranks = []
    for kk in range(TOP_K_EXPERTS):
        pk = pref[kk * N_EXPERTS:(kk + 1) * N_EXPERTS, :]
        r = jnp.sum(jnp.where(onehot[kk], offs + pk - 1.0, 0.0), axis=0, keepdims=True)
        ranks.append(r.astype(I32))
        offs = offs + pk[:, ROW_TILE - 1:ROW_TILE]
    base_ref[...] = jnp.broadcast_to(offs, base_ref.shape)
    cnt_ref[...] = jnp.broadcast_to(offs, cnt_ref.shape)

    zi = jnp.zeros((8 - TOP_K_EXPERTS, ROW_TILE), I32)
    eid_ref[...] = jnp.concatenate(ids + [zi], axis=0)
    rank_ref[...] = jnp.concatenate(ranks + [zi], axis=0)
    g8 = jnp.concatenate(gates + [jnp.zeros((128 - TOP_K_EXPERTS, ROW_TILE), F32)], axis=0)
    gcol_ref[...] = g8.T


def _mix(x2d, attn, y, ymeta, gate, wts, seq):
    n = x2d.shape[0]
    tiles_per_seq = seq // ROW_TILE
    halo_per_tile = ROW_TILE // HALO
    full = lambda a: pl.BlockSpec(a.shape, lambda i: (0,) * a.ndim)
    row = lambda w: pl.BlockSpec((ROW_TILE, w), lambda i: (i, 0))
    lane = lambda r: pl.BlockSpec((r, ROW_TILE), lambda i: (0, i))
    out_shape = (
        jax.ShapeDtypeStruct((n, D_MODEL), F32),
        jax.ShapeDtypeStruct((n, D_MODEL // 2), I32),
        jax.ShapeDtypeStruct((8, n), I32),
        jax.ShapeDtypeStruct((8, n), I32),
        jax.ShapeDtypeStruct((n, 128), F32),
        jax.ShapeDtypeStruct((N_EXPERTS, 128), F32),
    )
    out_specs = (row(D_MODEL), row(D_MODEL // 2), lane(8), lane(8), row(128), full(out_shape[5]))
    return pl.pallas_call(
        functools.partial(_mix_kernel, tiles_per_seq=tiles_per_seq),
        grid=(n // ROW_TILE,),
        in_specs=[row(D_MODEL), row(ATTN_WIDTH), row(CONV_CH),
                  pl.BlockSpec((HALO, CONV_CH), lambda i: (jnp.maximum(i * halo_per_tile - 1, 0), 0)),
                  full(ymeta), row(2 * D_MODEL)] + [full(w) for w in wts],
        out_specs=out_specs,
        out_shape=out_shape,
        scratch_shapes=[pltpu.VMEM((HALO + ROW_TILE, CONV_CH), F32),
                        pltpu.VMEM((N_EXPERTS, 128), F32)],
        compiler_params=pltpu.CompilerParams(dimension_semantics=("arbitrary",),
                                             vmem_limit_bytes=VMEM_LIMIT),
        name="mix",
    )(x2d, attn, y, y, ymeta, gate, *wts)


def _slots_kernel(starts_ref, eid_ref, rank_ref, slot_ref):
    eid = eid_ref[...]
    base = jnp.zeros(eid.shape, I32)
    for e in range(N_EXPERTS):
        base = jnp.where(eid == e, starts_ref[e], base)
    slot_ref[...] = base + rank_ref[...]


def _slots(starts, eid8, rank8):
    n = eid8.shape[1]
    cols = 2048
    spec = pl.BlockSpec((8, cols), lambda i: (0, i))
    return pl.pallas_call(
        _slots_kernel,
        grid=(n // cols,),
        in_specs=[pl.BlockSpec(memory_space=pltpu.SMEM), spec, spec],
        out_specs=spec,
        out_shape=jax.ShapeDtypeStruct((8, n), I32),
        compiler_params=pltpu.CompilerParams(dimension_semantics=("arbitrary",)),
        name="slots",
    )(starts, eid8, rank8)


COPIES_PER_TILE = TOP_K_EXPERTS * ROW_TILE
SLOT_UNROLL = 16
EXPERT_PIECES = 4


def _slot_source_kernel(slot_ref, init_ref, src_ref, sem, *, n):
    i = pl.program_id(0)

    @pl.when(i == 0)
    def _():
        cp = pltpu.make_async_copy(init_ref, src_ref, sem)
        cp.start()
        cp.wait()

    for kk in range(TOP_K_EXPERTS):
        def body(g, carry):
            for u in range(SLOT_UNROLL):
                t = g * SLOT_UNROLL + u
                src_ref[slot_ref[0, 0, kk * ROW_TILE + t]] = kk * n + i * ROW_TILE + t
            return carry
        lax.fori_loop(0, ROW_TILE // SLOT_UNROLL, body, 0)


def _slot_sources(slot_tiles, init, n):
    return pl.pallas_call(
        functools.partial(_slot_source_kernel, n=n),
        grid=(slot_tiles.shape[0],),
        in_specs=[pl.BlockSpec((1, 1, COPIES_PER_TILE), lambda i: (i, 0, 0), memory_space=pltpu.SMEM),
                  pl.BlockSpec(memory_space=pl.ANY)],
        out_specs=pl.BlockSpec(memory_space=pltpu.SMEM),
        out_shape=jax.ShapeDtypeStruct(init.shape, I32),
        scratch_shapes=[pltpu.SemaphoreType.DMA(())],
        compiler_params=pltpu.CompilerParams(dimension_semantics=("arbitrary",)),
        name="slot_sources",
    )(slot_tiles, init)


def _expert_kernel(te_ref, nused_ref, src_ref, src_next_ref, src_prev_ref, hn_hbm,
                   wug_ref, bug_ref, wd_ref, bd_ref, yk_hbm, xbuf, ybuf, wug_bf, wd_bf, xb_ref, acc_ref, gsem, ssem, *, n):
    i = pl.program_id(0)
    nused = nused_ref[0]

    def gather_copy(src, r, b):
        tok = src[0, 0, r] & (n - 1)
        return pltpu.make_async_copy(hn_hbm.at[pl.ds(tok, 1), :], xbuf.at[b, pl.ds(r, 1), :], gsem.at[b])

    def scatter_copy(src, r, b):
        return pltpu.make_async_copy(ybuf.at[b, pl.ds(r, 1), :], yk_hbm.at[pl.ds(src[0, 0, r], 1), :],
                                     ssem.at[b])

    @pl.when(i <= nused)
    def _():
        cur = i % 2
        oth = 1 - cur

        def wait_scatter(b):
            pltpu.make_async_copy(ybuf.at[b], yk_hbm.at[pl.ds(0, ROW_TILE), :], ssem.at[b]).wait()

        @pl.when(i == 0)
        def _():
            for r in range(ROW_TILE):
                gather_copy(src_ref, r, 0).start()
            ybuf[1] = jnp.zeros((ROW_TILE, D_MODEL), F32)
            spare = pltpu.make_async_copy(ybuf.at[1], yk_hbm.at[pl.ds(TOP_K_EXPERTS * n, ROW_TILE), :],
                                          ssem.at[1])
            spare.start()
            spare.wait()

        pltpu.make_async_copy(hn_hbm.at[pl.ds(0, ROW_TILE), :], xbuf.at[cur], gsem.at[cur]).wait()

        prev = te_ref[jnp.maximum(i - 1, 0)]
        fresh = (i == 0) | (te_ref[jnp.minimum(i, nused - 1)] != prev)

        @pl.when(fresh)
        def _():
            wug_bf[...] = wug_ref[0].astype(BF16)
            wd_bf[...] = wd_ref[0].astype(BF16)

        xb_ref[...] = _unpack_bf16_pairs(xbuf[cur])

        rows_per_piece = ROW_TILE // EXPERT_PIECES
        ff = D_FF // EXPERT_PIECES
        for c in range(EXPERT_PIECES):
            piece_rows = range(c * rows_per_piece, (c + 1) * rows_per_piece)

            @pl.when(i < nused)
            def _():
                for r in piece_rows:
                    gather_copy(src_next_ref, r, oth).start(priority=r % 2)

            @pl.when(i > 0)
            def _():
                for r in piece_rows:
                    scatter_copy(src_prev_ref, r, oth).start(priority=r % 2)

            gs = slice(c * ff, (c + 1) * ff)
            us = slice(D_FF + c * ff, D_FF + (c + 1) * ff)
            xb = xb_ref[...]
            gate = jnp.minimum(jnp.dot(xb, wug_bf[:, gs], preferred_element_type=F32) + bug_ref[0, :, gs],
                               SWIGLU_LIMIT)
            up = jnp.clip(jnp.dot(xb, wug_bf[:, us], preferred_element_type=F32) + bug_ref[0, :, us],
                          -SWIGLU_LIMIT, SWIGLU_LIMIT)
            act = (up + 1.0) * gate * _sigmoid(SWIGLU_ALPHA * gate)
            part = jnp.dot(act.astype(BF16), wd_bf[gs, :], preferred_element_type=F32)
            if c == 0:
                acc_ref[...] = part + bd_ref[0]
            else:
                acc_ref[...] += part

        @pl.when(i > 1)
        def _():
            wait_scatter(cur)

        ybuf[cur] = acc_ref[...]

        @pl.when((i == nused) & (i > 0))
        def _():
            wait_scatter(oth)


def _experts(tile_expert, nused, src_tiles, hn2p, w_ug, b_ug, w_down, b_down, n):
    ntiles = src_tiles.shape[0]
    clamp = lambda i, nu: jnp.maximum(jnp.minimum(i, nu[0] - 1), 0)
    smem = lambda f: pl.BlockSpec((1, 1, ROW_TILE), f, memory_space=pltpu.SMEM)
    expert = lambda i, te, nu: (te[clamp(i, nu)], 0, 0)
    grid_spec = pltpu.PrefetchScalarGridSpec(
        num_scalar_prefetch=2,
        grid=(ntiles + 1,),
        in_specs=[
            smem(lambda i, te, nu: (clamp(i, nu), 0, 0)),
            smem(lambda i, te, nu: (clamp(i + 1, nu), 0, 0)),
            smem(lambda i, te, nu: (clamp(i - 1, nu), 0, 0)),
            pl.BlockSpec(memory_space=pl.ANY),
            pl.BlockSpec((1, D_MODEL, 2 * D_FF), expert),
            pl.BlockSpec((1, 1, 2 * D_FF), expert),
            pl.BlockSpec((1, D_FF, D_MODEL), expert),
            pl.BlockSpec((1, 1, D_MODEL), expert),
        ],
        out_specs=pl.BlockSpec(memory_space=pl.ANY),
        scratch_shapes=[pltpu.VMEM((2, ROW_TILE, D_MODEL // 2), I32),
                        pltpu.VMEM((2, ROW_TILE, D_MODEL), F32),
                        pltpu.VMEM((D_MODEL, 2 * D_FF), BF16), pltpu.VMEM((D_FF, D_MODEL), BF16),
                        pltpu.VMEM((ROW_TILE, D_MODEL), BF16), pltpu.VMEM((ROW_TILE, D_MODEL), F32),
                        pltpu.SemaphoreType.DMA((2,)), pltpu.SemaphoreType.DMA((2,))],
    )
    return pl.pallas_call(
        functools.partial(_expert_kernel, n=n),
        grid_spec=grid_spec,
        out_shape=jax.ShapeDtypeStruct((TOP_K_EXPERTS * n + ROW_TILE, D_MODEL), F32),
        compiler_params=pltpu.CompilerParams(dimension_semantics=("arbitrary",),
                                             vmem_limit_bytes=VMEM_LIMIT),
        name="experts",
    )(tile_expert, nused, src_tiles, src_tiles, src_tiles, hn2p,
      w_ug, b_ug[:, None, :], w_down, b_down[:, None, :])


def _final_kernel(h2_ref, gcol_ref, nf_ref, *rest):
    y_refs, o_ref = rest[:TOP_K_EXPERTS], rest[TOP_K_EXPERTS]
    h = h2_ref[...]
    for kk in range(TOP_K_EXPERTS):
        h = h + gcol_ref[:, kk:kk + 1] * y_refs[kk][...]
    ms = jnp.mean(h * h, axis=-1, keepdims=True)
    o_ref[...] = h * lax.rsqrt(ms + EPS) * nf_ref[...]


def _final(h2, gcol, nf, yk):
    n = h2.shape[0]
    ntiles = n // ROW_TILE
    choice = lambda kk: pl.BlockSpec((ROW_TILE, D_MODEL), lambda i: (kk * ntiles + i, 0))
    return pl.pallas_call(
        _final_kernel,
        grid=(ntiles,),
        in_specs=[pl.BlockSpec((ROW_TILE, D_MODEL), lambda i: (i, 0)),
                  pl.BlockSpec((ROW_TILE, 128), lambda i: (i, 0)),
                  pl.BlockSpec((1, D_MODEL), lambda i: (0, 0))] + [choice(kk) for kk in range(TOP_K_EXPERTS)],
        out_specs=pl.BlockSpec((ROW_TILE, D_MODEL), lambda i: (i, 0)),
        out_shape=jax.ShapeDtypeStruct((n, D_MODEL), F32),
        compiler_params=pltpu.CompilerParams(dimension_semantics=("arbitrary",),
                                             vmem_limit_bytes=VMEM_LIMIT),
        name="final",
    )(h2, gcol, nf, *([yk] * TOP_K_EXPERTS))


def _split_w_in(w_in, b_gate):
    c = ATTN_WIDTH
    o = 0
    wq = w_in[:, o:o + c]; o += c
    wk = w_in[:, o:o + c]; o += c
    wv = w_in[:, o:o + c]; o += c
    wiq = w_in[:, o:o + IDX_HEADS * IDX_DIM]; o += IDX_HEADS * IDX_DIM
    wik = w_in[:, o:o + IDX_DIM]; o += IDX_DIM
    wiw = w_in[:, o:o + IDX_HEADS]; o += IDX_HEADS
    wglu = w_in[:, o:o + 2 * CONV_CH]; o += 2 * CONV_CH
    wgate = w_in[:, o:]
    wiwt = jnp.concatenate([wiw.T, jnp.zeros((16 - IDX_HEADS, D_MODEL), w_in.dtype)], axis=0)
    bf = lambda a: a.astype(BF16)
    return (bf(wq), bf(wk), bf(wiq), bf(wik), bf(wv.T), bf(wiwt), bf(wglu), bf(wgate),
            b_gate[None, :].astype(F32))


def kernel(x, meta_tokens, rel_bias, norm_mix, w_in, b_gate, w_attn_out, conv_w, conv_b, conv_ln_g, conv_ln_b, w_conv_out, w_out, norm_ffn, w_router, b_router, w_up_gate, b_up_gate, w_down, b_down, norm_final):
    batch, seq, d = x.shape
    n = batch * seq
    x2d = x.reshape(n, d)

    wts = _split_w_in(w_in[0], b_gate[0])
    g_mix = norm_mix[0][None, :]
    q3, k3, iq3, ik, vt, iwt, y, gate = _project(x2d, g_mix, wts, ROW_TILE)
    _, km3, _, ikm, vtm, _, ym, _ = _project(meta_tokens.astype(F32), g_mix, wts, N_META)

    tab, tabm = _bias_tables(rel_bias.astype(F32))
    attn = _attention(q3, iq3, iwt, k3, ik, vt, km3, ikm, vtm, tab, tabm, batch, seq)

    ymeta = jnp.concatenate([jnp.zeros((HALO - N_META, CONV_CH), F32), ym], axis=0)
    cw = jnp.concatenate([conv_w[0], jnp.zeros((32 - CONV_WIDTH, CONV_CH), F32)], axis=0)
    mix_w = (cw, conv_b[0][None, :], conv_ln_g[0][None, :], conv_ln_b[0][None, :],
             w_conv_out[0].astype(BF16), w_attn_out[0].astype(BF16), w_out[0].astype(BF16),
             norm_ffn[0][None, :], w_router[0].T, b_router[0][:, None])
    h2, hn2, eid8, rank8, gcol, cnt = _mix(x2d, attn, y, ymeta, gate, mix_w, seq)

    counts = cnt[:, 0].astype(I32)
    padded = ((counts + ROW_TILE - 1) // ROW_TILE) * ROW_TILE
    ends = jnp.cumsum(padded)
    starts = ends - padded
    slot8 = _slots(starts, eid8, rank8)
    slot_tiles = (slot8[:TOP_K_EXPERTS].reshape(TOP_K_EXPERTS, n // ROW_TILE, ROW_TILE)
                  .transpose(1, 0, 2).reshape(n // ROW_TILE, 1, COPIES_PER_TILE))
    nslots = n * TOP_K_EXPERTS + N_EXPERTS * ROW_TILE
    ntiles = nslots // ROW_TILE
    tile_start = jnp.arange(ntiles, dtype=I32) * ROW_TILE
    nused = (ends[-1] // ROW_TILE).astype(I32)
    last_start = jnp.maximum(ends[-1] - ROW_TILE, 0)
    tile_expert = jnp.sum((jnp.minimum(tile_start, last_start)[:, None] >= ends[None, :]).astype(I32), axis=1)

    spare = TOP_K_EXPERTS * n + jnp.arange(nslots, dtype=I32) % ROW_TILE
    src_tiles = _slot_sources(slot_tiles, spare, n).reshape(ntiles, 1, ROW_TILE)
    yk = _experts(tile_expert, nused[None], src_tiles, hn2, w_up_gate[0], b_up_gate[0], w_down[0], b_down[0], n)
    out = _final(h2, gcol, norm_final[None, :], yk)
    return out.reshape(batch, seq, d)
```

```python
import functools
import math

import jax
import jax.numpy as jnp
from jax import lax
from jax.experimental import pallas as pl
from jax.experimental.pallas import tpu as pltpu

F32 = jnp.float32
BF16 = jnp.bfloat16
I32 = jnp.int32

D_MODEL = 1024
N_META = 16
N_HEADS = 8
HEAD_DIM = 64
ATTN_WIDTH = N_HEADS * HEAD_DIM
IDX_HEADS = 8
IDX_DIM = 64
TOPK_MAX = 256
CONV_CH = 512
CONV_WIDTH = 31
N_BUCKETS = 32
MAX_DISTANCE = 128
N_EXPERTS = 32
TOP_K_EXPERTS = 4
D_FF = 1024
SWIGLU_LIMIT = 7.0
SWIGLU_ALPHA = 1.702
EPS = 1e-6
IDX_SCALE = (IDX_DIM ** -0.5) * (IDX_HEADS ** -0.5)

ROW_TILE = 256
Q_TILE = 256
K_CHUNK = 256
HALO = 32
NEG = -1e30
VMEM_LIMIT = 56 * 1024 * 1024

NT_DIMS = (((1,), (1,)), ((), ()))


def _sigmoid(x):
    return 1.0 / (1.0 + jnp.exp(-x))


def _pack_bf16_pairs(x):
    w = x.shape[1] // 2
    hi = lax.bitcast_convert_type(x[:, :w].astype(BF16).astype(F32), jnp.uint32)
    lo = lax.bitcast_convert_type(x[:, w:].astype(BF16).astype(F32), jnp.uint32)
    return lax.bitcast_convert_type(hi | (lo >> 16), I32)


def _unpack_bf16_pairs(p):
    u = lax.bitcast_convert_type(p, jnp.uint32)
    hi = lax.bitcast_convert_type(u & jnp.uint32(0xFFFF0000), F32)
    lo = lax.bitcast_convert_type(u << 16, F32)
    return jnp.concatenate([hi, lo], axis=1).astype(BF16)


def _proj_kernel(x_ref, g_ref, wq_ref, wk_ref, wiq_ref, wik_ref, wvt_ref, wiwt_ref, wglu_ref,
                 wgate_ref, bgate_ref,
                 q_ref, k_ref, iq_ref, ik_ref, vt_ref, iwt_ref, y_ref, gate_ref):
    x = x_ref[...]
    ms = jnp.mean(x * x, axis=-1, keepdims=True)
    xn = (x * lax.rsqrt(ms + EPS) * g_ref[...]).astype(BF16)

    q = jnp.dot(xn, wq_ref[...], preferred_element_type=F32) * (HEAD_DIM ** -0.5)
    k = jnp.dot(xn, wk_ref[...], preferred_element_type=F32)
    iq = jnp.dot(xn, wiq_ref[...], preferred_element_type=F32)
    for h in range(N_HEADS):
        sl = slice(h * HEAD_DIM, (h + 1) * HEAD_DIM)
        q_ref[h] = q[:, sl].astype(BF16)
        k_ref[h] = k[:, sl].astype(BF16)
        iq_ref[h] = iq[:, sl].astype(BF16)
    ik_ref[...] = jnp.dot(xn, wik_ref[...], preferred_element_type=F32).astype(BF16)
    vt_ref[...] = lax.dot_general(wvt_ref[...], xn, NT_DIMS, preferred_element_type=F32).astype(BF16)
    iwt = lax.dot_general(wiwt_ref[...], xn, NT_DIMS, preferred_element_type=F32)
    iwt_ref[...] = iwt[:IDX_HEADS] * IDX_SCALE
    glu = jnp.dot(xn, wglu_ref[...], preferred_element_type=F32)
    y_ref[...] = glu[:, :CONV_CH] * _sigmoid(glu[:, CONV_CH:])
    gate = jnp.dot(xn, wgate_ref[...], preferred_element_type=F32) + bgate_ref[...]
    gate_ref[...] = _sigmoid(gate)


def _project(x2d, g, wts, tm):
    n = x2d.shape[0]
    wq, wk, wiq, wik, wvt, wiwt, wglu, wgate, bgate = wts
    full = lambda a: pl.BlockSpec(a.shape, lambda i: (0,) * a.ndim)
    out_shape = (
        jax.ShapeDtypeStruct((N_HEADS, n, HEAD_DIM), BF16),
        jax.ShapeDtypeStruct((N_HEADS, n, HEAD_DIM), BF16),
        jax.ShapeDtypeStruct((IDX_HEADS, n, IDX_DIM), BF16),
        jax.ShapeDtypeStruct((n, IDX_DIM), BF16),
        jax.ShapeDtypeStruct((ATTN_WIDTH, n), BF16),
        jax.ShapeDtypeStruct((IDX_HEADS, n), F32),
        jax.ShapeDtypeStruct((n, CONV_CH), F32),
        jax.ShapeDtypeStruct((n, 2 * D_MODEL), F32),
    )
    out_specs = (
        pl.BlockSpec((N_HEADS, tm, HEAD_DIM), lambda i: (0, i, 0)),
        pl.BlockSpec((N_HEADS, tm, HEAD_DIM), lambda i: (0, i, 0)),
        pl.BlockSpec((IDX_HEADS, tm, IDX_DIM), lambda i: (0, i, 0)),
        pl.BlockSpec((tm, IDX_DIM), lambda i: (i, 0)),
        pl.BlockSpec((ATTN_WIDTH, tm), lambda i: (0, i)),
        pl.BlockSpec((IDX_HEADS, tm), lambda i: (0, i)),
        pl.BlockSpec((tm, CONV_CH), lambda i: (i, 0)),
        pl.BlockSpec((tm, 2 * D_MODEL), lambda i: (i, 0)),
    )
    return pl.pallas_call(
        _proj_kernel,
        grid=(n // tm,),
        in_specs=[pl.BlockSpec((tm, D_MODEL), lambda i: (i, 0)), full(g), full(wq), full(wk), full(wiq),
                  full(wik), full(wvt), full(wiwt), full(wglu), full(wgate), full(bgate)],
        out_specs=out_specs,
        out_shape=out_shape,
        compiler_params=pltpu.CompilerParams(dimension_semantics=("arbitrary",),
                                             vmem_limit_bytes=VMEM_LIMIT),
        name="proj",
    )(x2d, g, wq, wk, wiq, wik, wvt, wiwt, wglu, wgate, bgate)


def _t5_bucket(n):
    max_exact = N_BUCKETS // 2
    nf = jnp.maximum(n, 1).astype(F32)
    large = max_exact + (jnp.log(nf / max_exact) / math.log(MAX_DISTANCE / max_exact)
                         * (N_BUCKETS - max_exact)).astype(I32)
    large = jnp.minimum(large, N_BUCKETS - 1)
    return jnp.where(n < max_exact, n, large)


def _bias_lookup(rb_ref, dist, h):
    bucket = _t5_bucket(jnp.maximum(dist, 0))
    out = jnp.full(dist.shape, NEG, F32)
    for b in range(N_BUCKETS):
        out = jnp.where(bucket == b, rb_ref[b, h], out)
    return jnp.where(dist >= 0, out, NEG)


def _bias_kernel(rb_ref, tab_ref, tabm_ref):
    kind = pl.program_id(0)
    r = pl.program_id(1)
    rows = tab_ref.shape[2]
    s = lax.broadcasted_iota(I32, (rows, Q_TILE), 0) + r * rows
    t = lax.broadcasted_iota(I32, (rows, Q_TILE), 1)
    dist = jnp.where(kind == 2, 2 * K_CHUNK, t - s + kind * K_CHUNK)
    for h in range(N_HEADS):
        tab_ref[0, h] = _bias_lookup(rb_ref, dist, h)
    m = lax.broadcasted_iota(I32, (N_META, Q_TILE), 0)
    tm_ = lax.broadcasted_iota(I32, (N_META, Q_TILE), 1)
    distm = jnp.where(kind == 0, N_META + tm_ - m, 2 * K_CHUNK)
    for h in range(N_HEADS):
        tabm_ref[0, h] = _bias_lookup(rb_ref, distm, h)


def _bias_tables(rel_bias):
    rows = 64
    return pl.pallas_call(
        _bias_kernel,
        grid=(3, K_CHUNK // rows),
        in_specs=[pl.BlockSpec(memory_space=pltpu.SMEM)],
        out_specs=(pl.BlockSpec((1, N_HEADS, rows, Q_TILE), lambda kd, r: (kd, 0, r, 0)),
                   pl.BlockSpec((1, N_HEADS, N_META, Q_TILE), lambda kd, r: (kd, 0, 0, 0))),
        out_shape=(jax.ShapeDtypeStruct((3, N_HEADS, K_CHUNK, Q_TILE), F32),
                   jax.ShapeDtypeStruct((3, N_HEADS, N_META, Q_TILE), F32)),
        compiler_params=pltpu.CompilerParams(dimension_semantics=("arbitrary", "arbitrary")),
        name="bias_tables",
    )(rel_bias)


def _order_bits_to_float(u):
    bits = jnp.where(u < 0, u ^ jnp.int32(-2 ** 31), ~u)
    return lax.bitcast_convert_type(bits, F32)


def _fold_rows(x, op):
    r, l = x.shape
    x3 = x.reshape(r // 8, 8, l)
    return op(x3, axis=0)


def _attn_kernel(q_ref, iq_ref, iwt_ref, k_ref, ik_ref, vt_ref, km_ref, ikm_ref, vtm_ref,
                 tab_ref, tabm_ref, o_ref, sc_ref, scm_ref, l_ref, lm_ref, ot_ref, cst_ref):
    j = pl.program_id(1)
    nchunks = j + 1
    iw = iwt_ref[...]

    def chunk_rows(c):
        return pl.ds(pl.multiple_of(c * K_CHUNK, K_CHUNK), K_CHUNK)

    def idx_scores(ikc):
        acc = None
        for h in range(IDX_HEADS):
            s = lax.dot_general(ikc, iq_ref[h], NT_DIMS, preferred_element_type=F32)
            term = jnp.maximum(s, 0.0) * iw[h:h + 1, :]
            acc = term if acc is None else acc + term
        return acc

    scm_ref[...] = idx_scores(ikm_ref[...])

    row_minus_col = (lax.broadcasted_iota(I32, (K_CHUNK, Q_TILE), 0)
                     - lax.broadcasted_iota(I32, (K_CHUNK, Q_TILE), 1))

    def p1(c, carry):
        rows = chunk_rows(c)
        future = row_minus_col > jnp.where(c == j, 0, K_CHUNK)
        sc_ref[rows, :] = jnp.where(future, -jnp.inf, idx_scores(ik_ref[rows, :]))
        return carry

    lax.fori_loop(0, nchunks, p1, 0)

    def count(pred):
        def body(c, acc):
            blk = sc_ref[chunk_rows(c), :]
            return acc + _fold_rows(jnp.where(pred(blk, c), 1, 0).astype(I32), jnp.sum)
        acc = lax.fori_loop(0, nchunks, body, jnp.zeros((8, Q_TILE), I32))
        acc = acc + _fold_rows(jnp.where(pred(scm_ref[...], -1), 1, 0).astype(I32), jnp.sum)
        return jnp.sum(acc, axis=0, keepdims=True)

    def bisect(i, u):
        cand = u | lax.shift_left(jnp.int32(1), 31 - i)
        cf = _order_bits_to_float(cand)
        cnt = count(lambda blk, c: blk >= cf)
        return jnp.where(cnt >= TOPK_MAX, cand, u)

    u = lax.fori_loop(0, 32, bisect, jnp.zeros((1, Q_TILE), I32))
    thr = jnp.where((u >= 0) & (u < 0x00800000), -jnp.inf, _order_bits_to_float(u))

    cnt_ge = count(lambda blk, c: blk >= thr)
    tie = (cnt_ge > TOPK_MAX) & (thr > -jnp.inf)
    any_tie = jnp.max(tie.astype(I32))
    cst_ref[...] = jnp.full((8, Q_TILE), 2 ** 30, I32)

    def pos_of(c, shape):
        r = lax.broadcasted_iota(I32, shape, 0)
        return jnp.where(c < 0, r, r + N_META + c * K_CHUNK)

    @pl.when(any_tie > 0)
    def _():
        cnt_gt = count(lambda blk, c: blk > thr)
        need = TOPK_MAX - cnt_gt

        def bis_pos(i, cs):
            cand = cs | lax.shift_left(jnp.int32(1), 11 - i)
            f = count(lambda blk, c: (blk == thr) & (pos_of(c, blk.shape) < cand))
            return jnp.where(f <= need, cand, cs)

        cs = lax.fori_loop(0, 12, bis_pos, jnp.zeros((1, Q_TILE), I32))
        cs = jnp.where(tie, cs, 2 ** 30)
        cst_ref[...] = jnp.broadcast_to(cs, (8, Q_TILE))

    def to_mask(blk, c, with_ties):
        if with_ties:
            cs = cst_ref[0:1, :]
            sel = (blk > thr) | ((blk == thr) & (pos_of(c, blk.shape) < cs))
        else:
            sel = blk >= thr
        return jnp.where(sel, 0.0, NEG)

    for with_ties in (False, True):
        @pl.when((any_tie > 0) == with_ties)
        def _():
            def body(c, carry):
                rows = chunk_rows(c)
                sc_ref[rows, :] = to_mask(sc_ref[rows, :], c, with_ties)
                return carry
            lax.fori_loop(0, nchunks, body, 0)
            scm_ref[...] = to_mask(scm_ref[...], -1, with_ties)

    kindm = jnp.minimum(j, 1)
    heads = [slice(h * HEAD_DIM, (h + 1) * HEAD_DIM) for h in range(N_HEADS)]

    mx0 = []
    for h in range(N_HEADS):
        lm = (lax.dot_general(km_ref[h], q_ref[h], NT_DIMS, preferred_element_type=F32)
              + tabm_ref[kindm, h] + scm_ref[...])
        lm_ref[h] = lm
        mx0.append(_fold_rows(lm, jnp.max))

    def pass_a(c, mx):
        rows = chunk_rows(c)
        kind = jnp.minimum(j - c, 2)
        mask = sc_ref[rows, :]
        out = []
        for h in range(N_HEADS):
            l = (lax.dot_general(k_ref[h, rows, :], q_ref[h], NT_DIMS, preferred_element_type=F32)
                 + tab_ref[kind, h] + mask)
            l_ref[h, rows, :] = l
            out.append(jnp.maximum(mx[h], _fold_rows(l, jnp.max)))
        return tuple(out)

    mx = lax.fori_loop(0, nchunks, pass_a, tuple(mx0))
    m = [jnp.max(mx[h], axis=0, keepdims=True) for h in range(N_HEADS)]

    den0 = []
    for h in range(N_HEADS):
        pm = jnp.exp(lm_ref[h] - m[h])
        den0.append(_fold_rows(pm, jnp.sum))
        ot_ref[heads[h], :] = jnp.dot(vtm_ref[heads[h], :], pm.astype(BF16), preferred_element_type=F32)

    def pass_b(c, den):
        rows = chunk_rows(c)
        out = []
        for h in range(N_HEADS):
            p = jnp.exp(l_ref[h, rows, :] - m[h])
            out.append(den[h] + _fold_rows(p, jnp.sum))
            ot_ref[heads[h], :] += jnp.dot(vt_ref[heads[h], rows], p.astype(BF16),
                                           preferred_element_type=F32)
        return tuple(out)

    den = lax.fori_loop(0, nchunks, pass_b, tuple(den0))
    for h in range(N_HEADS):
        ot_ref[heads[h], :] = ot_ref[heads[h], :] / jnp.sum(den[h], axis=0, keepdims=True)
    o_ref[...] = ot_ref[...].T.astype(BF16)


def _attention(q3, iq3, iwt, k3, ik, vt, km3, ikm, vtm, tab, tabm, batch, seq):
    n = batch * seq
    tiles = seq // Q_TILE
    full = lambda a: pl.BlockSpec(a.shape, lambda b, j: (0,) * a.ndim, pipeline_mode=pl.Buffered(1))
    return pl.pallas_call(
        _attn_kernel,
        grid=(batch, tiles),
        in_specs=[
            pl.BlockSpec((N_HEADS, Q_TILE, HEAD_DIM), lambda b, j: (0, b * tiles + j, 0)),
            pl.BlockSpec((IDX_HEADS, Q_TILE, IDX_DIM), lambda b, j: (0, b * tiles + j, 0)),
            pl.BlockSpec((IDX_HEADS, Q_TILE), lambda b, j: (0, b * tiles + j)),
            pl.BlockSpec((N_HEADS, seq, HEAD_DIM), lambda b, j: (0, b, 0)),
            pl.BlockSpec((seq, IDX_DIM), lambda b, j: (b, 0)),
            pl.BlockSpec((ATTN_WIDTH, seq), lambda b, j: (0, b)),
            full(km3), full(ikm), full(vtm), full(tab), full(tabm),
        ],
        out_specs=pl.BlockSpec((Q_TILE, ATTN_WIDTH), lambda b, j: (b * tiles + j, 0)),
        out_shape=jax.ShapeDtypeStruct((n, ATTN_WIDTH), BF16),
        scratch_shapes=[
            pltpu.VMEM((seq, Q_TILE), F32),
            pltpu.VMEM((N_META, Q_TILE), F32),
            pltpu.VMEM((N_HEADS, seq, Q_TILE), F32),
            pltpu.VMEM((N_HEADS, N_META, Q_TILE), F32),
            pltpu.VMEM((ATTN_WIDTH, Q_TILE), F32),
            pltpu.VMEM((8, Q_TILE), I32),
        ],
        compiler_params=pltpu.CompilerParams(dimension_semantics=("arbitrary", "arbitrary"),
                                             vmem_limit_bytes=VMEM_LIMIT),
        name="attn",
    )(q3, iq3, iwt, k3, ik, vt, km3, ikm, vtm, tab, tabm)


def _mix_kernel(x_ref, attn_ref, y_ref, yprev_ref, ymeta_ref, gate_ref,
                cw_ref, cb_ref, lng_ref, lnb_ref, wco_ref, wao_ref, wout_ref, nf_ref, wrt_ref, br_ref,
                h2_ref, hn2_ref, eid_ref, rank_ref, gcol_ref, cnt_ref,
                win_ref, base_ref, *, tiles_per_seq):
    i = pl.program_id(0)

    @pl.when(i == 0)
    def _():
        base_ref[...] = jnp.zeros_like(base_ref)

    first = (i % tiles_per_seq) == 0
    win_ref[0:HALO, :] = jnp.where(first, ymeta_ref[...], yprev_ref[...])
    win_ref[HALO:, :] = y_ref[...]
    acc = jnp.zeros((ROW_TILE, CONV_CH), F32)
    lead = HALO - (CONV_WIDTH - 1)
    for b in range(8):
        taps = [w for w in range(b, CONV_WIDTH, 8)]
        span = ROW_TILE + 8 * (len(taps) - 1)
        yb = win_ref[pl.ds(lead + b, span), :]
        for a, w in enumerate(taps):
            acc = acc + cw_ref[w:w + 1, :] * yb[8 * a:8 * a + ROW_TILE, :]
    yc = acc + cb_ref[...]
    mu = jnp.mean(yc, axis=-1, keepdims=True)
    var = jnp.mean(jnp.square(yc - mu), axis=-1, keepdims=True)
    yn = (yc - mu) * lax.rsqrt(var + EPS) * lng_ref[...] + lnb_ref[...]
    ys = yn * _sigmoid(yn)
    y_b = jnp.dot(ys.astype(BF16), wco_ref[...], preferred_element_type=F32)

    y_a = jnp.dot(attn_ref[...], wao_ref[...], preferred_element_type=F32)
    merged = gate_ref[:, :D_MODEL] * y_a + gate_ref[:, D_MODEL:] * y_b
    h2 = x_ref[...] + jnp.dot(merged.astype(BF16), wout_ref[...], preferred_element_type=F32)
    h2_ref[...] = h2
    ms = jnp.mean(h2 * h2, axis=-1, keepdims=True)
    hn2 = h2 * lax.rsqrt(ms + EPS) * nf_ref[...]
    hn2_ref[:, 0, :] = _pack_bf16_pairs(hn2)

    logits = lax.dot_general(wrt_ref[...], hn2, NT_DIMS, preferred_element_type=F32,
                             precision=lax.Precision.HIGHEST) + br_ref[...]
    erow = lax.broadcasted_iota(I32, (N_EXPERTS, ROW_TILE), 0)
    vals, ids = [], []
    l = logits
    for _ in range(TOP_K_EXPERTS):
        m = jnp.max(l, axis=0, keepdims=True)
        idx = jnp.min(jnp.where(l == m, erow, N_EXPERTS), axis=0, keepdims=True)
        vals.append(m)
        ids.append(idx)
        l = jnp.where(erow == idx, -jnp.inf, l)
    ex = [jnp.exp(v - vals[0]) for v in vals]
    den = ex[0] + ex[1] + ex[2] + ex[3]
    gates = [e / den for e in ex]

    onehot = [(erow == idx) for idx in ids]
    oh = jnp.concatenate([jnp.where(o, 1.0, 0.0) for o in onehot], axis=0)
    tr = lax.broadcasted_iota(I32, (ROW_TILE, ROW_TILE), 0)
    tc = lax.broadcasted_iota(I32, (ROW_TILE, ROW_TILE), 1)
    upper = jnp.where(tr <= tc, 1.0, 0.0).astype(BF16)
    pref = jnp.dot(oh.astype(BF16), upper, preferred_element_type=F32)
    offs = base_ref[:, 0:1]
    ranks = []
    for kk in range(TOP_K_EXPERTS):
        pk = pref[kk * N_EXPERTS:(kk + 1) * N_EXPERTS, :]
        r = jnp.sum(jnp.where(onehot[kk], offs + pk - 1.0, 0.0), axis=0, keepdims=True)
        ranks.append(r.astype(I32))
        offs = offs + pk[:, ROW_TILE - 1:ROW_TILE]
    base_ref[...] = jnp.broadcast_to(offs, base_ref.shape)
    cnt_ref[...] = jnp.broadcast_to(offs, cnt_ref.shape)

    zi = jnp.zeros((8 - TOP_K_EXPERTS, ROW_TILE), I32)
    eid_ref[...] = jnp.concatenate(ids + [zi], axis=0)
    rank_ref[...] = jnp.concatenate(ranks + [zi], axis=0)
    g8 = jnp.concatenate(gates + [jnp.zeros((128 - TOP_K_EXPERTS, ROW_TILE), F32)], axis=0)
    gcol_ref[...] = g8.T


def _mix(x2d, attn, y, ymeta, gate, wts, seq):
    n = x2d.shape[0]
    tiles_per_seq = seq // ROW_TILE
    halo_per_tile = ROW_TILE // HALO
    full = lambda a: pl.BlockSpec(a.shape, lambda i: (0,) * a.ndim)
    row = lambda w: pl.BlockSpec((ROW_TILE, w), lambda i: (i, 0))
    lane = lambda r: pl.BlockSpec((r, ROW_TILE), lambda i: (0, i))
    out_shape = (
        jax.ShapeDtypeStruct((n, D_MODEL), F32),
        jax.ShapeDtypeStruct((n, 1, D_MODEL // 2), I32),
        jax.ShapeDtypeStruct((8, n), I32),
        jax.ShapeDtypeStruct((8, n), I32),
        jax.ShapeDtypeStruct((n, 128), F32),
        jax.ShapeDtypeStruct((N_EXPERTS, 128), F32),
    )
    out_specs = (row(D_MODEL), pl.BlockSpec((ROW_TILE, 1, D_MODEL // 2), lambda i: (i, 0, 0)),
                 lane(8), lane(8), row(128), full(out_shape[5]))
    return pl.pallas_call(
        functools.partial(_mix_kernel, tiles_per_seq=tiles_per_seq),
        grid=(n // ROW_TILE,),
        in_specs=[row(D_MODEL), row(ATTN_WIDTH), row(CONV_CH),
                  pl.BlockSpec((HALO, CONV_CH), lambda i: (jnp.maximum(i * halo_per_tile - 1, 0), 0)),
                  full(ymeta), row(2 * D_MODEL)] + [full(w) for w in wts],
        out_specs=out_specs,
        out_shape=out_shape,
        scratch_shapes=[pltpu.VMEM((HALO + ROW_TILE, CONV_CH), F32),
                        pltpu.VMEM((N_EXPERTS, 128), F32)],
        compiler_params=pltpu.CompilerParams(dimension_semantics=("arbitrary",),
                                             vmem_limit_bytes=VMEM_LIMIT),
        name="mix",
    )(x2d, attn, y, y, ymeta, gate, *wts)


def _slots_kernel(starts_ref, eid_ref, rank_ref, slot_ref):
    eid = eid_ref[...]
    base = jnp.zeros(eid.shape, I32)
    for e in range(N_EXPERTS):
        base = jnp.where(eid == e, starts_ref[e], base)
    slot_ref[...] = base + rank_ref[...]


def _slots(starts, eid8, rank8):
    n = eid8.shape[1]
    cols = 2048
    spec = pl.BlockSpec((8, cols), lambda i: (0, i))
    return pl.pallas_call(
        _slots_kernel,
        grid=(n // cols,),
        in_specs=[pl.BlockSpec(memory_space=pltpu.SMEM), spec, spec],
        out_specs=spec,
        out_shape=jax.ShapeDtypeStruct((8, n), I32),
        compiler_params=pltpu.CompilerParams(dimension_semantics=("arbitrary",)),
        name="slots",
    )(starts, eid8, rank8)


COPIES_PER_TILE = TOP_K_EXPERTS * ROW_TILE
SLOT_UNROLL = 16


def _slot_source_kernel(slot_ref, init_ref, src_ref, sem, *, n):
    i = pl.program_id(0)

    @pl.when(i == 0)
    def _():
        cp = pltpu.make_async_copy(init_ref, src_ref, sem)
        cp.start()
        cp.wait()

    for kk in range(TOP_K_EXPERTS):
        def body(g, carry):
            for u in range(SLOT_UNROLL):
                t = g * SLOT_UNROLL + u
                src_ref[slot_ref[0, 0, kk * ROW_TILE + t]] = kk * n + i * ROW_TILE + t
            return carry
        lax.fori_loop(0, ROW_TILE // SLOT_UNROLL, body, 0)


def _slot_sources(slot_tiles, init, n):
    return pl.pallas_call(
        functools.partial(_slot_source_kernel, n=n),
        grid=(slot_tiles.shape[0],),
        in_specs=[pl.BlockSpec((1, 1, COPIES_PER_TILE), lambda i: (i, 0, 0), memory_space=pltpu.SMEM),
                  pl.BlockSpec(memory_space=pl.ANY)],
        out_specs=pl.BlockSpec(memory_space=pltpu.SMEM),
        out_shape=jax.ShapeDtypeStruct(init.shape, I32),
        scratch_shapes=[pltpu.SemaphoreType.DMA(())],
        compiler_params=pltpu.CompilerParams(dimension_semantics=("arbitrary",)),
        name="slot_sources",
    )(slot_tiles, init)


def _expert_kernel(te_ref, nused_ref, src_ref, src_next_ref, src_prev_ref, hn_hbm,
                   wug_ref, bug_ref, wd_ref, bd_ref, yk_hbm, xbuf, ybuf, wug_bf, wd_bf, gsem, ssem, *, n):
    i = pl.program_id(0)
    nused = nused_ref[0]

    def gather_copy(src, r, b):
        tok = src[0, 0, r] & (n - 1)
        return pltpu.make_async_copy(hn_hbm.at[tok], xbuf.at[b, r], gsem.at[b])

    def scatter_copy(src, r, b):
        return pltpu.make_async_copy(ybuf.at[b, r], yk_hbm.at[src[0, 0, r]], ssem.at[b])

    @pl.when(i <= nused)
    def _():
        cur = i % 2
        oth = 1 - cur

        def wait_scatter(b):
            pltpu.make_async_copy(ybuf.at[b], yk_hbm.at[pl.ds(0, ROW_TILE)], ssem.at[b]).wait()

        @pl.when(i == 0)
        def _():
            for r in range(ROW_TILE):
                gather_copy(src_ref, r, 0).start()
            ybuf[1] = jnp.zeros(ybuf.shape[1:], F32)
            spare = pltpu.make_async_copy(ybuf.at[1], yk_hbm.at[pl.ds(TOP_K_EXPERTS * n, ROW_TILE)],
                                          ssem.at[1])
            spare.start()
            spare.wait()

        @pl.when(i < nused)
        def _():
            for r in range(ROW_TILE):
                gather_copy(src_next_ref, r, oth).start()

        @pl.when(i > 0)
        def _():
            for r in range(ROW_TILE):
                scatter_copy(src_prev_ref, r, oth).start()

        pltpu.make_async_copy(hn_hbm.at[pl.ds(0, ROW_TILE)], xbuf.at[cur], gsem.at[cur]).wait()

        prev = te_ref[jnp.maximum(i - 1, 0)]
        fresh = (i == 0) | (te_ref[jnp.minimum(i, nused - 1)] != prev)

        @pl.when(fresh)
        def _():
            wug_bf[...] = wug_ref[0].astype(BF16)
            wd_bf[...] = wd_ref[0].astype(BF16)

        xb = _unpack_bf16_pairs(xbuf[cur, :, 0, :])
        ug = jnp.dot(xb, wug_bf[...], preferred_element_type=F32) + bug_ref[0]
        gate = jnp.minimum(ug[:, :D_FF], SWIGLU_LIMIT)
        up = jnp.clip(ug[:, D_FF:], -SWIGLU_LIMIT, SWIGLU_LIMIT)
        act = (up + 1.0) * gate * _sigmoid(SWIGLU_ALPHA * gate)
        y = jnp.dot(act.astype(BF16), wd_bf[...], preferred_element_type=F32) + bd_ref[0]

        @pl.when(i > 1)
        def _():
            wait_scatter(cur)

        ybuf[cur, :, 0, :] = y

        @pl.when((i == nused) & (i > 0))
        def _():
            wait_scatter(oth)


def _experts(tile_expert, nused, src_tiles, hn2p, w_ug, b_ug, w_down, b_down, n):
    ntiles = src_tiles.shape[0]
    clamp = lambda i, nu: jnp.maximum(jnp.minimum(i, nu[0] - 1), 0)
    smem = lambda f: pl.BlockSpec((1, 1, ROW_TILE), f, memory_space=pltpu.SMEM)
    expert = lambda i, te, nu: (te[clamp(i, nu)], 0, 0)
    grid_spec = pltpu.PrefetchScalarGridSpec(
        num_scalar_prefetch=2,
        grid=(ntiles + 1,),
        in_specs=[
            smem(lambda i, te, nu: (clamp(i, nu), 0, 0)),
            smem(lambda i, te, nu: (clamp(i + 1, nu), 0, 0)),
            smem(lambda i, te, nu: (clamp(i - 1, nu), 0, 0)),
            pl.BlockSpec(memory_space=pl.ANY),
            pl.BlockSpec((1, D_MODEL, 2 * D_FF), expert),
            pl.BlockSpec((1, 1, 2 * D_FF), expert),
            pl.BlockSpec((1, D_FF, D_MODEL), expert),
            pl.BlockSpec((1, 1, D_MODEL), expert),
        ],
        out_specs=pl.BlockSpec(memory_space=pl.ANY),
        scratch_shapes=[pltpu.VMEM((2, ROW_TILE, 1, D_MODEL // 2), I32),
                        pltpu.VMEM((2, ROW_TILE, 1, D_MODEL), F32),
                        pltpu.VMEM((D_MODEL, 2 * D_FF), BF16), pltpu.VMEM((D_FF, D_MODEL), BF16),
                        pltpu.SemaphoreType.DMA((2,)), pltpu.SemaphoreType.DMA((2,))],
    )
    return pl.pallas_call(
        functools.partial(_expert_kernel, n=n),
        grid_spec=grid_spec,
        out_shape=jax.ShapeDtypeStruct((TOP_K_EXPERTS * n + ROW_TILE, 1, D_MODEL), F32),
        compiler_params=pltpu.CompilerParams(dimension_semantics=("arbitrary",),
                                             vmem_limit_bytes=VMEM_LIMIT),
        name="experts",
    )(tile_expert, nused, src_tiles, src_tiles, src_tiles, hn2p,
      w_ug, b_ug[:, None, :], w_down, b_down[:, None, :])


def _final_kernel(h2_ref, gcol_ref, nf_ref, *rest):
    y_refs, o_ref = rest[:TOP_K_EXPERTS], rest[TOP_K_EXPERTS]
    h = h2_ref[...]
    for kk in range(TOP_K_EXPERTS):
        h = h + gcol_ref[:, kk:kk + 1] * y_refs[kk][:, 0, :]
    ms = jnp.mean(h * h, axis=-1, keepdims=True)
    o_ref[...] = h * lax.rsqrt(ms + EPS) * nf_ref[...]


def _final(h2, gcol, nf, yk):
    n = h2.shape[0]
    ntiles = n // ROW_TILE
    choice = lambda kk: pl.BlockSpec((ROW_TILE, 1, D_MODEL), lambda i: (kk * ntiles + i, 0, 0))
    return pl.pallas_call(
        _final_kernel,
        grid=(ntiles,),
        in_specs=[pl.BlockSpec((ROW_TILE, D_MODEL), lambda i: (i, 0)),
                  pl.BlockSpec((ROW_TILE, 128), lambda i: (i, 0)),
                  pl.BlockSpec((1, D_MODEL), lambda i: (0, 0))] + [choice(kk) for kk in range(TOP_K_EXPERTS)],
        out_specs=pl.BlockSpec((ROW_TILE, D_MODEL), lambda i: (i, 0)),
        out_shape=jax.ShapeDtypeStruct((n, D_MODEL), F32),
        compiler_params=pltpu.CompilerParams(dimension_semantics=("arbitrary",),
                                             vmem_limit_bytes=VMEM_LIMIT),
        name="final",
    )(h2, gcol, nf, *([yk] * TOP_K_EXPERTS))


def _split_w_in(w_in, b_gate):
    c = ATTN_WIDTH
    o = 0
    wq = w_in[:, o:o + c]; o += c
    wk = w_in[:, o:o + c]; o += c
    wv = w_in[:, o:o + c]; o += c
    wiq = w_in[:, o:o + IDX_HEADS * IDX_DIM]; o += IDX_HEADS * IDX_DIM
    wik = w_in[:, o:o + IDX_DIM]; o += IDX_DIM
    wiw = w_in[:, o:o + IDX_HEADS]; o += IDX_HEADS
    wglu = w_in[:, o:o + 2 * CONV_CH]; o += 2 * CONV_CH
    wgate = w_in[:, o:]
    wiwt = jnp.concatenate([wiw.T, jnp.zeros((16 - IDX_HEADS, D_MODEL), w_in.dtype)], axis=0)
    bf = lambda a: a.astype(BF16)
    return (bf(wq), bf(wk), bf(wiq), bf(wik), bf(wv.T), bf(wiwt), bf(wglu), bf(wgate),
            b_gate[None, :].astype(F32))


def kernel(x, meta_tokens, rel_bias, norm_mix, w_in, b_gate, w_attn_out, conv_w, conv_b, conv_ln_g, conv_ln_b, w_conv_out, w_out, norm_ffn, w_router, b_router, w_up_gate, b_up_gate, w_down, b_down, norm_final):
    batch, seq, d = x.shape
    n = batch * seq
    x2d = x.reshape(n, d)

    wts = _split_w_in(w_in[0], b_gate[0])
    g_mix = norm_mix[0][None, :]
    q3, k3, iq3, ik, vt, iwt, y, gate = _project(x2d, g_mix, wts, ROW_TILE)
    _, km3, _, ikm, vtm, _, ym, _ = _project(meta_tokens.astype(F32), g_mix, wts, N_META)

    tab, tabm = _bias_tables(rel_bias.astype(F32))
    attn = _attention(q3, iq3, iwt, k3, ik, vt, km3, ikm, vtm, tab, tabm, batch, seq)

    ymeta = jnp.concatenate([jnp.zeros((HALO - N_META, CONV_CH), F32), ym], axis=0)
    cw = jnp.concatenate([conv_w[0], jnp.zeros((32 - CONV_WIDTH, CONV_CH), F32)], axis=0)
    mix_w = (cw, conv_b[0][None, :], conv_ln_g[0][None, :], conv_ln_b[0][None, :],
             w_conv_out[0].astype(BF16), w_attn_out[0].astype(BF16), w_out[0].astype(BF16),
             norm_ffn[0][None, :], w_router[0].T, b_router[0][:, None])
    h2, hn2, eid8, rank8, gcol, cnt = _mix(x2d, attn, y, ymeta, gate, mix_w, seq)

    counts = cnt[:, 0].astype(I32)
    padded = ((counts + ROW_TILE - 1) // ROW_TILE) * ROW_TILE
    ends = jnp.cumsum(padded)
    starts = ends - padded
    slot8 = _slots(starts, eid8, rank8)
    slot_tiles = (slot8[:TOP_K_EXPERTS].reshape(TOP_K_EXPERTS, n // ROW_TILE, ROW_TILE)
                  .transpose(1, 0, 2).reshape(n // ROW_TILE, 1, COPIES_PER_TILE))
    nslots = n * TOP_K_EXPERTS + N_EXPERTS * ROW_TILE
    ntiles = nslots // ROW_TILE
    tile_start = jnp.arange(ntiles, dtype=I32) * ROW_TILE
    nused = (ends[-1] // ROW_TILE).astype(I32)
    last_start = jnp.maximum(ends[-1] - ROW_TILE, 0)
    tile_expert = jnp.sum((jnp.minimum(tile_start, last_start)[:, None] >= ends[None, :]).astype(I32), axis=1)

    spare = TOP_K_EXPERTS * n + jnp.arange(nslots, dtype=I32) % ROW_TILE
    src_tiles = _slot_sources(slot_tiles, spare, n).reshape(ntiles, 1, ROW_TILE)
    yk = _experts(tile_expert, nused[None], src_tiles, hn2, w_up_gate[0], b_up_gate[0], w_down[0], b_down[0], n)
    out = _final(h2, gcol, norm_final[None, :], yk)
    return out.reshape(batch, seq, d)
```

```python
import functools
import math

import jax
import jax.numpy as jnp
from jax import lax
from jax.experimental import pallas as pl
from jax.experimental.pallas import tpu as pltpu

F32 = jnp.float32
BF16 = jnp.bfloat16
I32 = jnp.int32

D_MODEL = 1024
N_META = 16
N_HEADS = 8
HEAD_DIM = 64
ATTN_WIDTH = N_HEADS * HEAD_DIM
IDX_HEADS = 8
IDX_DIM = 64
TOPK_MAX = 256
CONV_CH = 512
CONV_WIDTH = 31
N_BUCKETS = 32
MAX_DISTANCE = 128
N_EXPERTS = 32
TOP_K_EXPERTS = 4
D_FF = 1024
SWIGLU_LIMIT = 7.0
SWIGLU_ALPHA = 1.702
EPS = 1e-6
IDX_SCALE = (IDX_DIM ** -0.5) * (IDX_HEADS ** -0.5)

ROW_TILE = 256
Q_TILE = 256
K_CHUNK = 256
HALO = 32
CONV_ROWS = 32
NEG = -1e30
VMEM_LIMIT = 56 * 1024 * 1024

NT_DIMS = (((1,), (1,)), ((), ()))


def _sigmoid(x):
    return 1.0 / (1.0 + jnp.exp(-x))


def _pack_bf16_pairs(x):
    w = x.shape[1] // 2
    hi = lax.bitcast_convert_type(x[:, :w].astype(BF16).astype(F32), jnp.uint32)
    lo = lax.bitcast_convert_type(x[:, w:].astype(BF16).astype(F32), jnp.uint32)
    return lax.bitcast_convert_type(hi | (lo >> 16), I32)


def _unpack_bf16_pairs(p):
    u = lax.bitcast_convert_type(p, jnp.uint32)
    hi = lax.bitcast_convert_type(u & jnp.uint32(0xFFFF0000), F32)
    lo = lax.bitcast_convert_type(u << 16, F32)
    return jnp.concatenate([hi, lo], axis=1).astype(BF16)


def _proj_kernel(x_ref, g_ref, wq_ref, wk_ref, wiq_ref, wik_ref, wvt_ref, wiwt_ref, wglu_ref,
                 wgate_ref, bgate_ref,
                 q_ref, k_ref, iq_ref, ik_ref, vt_ref, iwt_ref, y_ref, gate_ref):
    x = x_ref[...]
    ms = jnp.mean(x * x, axis=-1, keepdims=True)
    xn = (x * lax.rsqrt(ms + EPS) * g_ref[...]).astype(BF16)

    q = jnp.dot(xn, wq_ref[...], preferred_element_type=F32) * (HEAD_DIM ** -0.5)
    k = jnp.dot(xn, wk_ref[...], preferred_element_type=F32)
    iq = jnp.dot(xn, wiq_ref[...], preferred_element_type=F32)
    for h in range(N_HEADS):
        sl = slice(h * HEAD_DIM, (h + 1) * HEAD_DIM)
        q_ref[h] = q[:, sl].astype(BF16)
        k_ref[h] = k[:, sl].astype(BF16)
        iq_ref[h] = iq[:, sl].astype(BF16)
    ik_ref[...] = jnp.dot(xn, wik_ref[...], preferred_element_type=F32).astype(BF16)
    vt_ref[...] = lax.dot_general(wvt_ref[...], xn, NT_DIMS, preferred_element_type=F32).astype(BF16)
    iwt = lax.dot_general(wiwt_ref[...], xn, NT_DIMS, preferred_element_type=F32)
    iwt_ref[...] = iwt[:IDX_HEADS] * IDX_SCALE
    glu = jnp.dot(xn, wglu_ref[...], preferred_element_type=F32)
    y_ref[...] = glu[:, :CONV_CH] * _sigmoid(glu[:, CONV_CH:])
    gate = jnp.dot(xn, wgate_ref[...], preferred_element_type=F32) + bgate_ref[...]
    gate_ref[...] = _sigmoid(gate)


def _project(x2d, g, wts, tm):
    n = x2d.shape[0]
    wq, wk, wiq, wik, wvt, wiwt, wglu, wgate, bgate = wts
    full = lambda a: pl.BlockSpec(a.shape, lambda i: (0,) * a.ndim)
    out_shape = (
        jax.ShapeDtypeStruct((N_HEADS, n, HEAD_DIM), BF16),
        jax.ShapeDtypeStruct((N_HEADS, n, HEAD_DIM), BF16),
        jax.ShapeDtypeStruct((IDX_HEADS, n, IDX_DIM), BF16),
        jax.ShapeDtypeStruct((n, IDX_DIM), BF16),
        jax.ShapeDtypeStruct((ATTN_WIDTH, n), BF16),
        jax.ShapeDtypeStruct((IDX_HEADS, n), F32),
        jax.ShapeDtypeStruct((n, CONV_CH), F32),
        jax.ShapeDtypeStruct((n, 2 * D_MODEL), F32),
    )
    out_specs = (
        pl.BlockSpec((N_HEADS, tm, HEAD_DIM), lambda i: (0, i, 0)),
        pl.BlockSpec((N_HEADS, tm, HEAD_DIM), lambda i: (0, i, 0)),
        pl.BlockSpec((IDX_HEADS, tm, IDX_DIM), lambda i: (0, i, 0)),
        pl.BlockSpec((tm, IDX_DIM), lambda i: (i, 0)),
        pl.BlockSpec((ATTN_WIDTH, tm), lambda i: (0, i)),
        pl.BlockSpec((IDX_HEADS, tm), lambda i: (0, i)),
        pl.BlockSpec((tm, CONV_CH), lambda i: (i, 0)),
        pl.BlockSpec((tm, 2 * D_MODEL), lambda i: (i, 0)),
    )
    return pl.pallas_call(
        _proj_kernel,
        grid=(n // tm,),
        in_specs=[pl.BlockSpec((tm, D_MODEL), lambda i: (i, 0)), full(g), full(wq), full(wk), full(wiq),
                  full(wik), full(wvt), full(wiwt), full(wglu), full(wgate), full(bgate)],
        out_specs=out_specs,
        out_shape=out_shape,
        compiler_params=pltpu.CompilerParams(dimension_semantics=("arbitrary",),
                                             vmem_limit_bytes=VMEM_LIMIT),
        name="proj",
    )(x2d, g, wq, wk, wiq, wik, wvt, wiwt, wglu, wgate, bgate)


def _t5_bucket(n):
    max_exact = N_BUCKETS // 2
    nf = jnp.maximum(n, 1).astype(F32)
    large = max_exact + (jnp.log(nf / max_exact) / math.log(MAX_DISTANCE / max_exact)
                         * (N_BUCKETS - max_exact)).astype(I32)
    large = jnp.minimum(large, N_BUCKETS - 1)
    return jnp.where(n < max_exact, n, large)


def _bias_lookup(rb_ref, dist, h):
    bucket = _t5_bucket(jnp.maximum(dist, 0))
    out = jnp.full(dist.shape, NEG, F32)
    for b in range(N_BUCKETS):
        out = jnp.where(bucket == b, rb_ref[b, h], out)
    return jnp.where(dist >= 0, out, NEG)


def _bias_kernel(rb_ref, tab_ref, tabm_ref):
    kind = pl.program_id(0)
    r = pl.program_id(1)
    rows = tab_ref.shape[2]
    s = lax.broadcasted_iota(I32, (rows, Q_TILE), 0) + r * rows
    t = lax.broadcasted_iota(I32, (rows, Q_TILE), 1)
    dist = jnp.where(kind == 2, 2 * K_CHUNK, t - s + kind * K_CHUNK)
    for h in range(N_HEADS):
        tab_ref[0, h] = _bias_lookup(rb_ref, dist, h)
    m = lax.broadcasted_iota(I32, (N_META, Q_TILE), 0)
    tm_ = lax.broadcasted_iota(I32, (N_META, Q_TILE), 1)
    distm = jnp.where(kind == 0, N_META + tm_ - m, 2 * K_CHUNK)
    for h in range(N_HEADS):
        tabm_ref[0, h] = _bias_lookup(rb_ref, distm, h)


def _bias_tables(rel_bias):
    rows = 64
    return pl.pallas_call(
        _bias_kernel,
        grid=(3, K_CHUNK // rows),
        in_specs=[pl.BlockSpec(memory_space=pltpu.SMEM)],
        out_specs=(pl.BlockSpec((1, N_HEADS, rows, Q_TILE), lambda kd, r: (kd, 0, r, 0)),
                   pl.BlockSpec((1, N_HEADS, N_META, Q_TILE), lambda kd, r: (kd, 0, 0, 0))),
        out_shape=(jax.ShapeDtypeStruct((3, N_HEADS, K_CHUNK, Q_TILE), F32),
                   jax.ShapeDtypeStruct((3, N_HEADS, N_META, Q_TILE), F32)),
        compiler_params=pltpu.CompilerParams(dimension_semantics=("arbitrary", "arbitrary")),
        name="bias_tables",
    )(rel_bias)


def _order_bits_to_float(u):
    bits = jnp.where(u < 0, u ^ jnp.int32(-2 ** 31), ~u)
    return lax.bitcast_convert_type(bits, F32)


def _fold_rows(x, op):
    r, l = x.shape
    x3 = x.reshape(r // 8, 8, l)
    return op(x3, axis=0)


def _attn_kernel(q_ref, iq_ref, iwt_ref, k_ref, ik_ref, vt_ref, km_ref, ikm_ref, vtm_ref,
                 tab_ref, tabm_ref, o_ref, sc_ref, scm_ref, l_ref, lm_ref, ot_ref, cst_ref,
                 hi_ref, lo_ref, him_ref, lom_ref):
    j = pl.program_id(1)
    nchunks = j + 1
    iw = iwt_ref[...]

    def chunk_rows(c):
        return pl.ds(pl.multiple_of(c * K_CHUNK, K_CHUNK), K_CHUNK)

    def idx_scores(ikc):
        acc = None
        for h in range(IDX_HEADS):
            s = lax.dot_general(ikc, iq_ref[h], NT_DIMS, preferred_element_type=F32)
            term = jnp.maximum(s, 0.0) * iw[h:h + 1, :]
            acc = term if acc is None else acc + term
        return acc

    scm_ref[...] = idx_scores(ikm_ref[...])

    row_minus_col = (lax.broadcasted_iota(I32, (K_CHUNK, Q_TILE), 0)
                     - lax.broadcasted_iota(I32, (K_CHUNK, Q_TILE), 1))

    def p1(c, carry):
        rows = chunk_rows(c)
        future = row_minus_col > jnp.where(c == j, 0, K_CHUNK)
        sc_ref[rows, :] = jnp.where(future, -jnp.inf, idx_scores(ik_ref[rows, :]))
        return carry

    lax.fori_loop(0, nchunks, p1, 0)

    def count(pred):
        def body(c, acc):
            blk = sc_ref[chunk_rows(c), :]
            return acc + _fold_rows(jnp.where(pred(blk, c), 1, 0).astype(I32), jnp.sum)
        acc = lax.fori_loop(0, nchunks, body, jnp.zeros((8, Q_TILE), I32))
        acc = acc + _fold_rows(jnp.where(pred(scm_ref[...], -1), 1, 0).astype(I32), jnp.sum)
        return jnp.sum(acc, axis=0, keepdims=True)

    I16 = jnp.int16
    LOW = -2 ** 15

    def split_key(s):
        bits = lax.bitcast_convert_type(s, I32)
        key = jnp.where(bits < 0, bits ^ jnp.int32(0x7FFFFFFF), bits)
        hi = lax.shift_right_arithmetic(key, 16).astype(I16)
        lo = ((key & 0xFFFF) + LOW).astype(I16)
        return hi, lo

    him_ref[...], lom_ref[...] = split_key(scm_ref[...])

    def p2(c, carry):
        rows = chunk_rows(c)
        hi_ref[rows, :], lo_ref[rows, :] = split_key(sc_ref[rows, :])
        return carry

    lax.fori_loop(0, nchunks, p2, 0)

    @pl.when((nchunks & 1) == 1)
    def _():
        rows = chunk_rows(nchunks)
        hi_ref[rows, :] = jnp.full((K_CHUNK, Q_TILE), LOW, I16)
        lo_ref[rows, :] = jnp.full((K_CHUNK, Q_TILE), LOW, I16)

    def tree_sum16(ind):
        parts = [ind[a * 16:(a + 1) * 16, :] for a in range(ind.shape[0] // 16)]
        while len(parts) > 1:
            parts = [parts[a] + parts[a + 1] for a in range(0, len(parts), 2)]
        return parts[0]

    def count16(ref, mref, pred):
        one, zero = jnp.ones((), BF16), jnp.zeros((), BF16)

        def body(cc, acc):
            rows = pl.ds(pl.multiple_of(cc * (2 * K_CHUNK), 2 * K_CHUNK), 2 * K_CHUNK)
            return acc + tree_sum16(jnp.where(pred(ref[rows, :]), one, zero)).astype(F32)
        acc = lax.fori_loop(0, lax.shift_right_logical(nchunks + 1, 1), body, jnp.zeros((16, Q_TILE), F32))
        acc = acc + jnp.where(pred(mref[...]), one, zero).astype(F32)
        return jnp.sum(acc, axis=0, keepdims=True).astype(I32)

    def to16(u):
        return (u + LOW).astype(I16)

    def search_hi(i, u):
        cand = u | lax.shift_left(jnp.int32(1), 15 - i)
        c16 = to16(cand)
        cnt = count16(hi_ref, him_ref, lambda blk: blk >= c16)
        return jnp.where(cnt >= TOPK_MAX, cand, u)

    u_hi = lax.fori_loop(0, 16, search_hi, jnp.zeros((1, Q_TILE), I32))
    t16 = to16(u_hi)
    need = TOPK_MAX - count16(hi_ref, him_ref, lambda blk: blk > t16)

    lom_ref[...] = jnp.where(him_ref[...] == t16, lom_ref[...], jnp.int16(LOW))

    def p2b(c, carry):
        rows = chunk_rows(c)
        lo_ref[rows, :] = jnp.where(hi_ref[rows, :] == t16, lo_ref[rows, :], jnp.int16(LOW))
        return carry

    lax.fori_loop(0, nchunks, p2b, 0)

    def search_lo(i, v):
        cand = v | lax.shift_left(jnp.int32(1), 15 - i)
        c16 = to16(cand)
        cnt = count16(lo_ref, lom_ref, lambda blk: blk >= c16)
        return jnp.where(cnt >= need, cand, v)

    u_lo = lax.fori_loop(0, 16, search_lo, jnp.zeros((1, Q_TILE), I32))
    thr_key = lax.shift_left(u_hi + LOW, 16) | u_lo
    thr_bits = jnp.where(thr_key < 0, thr_key ^ jnp.int32(0x7FFFFFFF), thr_key)
    thr = lax.bitcast_convert_type(thr_bits, F32)

    cnt_ge = count(lambda blk, c: blk >= thr)
    tie = (cnt_ge > TOPK_MAX) & (thr > -jnp.inf)
    any_tie = jnp.max(tie.astype(I32))
    cst_ref[...] = jnp.full((8, Q_TILE), 2 ** 30, I32)

    def pos_of(c, shape):
        r = lax.broadcasted_iota(I32, shape, 0)
        return jnp.where(c < 0, r, r + N_META + c * K_CHUNK)

    @pl.when(any_tie > 0)
    def _():
        cnt_gt = count(lambda blk, c: blk > thr)
        need = TOPK_MAX - cnt_gt

        def bis_pos(i, cs):
            cand = cs | lax.shift_left(jnp.int32(1), 11 - i)
            f = count(lambda blk, c: (blk == thr) & (pos_of(c, blk.shape) < cand))
            return jnp.where(f <= need, cand, cs)

        cs = lax.fori_loop(0, 12, bis_pos, jnp.zeros((1, Q_TILE), I32))
        cs = jnp.where(tie, cs, 2 ** 30)
        cst_ref[...] = jnp.broadcast_to(cs, (8, Q_TILE))

    def to_mask(blk, c, with_ties):
        if with_ties:
            cs = cst_ref[0:1, :]
            sel = (blk > thr) | ((blk == thr) & (pos_of(c, blk.shape) < cs))
        else:
            sel = blk >= thr
        return jnp.where(sel, 0.0, NEG)

    for with_ties in (False, True):
        @pl.when((any_tie > 0) == with_ties)
        def _():
            def body(c, carry):
                rows = chunk_rows(c)
                sc_ref[rows, :] = to_mask(sc_ref[rows, :], c, with_ties)
                return carry
            lax.fori_loop(0, nchunks, body, 0)
            scm_ref[...] = to_mask(scm_ref[...], -1, with_ties)

    kindm = jnp.minimum(j, 1)
    heads = [slice(h * HEAD_DIM, (h + 1) * HEAD_DIM) for h in range(N_HEADS)]

    mx0 = []
    for h in range(N_HEADS):
        lm = (lax.dot_general(km_ref[h], q_ref[h], NT_DIMS, preferred_element_type=F32)
              + tabm_ref[kindm, h] + scm_ref[...])
        lm_ref[h] = lm
        mx0.append(_fold_rows(lm, jnp.max))

    def pass_a(c, mx):
        rows = chunk_rows(c)
        kind = jnp.minimum(j - c, 2)
        mask = sc_ref[rows, :]
        out = []
        for h in range(N_HEADS):
            l = (lax.dot_general(k_ref[h, rows, :], q_ref[h], NT_DIMS, preferred_element_type=F32)
                 + tab_ref[kind, h] + mask)
            l_ref[h, rows, :] = l
            out.append(jnp.maximum(mx[h], _fold_rows(l, jnp.max)))
        return tuple(out)

    mx = lax.fori_loop(0, nchunks, pass_a, tuple(mx0))
    m = [jnp.max(mx[h], axis=0, keepdims=True) for h in range(N_HEADS)]

    den0 = []
    for h in range(N_HEADS):
        pm = jnp.exp(lm_ref[h] - m[h])
        den0.append(_fold_rows(pm, jnp.sum))
        ot_ref[heads[h], :] = jnp.dot(vtm_ref[heads[h], :], pm.astype(BF16), preferred_element_type=F32)

    def pass_b(c, den):
        rows = chunk_rows(c)
        out = []
        for h in range(N_HEADS):
            p = jnp.exp(l_ref[h, rows, :] - m[h])
            out.append(den[h] + _fold_rows(p, jnp.sum))
            ot_ref[heads[h], :] += jnp.dot(vt_ref[heads[h], rows], p.astype(BF16),
                                           preferred_element_type=F32)
        return tuple(out)

    den = lax.fori_loop(0, nchunks, pass_b, tuple(den0))
    for h in range(N_HEADS):
        ot_ref[heads[h], :] = ot_ref[heads[h], :] / jnp.sum(den[h], axis=0, keepdims=True)
    o_ref[...] = ot_ref[...].T.astype(BF16)


def _attention(q3, iq3, iwt, k3, ik, vt, km3, ikm, vtm, tab, tabm, batch, seq):
    n = batch * seq
    tiles = seq // Q_TILE
    full = lambda a: pl.BlockSpec(a.shape, lambda b, j: (0,) * a.ndim, pipeline_mode=pl.Buffered(1))
    return pl.pallas_call(
        _attn_kernel,
        grid=(batch, tiles),
        in_specs=[
            pl.BlockSpec((N_HEADS, Q_TILE, HEAD_DIM), lambda b, j: (0, b * tiles + j, 0)),
            pl.BlockSpec((IDX_HEADS, Q_TILE, IDX_DIM), lambda b, j: (0, b * tiles + j, 0)),
            pl.BlockSpec((IDX_HEADS, Q_TILE), lambda b, j: (0, b * tiles + j)),
            pl.BlockSpec((N_HEADS, seq, HEAD_DIM), lambda b, j: (0, b, 0)),
            pl.BlockSpec((seq, IDX_DIM), lambda b, j: (b, 0)),
            pl.BlockSpec((ATTN_WIDTH, seq), lambda b, j: (0, b)),
            full(km3), full(ikm), full(vtm), full(tab), full(tabm),
        ],
        out_specs=pl.BlockSpec((Q_TILE, ATTN_WIDTH), lambda b, j: (b * tiles + j, 0)),
        out_shape=jax.ShapeDtypeStruct((n, ATTN_WIDTH), BF16),
        scratch_shapes=[
            pltpu.VMEM((seq, Q_TILE), F32),
            pltpu.VMEM((N_META, Q_TILE), F32),
            pltpu.VMEM((N_HEADS, seq, Q_TILE), F32),
            pltpu.VMEM((N_HEADS, N_META, Q_TILE), F32),
            pltpu.VMEM((ATTN_WIDTH, Q_TILE), F32),
            pltpu.VMEM((8, Q_TILE), I32),
            pltpu.VMEM((seq, Q_TILE), jnp.int16),
            pltpu.VMEM((seq, Q_TILE), jnp.int16),
            pltpu.VMEM((N_META, Q_TILE), jnp.int16),
            pltpu.VMEM((N_META, Q_TILE), jnp.int16),
        ],
        compiler_params=pltpu.CompilerParams(dimension_semantics=("arbitrary", "arbitrary"),
                                             vmem_limit_bytes=VMEM_LIMIT),
        name="attn",
    )(q3, iq3, iwt, k3, ik, vt, km3, ikm, vtm, tab, tabm)


def _mix_kernel(x_ref, attn_ref, y_ref, yprev_ref, ymeta_ref, gate_ref,
                cw_ref, cb_ref, lng_ref, lnb_ref, wco_ref, wao_ref, wout_ref, nf_ref, wrt_ref, br_ref,
                h2_ref, hn2_ref, eid_ref, rank_ref, gcol_ref, cnt_ref,
                win_ref, shift_ref, base_ref, *, tiles_per_seq):
    i = pl.program_id(0)

    @pl.when(i == 0)
    def _():
        base_ref[...] = jnp.zeros_like(base_ref)

    first = (i % tiles_per_seq) == 0
    win_ref[0:HALO, :] = jnp.where(first, ymeta_ref[...], yprev_ref[...])
    win_ref[HALO:, :] = y_ref[...]
    lead = HALO - (CONV_WIDTH - 1)
    for b in range(8):
        span = ROW_TILE + 8 * (len(range(b, CONV_WIDTH, 8)) - 1)
        shift_ref[b, 0:span, :] = win_ref[pl.ds(lead + b, span), :]
    blocks = []
    for r0 in range(0, ROW_TILE, CONV_ROWS):
        acc = jnp.broadcast_to(cb_ref[...], (CONV_ROWS, CONV_CH))
        for b in range(8):
            for a, w in enumerate(range(b, CONV_WIDTH, 8)):
                acc = acc + cw_ref[w:w + 1, :] * shift_ref[b, r0 + 8 * a:r0 + 8 * a + CONV_ROWS, :]
        blocks.append(acc)
    yc = jnp.concatenate(blocks, axis=0)
    mu = jnp.mean(yc, axis=-1, keepdims=True)
    var = jnp.mean(jnp.square(yc - mu), axis=-1, keepdims=True)
    yn = (yc - mu) * lax.rsqrt(var + EPS) * lng_ref[...] + lnb_ref[...]
    ys = yn * _sigmoid(yn)
    y_b = jnp.dot(ys.astype(BF16), wco_ref[...], preferred_element_type=F32)

    y_a = jnp.dot(attn_ref[...], wao_ref[...], preferred_element_type=F32)
    merged = gate_ref[:, :D_MODEL] * y_a + gate_ref[:, D_MODEL:] * y_b
    h2 = x_ref[...] + jnp.dot(merged.astype(BF16), wout_ref[...], preferred_element_type=F32)
    h2_ref[...] = h2
    ms = jnp.mean(h2 * h2, axis=-1, keepdims=True)
    hn2 = h2 * lax.rsqrt(ms + EPS) * nf_ref[...]
    hn2_ref[:, 0, :] = _pack_bf16_pairs(hn2)

    logits = lax.dot_general(wrt_ref[...], hn2, NT_DIMS, preferred_element_type=F32,
                             precision=lax.Precision.HIGHEST) + br_ref[...]
    erow = lax.broadcasted_iota(I32, (N_EXPERTS, ROW_TILE), 0)
    vals, ids = [], []
    l = logits
    for _ in range(TOP_K_EXPERTS):
        m = jnp.max(l, axis=0, keepdims=True)
        idx = jnp.min(jnp.where(l == m, erow, N_EXPERTS), axis=0, keepdims=True)
        vals.append(m)
        ids.append(idx)
        l = jnp.where(erow == idx, -jnp.inf, l)
    ex = [jnp.exp(v - vals[0]) for v in vals]
    den = ex[0] + ex[1] + ex[2] + ex[3]
    gates = [e / den for e in ex]

    onehot = [(erow == idx) for idx in ids]
    oh = jnp.concatenate([jnp.where(o, 1.0, 0.0) for o in onehot], axis=0)
    tr = lax.broadcasted_iota(I32, (ROW_TILE, ROW_TILE), 0)
    tc = lax.broadcasted_iota(I32, (ROW_TILE, ROW_TILE), 1)
    upper = jnp.where(tr <= tc, 1.0, 0.0).astype(BF16)
    pref = jnp.dot(oh.astype(BF16), upper, preferred_element_type=F32)
    offs = base_ref[:, 0:1]
    ranks = []
    for kk in range(TOP_K_EXPERTS):
        pk = pref[kk * N_EXPERTS:(kk + 1) * N_EXPERTS, :]
        r = jnp.sum(jnp.where(onehot[kk], offs + pk - 1.0, 0.0), axis=0, keepdims=True)
        ranks.append(r.astype(I32))
        offs = offs + pk[:, ROW_TILE - 1:ROW_TILE]
    base_ref[...] = jnp.broadcast_to(offs, base_ref.shape)
    cnt_ref[...] = jnp.broadcast_to(offs, cnt_ref.shape)

    zi = jnp.zeros((8 - TOP_K_EXPERTS, ROW_TILE), I32)
    eid_ref[...] = jnp.concatenate(ids + [zi], axis=0)
    rank_ref[...] = jnp.concatenate(ranks + [zi], axis=0)
    g8 = jnp.concatenate(gates + [jnp.zeros((128 - TOP_K_EXPERTS, ROW_TILE), F32)], axis=0)
    gcol_ref[...] = g8.T


def _mix(x2d, attn, y, ymeta, gate, wts, seq):
    n = x2d.shape[0]
    tiles_per_seq = seq // ROW_TILE
    halo_per_tile = ROW_TILE // HALO
    full = lambda a: pl.BlockSpec(a.shape, lambda i: (0,) * a.ndim)
    row = lambda w: pl.BlockSpec((ROW_TILE, w), lambda i: (i, 0))
    lane = lambda r: pl.BlockSpec((r, ROW_TILE), lambda i: (0, i))
    out_shape = (
        jax.ShapeDtypeStruct((n, D_MODEL), F32),
        jax.ShapeDtypeStruct((n, 1, D_MODEL // 2), I32),
        jax.ShapeDtypeStruct((8, n), I32),
        jax.ShapeDtypeStruct((8, n), I32),
        jax.ShapeDtypeStruct((n, 128), F32),
        jax.ShapeDtypeStruct((N_EXPERTS, 128), F32),
    )
    out_specs = (row(D_MODEL), pl.BlockSpec((ROW_TILE, 1, D_MODEL // 2), lambda i: (i, 0, 0)),
                 lane(8), lane(8), row(128), full(out_shape[5]))
    return pl.pallas_call(
        functools.partial(_mix_kernel, tiles_per_seq=tiles_per_seq),
        grid=(n // ROW_TILE,),
        in_specs=[row(D_MODEL), row(ATTN_WIDTH), row(CONV_CH),
                  pl.BlockSpec((HALO, CONV_CH), lambda i: (jnp.maximum(i * halo_per_tile - 1, 0), 0)),
                  full(ymeta), row(2 * D_MODEL)] + [full(w) for w in wts],
        out_specs=out_specs,
        out_shape=out_shape,
        scratch_shapes=[pltpu.VMEM((HALO + ROW_TILE, CONV_CH), F32),
                        pltpu.VMEM((8, HALO + ROW_TILE, CONV_CH), F32),
                        pltpu.VMEM((N_EXPERTS, 128), F32)],
        compiler_params=pltpu.CompilerParams(dimension_semantics=("arbitrary",),
                                             vmem_limit_bytes=VMEM_LIMIT),
        name="mix",
    )(x2d, attn, y, y, ymeta, gate, *wts)


def _slots_kernel(starts_ref, eid_ref, rank_ref, slot_ref):
    eid = eid_ref[...]
    base = jnp.zeros(eid.shape, I32)
    for e in range(N_EXPERTS):
        base = jnp.where(eid == e, starts_ref[e], base)
    slot_ref[...] = base + rank_ref[...]


def _slots(starts, eid8, rank8):
    n = eid8.shape[1]
    cols = 2048
    spec = pl.BlockSpec((8, cols), lambda i: (0, i))
    return pl.pallas_call(
        _slots_kernel,
        grid=(n // cols,),
        in_specs=[pl.BlockSpec(memory_space=pltpu.SMEM), spec, spec],
        out_specs=spec,
        out_shape=jax.ShapeDtypeStruct((8, n), I32),
        compiler_params=pltpu.CompilerParams(dimension_semantics=("arbitrary",)),
        name="slots",
    )(starts, eid8, rank8)


COPIES_PER_TILE = TOP_K_EXPERTS * ROW_TILE
SLOT_UNROLL = 16


def _slot_source_kernel(slot_ref, init_ref, src_ref, sem, *, n):
    i = pl.program_id(0)

    @pl.when(i == 0)
    def _():
        cp = pltpu.make_async_copy(init_ref, src_ref, sem)
        cp.start()
        cp.wait()

    for kk in range(TOP_K_EXPERTS):
        def body(g, carry):
            for u in range(SLOT_UNROLL):
                t = g * SLOT_UNROLL + u
                src_ref[slot_ref[0, 0, kk * ROW_TILE + t]] = kk * n + i * ROW_TILE + t
            return carry
        lax.fori_loop(0, ROW_TILE // SLOT_UNROLL, body, 0)


def _slot_sources(slot_tiles, init, n):
    return pl.pallas_call(
        functools.partial(_slot_source_kernel, n=n),
        grid=(slot_tiles.shape[0],),
        in_specs=[pl.BlockSpec((1, 1, COPIES_PER_TILE), lambda i: (i, 0, 0), memory_space=pltpu.SMEM),
                  pl.BlockSpec(memory_space=pl.ANY)],
        out_specs=pl.BlockSpec(memory_space=pltpu.SMEM),
        out_shape=jax.ShapeDtypeStruct(init.shape, I32),
        scratch_shapes=[pltpu.SemaphoreType.DMA(())],
        compiler_params=pltpu.CompilerParams(dimension_semantics=("arbitrary",)),
        name="slot_sources",
    )(slot_tiles, init)


def _expert_kernel(te_ref, nused_ref, src_ref, src_next_ref, src_prev_ref, hn_hbm,
                   wug_ref, bug_ref, wd_ref, bd_ref, yk_hbm, xbuf, ybuf, wug_bf, wd_bf, gsem, ssem, *, n):
    i = pl.program_id(0)
    nused = nused_ref[0]

    def gather_copy(src, r, b):
        tok = src[0, 0, r] & (n - 1)
        return pltpu.make_async_copy(hn_hbm.at[tok], xbuf.at[b, r], gsem.at[b])

    def scatter_copy(src, r, b):
        return pltpu.make_async_copy(ybuf.at[b, r], yk_hbm.at[src[0, 0, r]], ssem.at[b])

    @pl.when(i <= nused)
    def _():
        cur = i % 2
        oth = 1 - cur

        def wait_scatter(b):
            pltpu.make_async_copy(ybuf.at[b], yk_hbm.at[pl.ds(0, ROW_TILE)], ssem.at[b]).wait()

        @pl.when(i == 0)
        def _():
            for r in range(ROW_TILE):
                gather_copy(src_ref, r, 0).start()
            ybuf[1] = jnp.zeros(ybuf.shape[1:], F32)
            spare = pltpu.make_async_copy(ybuf.at[1], yk_hbm.at[pl.ds(TOP_K_EXPERTS * n, ROW_TILE)],
                                          ssem.at[1])
            spare.start()
            spare.wait()

        @pl.when(i < nused)
        def _():
            for r in range(ROW_TILE):
                gather_copy(src_next_ref, r, oth).start()

        @pl.when(i > 0)
        def _():
            for r in range(ROW_TILE):
                scatter_copy(src_prev_ref, r, oth).start()

        pltpu.make_async_copy(hn_hbm.at[pl.ds(0, ROW_TILE)], xbuf.at[cur], gsem.at[cur]).wait()

        prev = te_ref[jnp.maximum(i - 1, 0)]
        fresh = (i == 0) | (te_ref[jnp.minimum(i, nused - 1)] != prev)

        @pl.when(fresh)
        def _():
            wug_bf[...] = wug_ref[0].astype(BF16)
            wd_bf[...] = wd_ref[0].astype(BF16)

        xb = _unpack_bf16_pairs(xbuf[cur, :, 0, :])
        ug = jnp.dot(xb, wug_bf[...], preferred_element_type=F32) + bug_ref[0]
        gate = jnp.minimum(ug[:, :D_FF], SWIGLU_LIMIT)
        up = jnp.clip(ug[:, D_FF:], -SWIGLU_LIMIT, SWIGLU_LIMIT)
        act = (up + 1.0) * gate * _sigmoid(SWIGLU_ALPHA * gate)
        y = jnp.dot(act.astype(BF16), wd_bf[...], preferred_element_type=F32) + bd_ref[0]

        @pl.when(i > 1)
        def _():
            wait_scatter(cur)

        ybuf[cur, :, 0, :] = y

        @pl.when((i == nused) & (i > 0))
        def _():
            wait_scatter(oth)


def _experts(tile_expert, nused, src_tiles, hn2p, w_ug, b_ug, w_down, b_down, n):
    ntiles = src_tiles.shape[0]
    clamp = lambda i, nu: jnp.maximum(jnp.minimum(i, nu[0] - 1), 0)
    smem = lambda f: pl.BlockSpec((1, 1, ROW_TILE), f, memory_space=pltpu.SMEM)
    expert = lambda i, te, nu: (te[clamp(i, nu)], 0, 0)
    grid_spec = pltpu.PrefetchScalarGridSpec(
        num_scalar_prefetch=2,
        grid=(ntiles + 1,),
        in_specs=[
            smem(lambda i, te, nu: (clamp(i, nu), 0, 0)),
            smem(lambda i, te, nu: (clamp(i + 1, nu), 0, 0)),
            smem(lambda i, te, nu: (clamp(i - 1, nu), 0, 0)),
            pl.BlockSpec(memory_space=pl.ANY),
            pl.BlockSpec((1, D_MODEL, 2 * D_FF), expert),
            pl.BlockSpec((1, 1, 2 * D_FF), expert),
            pl.BlockSpec((1, D_FF, D_MODEL), expert),
            pl.BlockSpec((1, 1, D_MODEL), expert),
        ],
        out_specs=pl.BlockSpec(memory_space=pl.ANY),
        scratch_shapes=[pltpu.VMEM((2, ROW_TILE, 1, D_MODEL // 2), I32),
                        pltpu.VMEM((2, ROW_TILE, 1, D_MODEL), F32),
                        pltpu.VMEM((D_MODEL, 2 * D_FF), BF16), pltpu.VMEM((D_FF, D_MODEL), BF16),
                        pltpu.SemaphoreType.DMA((2,)), pltpu.SemaphoreType.DMA((2,))],
    )
    return pl.pallas_call(
        functools.partial(_expert_kernel, n=n),
        grid_spec=grid_spec,
        out_shape=jax.ShapeDtypeStruct((TOP_K_EXPERTS * n + ROW_TILE, 1, D_MODEL), F32),
        compiler_params=pltpu.CompilerParams(dimension_semantics=("arbitrary",),
                                             vmem_limit_bytes=VMEM_LIMIT),
        name="experts",
    )(tile_expert, nused, src_tiles, src_tiles, src_tiles, hn2p,
      w_ug, b_ug[:, None, :], w_down, b_down[:, None, :])


def _final_kernel(h2_ref, gcol_ref, nf_ref, *rest):
    y_refs, o_ref = rest[:TOP_K_EXPERTS], rest[TOP_K_EXPERTS]
    h = h2_ref[...]
    for kk in range(TOP_K_EXPERTS):
        h = h + gcol_ref[:, kk:kk + 1] * y_refs[kk][:, 0, :]
    ms = jnp.mean(h * h, axis=-1, keepdims=True)
    o_ref[...] = h * lax.rsqrt(ms + EPS) * nf_ref[...]


def _final(h2, gcol, nf, yk):
    n = h2.shape[0]
    ntiles = n // ROW_TILE
    choice = lambda kk: pl.BlockSpec((ROW_TILE, 1, D_MODEL), lambda i: (kk * ntiles + i, 0, 0))
    return pl.pallas_call(
        _final_kernel,
        grid=(ntiles,),
        in_specs=[pl.BlockSpec((ROW_TILE, D_MODEL), lambda i: (i, 0)),
                  pl.BlockSpec((ROW_TILE, 128), lambda i: (i, 0)),
                  pl.BlockSpec((1, D_MODEL), lambda i: (0, 0))] + [choice(kk) for kk in range(TOP_K_EXPERTS)],
        out_specs=pl.BlockSpec((ROW_TILE, D_MODEL), lambda i: (i, 0)),
        out_shape=jax.ShapeDtypeStruct((n, D_MODEL), F32),
        compiler_params=pltpu.CompilerParams(dimension_semantics=("arbitrary",),
                                             vmem_limit_bytes=VMEM_LIMIT),
        name="final",
    )(h2, gcol, nf, *([yk] * TOP_K_EXPERTS))


def _split_w_in(w_in, b_gate):
    c = ATTN_WIDTH
    o = 0
    wq = w_in[:, o:o + c]; o += c
    wk = w_in[:, o:o + c]; o += c
    wv = w_in[:, o:o + c]; o += c
    wiq = w_in[:, o:o + IDX_HEADS * IDX_DIM]; o += IDX_HEADS * IDX_DIM
    wik = w_in[:, o:o + IDX_DIM]; o += IDX_DIM
    wiw = w_in[:, o:o + IDX_HEADS]; o += IDX_HEADS
    wglu = w_in[:, o:o + 2 * CONV_CH]; o += 2 * CONV_CH
    wgate = w_in[:, o:]
    wiwt = jnp.concatenate([wiw.T, jnp.zeros((16 - IDX_HEADS, D_MODEL), w_in.dtype)], axis=0)
    bf = lambda a: a.astype(BF16)
    return (bf(wq), bf(wk), bf(wiq), bf(wik), bf(wv.T), bf(wiwt), bf(wglu), bf(wgate),
            b_gate[None, :].astype(F32))


def kernel(x, meta_tokens, rel_bias, norm_mix, w_in, b_gate, w_attn_out, conv_w, conv_b, conv_ln_g, conv_ln_b, w_conv_out, w_out, norm_ffn, w_router, b_router, w_up_gate, b_up_gate, w_down, b_down, norm_final):
    batch, seq, d = x.shape
    n = batch * seq
    x2d = x.reshape(n, d)

    wts = _split_w_in(w_in[0], b_gate[0])
    g_mix = norm_mix[0][None, :]
    q3, k3, iq3, ik, vt, iwt, y, gate = _project(x2d, g_mix, wts, ROW_TILE)
    _, km3, _, ikm, vtm, _, ym, _ = _project(meta_tokens.astype(F32), g_mix, wts, N_META)

    tab, tabm = _bias_tables(rel_bias.astype(F32))
    attn = _attention(q3, iq3, iwt, k3, ik, vt, km3, ikm, vtm, tab, tabm, batch, seq)

    ymeta = jnp.concatenate([jnp.zeros((HALO - N_META, CONV_CH), F32), ym], axis=0)
    cw = jnp.concatenate([conv_w[0], jnp.zeros((32 - CONV_WIDTH, CONV_CH), F32)], axis=0)
    mix_w = (cw, conv_b[0][None, :], conv_ln_g[0][None, :], conv_ln_b[0][None, :],
             w_conv_out[0].astype(BF16), w_attn_out[0].astype(BF16), w_out[0].astype(BF16),
             norm_ffn[0][None, :], w_router[0].T, b_router[0][:, None])
    h2, hn2, eid8, rank8, gcol, cnt = _mix(x2d, attn, y, ymeta, gate, mix_w, seq)

    counts = cnt[:, 0].astype(I32)
    padded = ((counts + ROW_TILE - 1) // ROW_TILE) * ROW_TILE
    ends = jnp.cumsum(padded)
    starts = ends - padded
    slot8 = _slots(starts, eid8, rank8)
    slot_tiles = (slot8[:TOP_K_EXPERTS].reshape(TOP_K_EXPERTS, n // ROW_TILE, ROW_TILE)
                  .transpose(1, 0, 2).reshape(n // ROW_TILE, 1, COPIES_PER_TILE))
    nslots = n * TOP_K_EXPERTS + N_EXPERTS * ROW_TILE
    ntiles = nslots // ROW_TILE
    tile_start = jnp.arange(ntiles, dtype=I32) * ROW_TILE
    nused = (ends[-1] // ROW_TILE).astype(I32)
    last_start = jnp.maximum(ends[-1] - ROW_TILE, 0)
    tile_expert = jnp.sum((jnp.minimum(tile_start, last_start)[:, None] >= ends[None, :]).astype(I32), axis=1)

    spare = TOP_K_EXPERTS * n + jnp.arange(nslots, dtype=I32) % ROW_TILE
    src_tiles = _slot_sources(slot_tiles, spare, n).reshape(ntiles, 1, ROW_TILE)
    yk = _experts(tile_expert, nused[None], src_tiles, hn2, w_up_gate[0], b_up_gate[0], w_down[0], b_down[0], n)
    out = _final(h2, gcol, norm_final[None, :], yk)
    return out.reshape(batch, seq, d)
```

```python
import functools
import math

import jax
import jax.numpy as jnp
from jax import lax
from jax.experimental import pallas as pl
from jax.experimental.pallas import tpu as pltpu

F32 = jnp.float32
BF16 = jnp.bfloat16
I32 = jnp.int32

D_MODEL = 1024
N_META = 16
N_HEADS = 8
HEAD_DIM = 64
ATTN_WIDTH = N_HEADS * HEAD_DIM
IDX_HEADS = 8
IDX_DIM = 64
TOPK_MAX = 256
CONV_CH = 512
CONV_WIDTH = 31
N_BUCKETS = 32
MAX_DISTANCE = 128
N_EXPERTS = 32
TOP_K_EXPERTS = 4
D_FF = 1024
SWIGLU_LIMIT = 7.0
SWIGLU_ALPHA = 1.702
EPS = 1e-6
IDX_SCALE = (IDX_DIM ** -0.5) * (IDX_HEADS ** -0.5)

ROW_TILE = 256
PROJ_TILE = 512
Q_TILE = 256
K_CHUNK = 256
HALO = 32
CONV_ROWS = 32
NEG = -1e30
LOG2E = math.log2(math.e)
VMEM_LIMIT = 56 * 1024 * 1024

NT_DIMS = (((1,), (1,)), ((), ()))


def _sigmoid(x):
    return 1.0 / (1.0 + jnp.exp(-x))


def _pack_bf16_pairs(x):
    w = x.shape[1] // 2
    hi = lax.bitcast_convert_type(x[:, :w].astype(BF16).astype(F32), jnp.uint32)
    lo = lax.bitcast_convert_type(x[:, w:].astype(BF16).astype(F32), jnp.uint32)
    return lax.bitcast_convert_type(hi | (lo >> 16), I32)


def _unpack_bf16_pairs(p):
    u = lax.bitcast_convert_type(p, jnp.uint32)
    hi = lax.bitcast_convert_type(u & jnp.uint32(0xFFFF0000), F32)
    lo = lax.bitcast_convert_type(u << 16, F32)
    return jnp.concatenate([hi, lo], axis=1).astype(BF16)


def _proj_kernel(x_ref, g_ref, wq_ref, wk_ref, wiq_ref, wik_ref, wvt_ref, wiwt_ref, wglu_ref,
                 wgate_ref, bgate_ref,
                 q_ref, k_ref, iq_ref, ik_ref, vt_ref, iwt_ref, y_ref, gate_ref):
    x = x_ref[...]
    ms = jnp.mean(x * x, axis=-1, keepdims=True)
    xn = (x * lax.rsqrt(ms + EPS) * g_ref[...]).astype(BF16)

    q = jnp.dot(xn, wq_ref[...], preferred_element_type=F32) * (HEAD_DIM ** -0.5 * LOG2E)
    k = jnp.dot(xn, wk_ref[...], preferred_element_type=F32)
    iq = jnp.dot(xn, wiq_ref[...], preferred_element_type=F32)
    for h in range(N_HEADS):
        sl = slice(h * HEAD_DIM, (h + 1) * HEAD_DIM)
        q_ref[h] = q[:, sl].astype(BF16)
        k_ref[h] = k[:, sl].astype(BF16)
        iq_ref[h] = iq[:, sl].astype(BF16)
    ik_ref[...] = jnp.dot(xn, wik_ref[...], preferred_element_type=F32).astype(BF16)
    vt_ref[...] = lax.dot_general(wvt_ref[...], xn, NT_DIMS, preferred_element_type=F32).astype(BF16)
    iwt = lax.dot_general(wiwt_ref[...], xn, NT_DIMS, preferred_element_type=F32)
    iwt_ref[...] = iwt[:IDX_HEADS] * IDX_SCALE
    glu = jnp.dot(xn, wglu_ref[...], preferred_element_type=F32)
    y_ref[...] = glu[:, :CONV_CH] * _sigmoid(glu[:, CONV_CH:])
    gate = jnp.dot(xn, wgate_ref[...], preferred_element_type=F32) + bgate_ref[...]
    gate_ref[...] = _sigmoid(gate)


def _project(x2d, g, wts, tm):
    n = x2d.shape[0]
    wq, wk, wiq, wik, wvt, wiwt, wglu, wgate, bgate = wts
    full = lambda a: pl.BlockSpec(a.shape, lambda i: (0,) * a.ndim, pipeline_mode=pl.Buffered(1))
    out_shape = (
        jax.ShapeDtypeStruct((N_HEADS, n, HEAD_DIM), BF16),
        jax.ShapeDtypeStruct((N_HEADS, n, HEAD_DIM), BF16),
        jax.ShapeDtypeStruct((IDX_HEADS, n, IDX_DIM), BF16),
        jax.ShapeDtypeStruct((n, IDX_DIM), BF16),
        jax.ShapeDtypeStruct((ATTN_WIDTH, n), BF16),
        jax.ShapeDtypeStruct((IDX_HEADS, n), F32),
        jax.ShapeDtypeStruct((n, CONV_CH), F32),
        jax.ShapeDtypeStruct((n, 2 * D_MODEL), F32),
    )
    out_specs = (
        pl.BlockSpec((N_HEADS, tm, HEAD_DIM), lambda i: (0, i, 0)),
        pl.BlockSpec((N_HEADS, tm, HEAD_DIM), lambda i: (0, i, 0)),
        pl.BlockSpec((IDX_HEADS, tm, IDX_DIM), lambda i: (0, i, 0)),
        pl.BlockSpec((tm, IDX_DIM), lambda i: (i, 0)),
        pl.BlockSpec((ATTN_WIDTH, tm), lambda i: (0, i)),
        pl.BlockSpec((IDX_HEADS, tm), lambda i: (0, i)),
        pl.BlockSpec((tm, CONV_CH), lambda i: (i, 0)),
        pl.BlockSpec((tm, 2 * D_MODEL), lambda i: (i, 0)),
    )
    return pl.pallas_call(
        _proj_kernel,
        grid=(n // tm,),
        in_specs=[pl.BlockSpec((tm, D_MODEL), lambda i: (i, 0)), full(g), full(wq), full(wk), full(wiq),
                  full(wik), full(wvt), full(wiwt), full(wglu), full(wgate), full(bgate)],
        out_specs=out_specs,
        out_shape=out_shape,
        compiler_params=pltpu.CompilerParams(dimension_semantics=("arbitrary",),
                                             vmem_limit_bytes=VMEM_LIMIT),
        name="proj",
    )(x2d, g, wq, wk, wiq, wik, wvt, wiwt, wglu, wgate, bgate)


def _t5_bucket(n):
    max_exact = N_BUCKETS // 2
    nf = jnp.maximum(n, 1).astype(F32)
    large = max_exact + (jnp.log(nf / max_exact) / math.log(MAX_DISTANCE / max_exact)
                         * (N_BUCKETS - max_exact)).astype(I32)
    large = jnp.minimum(large, N_BUCKETS - 1)
    return jnp.where(n < max_exact, n, large)


def _bias_lookup(rb_ref, dist, h):
    bucket = _t5_bucket(jnp.maximum(dist, 0))
    out = jnp.full(dist.shape, NEG, F32)
    for b in range(N_BUCKETS):
        out = jnp.where(bucket == b, rb_ref[b, h] * LOG2E, out)
    return jnp.where(dist >= 0, out, NEG)


def _bias_kernel(rb_ref, tab_ref, tabm_ref):
    kind = pl.program_id(0)
    r = pl.program_id(1)
    rows = tab_ref.shape[2]
    s = lax.broadcasted_iota(I32, (rows, Q_TILE), 0) + r * rows
    t = lax.broadcasted_iota(I32, (rows, Q_TILE), 1)
    dist = jnp.where(kind == 2, 2 * K_CHUNK, t - s + kind * K_CHUNK)
    for h in range(N_HEADS):
        tab_ref[0, h] = _bias_lookup(rb_ref, dist, h)
    m = lax.broadcasted_iota(I32, (N_META, Q_TILE), 0)
    tm_ = lax.broadcasted_iota(I32, (N_META, Q_TILE), 1)
    distm = jnp.where(kind == 0, N_META + tm_ - m, 2 * K_CHUNK)
    for h in range(N_HEADS):
        tabm_ref[0, h] = _bias_lookup(rb_ref, distm, h)


def _bias_tables(rel_bias):
    rows = 64
    return pl.pallas_call(
        _bias_kernel,
        grid=(3, K_CHUNK // rows),
        in_specs=[pl.BlockSpec(memory_space=pltpu.SMEM)],
        out_specs=(pl.BlockSpec((1, N_HEADS, rows, Q_TILE), lambda kd, r: (kd, 0, r, 0)),
                   pl.BlockSpec((1, N_HEADS, N_META, Q_TILE), lambda kd, r: (kd, 0, 0, 0))),
        out_shape=(jax.ShapeDtypeStruct((3, N_HEADS, K_CHUNK, Q_TILE), F32),
                   jax.ShapeDtypeStruct((3, N_HEADS, N_META, Q_TILE), F32)),
        compiler_params=pltpu.CompilerParams(dimension_semantics=("arbitrary", "arbitrary")),
        name="bias_tables",
    )(rel_bias)


def _order_bits_to_float(u):
    bits = jnp.where(u < 0, u ^ jnp.int32(-2 ** 31), ~u)
    return lax.bitcast_convert_type(bits, F32)


def _fold_rows(x, op):
    r, l = x.shape
    x3 = x.reshape(r // 8, 8, l)
    return op(x3, axis=0)


def _attn_kernel(q_ref, iq_ref, iwt_ref, k_ref, ik_ref, vt_ref, km_ref, ikm_ref, vtm_ref,
                 tab_ref, tabm_ref, o_ref, sc_ref, scm_ref, l_ref, lm_ref, ot_ref, cst_ref,
                 hi_ref, lo_ref, him_ref, lom_ref):
    j = pl.program_id(1)
    nchunks = j + 1
    iw = iwt_ref[...]

    def chunk_rows(c):
        return pl.ds(pl.multiple_of(c * K_CHUNK, K_CHUNK), K_CHUNK)

    def idx_scores(ikc):
        acc = None
        for h in range(IDX_HEADS):
            s = lax.dot_general(ikc, iq_ref[h], NT_DIMS, preferred_element_type=F32)
            term = jnp.maximum(s, 0.0) * iw[h:h + 1, :]
            acc = term if acc is None else acc + term
        return acc

    scm_ref[...] = idx_scores(ikm_ref[...])

    row_minus_col = (lax.broadcasted_iota(I32, (K_CHUNK, Q_TILE), 0)
                     - lax.broadcasted_iota(I32, (K_CHUNK, Q_TILE), 1))

    def p1(c, carry):
        rows = chunk_rows(c)
        future = row_minus_col > jnp.where(c == j, 0, K_CHUNK)
        sc_ref[rows, :] = jnp.where(future, -jnp.inf, idx_scores(ik_ref[rows, :]))
        return carry

    lax.fori_loop(0, nchunks, p1, 0)

    def count(pred):
        def body(c, acc):
            blk = sc_ref[chunk_rows(c), :]
            return acc + _fold_rows(jnp.where(pred(blk, c), 1, 0).astype(I32), jnp.sum)
        acc = lax.fori_loop(0, nchunks, body, jnp.zeros((8, Q_TILE), I32))
        acc = acc + _fold_rows(jnp.where(pred(scm_ref[...], -1), 1, 0).astype(I32), jnp.sum)
        return jnp.sum(acc, axis=0, keepdims=True)

    I16 = jnp.int16
    LOW = -2 ** 15

    def split_key(s):
        bits = lax.bitcast_convert_type(s, I32)
        key = jnp.where(bits < 0, bits ^ jnp.int32(0x7FFFFFFF), bits)
        hi = lax.shift_right_arithmetic(key, 16).astype(I16)
        lo = ((key & 0xFFFF) + LOW).astype(I16)
        return hi, lo

    him_ref[...], lom_ref[...] = split_key(scm_ref[...])

    def p2(c, carry):
        rows = chunk_rows(c)
        hi_ref[rows, :], lo_ref[rows, :] = split_key(sc_ref[rows, :])
        return carry

    lax.fori_loop(0, nchunks, p2, 0)

    @pl.when((nchunks & 1) == 1)
    def _():
        rows = chunk_rows(nchunks)
        hi_ref[rows, :] = jnp.full((K_CHUNK, Q_TILE), LOW, I16)
        lo_ref[rows, :] = jnp.full((K_CHUNK, Q_TILE), LOW, I16)

    def tree_sum16(ind):
        parts = [ind[a * 16:(a + 1) * 16, :] for a in range(ind.shape[0] // 16)]
        while len(parts) > 1:
            parts = [parts[a] + parts[a + 1] for a in range(0, len(parts), 2)]
        return parts[0]

    def count16(ref, mref, pred):
        one, zero = jnp.ones((), BF16), jnp.zeros((), BF16)

        def body(cc, acc):
            rows = pl.ds(pl.multiple_of(cc * (2 * K_CHUNK), 2 * K_CHUNK), 2 * K_CHUNK)
            return acc + tree_sum16(jnp.where(pred(ref[rows, :]), one, zero)).astype(F32)
        acc = lax.fori_loop(0, lax.shift_right_logical(nchunks + 1, 1), body, jnp.zeros((16, Q_TILE), F32))
        acc = acc + jnp.where(pred(mref[...]), one, zero).astype(F32)
        return jnp.sum(acc, axis=0, keepdims=True).astype(I32)

    def to16(u):
        return (u + LOW).astype(I16)

    def search_hi(i, u):
        cand = u | lax.shift_left(jnp.int32(1), 15 - i)
        c16 = to16(cand)
        cnt = count16(hi_ref, him_ref, lambda blk: blk >= c16)
        return jnp.where(cnt >= TOPK_MAX, cand, u)

    u_hi = lax.fori_loop(0, 16, search_hi, jnp.zeros((1, Q_TILE), I32))
    t16 = to16(u_hi)
    need = TOPK_MAX - count16(hi_ref, him_ref, lambda blk: blk > t16)

    lom_ref[...] = jnp.where(him_ref[...] == t16, lom_ref[...], jnp.int16(LOW))

    def p2b(c, carry):
        rows = chunk_rows(c)
        lo_ref[rows, :] = jnp.where(hi_ref[rows, :] == t16, lo_ref[rows, :], jnp.int16(LOW))
        return carry

    lax.fori_loop(0, nchunks, p2b, 0)

    def search_lo(i, v):
        cand = v | lax.shift_left(jnp.int32(1), 15 - i)
        c16 = to16(cand)
        cnt = count16(lo_ref, lom_ref, lambda blk: blk >= c16)
        return jnp.where(cnt >= need, cand, v)

    u_lo = lax.fori_loop(0, 16, search_lo, jnp.zeros((1, Q_TILE), I32))
    thr_key = lax.shift_left(u_hi + LOW, 16) | u_lo
    thr_bits = jnp.where(thr_key < 0, thr_key ^ jnp.int32(0x7FFFFFFF), thr_key)
    thr = lax.bitcast_convert_type(thr_bits, F32)

    cnt_ge = count(lambda blk, c: blk >= thr)
    tie = (cnt_ge > TOPK_MAX) & (thr > -jnp.inf)
    any_tie = jnp.max(tie.astype(I32))
    cst_ref[...] = jnp.full((8, Q_TILE), 2 ** 30, I32)

    def pos_of(c, shape):
        r = lax.broadcasted_iota(I32, shape, 0)
        return jnp.where(c < 0, r, r + N_META + c * K_CHUNK)

    @pl.when(any_tie > 0)
    def _():
        cnt_gt = count(lambda blk, c: blk > thr)
        need = TOPK_MAX - cnt_gt

        def bis_pos(i, cs):
            cand = cs | lax.shift_left(jnp.int32(1), 11 - i)
            f = count(lambda blk, c: (blk == thr) & (pos_of(c, blk.shape) < cand))
            return jnp.where(f <= need, cand, cs)

        cs = lax.fori_loop(0, 12, bis_pos, jnp.zeros((1, Q_TILE), I32))
        cs = jnp.where(tie, cs, 2 ** 30)
        cst_ref[...] = jnp.broadcast_to(cs, (8, Q_TILE))

    def to_mask(blk, c, with_ties):
        if with_ties:
            cs = cst_ref[0:1, :]
            sel = (blk > thr) | ((blk == thr) & (pos_of(c, blk.shape) < cs))
        else:
            sel = blk >= thr
        return jnp.where(sel, 0.0, NEG)

    for with_ties in (False, True):
        @pl.when((any_tie > 0) == with_ties)
        def _():
            def body(c, carry):
                rows = chunk_rows(c)
                sc_ref[rows, :] = to_mask(sc_ref[rows, :], c, with_ties)
                return carry
            lax.fori_loop(0, nchunks, body, 0)
            scm_ref[...] = to_mask(scm_ref[...], -1, with_ties)

    kindm = jnp.minimum(j, 1)
    heads = [slice(h * HEAD_DIM, (h + 1) * HEAD_DIM) for h in range(N_HEADS)]

    mx0 = []
    for h in range(N_HEADS):
        lm = (lax.dot_general(km_ref[h], q_ref[h], NT_DIMS, preferred_element_type=F32)
              + tabm_ref[kindm, h] + scm_ref[...])
        lm_ref[h] = lm
        mx0.append(_fold_rows(lm, jnp.max))

    def pass_a(c, mx):
        rows = chunk_rows(c)
        kind = jnp.minimum(j - c, 2)
        mask = sc_ref[rows, :]
        out = []
        for h in range(N_HEADS):
            l = (lax.dot_general(k_ref[h, rows, :], q_ref[h], NT_DIMS, preferred_element_type=F32)
                 + tab_ref[kind, h] + mask)
            l_ref[h, rows, :] = l
            out.append(jnp.maximum(mx[h], _fold_rows(l, jnp.max)))
        return tuple(out)

    mx = lax.fori_loop(0, nchunks, pass_a, tuple(mx0))
    m = [jnp.max(mx[h], axis=0, keepdims=True) for h in range(N_HEADS)]

    den0 = []
    for h in range(N_HEADS):
        pm = jnp.exp2(lm_ref[h] - m[h])
        den0.append(_fold_rows(pm, jnp.sum))
        ot_ref[heads[h], :] = jnp.dot(vtm_ref[heads[h], :], pm.astype(BF16), preferred_element_type=F32)

    def pass_b(c, den):
        rows = chunk_rows(c)
        out = []
        for h in range(N_HEADS):
            p = jnp.exp2(l_ref[h, rows, :] - m[h])
            out.append(den[h] + _fold_rows(p, jnp.sum))
            ot_ref[heads[h], :] += jnp.dot(vt_ref[heads[h], rows], p.astype(BF16),
                                           preferred_element_type=F32)
        return tuple(out)

    den = lax.fori_loop(0, nchunks, pass_b, tuple(den0))
    for h in range(N_HEADS):
        ot_ref[heads[h], :] = ot_ref[heads[h], :] / jnp.sum(den[h], axis=0, keepdims=True)
    o_ref[...] = ot_ref[...].T.astype(BF16)


def _attention(q3, iq3, iwt, k3, ik, vt, km3, ikm, vtm, tab, tabm, batch, seq):
    n = batch * seq
    tiles = seq // Q_TILE
    full = lambda a: pl.BlockSpec(a.shape, lambda b, j: (0,) * a.ndim, pipeline_mode=pl.Buffered(1))
    return pl.pallas_call(
        _attn_kernel,
        grid=(batch, tiles),
        in_specs=[
            pl.BlockSpec((N_HEADS, Q_TILE, HEAD_DIM), lambda b, j: (0, b * tiles + j, 0)),
            pl.BlockSpec((IDX_HEADS, Q_TILE, IDX_DIM), lambda b, j: (0, b * tiles + j, 0)),
            pl.BlockSpec((IDX_HEADS, Q_TILE), lambda b, j: (0, b * tiles + j)),
            pl.BlockSpec((N_HEADS, seq, HEAD_DIM), lambda b, j: (0, b, 0)),
            pl.BlockSpec((seq, IDX_DIM), lambda b, j: (b, 0)),
            pl.BlockSpec((ATTN_WIDTH, seq), lambda b, j: (0, b)),
            full(km3), full(ikm), full(vtm), full(tab), full(tabm),
        ],
        out_specs=pl.BlockSpec((Q_TILE, ATTN_WIDTH), lambda b, j: (b * tiles + j, 0)),
        out_shape=jax.ShapeDtypeStruct((n, ATTN_WIDTH), BF16),
        scratch_shapes=[
            pltpu.VMEM((seq, Q_TILE), F32),
            pltpu.VMEM((N_META, Q_TILE), F32),
            pltpu.VMEM((N_HEADS, seq, Q_TILE), F32),
            pltpu.VMEM((N_HEADS, N_META, Q_TILE), F32),
            pltpu.VMEM((ATTN_WIDTH, Q_TILE), F32),
            pltpu.VMEM((8, Q_TILE), I32),
            pltpu.VMEM((seq, Q_TILE), jnp.int16),
            pltpu.VMEM((seq, Q_TILE), jnp.int16),
            pltpu.VMEM((N_META, Q_TILE), jnp.int16),
            pltpu.VMEM((N_META, Q_TILE), jnp.int16),
        ],
        compiler_params=pltpu.CompilerParams(dimension_semantics=("arbitrary", "arbitrary"),
                                             vmem_limit_bytes=VMEM_LIMIT),
        name="attn",
    )(q3, iq3, iwt, k3, ik, vt, km3, ikm, vtm, tab, tabm)


def _mix_kernel(x_ref, attn_ref, y_ref, yprev_ref, ymeta_ref, gate_ref,
                cw_ref, cb_ref, lng_ref, lnb_ref, wco_ref, wao_ref, wout_ref, nf_ref, wrt_ref, br_ref,
                h2_ref, hn2_ref, eid_ref, rank_ref, gcol_ref, cnt_ref,
                win_ref, shift_ref, base_ref, *, tiles_per_seq):
    i = pl.program_id(0)

    @pl.when(i == 0)
    def _():
        base_ref[...] = jnp.zeros_like(base_ref)

    first = (i % tiles_per_seq) == 0
    win_ref[0:HALO, :] = jnp.where(first, ymeta_ref[...], yprev_ref[...])
    win_ref[HALO:, :] = y_ref[...]
    lead = HALO - (CONV_WIDTH - 1)
    for b in range(8):
        span = ROW_TILE + 8 * (len(range(b, CONV_WIDTH, 8)) - 1)
        shift_ref[b, 0:span, :] = win_ref[pl.ds(lead + b, span), :]
    blocks = []
    for r0 in range(0, ROW_TILE, CONV_ROWS):
        acc = jnp.broadcast_to(cb_ref[...], (CONV_ROWS, CONV_CH))
        for b in range(8):
            for a, w in enumerate(range(b, CONV_WIDTH, 8)):
                acc = acc + cw_ref[w:w + 1, :] * shift_ref[b, r0 + 8 * a:r0 + 8 * a + CONV_ROWS, :]
        blocks.append(acc)
    yc = jnp.concatenate(blocks, axis=0)
    mu = jnp.mean(yc, axis=-1, keepdims=True)
    var = jnp.mean(jnp.square(yc - mu), axis=-1, keepdims=True)
    yn = (yc - mu) * lax.rsqrt(var + EPS) * lng_ref[...] + lnb_ref[...]
    ys = yn * _sigmoid(yn)
    y_b = jnp.dot(ys.astype(BF16), wco_ref[...], preferred_element_type=F32)

    y_a = jnp.dot(attn_ref[...], wao_ref[...], preferred_element_type=F32)
    merged = gate_ref[:, :D_MODEL] * y_a + gate_ref[:, D_MODEL:] * y_b
    h2 = x_ref[...] + jnp.dot(merged.astype(BF16), wout_ref[...], preferred_element_type=F32)
    h2_ref[...] = h2
    ms = jnp.mean(h2 * h2, axis=-1, keepdims=True)
    hn2 = h2 * lax.rsqrt(ms + EPS) * nf_ref[...]
    hn2_ref[:, 0, :] = _pack_bf16_pairs(hn2)

    logits = lax.dot_general(wrt_ref[...], hn2, NT_DIMS, preferred_element_type=F32,
                             precision=lax.Precision.HIGHEST) + br_ref[...]
    erow = lax.broadcasted_iota(I32, (N_EXPERTS, ROW_TILE), 0)
    vals, ids = [], []
    l = logits
    for _ in range(TOP_K_EXPERTS):
        m = jnp.max(l, axis=0, keepdims=True)
        idx = jnp.min(jnp.where(l == m, erow, N_EXPERTS), axis=0, keepdims=True)
        vals.append(m)
        ids.append(idx)
        l = jnp.where(erow == idx, -jnp.inf, l)
    ex = [jnp.exp(v - vals[0]) for v in vals]
    den = ex[0] + ex[1] + ex[2] + ex[3]
    gates = [e / den for e in ex]

    onehot = [(erow == idx) for idx in ids]
    oh = jnp.concatenate([jnp.where(o, 1.0, 0.0) for o in onehot], axis=0)
    tr = lax.broadcasted_iota(I32, (ROW_TILE, ROW_TILE), 0)
    tc = lax.broadcasted_iota(I32, (ROW_TILE, ROW_TILE), 1)
    upper = jnp.where(tr <= tc, 1.0, 0.0).astype(BF16)
    pref = jnp.dot(oh.astype(BF16), upper, preferred_element_type=F32)
    offs = base_ref[:, 0:1]
    ranks = []
    for kk in range(TOP_K_EXPERTS):
        pk = pref[kk * N_EXPERTS:(kk + 1) * N_EXPERTS, :]
        r = jnp.sum(jnp.where(onehot[kk], offs + pk - 1.0, 0.0), axis=0, keepdims=True)
        ranks.append(r.astype(I32))
        offs = offs + pk[:, ROW_TILE - 1:ROW_TILE]
    base_ref[...] = jnp.broadcast_to(offs, base_ref.shape)
    cnt_ref[...] = jnp.broadcast_to(offs, cnt_ref.shape)

    zi = jnp.zeros((8 - TOP_K_EXPERTS, ROW_TILE), I32)
    eid_ref[...] = jnp.concatenate(ids + [zi], axis=0)
    rank_ref[...] = jnp.concatenate(ranks + [zi], axis=0)
    g8 = jnp.concatenate(gates + [jnp.zeros((128 - TOP_K_EXPERTS, ROW_TILE), F32)], axis=0)
    gcol_ref[...] = g8.T


def _mix(x2d, attn, y, ymeta, gate, wts, seq):
    n = x2d.shape[0]
    tiles_per_seq = seq // ROW_TILE
    halo_per_tile = ROW_TILE // HALO
    full = lambda a: pl.BlockSpec(a.shape, lambda i: (0,) * a.ndim)
    row = lambda w: pl.BlockSpec((ROW_TILE, w), lambda i: (i, 0))
    lane = lambda r: pl.BlockSpec((r, ROW_TILE), lambda i: (0, i))
    out_shape = (
        jax.ShapeDtypeStruct((n, D_MODEL), F32),
        jax.ShapeDtypeStruct((n, 1, D_MODEL // 2), I32),
        jax.ShapeDtypeStruct((8, n), I32),
        jax.ShapeDtypeStruct((8, n), I32),
        jax.ShapeDtypeStruct((n, 128), F32),
        jax.ShapeDtypeStruct((N_EXPERTS, 128), F32),
    )
    out_specs = (row(D_MODEL), pl.BlockSpec((ROW_TILE, 1, D_MODEL // 2), lambda i: (i, 0, 0)),
                 lane(8), lane(8), row(128), full(out_shape[5]))
    return pl.pallas_call(
        functools.partial(_mix_kernel, tiles_per_seq=tiles_per_seq),
        grid=(n // ROW_TILE,),
        in_specs=[row(D_MODEL), row(ATTN_WIDTH), row(CONV_CH),
                  pl.BlockSpec((HALO, CONV_CH), lambda i: (jnp.maximum(i * halo_per_tile - 1, 0), 0)),
                  full(ymeta), row(2 * D_MODEL)] + [full(w) for w in wts],
        out_specs=out_specs,
        out_shape=out_shape,
        scratch_shapes=[pltpu.VMEM((HALO + ROW_TILE, CONV_CH), F32),
                        pltpu.VMEM((8, HALO + ROW_TILE, CONV_CH), F32),
                        pltpu.VMEM((N_EXPERTS, 128), F32)],
        compiler_params=pltpu.CompilerParams(dimension_semantics=("arbitrary",),
                                             vmem_limit_bytes=VMEM_LIMIT),
        name="mix",
    )(x2d, attn, y, y, ymeta, gate, *wts)


def _slots_kernel(starts_ref, eid_ref, rank_ref, slot_ref):
    eid = eid_ref[...]
    base = jnp.zeros(eid.shape, I32)
    for e in range(N_EXPERTS):
        base = jnp.where(eid == e, starts_ref[e], base)
    slot = base + rank_ref[...]
    for t in range(slot_ref.shape[0]):
        for kk in range(TOP_K_EXPERTS):
            slot_ref[t, :, kk * ROW_TILE:(kk + 1) * ROW_TILE] = slot[kk:kk + 1, t * ROW_TILE:(t + 1) * ROW_TILE]


def _slots(starts, eid8, rank8):
    n = eid8.shape[1]
    tiles = 8
    spec = pl.BlockSpec((8, tiles * ROW_TILE), lambda i: (0, i))
    return pl.pallas_call(
        _slots_kernel,
        grid=(n // (tiles * ROW_TILE),),
        in_specs=[pl.BlockSpec(memory_space=pltpu.SMEM), spec, spec],
        out_specs=pl.BlockSpec((tiles, 1, COPIES_PER_TILE), lambda i: (i, 0, 0)),
        out_shape=jax.ShapeDtypeStruct((n // ROW_TILE, 1, COPIES_PER_TILE), I32),
        compiler_params=pltpu.CompilerParams(dimension_semantics=("arbitrary",)),
        name="slots",
    )(starts, eid8, rank8)


COPIES_PER_TILE = TOP_K_EXPERTS * ROW_TILE
SLOT_UNROLL = 16


def _slot_source_kernel(slot_ref, init_ref, src_ref, sem, *, n):
    i = pl.program_id(0)

    @pl.when(i == 0)
    def _():
        cp = pltpu.make_async_copy(init_ref, src_ref, sem)
        cp.start()
        cp.wait()

    for kk in range(TOP_K_EXPERTS):
        def body(g, carry):
            for u in range(SLOT_UNROLL):
                t = g * SLOT_UNROLL + u
                src_ref[slot_ref[0, 0, kk * ROW_TILE + t]] = kk * n + i * ROW_TILE + t
            return carry
        lax.fori_loop(0, ROW_TILE // SLOT_UNROLL, body, 0)


def _slot_sources(slot_tiles, init, n):
    return pl.pallas_call(
        functools.partial(_slot_source_kernel, n=n),
        grid=(slot_tiles.shape[0],),
        in_specs=[pl.BlockSpec((1, 1, COPIES_PER_TILE), lambda i: (i, 0, 0), memory_space=pltpu.SMEM),
                  pl.BlockSpec(memory_space=pl.ANY)],
        out_specs=pl.BlockSpec(memory_space=pltpu.SMEM),
        out_shape=jax.ShapeDtypeStruct(init.shape, I32),
        scratch_shapes=[pltpu.SemaphoreType.DMA(())],
        compiler_params=pltpu.CompilerParams(dimension_semantics=("arbitrary",)),
        name="slot_sources",
    )(slot_tiles, init)


def _expert_kernel(te_ref, nused_ref, src_ref, src_next_ref, src_prev_ref, hn_hbm,
                   wug_ref, bug_ref, wd_ref, bd_ref, yk_hbm, xbuf, ybuf, wug_bf, wd_bf, gsem, ssem, *, n):
    i = pl.program_id(0)
    nused = nused_ref[0]

    def gather_copy(src, r, b):
        tok = src[0, 0, r] & (n - 1)
        return pltpu.make_async_copy(hn_hbm.at[tok], xbuf.at[b, r], gsem.at[b])

    def scatter_copy(src, r, b):
        return pltpu.make_async_copy(ybuf.at[b, r], yk_hbm.at[src[0, 0, r]], ssem.at[b])

    @pl.when(i <= nused)
    def _():
        cur = i % 2
        oth = 1 - cur

        def wait_scatter(b):
            pltpu.make_async_copy(ybuf.at[b], yk_hbm.at[pl.ds(0, ROW_TILE)], ssem.at[b]).wait()

        @pl.when(i == 0)
        def _():
            for r in range(ROW_TILE):
                gather_copy(src_ref, r, 0).start()
            ybuf[1] = jnp.zeros(ybuf.shape[1:], F32)
            spare = pltpu.make_async_copy(ybuf.at[1], yk_hbm.at[pl.ds(TOP_K_EXPERTS * n, ROW_TILE)],
                                          ssem.at[1])
            spare.start()
            spare.wait()

        @pl.when(i < nused)
        def _():
            for r in range(ROW_TILE):
                gather_copy(src_next_ref, r, oth).start()

        @pl.when(i > 0)
        def _():
            for r in range(ROW_TILE):
                scatter_copy(src_prev_ref, r, oth).start()

        pltpu.make_async_copy(hn_hbm.at[pl.ds(0, ROW_TILE)], xbuf.at[cur], gsem.at[cur]).wait()

        prev = te_ref[jnp.maximum(i - 1, 0)]
        fresh = (i == 0) | (te_ref[jnp.minimum(i, nused - 1)] != prev)

        @pl.when(fresh)
        def _():
            wug_bf[...] = wug_ref[0].astype(BF16)
            wd_bf[...] = wd_ref[0].astype(BF16)

        xb = _unpack_bf16_pairs(xbuf[cur, :, 0, :])
        ug = jnp.dot(xb, wug_bf[...], preferred_element_type=F32) + bug_ref[0]
        gate = jnp.minimum(ug[:, :D_FF], SWIGLU_LIMIT)
        up = jnp.clip(ug[:, D_FF:], -SWIGLU_LIMIT, SWIGLU_LIMIT)
        act = (up + 1.0) * gate * _sigmoid(SWIGLU_ALPHA * gate)
        y = jnp.dot(act.astype(BF16), wd_bf[...], preferred_element_type=F32) + bd_ref[0]

        @pl.when(i > 1)
        def _():
            wait_scatter(cur)

        ybuf[cur, :, 0, :] = y

        @pl.when((i == nused) & (i > 0))
        def _():
            wait_scatter(oth)


def _experts(tile_expert, nused, src_tiles, hn2p, w_ug, b_ug, w_down, b_down, n):
    ntiles = src_tiles.shape[0]
    clamp = lambda i, nu: jnp.maximum(jnp.minimum(i, nu[0] - 1), 0)
    smem = lambda f: pl.BlockSpec((1, 1, ROW_TILE), f, memory_space=pltpu.SMEM)
    expert = lambda i, te, nu: (te[clamp(i, nu)], 0, 0)
    grid_spec = pltpu.PrefetchScalarGridSpec(
        num_scalar_prefetch=2,
        grid=(ntiles + 1,),
        in_specs=[
            smem(lambda i, te, nu: (clamp(i, nu), 0, 0)),
            smem(lambda i, te, nu: (clamp(i + 1, nu), 0, 0)),
            smem(lambda i, te, nu: (clamp(i - 1, nu), 0, 0)),
            pl.BlockSpec(memory_space=pl.ANY),
            pl.BlockSpec((1, D_MODEL, 2 * D_FF), expert),
            pl.BlockSpec((1, 1, 2 * D_FF), expert),
            pl.BlockSpec((1, D_FF, D_MODEL), expert),
            pl.BlockSpec((1, 1, D_MODEL), expert),
        ],
        out_specs=pl.BlockSpec(memory_space=pl.ANY),
        scratch_shapes=[pltpu.VMEM((2, ROW_TILE, 1, D_MODEL // 2), I32),
                        pltpu.VMEM((2, ROW_TILE, 1, D_MODEL), F32),
                        pltpu.VMEM((D_MODEL, 2 * D_FF), BF16), pltpu.VMEM((D_FF, D_MODEL), BF16),
                        pltpu.SemaphoreType.DMA((2,)), pltpu.SemaphoreType.DMA((2,))],
    )
    return pl.pallas_call(
        functools.partial(_expert_kernel, n=n),
        grid_spec=grid_spec,
        out_shape=jax.ShapeDtypeStruct((TOP_K_EXPERTS * n + ROW_TILE, 1, D_MODEL), F32),
        compiler_params=pltpu.CompilerParams(dimension_semantics=("arbitrary",),
                                             vmem_limit_bytes=VMEM_LIMIT),
        name="experts",
    )(tile_expert, nused, src_tiles, src_tiles, src_tiles, hn2p,
      w_ug, b_ug[:, None, :], w_down, b_down[:, None, :])


def _final_kernel(h2_ref, gcol_ref, nf_ref, *rest):
    y_refs, o_ref = rest[:TOP_K_EXPERTS], rest[TOP_K_EXPERTS]
    h = h2_ref[...]
    for kk in range(TOP_K_EXPERTS):
        h = h + gcol_ref[:, kk:kk + 1] * y_refs[kk][:, 0, :]
    ms = jnp.mean(h * h, axis=-1, keepdims=True)
    o_ref[...] = h * lax.rsqrt(ms + EPS) * nf_ref[...]


def _final(h2, gcol, nf, yk):
    n = h2.shape[0]
    ntiles = n // ROW_TILE
    choice = lambda kk: pl.BlockSpec((ROW_TILE, 1, D_MODEL), lambda i: (kk * ntiles + i, 0, 0))
    return pl.pallas_call(
        _final_kernel,
        grid=(ntiles,),
        in_specs=[pl.BlockSpec((ROW_TILE, D_MODEL), lambda i: (i, 0)),
                  pl.BlockSpec((ROW_TILE, 128), lambda i: (i, 0)),
                  pl.BlockSpec((1, D_MODEL), lambda i: (0, 0))] + [choice(kk) for kk in range(TOP_K_EXPERTS)],
        out_specs=pl.BlockSpec((ROW_TILE, D_MODEL), lambda i: (i, 0)),
        out_shape=jax.ShapeDtypeStruct((n, D_MODEL), F32),
        compiler_params=pltpu.CompilerParams(dimension_semantics=("arbitrary",),
                                             vmem_limit_bytes=VMEM_LIMIT),
        name="final",
    )(h2, gcol, nf, *([yk] * TOP_K_EXPERTS))


def _split_w_in(w_in, b_gate):
    c = ATTN_WIDTH
    o = 0
    wq = w_in[:, o:o + c]; o += c
    wk = w_in[:, o:o + c]; o += c
    wv = w_in[:, o:o + c]; o += c
    wiq = w_in[:, o:o + IDX_HEADS * IDX_DIM]; o += IDX_HEADS * IDX_DIM
    wik = w_in[:, o:o + IDX_DIM]; o += IDX_DIM
    wiw = w_in[:, o:o + IDX_HEADS]; o += IDX_HEADS
    wglu = w_in[:, o:o + 2 * CONV_CH]; o += 2 * CONV_CH
    wgate = w_in[:, o:]
    wiwt = jnp.concatenate([wiw.T, jnp.zeros((16 - IDX_HEADS, D_MODEL), w_in.dtype)], axis=0)
    bf = lambda a: a.astype(BF16)
    return (bf(wq), bf(wk), bf(wiq), bf(wik), bf(wv.T), bf(wiwt), bf(wglu), bf(wgate),
            b_gate[None, :].astype(F32))


def kernel(x, meta_tokens, rel_bias, norm_mix, w_in, b_gate, w_attn_out, conv_w, conv_b, conv_ln_g, conv_ln_b, w_conv_out, w_out, norm_ffn, w_router, b_router, w_up_gate, b_up_gate, w_down, b_down, norm_final):
    batch, seq, d = x.shape
    n = batch * seq
    x2d = x.reshape(n, d)

    wts = _split_w_in(w_in[0], b_gate[0])
    g_mix = norm_mix[0][None, :]
    q3, k3, iq3, ik, vt, iwt, y, gate = _project(x2d, g_mix, wts, PROJ_TILE)
    _, km3, _, ikm, vtm, _, ym, _ = _project(meta_tokens.astype(F32), g_mix, wts, N_META)

    tab, tabm = _bias_tables(rel_bias.astype(F32))
    attn = _attention(q3, iq3, iwt, k3, ik, vt, km3, ikm, vtm, tab, tabm, batch, seq)

    ymeta = jnp.concatenate([jnp.zeros((HALO - N_META, CONV_CH), F32), ym], axis=0)
    cw = jnp.concatenate([conv_w[0], jnp.zeros((32 - CONV_WIDTH, CONV_CH), F32)], axis=0)
    mix_w = (cw, conv_b[0][None, :], conv_ln_g[0][None, :], conv_ln_b[0][None, :],
             w_conv_out[0].astype(BF16), w_attn_out[0].astype(BF16), w_out[0].astype(BF16),
             norm_ffn[0][None, :], w_router[0].T, b_router[0][:, None])
    h2, hn2, eid8, rank8, gcol, cnt = _mix(x2d, attn, y, ymeta, gate, mix_w, seq)

    counts = cnt[:, 0].astype(I32)
    padded = ((counts + ROW_TILE - 1) // ROW_TILE) * ROW_TILE
    ends = jnp.cumsum(padded)
    starts = ends - padded
    slot_tiles = _slots(starts, eid8, rank8)
    nslots = n * TOP_K_EXPERTS + N_EXPERTS * ROW_TILE
    ntiles = nslots // ROW_TILE
    tile_start = jnp.arange(ntiles, dtype=I32) * ROW_TILE
    nused = (ends[-1] // ROW_TILE).astype(I32)
    last_start = jnp.maximum(ends[-1] - ROW_TILE, 0)
    tile_expert = jnp.sum((jnp.minimum(tile_start, last_start)[:, None] >= ends[None, :]).astype(I32), axis=1)

    spare = TOP_K_EXPERTS * n + jnp.arange(nslots, dtype=I32) % ROW_TILE
    src_tiles = _slot_sources(slot_tiles, spare, n).reshape(ntiles, 1, ROW_TILE)
    yk = _experts(tile_expert, nused[None], src_tiles, hn2, w_up_gate[0], b_up_gate[0], w_down[0], b_down[0], n)
    out = _final(h2, gcol, norm_final[None, :], yk)
    return out.reshape(batch, seq, d)
```

```python
import functools
import math

import jax
import jax.numpy as jnp
from jax import lax
from jax.experimental import pallas as pl
from jax.experimental.pallas import tpu as pltpu

F32 = jnp.float32
BF16 = jnp.bfloat16
I32 = jnp.int32

D_MODEL = 1024
N_META = 16
N_HEADS = 8
HEAD_DIM = 64
ATTN_WIDTH = N_HEADS * HEAD_DIM
IDX_HEADS = 8
IDX_DIM = 64
TOPK_MAX = 256
CONV_CH = 512
CONV_WIDTH = 31
N_BUCKETS = 32
MAX_DISTANCE = 128
N_EXPERTS = 32
TOP_K_EXPERTS = 4
D_FF = 1024
SWIGLU_LIMIT = 7.0
SWIGLU_ALPHA = 1.702
EPS = 1e-6
IDX_SCALE = (IDX_DIM ** -0.5) * (IDX_HEADS ** -0.5)

ROW_TILE = 256
PROJ_TILE = 512
Q_TILE = 256
K_CHUNK = 256
HALO = 32
CONV_ROWS = 32
NEG = -1e30
LOG2E = math.log2(math.e)
VMEM_LIMIT = 56 * 1024 * 1024

NT_DIMS = (((1,), (1,)), ((), ()))


def _sigmoid(x):
    return 1.0 / (1.0 + jnp.exp(-x))


def _pack_bf16_pairs(x):
    w = x.shape[1] // 2
    hi = lax.bitcast_convert_type(x[:, :w].astype(BF16).astype(F32), jnp.uint32)
    lo = lax.bitcast_convert_type(x[:, w:].astype(BF16).astype(F32), jnp.uint32)
    return lax.bitcast_convert_type(hi | (lo >> 16), I32)


def _unpack_bf16_pairs(p):
    u = lax.bitcast_convert_type(p, jnp.uint32)
    hi = lax.bitcast_convert_type(u & jnp.uint32(0xFFFF0000), F32)
    lo = lax.bitcast_convert_type(u << 16, F32)
    return jnp.concatenate([hi, lo], axis=1).astype(BF16)


def _proj_kernel(x_ref, g_ref, wq_ref, wk_ref, wiq_ref, wik_ref, wvt_ref, wiwt_ref, wglu_ref,
                 wgate_ref, bgate_ref,
                 q_ref, k_ref, iq_ref, ik_ref, vt_ref, iwt_ref, y_ref, gate_ref):
    x = x_ref[...]
    ms = jnp.mean(x * x, axis=-1, keepdims=True)
    xn = (x * lax.rsqrt(ms + EPS) * g_ref[...]).astype(BF16)

    q = jnp.dot(xn, wq_ref[...], preferred_element_type=F32) * (HEAD_DIM ** -0.5 * LOG2E)
    k = jnp.dot(xn, wk_ref[...], preferred_element_type=F32)
    iq = jnp.dot(xn, wiq_ref[...], preferred_element_type=F32)
    for h in range(N_HEADS):
        sl = slice(h * HEAD_DIM, (h + 1) * HEAD_DIM)
        q_ref[h] = q[:, sl].astype(BF16)
        k_ref[h] = k[:, sl].astype(BF16)
        iq_ref[h] = iq[:, sl].astype(BF16)
    ik_ref[...] = jnp.dot(xn, wik_ref[...], preferred_element_type=F32).astype(BF16)
    vt_ref[...] = lax.dot_general(wvt_ref[...], xn, NT_DIMS, preferred_element_type=F32).astype(BF16)
    iwt = lax.dot_general(wiwt_ref[...], xn, NT_DIMS, preferred_element_type=F32)
    iwt_ref[...] = iwt[:IDX_HEADS] * IDX_SCALE
    glu = jnp.dot(xn, wglu_ref[...], preferred_element_type=F32)
    y_ref[...] = glu[:, :CONV_CH] * _sigmoid(glu[:, CONV_CH:])
    gate = jnp.dot(xn, wgate_ref[...], preferred_element_type=F32) + bgate_ref[...]
    gate_ref[...] = _sigmoid(gate)


def _project(x2d, g, wts, tm):
    n = x2d.shape[0]
    wq, wk, wiq, wik, wvt, wiwt, wglu, wgate, bgate = wts
    full = lambda a: pl.BlockSpec(a.shape, lambda i: (0,) * a.ndim, pipeline_mode=pl.Buffered(1))
    out_shape = (
        jax.ShapeDtypeStruct((N_HEADS, n, HEAD_DIM), BF16),
        jax.ShapeDtypeStruct((N_HEADS, n, HEAD_DIM), BF16),
        jax.ShapeDtypeStruct((IDX_HEADS, n, IDX_DIM), BF16),
        jax.ShapeDtypeStruct((n, IDX_DIM), BF16),
        jax.ShapeDtypeStruct((ATTN_WIDTH, n), BF16),
        jax.ShapeDtypeStruct((IDX_HEADS, n), F32),
        jax.ShapeDtypeStruct((n, CONV_CH), F32),
        jax.ShapeDtypeStruct((n, 2 * D_MODEL), F32),
    )
    out_specs = (
        pl.BlockSpec((N_HEADS, tm, HEAD_DIM), lambda i: (0, i, 0)),
        pl.BlockSpec((N_HEADS, tm, HEAD_DIM), lambda i: (0, i, 0)),
        pl.BlockSpec((IDX_HEADS, tm, IDX_DIM), lambda i: (0, i, 0)),
        pl.BlockSpec((tm, IDX_DIM), lambda i: (i, 0)),
        pl.BlockSpec((ATTN_WIDTH, tm), lambda i: (0, i)),
        pl.BlockSpec((IDX_HEADS, tm), lambda i: (0, i)),
        pl.BlockSpec((tm, CONV_CH), lambda i: (i, 0)),
        pl.BlockSpec((tm, 2 * D_MODEL), lambda i: (i, 0)),
    )
    return pl.pallas_call(
        _proj_kernel,
        grid=(n // tm,),
        in_specs=[pl.BlockSpec((tm, D_MODEL), lambda i: (i, 0)), full(g), full(wq), full(wk), full(wiq),
                  full(wik), full(wvt), full(wiwt), full(wglu), full(wgate), full(bgate)],
        out_specs=out_specs,
        out_shape=out_shape,
        compiler_params=pltpu.CompilerParams(dimension_semantics=("arbitrary",),
                                             vmem_limit_bytes=VMEM_LIMIT),
        name="proj",
    )(x2d, g, wq, wk, wiq, wik, wvt, wiwt, wglu, wgate, bgate)


def _t5_bucket(n):
    max_exact = N_BUCKETS // 2
    nf = jnp.maximum(n, 1).astype(F32)
    large = max_exact + (jnp.log(nf / max_exact) / math.log(MAX_DISTANCE / max_exact)
                         * (N_BUCKETS - max_exact)).astype(I32)
    large = jnp.minimum(large, N_BUCKETS - 1)
    return jnp.where(n < max_exact, n, large)


def _bias_lookup(rb_ref, dist, h):
    bucket = _t5_bucket(jnp.maximum(dist, 0))
    out = jnp.full(dist.shape, NEG, F32)
    for b in range(N_BUCKETS):
        out = jnp.where(bucket == b, rb_ref[b, h] * LOG2E, out)
    return jnp.where(dist >= 0, out, NEG)


def _bias_kernel(rb_ref, tab_ref, tabm_ref):
    kind = pl.program_id(0)
    r = pl.program_id(1)
    rows = tab_ref.shape[2]
    s = lax.broadcasted_iota(I32, (rows, Q_TILE), 0) + r * rows
    t = lax.broadcasted_iota(I32, (rows, Q_TILE), 1)
    dist = jnp.where(kind == 2, 2 * K_CHUNK, t - s + kind * K_CHUNK)
    for h in range(N_HEADS):
        tab_ref[0, h] = _bias_lookup(rb_ref, dist, h)
    m = lax.broadcasted_iota(I32, (N_META, Q_TILE), 0)
    tm_ = lax.broadcasted_iota(I32, (N_META, Q_TILE), 1)
    distm = jnp.where(kind == 0, N_META + tm_ - m, 2 * K_CHUNK)
    for h in range(N_HEADS):
        tabm_ref[0, h] = _bias_lookup(rb_ref, distm, h)


def _bias_tables(rel_bias):
    rows = 64
    return pl.pallas_call(
        _bias_kernel,
        grid=(3, K_CHUNK // rows),
        in_specs=[pl.BlockSpec(memory_space=pltpu.SMEM)],
        out_specs=(pl.BlockSpec((1, N_HEADS, rows, Q_TILE), lambda kd, r: (kd, 0, r, 0)),
                   pl.BlockSpec((1, N_HEADS, N_META, Q_TILE), lambda kd, r: (kd, 0, 0, 0))),
        out_shape=(jax.ShapeDtypeStruct((3, N_HEADS, K_CHUNK, Q_TILE), F32),
                   jax.ShapeDtypeStruct((3, N_HEADS, N_META, Q_TILE), F32)),
        compiler_params=pltpu.CompilerParams(dimension_semantics=("arbitrary", "arbitrary")),
        name="bias_tables",
    )(rel_bias)


def _order_bits_to_float(u):
    bits = jnp.where(u < 0, u ^ jnp.int32(-2 ** 31), ~u)
    return lax.bitcast_convert_type(bits, F32)


def _fold_rows(x, op):
    r, l = x.shape
    x3 = x.reshape(r // 8, 8, l)
    return op(x3, axis=0)


def _attn_kernel(q_ref, iq_ref, iwt_ref, k_ref, ik_ref, vt_ref, km_ref, ikm_ref, vtm_ref,
                 tab_ref, tabm_ref, o_ref, sc_ref, scm_ref, l_ref, lm_ref, ot_ref, cst_ref,
                 hi_ref, lo_ref, him_ref, lom_ref):
    j = pl.program_id(1)
    nchunks = j + 1
    iw = iwt_ref[...]

    def chunk_rows(c):
        return pl.ds(pl.multiple_of(c * K_CHUNK, K_CHUNK), K_CHUNK)

    def idx_scores(ikc):
        acc = None
        for h in range(IDX_HEADS):
            s = lax.dot_general(ikc, iq_ref[h], NT_DIMS, preferred_element_type=F32)
            term = jnp.maximum(s, 0.0) * iw[h:h + 1, :]
            acc = term if acc is None else acc + term
        return acc

    scm_ref[...] = idx_scores(ikm_ref[...])

    row_minus_col = (lax.broadcasted_iota(I32, (K_CHUNK, Q_TILE), 0)
                     - lax.broadcasted_iota(I32, (K_CHUNK, Q_TILE), 1))

    def p1(c, carry):
        rows = chunk_rows(c)
        future = row_minus_col > jnp.where(c == j, 0, K_CHUNK)
        sc_ref[rows, :] = jnp.where(future, -jnp.inf, idx_scores(ik_ref[rows, :]))
        return carry

    lax.fori_loop(0, nchunks, p1, 0)

    def count(pred):
        def body(c, acc):
            blk = sc_ref[chunk_rows(c), :]
            return acc + _fold_rows(jnp.where(pred(blk, c), 1, 0).astype(I32), jnp.sum)
        acc = lax.fori_loop(0, nchunks, body, jnp.zeros((8, Q_TILE), I32))
        acc = acc + _fold_rows(jnp.where(pred(scm_ref[...], -1), 1, 0).astype(I32), jnp.sum)
        return jnp.sum(acc, axis=0, keepdims=True)

    I16 = jnp.int16
    LOW = -2 ** 15

    def split_key(s):
        bits = lax.bitcast_convert_type(s, I32)
        key = jnp.where(bits < 0, bits ^ jnp.int32(0x7FFFFFFF), bits)
        hi = lax.shift_right_arithmetic(key, 16).astype(I16)
        lo = ((key & 0xFFFF) + LOW).astype(I16)
        return hi, lo

    him_ref[...], lom_ref[...] = split_key(scm_ref[...])

    def p2(c, carry):
        rows = chunk_rows(c)
        hi_ref[rows, :], lo_ref[rows, :] = split_key(sc_ref[rows, :])
        return carry

    lax.fori_loop(0, nchunks, p2, 0)

    @pl.when((nchunks & 1) == 1)
    def _():
        rows = chunk_rows(nchunks)
        hi_ref[rows, :] = jnp.full((K_CHUNK, Q_TILE), LOW, I16)
        lo_ref[rows, :] = jnp.full((K_CHUNK, Q_TILE), LOW, I16)

    def tree_sum16(ind):
        parts = [ind[a * 16:(a + 1) * 16, :] for a in range(ind.shape[0] // 16)]
        while len(parts) > 1:
            parts = [parts[a] + parts[a + 1] for a in range(0, len(parts), 2)]
        return parts[0]

    def count16(ref, mref, pred):
        one, zero = jnp.ones((), BF16), jnp.zeros((), BF16)

        def body(cc, acc):
            rows = pl.ds(pl.multiple_of(cc * (2 * K_CHUNK), 2 * K_CHUNK), 2 * K_CHUNK)
            return acc + tree_sum16(jnp.where(pred(ref[rows, :]), one, zero)).astype(F32)
        acc = lax.fori_loop(0, lax.shift_right_logical(nchunks + 1, 1), body, jnp.zeros((16, Q_TILE), F32))
        acc = acc + jnp.where(pred(mref[...]), one, zero).astype(F32)
        return jnp.sum(acc, axis=0, keepdims=True).astype(I32)

    def to16(u):
        return (u + LOW).astype(I16)

    def search_hi(i, u):
        cand = u | lax.shift_left(jnp.int32(1), 15 - i)
        c16 = to16(cand)
        cnt = count16(hi_ref, him_ref, lambda blk: blk >= c16)
        return jnp.where(cnt >= TOPK_MAX, cand, u)

    u_hi = lax.fori_loop(0, 16, search_hi, jnp.zeros((1, Q_TILE), I32))
    t16 = to16(u_hi)
    need = TOPK_MAX - count16(hi_ref, him_ref, lambda blk: blk > t16)

    lom_ref[...] = jnp.where(him_ref[...] == t16, lom_ref[...], jnp.int16(LOW))

    def p2b(c, carry):
        rows = chunk_rows(c)
        lo_ref[rows, :] = jnp.where(hi_ref[rows, :] == t16, lo_ref[rows, :], jnp.int16(LOW))
        return carry

    lax.fori_loop(0, nchunks, p2b, 0)

    def search_lo(i, v):
        cand = v | lax.shift_left(jnp.int32(1), 15 - i)
        c16 = to16(cand)
        cnt = count16(lo_ref, lom_ref, lambda blk: blk >= c16)
        return jnp.where(cnt >= need, cand, v)

    u_lo = lax.fori_loop(0, 16, search_lo, jnp.zeros((1, Q_TILE), I32))
    thr_key = lax.shift_left(u_hi + LOW, 16) | u_lo
    thr_bits = jnp.where(thr_key < 0, thr_key ^ jnp.int32(0x7FFFFFFF), thr_key)
    thr = lax.bitcast_convert_type(thr_bits, F32)

    cnt_ge = count(lambda blk, c: blk >= thr)
    tie = (cnt_ge > TOPK_MAX) & (thr > -jnp.inf)
    any_tie = jnp.max(tie.astype(I32))
    cst_ref[...] = jnp.full((8, Q_TILE), 2 ** 30, I32)

    def pos_of(c, shape):
        r = lax.broadcasted_iota(I32, shape, 0)
        return jnp.where(c < 0, r, r + N_META + c * K_CHUNK)

    @pl.when(any_tie > 0)
    def _():
        cnt_gt = count(lambda blk, c: blk > thr)
        need = TOPK_MAX - cnt_gt

        def bis_pos(i, cs):
            cand = cs | lax.shift_left(jnp.int32(1), 11 - i)
            f = count(lambda blk, c: (blk == thr) & (pos_of(c, blk.shape) < cand))
            return jnp.where(f <= need, cand, cs)

        cs = lax.fori_loop(0, 12, bis_pos, jnp.zeros((1, Q_TILE), I32))
        cs = jnp.where(tie, cs, 2 ** 30)
        cst_ref[...] = jnp.broadcast_to(cs, (8, Q_TILE))

    def to_mask(blk, c, with_ties):
        if with_ties:
            cs = cst_ref[0:1, :]
            sel = (blk > thr) | ((blk == thr) & (pos_of(c, blk.shape) < cs))
        else:
            sel = blk >= thr
        return jnp.where(sel, 0.0, NEG)

    for with_ties in (False, True):
        @pl.when((any_tie > 0) == with_ties)
        def _():
            def body(c, carry):
                rows = chunk_rows(c)
                sc_ref[rows, :] = to_mask(sc_ref[rows, :], c, with_ties)
                return carry
            lax.fori_loop(0, nchunks, body, 0)
            scm_ref[...] = to_mask(scm_ref[...], -1, with_ties)

    kindm = jnp.minimum(j, 1)
    heads = [slice(h * HEAD_DIM, (h + 1) * HEAD_DIM) for h in range(N_HEADS)]

    mx0 = []
    for h in range(N_HEADS):
        lm = (lax.dot_general(km_ref[h], q_ref[h], NT_DIMS, preferred_element_type=F32)
              + tabm_ref[kindm, h] + scm_ref[...])
        lm_ref[h] = lm
        mx0.append(_fold_rows(lm, jnp.max))

    def pass_a(c, mx):
        rows = chunk_rows(c)
        kind = jnp.minimum(j - c, 2)
        mask = sc_ref[rows, :]
        out = []
        for h in range(N_HEADS):
            l = (lax.dot_general(k_ref[h, rows, :], q_ref[h], NT_DIMS, preferred_element_type=F32)
                 + tab_ref[kind, h] + mask)
            l_ref[h, rows, :] = l
            out.append(jnp.maximum(mx[h], _fold_rows(l, jnp.max)))
        return tuple(out)

    mx = lax.fori_loop(0, nchunks, pass_a, tuple(mx0))
    m = [jnp.max(mx[h], axis=0, keepdims=True) for h in range(N_HEADS)]

    den0 = []
    for h in range(N_HEADS):
        pm = jnp.exp2(lm_ref[h] - m[h])
        den0.append(_fold_rows(pm, jnp.sum))
        ot_ref[heads[h], :] = jnp.dot(vtm_ref[heads[h], :], pm.astype(BF16), preferred_element_type=F32)

    def pass_b(c, den):
        rows = chunk_rows(c)
        out = []
        for h in range(N_HEADS):
            p = jnp.exp2(l_ref[h, rows, :] - m[h])
            out.append(den[h] + _fold_rows(p, jnp.sum))
            ot_ref[heads[h], :] += jnp.dot(vt_ref[heads[h], rows], p.astype(BF16),
                                           preferred_element_type=F32)
        return tuple(out)

    den = lax.fori_loop(0, nchunks, pass_b, tuple(den0))
    for h in range(N_HEADS):
        ot_ref[heads[h], :] = ot_ref[heads[h], :] / jnp.sum(den[h], axis=0, keepdims=True)
    o_ref[...] = ot_ref[...].T.astype(BF16)


def _attention(q3, iq3, iwt, k3, ik, vt, km3, ikm, vtm, tab, tabm, batch, seq):
    n = batch * seq
    tiles = seq // Q_TILE
    full = lambda a: pl.BlockSpec(a.shape, lambda b, j: (0,) * a.ndim, pipeline_mode=pl.Buffered(1))
    return pl.pallas_call(
        _attn_kernel,
        grid=(batch, tiles),
        in_specs=[
            pl.BlockSpec((N_HEADS, Q_TILE, HEAD_DIM), lambda b, j: (0, b * tiles + j, 0)),
            pl.BlockSpec((IDX_HEADS, Q_TILE, IDX_DIM), lambda b, j: (0, b * tiles + j, 0)),
            pl.BlockSpec((IDX_HEADS, Q_TILE), lambda b, j: (0, b * tiles + j)),
            pl.BlockSpec((N_HEADS, seq, HEAD_DIM), lambda b, j: (0, b, 0)),
            pl.BlockSpec((seq, IDX_DIM), lambda b, j: (b, 0)),
            pl.BlockSpec((ATTN_WIDTH, seq), lambda b, j: (0, b)),
            full(km3), full(ikm), full(vtm), full(tab), full(tabm),
        ],
        out_specs=pl.BlockSpec((Q_TILE, ATTN_WIDTH), lambda b, j: (b * tiles + j, 0)),
        out_shape=jax.ShapeDtypeStruct((n, ATTN_WIDTH), BF16),
        scratch_shapes=[
            pltpu.VMEM((seq, Q_TILE), F32),
            pltpu.VMEM((N_META, Q_TILE), F32),
            pltpu.VMEM((N_HEADS, seq, Q_TILE), F32),
            pltpu.VMEM((N_HEADS, N_META, Q_TILE), F32),
            pltpu.VMEM((ATTN_WIDTH, Q_TILE), F32),
            pltpu.VMEM((8, Q_TILE), I32),
            pltpu.VMEM((seq, Q_TILE), jnp.int16),
            pltpu.VMEM((seq, Q_TILE), jnp.int16),
            pltpu.VMEM((N_META, Q_TILE), jnp.int16),
            pltpu.VMEM((N_META, Q_TILE), jnp.int16),
        ],
        compiler_params=pltpu.CompilerParams(dimension_semantics=("arbitrary", "arbitrary"),
                                             vmem_limit_bytes=VMEM_LIMIT),
        name="attn",
    )(q3, iq3, iwt, k3, ik, vt, km3, ikm, vtm, tab, tabm)


def _mix_kernel(x_ref, attn_ref, y_ref, yprev_ref, ymeta_ref, gate_ref,
                cw_ref, cb_ref, lng_ref, lnb_ref, wco_ref, wao_ref, wout_ref, nf_ref, wrt_ref, br_ref,
                h2_ref, hn2_ref, eid_ref, rank_ref, gcol_ref, cnt_ref,
                win_ref, shift_ref, base_ref, *, tiles_per_seq):
    i = pl.program_id(0)

    @pl.when(i == 0)
    def _():
        base_ref[...] = jnp.zeros_like(base_ref)

    first = (i % tiles_per_seq) == 0
    win_ref[0:HALO, :] = jnp.where(first, ymeta_ref[...], yprev_ref[...])
    win_ref[HALO:, :] = y_ref[...]
    lead = HALO - (CONV_WIDTH - 1)
    for b in range(8):
        span = ROW_TILE + 8 * (len(range(b, CONV_WIDTH, 8)) - 1)
        shift_ref[b, 0:span, :] = win_ref[pl.ds(lead + b, span), :]
    blocks = []
    for r0 in range(0, ROW_TILE, CONV_ROWS):
        acc = jnp.broadcast_to(cb_ref[...], (CONV_ROWS, CONV_CH))
        for b in range(8):
            for a, w in enumerate(range(b, CONV_WIDTH, 8)):
                acc = acc + cw_ref[w:w + 1, :] * shift_ref[b, r0 + 8 * a:r0 + 8 * a + CONV_ROWS, :]
        blocks.append(acc)
    yc = jnp.concatenate(blocks, axis=0)
    mu = jnp.mean(yc, axis=-1, keepdims=True)
    var = jnp.mean(jnp.square(yc - mu), axis=-1, keepdims=True)
    yn = (yc - mu) * lax.rsqrt(var + EPS) * lng_ref[...] + lnb_ref[...]
    ys = yn * _sigmoid(yn)
    y_b = jnp.dot(ys.astype(BF16), wco_ref[...], preferred_element_type=F32)

    y_a = jnp.dot(attn_ref[...], wao_ref[...], preferred_element_type=F32)
    merged = gate_ref[:, :D_MODEL] * y_a + gate_ref[:, D_MODEL:] * y_b
    h2 = x_ref[...] + jnp.dot(merged.astype(BF16), wout_ref[...], preferred_element_type=F32)
    h2_ref[...] = h2
    ms = jnp.mean(h2 * h2, axis=-1, keepdims=True)
    hn2 = h2 * lax.rsqrt(ms + EPS) * nf_ref[...]
    hn2_ref[:, 0, :] = _pack_bf16_pairs(hn2)

    logits = lax.dot_general(wrt_ref[...], hn2, NT_DIMS, preferred_element_type=F32,
                             precision=lax.Precision.HIGHEST) + br_ref[...]
    erow = lax.broadcasted_iota(I32, (N_EXPERTS, ROW_TILE), 0)
    vals, ids = [], []
    l = logits
    for _ in range(TOP_K_EXPERTS):
        m = jnp.max(l, axis=0, keepdims=True)
        idx = jnp.min(jnp.where(l == m, erow, N_EXPERTS), axis=0, keepdims=True)
        vals.append(m)
        ids.append(idx)
        l = jnp.where(erow == idx, -jnp.inf, l)
    ex = [jnp.exp(v - vals[0]) for v in vals]
    den = ex[0] + ex[1] + ex[2] + ex[3]
    gates = [e / den for e in ex]

    onehot = [(erow == idx) for idx in ids]
    oh = jnp.concatenate([jnp.where(o, 1.0, 0.0) for o in onehot], axis=0)
    tr = lax.broadcasted_iota(I32, (ROW_TILE, ROW_TILE), 0)
    tc = lax.broadcasted_iota(I32, (ROW_TILE, ROW_TILE), 1)
    upper = jnp.where(tr <= tc, 1.0, 0.0).astype(BF16)
    pref = jnp.dot(oh.astype(BF16), upper, preferred_element_type=F32)
    offs = base_ref[:, 0:1]
    ranks = []
    for kk in range(TOP_K_EXPERTS):
        pk = pref[kk * N_EXPERTS:(kk + 1) * N_EXPERTS, :]
        r = jnp.sum(jnp.where(onehot[kk], offs + pk - 1.0, 0.0), axis=0, keepdims=True)
        ranks.append(r.astype(I32))
        offs = offs + pk[:, ROW_TILE - 1:ROW_TILE]
    base_ref[...] = jnp.broadcast_to(offs, base_ref.shape)
    cnt_ref[...] = jnp.broadcast_to(offs, cnt_ref.shape)

    zi = jnp.zeros((8 - TOP_K_EXPERTS, ROW_TILE), I32)
    eid_ref[...] = jnp.concatenate(ids + [zi], axis=0)
    rank_ref[...] = jnp.concatenate(ranks + [zi], axis=0)
    g8 = jnp.concatenate(gates + [jnp.zeros((128 - TOP_K_EXPERTS, ROW_TILE), F32)], axis=0)
    gcol_ref[...] = g8.T


def _mix(x2d, attn, y, ymeta, gate, wts, seq):
    n = x2d.shape[0]
    tiles_per_seq = seq // ROW_TILE
    halo_per_tile = ROW_TILE // HALO
    full = lambda a: pl.BlockSpec(a.shape, lambda i: (0,) * a.ndim)
    row = lambda w: pl.BlockSpec((ROW_TILE, w), lambda i: (i, 0))
    lane = lambda r: pl.BlockSpec((r, ROW_TILE), lambda i: (0, i))
    out_shape = (
        jax.ShapeDtypeStruct((n, D_MODEL), F32),
        jax.ShapeDtypeStruct((n, 1, D_MODEL // 2), I32),
        jax.ShapeDtypeStruct((8, n), I32),
        jax.ShapeDtypeStruct((8, n), I32),
        jax.ShapeDtypeStruct((n, 128), F32),
        jax.ShapeDtypeStruct((N_EXPERTS, 128), F32),
    )
    out_specs = (row(D_MODEL), pl.BlockSpec((ROW_TILE, 1, D_MODEL // 2), lambda i: (i, 0, 0)),
                 lane(8), lane(8), row(128), full(out_shape[5]))
    return pl.pallas_call(
        functools.partial(_mix_kernel, tiles_per_seq=tiles_per_seq),
        grid=(n // ROW_TILE,),
        in_specs=[row(D_MODEL), row(ATTN_WIDTH), row(CONV_CH),
                  pl.BlockSpec((HALO, CONV_CH), lambda i: (jnp.maximum(i * halo_per_tile - 1, 0), 0)),
                  full(ymeta), row(2 * D_MODEL)] + [full(w) for w in wts],
        out_specs=out_specs,
        out_shape=out_shape,
        scratch_shapes=[pltpu.VMEM((HALO + ROW_TILE, CONV_CH), F32),
                        pltpu.VMEM((8, HALO + ROW_TILE, CONV_CH), F32),
                        pltpu.VMEM((N_EXPERTS, 128), F32)],
        compiler_params=pltpu.CompilerParams(dimension_semantics=("arbitrary",),
                                             vmem_limit_bytes=VMEM_LIMIT),
        name="mix",
    )(x2d, attn, y, y, ymeta, gate, *wts)


def _slots_kernel(starts_ref, eid_ref, rank_ref, slot_ref):
    eid = eid_ref[...]
    base = jnp.zeros(eid.shape, I32)
    for e in range(N_EXPERTS):
        base = jnp.where(eid == e, starts_ref[e], base)
    slot = base + rank_ref[...]
    for t in range(slot_ref.shape[0]):
        for kk in range(TOP_K_EXPERTS):
            slot_ref[t, :, kk * ROW_TILE:(kk + 1) * ROW_TILE] = slot[kk:kk + 1, t * ROW_TILE:(t + 1) * ROW_TILE]


def _slots(starts, eid8, rank8):
    n = eid8.shape[1]
    tiles = 8
    spec = pl.BlockSpec((8, tiles * ROW_TILE), lambda i: (0, i))
    return pl.pallas_call(
        _slots_kernel,
        grid=(n // (tiles * ROW_TILE),),
        in_specs=[pl.BlockSpec(memory_space=pltpu.SMEM), spec, spec],
        out_specs=pl.BlockSpec((tiles, 1, COPIES_PER_TILE), lambda i: (i, 0, 0)),
        out_shape=jax.ShapeDtypeStruct((n // ROW_TILE, 1, COPIES_PER_TILE), I32),
        compiler_params=pltpu.CompilerParams(dimension_semantics=("arbitrary",)),
        name="slots",
    )(starts, eid8, rank8)


COPIES_PER_TILE = TOP_K_EXPERTS * ROW_TILE
SLOT_UNROLL = 16


def _slot_source_kernel(slot_ref, init_ref, src_ref, sem, *, n):
    i = pl.program_id(0)

    @pl.when(i == 0)
    def _():
        cp = pltpu.make_async_copy(init_ref, src_ref, sem)
        cp.start()
        cp.wait()

    for kk in range(TOP_K_EXPERTS):
        def body(g, carry):
            for u in range(SLOT_UNROLL):
                t = g * SLOT_UNROLL + u
                src_ref[slot_ref[0, 0, kk * ROW_TILE + t]] = kk * n + i * ROW_TILE + t
            return carry
        lax.fori_loop(0, ROW_TILE // SLOT_UNROLL, body, 0)


def _slot_sources(slot_tiles, init, n):
    return pl.pallas_call(
        functools.partial(_slot_source_kernel, n=n),
        grid=(slot_tiles.shape[0],),
        in_specs=[pl.BlockSpec((1, 1, COPIES_PER_TILE), lambda i: (i, 0, 0), memory_space=pltpu.SMEM),
                  pl.BlockSpec(memory_space=pl.ANY)],
        out_specs=pl.BlockSpec(memory_space=pltpu.SMEM),
        out_shape=jax.ShapeDtypeStruct(init.shape, I32),
        scratch_shapes=[pltpu.SemaphoreType.DMA(())],
        compiler_params=pltpu.CompilerParams(dimension_semantics=("arbitrary",)),
        name="slot_sources",
    )(slot_tiles, init)


def _expert_kernel(te_ref, nused_ref, src_ref, src_next_ref, src_prev_ref, hn_hbm,
                   wug_ref, bug_ref, wd_ref, bd_ref, yk_hbm, xbuf, ybuf, wug_bf, wd_bf, gsem, ssem, *, n):
    i = pl.program_id(0)
    nused = nused_ref[0]

    def gather_copy(src, r, b):
        tok = src[0, 0, r] & (n - 1)
        return pltpu.make_async_copy(hn_hbm.at[tok], xbuf.at[b, r], gsem.at[b])

    def scatter_copy(src, r, b):
        return pltpu.make_async_copy(ybuf.at[b, r], yk_hbm.at[src[0, 0, r]], ssem.at[b])

    @pl.when(i <= nused)
    def _():
        cur = i % 2
        oth = 1 - cur

        def wait_scatter(b):
            pltpu.make_async_copy(ybuf.at[b], yk_hbm.at[pl.ds(0, ROW_TILE)], ssem.at[b]).wait()

        @pl.when(i == 0)
        def _():
            for r in range(ROW_TILE):
                gather_copy(src_ref, r, 0).start()
            ybuf[1] = jnp.zeros(ybuf.shape[1:], F32)
            spare = pltpu.make_async_copy(ybuf.at[1], yk_hbm.at[pl.ds(TOP_K_EXPERTS * n, ROW_TILE)],
                                          ssem.at[1])
            spare.start()
            spare.wait()

        @pl.when(i < nused)
        def _():
            for r in range(ROW_TILE):
                gather_copy(src_next_ref, r, oth).start(priority=r % 2)

        @pl.when(i > 0)
        def _():
            for r in range(ROW_TILE):
                scatter_copy(src_prev_ref, r, oth).start(priority=r % 2)

        pltpu.make_async_copy(hn_hbm.at[pl.ds(0, ROW_TILE)], xbuf.at[cur], gsem.at[cur]).wait()

        prev = te_ref[jnp.maximum(i - 1, 0)]
        fresh = (i == 0) | (te_ref[jnp.minimum(i, nused - 1)] != prev)

        @pl.when(fresh)
        def _():
            wug_bf[...] = wug_ref[0].astype(BF16)
            wd_bf[...] = wd_ref[0].astype(BF16)

        xb = _unpack_bf16_pairs(xbuf[cur, :, 0, :])
        ug = jnp.dot(xb, wug_bf[...], preferred_element_type=F32) + bug_ref[0]
        gate = jnp.minimum(ug[:, :D_FF], SWIGLU_LIMIT)
        up = jnp.clip(ug[:, D_FF:], -SWIGLU_LIMIT, SWIGLU_LIMIT)
        act = (up + 1.0) * gate * _sigmoid(SWIGLU_ALPHA * gate)
        y = jnp.dot(act.astype(BF16), wd_bf[...], preferred_element_type=F32) + bd_ref[0]

        @pl.when(i > 1)
        def _():
            wait_scatter(cur)

        ybuf[cur, :, 0, :] = y

        @pl.when((i == nused) & (i > 0))
        def _():
            wait_scatter(oth)


def _experts(tile_expert, nused, src_tiles, hn2p, w_ug, b_ug, w_down, b_down, n):
    ntiles = src_tiles.shape[0]
    clamp = lambda i, nu: jnp.maximum(jnp.minimum(i, nu[0] - 1), 0)
    smem = lambda f: pl.BlockSpec((1, 1, ROW_TILE), f, memory_space=pltpu.SMEM)
    expert = lambda i, te, nu: (te[clamp(i, nu)], 0, 0)
    grid_spec = pltpu.PrefetchScalarGridSpec(
        num_scalar_prefetch=2,
        grid=(ntiles + 1,),
        in_specs=[
            smem(lambda i, te, nu: (clamp(i, nu), 0, 0)),
            smem(lambda i, te, nu: (clamp(i + 1, nu), 0, 0)),
            smem(lambda i, te, nu: (clamp(i - 1, nu), 0, 0)),
            pl.BlockSpec(memory_space=pl.ANY),
            pl.BlockSpec((1, D_MODEL, 2 * D_FF), expert),
            pl.BlockSpec((1, 1, 2 * D_FF), expert),
            pl.BlockSpec((1, D_FF, D_MODEL), expert),
            pl.BlockSpec((1, 1, D_MODEL), expert),
        ],
        out_specs=pl.BlockSpec(memory_space=pl.ANY),
        scratch_shapes=[pltpu.VMEM((2, ROW_TILE, 1, D_MODEL // 2), I32),
                        pltpu.VMEM((2, ROW_TILE, 1, D_MODEL), F32),
                        pltpu.VMEM((D_MODEL, 2 * D_FF), BF16), pltpu.VMEM((D_FF, D_MODEL), BF16),
                        pltpu.SemaphoreType.DMA((2,)), pltpu.SemaphoreType.DMA((2,))],
    )
    return pl.pallas_call(
        functools.partial(_expert_kernel, n=n),
        grid_spec=grid_spec,
        out_shape=jax.ShapeDtypeStruct((TOP_K_EXPERTS * n + ROW_TILE, 1, D_MODEL), F32),
        compiler_params=pltpu.CompilerParams(dimension_semantics=("arbitrary",),
                                             vmem_limit_bytes=VMEM_LIMIT),
        name="experts",
    )(tile_expert, nused, src_tiles, src_tiles, src_tiles, hn2p,
      w_ug, b_ug[:, None, :], w_down, b_down[:, None, :])


def _final_kernel(h2_ref, gcol_ref, nf_ref, *rest):
    y_refs, o_ref = rest[:TOP_K_EXPERTS], rest[TOP_K_EXPERTS]
    h = h2_ref[...]
    for kk in range(TOP_K_EXPERTS):
        h = h + gcol_ref[:, kk:kk + 1] * y_refs[kk][:, 0, :]
    ms = jnp.mean(h * h, axis=-1, keepdims=True)
    o_ref[...] = h * lax.rsqrt(ms + EPS) * nf_ref[...]


def _final(h2, gcol, nf, yk):
    n = h2.shape[0]
    ntiles = n // ROW_TILE
    choice = lambda kk: pl.BlockSpec((ROW_TILE, 1, D_MODEL), lambda i: (kk * ntiles + i, 0, 0))
    return pl.pallas_call(
        _final_kernel,
        grid=(ntiles,),
        in_specs=[pl.BlockSpec((ROW_TILE, D_MODEL), lambda i: (i, 0)),
                  pl.BlockSpec((ROW_TILE, 128), lambda i: (i, 0)),
                  pl.BlockSpec((1, D_MODEL), lambda i: (0, 0))] + [choice(kk) for kk in range(TOP_K_EXPERTS)],
        out_specs=pl.BlockSpec((ROW_TILE, D_MODEL), lambda i: (i, 0)),
        out_shape=jax.ShapeDtypeStruct((n, D_MODEL), F32),
        compiler_params=pltpu.CompilerParams(dimension_semantics=("arbitrary",),
                                             vmem_limit_bytes=VMEM_LIMIT),
        name="final",
    )(h2, gcol, nf, *([yk] * TOP_K_EXPERTS))


def _split_w_in(w_in, b_gate):
    c = ATTN_WIDTH
    o = 0
    wq = w_in[:, o:o + c]; o += c
    wk = w_in[:, o:o + c]; o += c
    wv = w_in[:, o:o + c]; o += c
    wiq = w_in[:, o:o + IDX_HEADS * IDX_DIM]; o += IDX_HEADS * IDX_DIM
    wik = w_in[:, o:o + IDX_DIM]; o += IDX_DIM
    wiw = w_in[:, o:o + IDX_HEADS]; o += IDX_HEADS
    wglu = w_in[:, o:o + 2 * CONV_CH]; o += 2 * CONV_CH
    wgate = w_in[:, o:]
    wiwt = jnp.concatenate([wiw.T, jnp.zeros((16 - IDX_HEADS, D_MODEL), w_in.dtype)], axis=0)
    bf = lambda a: a.astype(BF16)
    return (bf(wq), bf(wk), bf(wiq), bf(wik), bf(wv.T), bf(wiwt), bf(wglu), bf(wgate),
            b_gate[None, :].astype(F32))


def kernel(x, meta_tokens, rel_bias, norm_mix, w_in, b_gate, w_attn_out, conv_w, conv_b, conv_ln_g, conv_ln_b, w_conv_out, w_out, norm_ffn, w_router, b_router, w_up_gate, b_up_gate, w_down, b_down, norm_final):
    batch, seq, d = x.shape
    n = batch * seq
    x2d = x.reshape(n, d)

    wts = _split_w_in(w_in[0], b_gate[0])
    g_mix = norm_mix[0][None, :]
    q3, k3, iq3, ik, vt, iwt, y, gate = _project(x2d, g_mix, wts, PROJ_TILE)
    _, km3, _, ikm, vtm, _, ym, _ = _project(meta_tokens.astype(F32), g_mix, wts, N_META)

    tab, tabm = _bias_tables(rel_bias.astype(F32))
    attn = _attention(q3, iq3, iwt, k3, ik, vt, km3, ikm, vtm, tab, tabm, batch, seq)

    ymeta = jnp.concatenate([jnp.zeros((HALO - N_META, CONV_CH), F32), ym], axis=0)
    cw = jnp.concatenate([conv_w[0], jnp.zeros((32 - CONV_WIDTH, CONV_CH), F32)], axis=0)
    mix_w = (cw, conv_b[0][None, :], conv_ln_g[0][None, :], conv_ln_b[0][None, :],
             w_conv_out[0].astype(BF16), w_attn_out[0].astype(BF16), w_out[0].astype(BF16),
             norm_ffn[0][None, :], w_router[0].T, b_router[0][:, None])
    h2, hn2, eid8, rank8, gcol, cnt = _mix(x2d, attn, y, ymeta, gate, mix_w, seq)

    counts = cnt[:, 0].astype(I32)
    padded = ((counts + ROW_TILE - 1) // ROW_TILE) * ROW_TILE
    ends = jnp.cumsum(padded)
    starts = ends - padded
    slot_tiles = _slots(starts, eid8, rank8)
    nslots = n * TOP_K_EXPERTS + N_EXPERTS * ROW_TILE
    ntiles = nslots // ROW_TILE
    tile_start = jnp.arange(ntiles, dtype=I32) * ROW_TILE
    nused = (ends[-1] // ROW_TILE).astype(I32)
    last_start = jnp.maximum(ends[-1] - ROW_TILE, 0)
    tile_expert = jnp.sum((jnp.minimum(tile_start, last_start)[:, None] >= ends[None, :]).astype(I32), axis=1)

    spare = TOP_K_EXPERTS * n + jnp.arange(nslots, dtype=I32) % ROW_TILE
    src_tiles = _slot_sources(slot_tiles, spare, n).reshape(ntiles, 1, ROW_TILE)
    yk = _experts(tile_expert, nused[None], src_tiles, hn2, w_up_gate[0], b_up_gate[0], w_down[0], b_down[0], n)
    out = _final(h2, gcol, norm_final[None, :], yk)
    return out.reshape(batch, seq, d)
```

```python
import functools
import math

import jax
import jax.numpy as jnp
from jax import lax
from jax.experimental import pallas as pl
from jax.experimental.pallas import tpu as pltpu

F32 = jnp.float32
BF16 = jnp.bfloat16
I32 = jnp.int32

D_MODEL = 1024
N_META = 16
N_HEADS = 8
HEAD_DIM = 64
ATTN_WIDTH = N_HEADS * HEAD_DIM
IDX_HEADS = 8
IDX_DIM = 64
TOPK_MAX = 256
CONV_CH = 512
CONV_WIDTH = 31
N_BUCKETS = 32
MAX_DISTANCE = 128
N_EXPERTS = 32
TOP_K_EXPERTS = 4
D_FF = 1024
SWIGLU_LIMIT = 7.0
SWIGLU_ALPHA = 1.702
EPS = 1e-6
IDX_SCALE = (IDX_DIM ** -0.5) * (IDX_HEADS ** -0.5)

ROW_TILE = 256
PROJ_TILE = 512
Q_TILE = 256
K_CHUNK = 256
HALO = 32
CONV_ROWS = 32
NEG = -1e30
LOG2E = math.log2(math.e)
VMEM_LIMIT = 56 * 1024 * 1024

NT_DIMS = (((1,), (1,)), ((), ()))


def _sigmoid(x):
    return 1.0 / (1.0 + jnp.exp(-x))


def _pack_bf16_pairs(x):
    w = x.shape[1] // 2
    hi = lax.bitcast_convert_type(x[:, :w].astype(BF16).astype(F32), jnp.uint32)
    lo = lax.bitcast_convert_type(x[:, w:].astype(BF16).astype(F32), jnp.uint32)
    return lax.bitcast_convert_type(hi | (lo >> 16), I32)


def _unpack_bf16_pairs(p):
    u = lax.bitcast_convert_type(p, jnp.uint32)
    hi = lax.bitcast_convert_type(u & jnp.uint32(0xFFFF0000), F32)
    lo = lax.bitcast_convert_type(u << 16, F32)
    return jnp.concatenate([hi, lo], axis=1).astype(BF16)


def _proj_kernel(x_ref, g_ref, wq_ref, wk_ref, wiq_ref, wik_ref, wvt_ref, wiwt_ref, wglu_ref,
                 wgate_ref, bgate_ref,
                 q_ref, k_ref, iq_ref, ik_ref, vt_ref, iwt_ref, y_ref, gate_ref):
    x = x_ref[...]
    ms = jnp.mean(x * x, axis=-1, keepdims=True)
    xn = (x * lax.rsqrt(ms + EPS) * g_ref[...]).astype(BF16)

    q = jnp.dot(xn, wq_ref[...], preferred_element_type=F32) * (HEAD_DIM ** -0.5 * LOG2E)
    k = jnp.dot(xn, wk_ref[...], preferred_element_type=F32)
    iq = jnp.dot(xn, wiq_ref[...], preferred_element_type=F32)
    for h in range(N_HEADS):
        sl = slice(h * HEAD_DIM, (h + 1) * HEAD_DIM)
        q_ref[h] = q[:, sl].astype(BF16)
        k_ref[h] = k[:, sl].astype(BF16)
        iq_ref[h] = iq[:, sl].astype(BF16)
    ik_ref[...] = jnp.dot(xn, wik_ref[...], preferred_element_type=F32).astype(BF16)
    vt_ref[...] = lax.dot_general(wvt_ref[...], xn, NT_DIMS, preferred_element_type=F32).astype(BF16)
    iwt = lax.dot_general(wiwt_ref[...], xn, NT_DIMS, preferred_element_type=F32)
    iwt_ref[...] = iwt[:IDX_HEADS] * IDX_SCALE
    glu = jnp.dot(xn, wglu_ref[...], preferred_element_type=F32)
    y_ref[...] = glu[:, :CONV_CH] * _sigmoid(glu[:, CONV_CH:])
    gate = jnp.dot(xn, wgate_ref[...], preferred_element_type=F32) + bgate_ref[...]
    gate_ref[...] = _sigmoid(gate)


def _project(x2d, g, wts, tm):
    n = x2d.shape[0]
    wq, wk, wiq, wik, wvt, wiwt, wglu, wgate, bgate = wts
    full = lambda a: pl.BlockSpec(a.shape, lambda i: (0,) * a.ndim, pipeline_mode=pl.Buffered(1))
    out_shape = (
        jax.ShapeDtypeStruct((N_HEADS, n, HEAD_DIM), BF16),
        jax.ShapeDtypeStruct((N_HEADS, n, HEAD_DIM), BF16),
        jax.ShapeDtypeStruct((IDX_HEADS, n, IDX_DIM), BF16),
        jax.ShapeDtypeStruct((n, IDX_DIM), BF16),
        jax.ShapeDtypeStruct((ATTN_WIDTH, n), BF16),
        jax.ShapeDtypeStruct((IDX_HEADS, n), F32),
        jax.ShapeDtypeStruct((n, CONV_CH), F32),
        jax.ShapeDtypeStruct((n, 2 * D_MODEL), F32),
    )
    out_specs = (
        pl.BlockSpec((N_HEADS, tm, HEAD_DIM), lambda i: (0, i, 0)),
        pl.BlockSpec((N_HEADS, tm, HEAD_DIM), lambda i: (0, i, 0)),
        pl.BlockSpec((IDX_HEADS, tm, IDX_DIM), lambda i: (0, i, 0)),
        pl.BlockSpec((tm, IDX_DIM), lambda i: (i, 0)),
        pl.BlockSpec((ATTN_WIDTH, tm), lambda i: (0, i)),
        pl.BlockSpec((IDX_HEADS, tm), lambda i: (0, i)),
        pl.BlockSpec((tm, CONV_CH), lambda i: (i, 0)),
        pl.BlockSpec((tm, 2 * D_MODEL), lambda i: (i, 0)),
    )
    return pl.pallas_call(
        _proj_kernel,
        grid=(n // tm,),
        in_specs=[pl.BlockSpec((tm, D_MODEL), lambda i: (i, 0)), full(g), full(wq), full(wk), full(wiq),
                  full(wik), full(wvt), full(wiwt), full(wglu), full(wgate), full(bgate)],
        out_specs=out_specs,
        out_shape=out_shape,
        compiler_params=pltpu.CompilerParams(dimension_semantics=("arbitrary",),
                                             vmem_limit_bytes=VMEM_LIMIT),
        name="proj",
    )(x2d, g, wq, wk, wiq, wik, wvt, wiwt, wglu, wgate, bgate)


def _t5_bucket(n):
    max_exact = N_BUCKETS // 2
    nf = jnp.maximum(n, 1).astype(F32)
    large = max_exact + (jnp.log(nf / max_exact) / math.log(MAX_DISTANCE / max_exact)
                         * (N_BUCKETS - max_exact)).astype(I32)
    large = jnp.minimum(large, N_BUCKETS - 1)
    return jnp.where(n < max_exact, n, large)


def _bias_lookup(rb_ref, dist, h):
    bucket = _t5_bucket(jnp.maximum(dist, 0))
    out = jnp.full(dist.shape, NEG, F32)
    for b in range(N_BUCKETS):
        out = jnp.where(bucket == b, rb_ref[b, h] * LOG2E, out)
    return jnp.where(dist >= 0, out, NEG)


def _bias_kernel(rb_ref, tab_ref, tabm_ref):
    kind = pl.program_id(0)
    r = pl.program_id(1)
    rows = tab_ref.shape[2]
    s = lax.broadcasted_iota(I32, (rows, Q_TILE), 0) + r * rows
    t = lax.broadcasted_iota(I32, (rows, Q_TILE), 1)
    dist = jnp.where(kind == 2, 2 * K_CHUNK, t - s + kind * K_CHUNK)
    for h in range(N_HEADS):
        tab_ref[0, h] = _bias_lookup(rb_ref, dist, h)
    m = lax.broadcasted_iota(I32, (N_META, Q_TILE), 0)
    tm_ = lax.broadcasted_iota(I32, (N_META, Q_TILE), 1)
    distm = jnp.where(kind == 0, N_META + tm_ - m, 2 * K_CHUNK)
    for h in range(N_HEADS):
        tabm_ref[0, h] = _bias_lookup(rb_ref, distm, h)


def _bias_tables(rel_bias):
    rows = 64
    return pl.pallas_call(
        _bias_kernel,
        grid=(3, K_CHUNK // rows),
        in_specs=[pl.BlockSpec(memory_space=pltpu.SMEM)],
        out_specs=(pl.BlockSpec((1, N_HEADS, rows, Q_TILE), lambda kd, r: (kd, 0, r, 0)),
                   pl.BlockSpec((1, N_HEADS, N_META, Q_TILE), lambda kd, r: (kd, 0, 0, 0))),
        out_shape=(jax.ShapeDtypeStruct((3, N_HEADS, K_CHUNK, Q_TILE), F32),
                   jax.ShapeDtypeStruct((3, N_HEADS, N_META, Q_TILE), F32)),
        compiler_params=pltpu.CompilerParams(dimension_semantics=("arbitrary", "arbitrary")),
        name="bias_tables",
    )(rel_bias)


def _order_bits_to_float(u):
    bits = jnp.where(u < 0, u ^ jnp.int32(-2 ** 31), ~u)
    return lax.bitcast_convert_type(bits, F32)


def _fold_rows(x, op):
    r, l = x.shape
    x3 = x.reshape(r // 8, 8, l)
    return op(x3, axis=0)


def _attn_kernel(q_ref, iq_ref, iwt_ref, k_ref, ik_ref, vt_ref, km_ref, ikm_ref, vtm_ref,
                 tab_ref, tabm_ref, o_ref, sc_ref, scm_ref, l_ref, lm_ref, ot_ref, cst_ref,
                 hi_ref, lo_ref, him_ref, lom_ref, qt_ref, iqt_ref):
    j = pl.program_id(1)
    nchunks = j + 1
    iw = iwt_ref[...]

    def chunk_rows(c):
        return pl.ds(pl.multiple_of(c * K_CHUNK, K_CHUNK), K_CHUNK)

    for h in range(N_HEADS):
        qt_ref[h] = q_ref[h].astype(F32).T.astype(BF16)
        iqt_ref[h] = iq_ref[h].astype(F32).T.astype(BF16)

    def idx_scores(ikc):
        acc = None
        for h in range(IDX_HEADS):
            s = jnp.dot(ikc, iqt_ref[h], preferred_element_type=F32)
            term = jnp.maximum(s, 0.0) * iw[h:h + 1, :]
            acc = term if acc is None else acc + term
        return acc

    scm_ref[...] = idx_scores(ikm_ref[...])

    row_minus_col = (lax.broadcasted_iota(I32, (K_CHUNK, Q_TILE), 0)
                     - lax.broadcasted_iota(I32, (K_CHUNK, Q_TILE), 1))

    def p1(c, carry):
        rows = chunk_rows(c)
        future = row_minus_col > jnp.where(c == j, 0, K_CHUNK)
        sc_ref[rows, :] = jnp.where(future, -jnp.inf, idx_scores(ik_ref[rows, :]))
        return carry

    lax.fori_loop(0, nchunks, p1, 0)

    def count(pred):
        def body(c, acc):
            blk = sc_ref[chunk_rows(c), :]
            return acc + _fold_rows(jnp.where(pred(blk, c), 1, 0).astype(I32), jnp.sum)
        acc = lax.fori_loop(0, nchunks, body, jnp.zeros((8, Q_TILE), I32))
        acc = acc + _fold_rows(jnp.where(pred(scm_ref[...], -1), 1, 0).astype(I32), jnp.sum)
        return jnp.sum(acc, axis=0, keepdims=True)

    I16 = jnp.int16
    LOW = -2 ** 15

    def split_key(s):
        bits = lax.bitcast_convert_type(s, I32)
        key = jnp.where(bits < 0, bits ^ jnp.int32(0x7FFFFFFF), bits)
        hi = lax.shift_right_arithmetic(key, 16).astype(I16)
        lo = ((key & 0xFFFF) + LOW).astype(I16)
        return hi, lo

    him_ref[...], lom_ref[...] = split_key(scm_ref[...])

    def p2(c, carry):
        rows = chunk_rows(c)
        hi_ref[rows, :], lo_ref[rows, :] = split_key(sc_ref[rows, :])
        return carry

    lax.fori_loop(0, nchunks, p2, 0)

    @pl.when((nchunks & 1) == 1)
    def _():
        rows = chunk_rows(nchunks)
        hi_ref[rows, :] = jnp.full((K_CHUNK, Q_TILE), LOW, I16)
        lo_ref[rows, :] = jnp.full((K_CHUNK, Q_TILE), LOW, I16)

    def tree_sum16(ind):
        parts = [ind[a * 16:(a + 1) * 16, :] for a in range(ind.shape[0] // 16)]
        while len(parts) > 1:
            parts = [parts[a] + parts[a + 1] for a in range(0, len(parts), 2)]
        return parts[0]

    def count16(ref, mref, pred):
        one, zero = jnp.ones((), BF16), jnp.zeros((), BF16)

        def body(cc, acc):
            rows = pl.ds(pl.multiple_of(cc * (2 * K_CHUNK), 2 * K_CHUNK), 2 * K_CHUNK)
            return acc + tree_sum16(jnp.where(pred(ref[rows, :]), one, zero)).astype(F32)
        acc = lax.fori_loop(0, lax.shift_right_logical(nchunks + 1, 1), body, jnp.zeros((16, Q_TILE), F32))
        acc = acc + jnp.where(pred(mref[...]), one, zero).astype(F32)
        return jnp.sum(acc, axis=0, keepdims=True).astype(I32)

    def to16(u):
        return (u + LOW).astype(I16)

    def search_hi(i, u):
        cand = u | lax.shift_left(jnp.int32(1), 15 - i)
        c16 = to16(cand)
        cnt = count16(hi_ref, him_ref, lambda blk: blk >= c16)
        return jnp.where(cnt >= TOPK_MAX, cand, u)

    u_hi = lax.fori_loop(0, 16, search_hi, jnp.zeros((1, Q_TILE), I32))
    t16 = to16(u_hi)
    need = TOPK_MAX - count16(hi_ref, him_ref, lambda blk: blk > t16)

    lom_ref[...] = jnp.where(him_ref[...] == t16, lom_ref[...], jnp.int16(LOW))

    def p2b(c, carry):
        rows = chunk_rows(c)
        lo_ref[rows, :] = jnp.where(hi_ref[rows, :] == t16, lo_ref[rows, :], jnp.int16(LOW))
        return carry

    lax.fori_loop(0, nchunks, p2b, 0)

    def search_lo(i, v):
        cand = v | lax.shift_left(jnp.int32(1), 15 - i)
        c16 = to16(cand)
        cnt = count16(lo_ref, lom_ref, lambda blk: blk >= c16)
        return jnp.where(cnt >= need, cand, v)

    u_lo = lax.fori_loop(0, 16, search_lo, jnp.zeros((1, Q_TILE), I32))
    thr_key = lax.shift_left(u_hi + LOW, 16) | u_lo
    thr_bits = jnp.where(thr_key < 0, thr_key ^ jnp.int32(0x7FFFFFFF), thr_key)
    thr = lax.bitcast_convert_type(thr_bits, F32)

    cnt_ge = count(lambda blk, c: blk >= thr)
    tie = (cnt_ge > TOPK_MAX) & (thr > -jnp.inf)
    any_tie = jnp.max(tie.astype(I32))
    cst_ref[...] = jnp.full((8, Q_TILE), 2 ** 30, I32)

    def pos_of(c, shape):
        r = lax.broadcasted_iota(I32, shape, 0)
        return jnp.where(c < 0, r, r + N_META + c * K_CHUNK)

    @pl.when(any_tie > 0)
    def _():
        cnt_gt = count(lambda blk, c: blk > thr)
        need = TOPK_MAX - cnt_gt

        def bis_pos(i, cs):
            cand = cs | lax.shift_left(jnp.int32(1), 11 - i)
            f = count(lambda blk, c: (blk == thr) & (pos_of(c, blk.shape) < cand))
            return jnp.where(f <= need, cand, cs)

        cs = lax.fori_loop(0, 12, bis_pos, jnp.zeros((1, Q_TILE), I32))
        cs = jnp.where(tie, cs, 2 ** 30)
        cst_ref[...] = jnp.broadcast_to(cs, (8, Q_TILE))

    def to_mask(blk, c, with_ties):
        if with_ties:
            cs = cst_ref[0:1, :]
            sel = (blk > thr) | ((blk == thr) & (pos_of(c, blk.shape) < cs))
        else:
            sel = blk >= thr
        return jnp.where(sel, 0.0, NEG)

    for with_ties in (False, True):
        @pl.when((any_tie > 0) == with_ties)
        def _():
            def body(c, carry):
                rows = chunk_rows(c)
                sc_ref[rows, :] = to_mask(sc_ref[rows, :], c, with_ties)
                return carry
            lax.fori_loop(0, nchunks, body, 0)
            scm_ref[...] = to_mask(scm_ref[...], -1, with_ties)

    kindm = jnp.minimum(j, 1)
    heads = [slice(h * HEAD_DIM, (h + 1) * HEAD_DIM) for h in range(N_HEADS)]

    mx0 = []
    for h in range(N_HEADS):
        lm = (jnp.dot(km_ref[h], qt_ref[h], preferred_element_type=F32)
              + tabm_ref[kindm, h] + scm_ref[...])
        lm_ref[h] = lm
        mx0.append(_fold_rows(lm, jnp.max))

    def pass_a(c, mx):
        rows = chunk_rows(c)
        kind = jnp.minimum(j - c, 2)
        mask = sc_ref[rows, :]
        out = []
        for h in range(N_HEADS):
            l = (jnp.dot(k_ref[h, rows, :], qt_ref[h], preferred_element_type=F32)
                 + tab_ref[kind, h] + mask)
            l_ref[h, rows, :] = l
            out.append(jnp.maximum(mx[h], _fold_rows(l, jnp.max)))
        return tuple(out)

    mx = lax.fori_loop(0, nchunks, pass_a, tuple(mx0))
    m = [jnp.max(mx[h], axis=0, keepdims=True) for h in range(N_HEADS)]

    den0 = []
    for h in range(N_HEADS):
        pm = jnp.exp2(lm_ref[h] - m[h])
        den0.append(_fold_rows(pm, jnp.sum))
        ot_ref[heads[h], :] = jnp.dot(vtm_ref[heads[h], :], pm.astype(BF16), preferred_element_type=F32)

    def pass_b(c, den):
        rows = chunk_rows(c)
        out = []
        for h in range(N_HEADS):
            p = jnp.exp2(l_ref[h, rows, :] - m[h])
            out.append(den[h] + _fold_rows(p, jnp.sum))
            ot_ref[heads[h], :] += jnp.dot(vt_ref[heads[h], rows], p.astype(BF16),
                                           preferred_element_type=F32)
        return tuple(out)

    den = lax.fori_loop(0, nchunks, pass_b, tuple(den0))
    for h in range(N_HEADS):
        ot_ref[heads[h], :] = ot_ref[heads[h], :] / jnp.sum(den[h], axis=0, keepdims=True)
    o_ref[...] = ot_ref[...].T.astype(BF16)


def _attention(q3, iq3, iwt, k3, ik, vt, km3, ikm, vtm, tab, tabm, batch, seq):
    n = batch * seq
    tiles = seq // Q_TILE
    full = lambda a: pl.BlockSpec(a.shape, lambda b, j: (0,) * a.ndim, pipeline_mode=pl.Buffered(1))
    return pl.pallas_call(
        _attn_kernel,
        grid=(batch, tiles),
        in_specs=[
            pl.BlockSpec((N_HEADS, Q_TILE, HEAD_DIM), lambda b, j: (0, b * tiles + j, 0)),
            pl.BlockSpec((IDX_HEADS, Q_TILE, IDX_DIM), lambda b, j: (0, b * tiles + j, 0)),
            pl.BlockSpec((IDX_HEADS, Q_TILE), lambda b, j: (0, b * tiles + j)),
            pl.BlockSpec((N_HEADS, seq, HEAD_DIM), lambda b, j: (0, b, 0)),
            pl.BlockSpec((seq, IDX_DIM), lambda b, j: (b, 0)),
            pl.BlockSpec((ATTN_WIDTH, seq), lambda b, j: (0, b)),
            full(km3), full(ikm), full(vtm), full(tab), full(tabm),
        ],
        out_specs=pl.BlockSpec((Q_TILE, ATTN_WIDTH), lambda b, j: (b * tiles + j, 0)),
        out_shape=jax.ShapeDtypeStruct((n, ATTN_WIDTH), BF16),
        scratch_shapes=[
            pltpu.VMEM((seq, Q_TILE), F32),
            pltpu.VMEM((N_META, Q_TILE), F32),
            pltpu.VMEM((N_HEADS, seq, Q_TILE), F32),
            pltpu.VMEM((N_HEADS, N_META, Q_TILE), F32),
            pltpu.VMEM((ATTN_WIDTH, Q_TILE), F32),
            pltpu.VMEM((8, Q_TILE), I32),
            pltpu.VMEM((seq, Q_TILE), jnp.int16),
            pltpu.VMEM((seq, Q_TILE), jnp.int16),
            pltpu.VMEM((N_META, Q_TILE), jnp.int16),
            pltpu.VMEM((N_META, Q_TILE), jnp.int16),
            pltpu.VMEM((N_HEADS, HEAD_DIM, Q_TILE), BF16),
            pltpu.VMEM((IDX_HEADS, IDX_DIM, Q_TILE), BF16),
        ],
        compiler_params=pltpu.CompilerParams(dimension_semantics=("arbitrary", "arbitrary"),
                                             vmem_limit_bytes=VMEM_LIMIT),
        name="attn",
    )(q3, iq3, iwt, k3, ik, vt, km3, ikm, vtm, tab, tabm)


def _mix_kernel(x_ref, attn_ref, y_ref, yprev_ref, ymeta_ref, gate_ref,
                cw_ref, cb_ref, lng_ref, lnb_ref, wco_ref, wao_ref, wout_ref, nf_ref, wrt_ref, br_ref,
                h2_ref, hn2_ref, eid_ref, rank_ref, gcol_ref, cnt_ref,
                win_ref, shift_ref, base_ref, *, tiles_per_seq):
    i = pl.program_id(0)

    @pl.when(i == 0)
    def _():
        base_ref[...] = jnp.zeros_like(base_ref)

    first = (i % tiles_per_seq) == 0
    win_ref[0:HALO, :] = jnp.where(first, ymeta_ref[...], yprev_ref[...])
    win_ref[HALO:, :] = y_ref[...]
    lead = HALO - (CONV_WIDTH - 1)
    for b in range(8):
        span = ROW_TILE + 8 * (len(range(b, CONV_WIDTH, 8)) - 1)
        shift_ref[b, 0:span, :] = win_ref[pl.ds(lead + b, span), :]
    blocks = []
    for r0 in range(0, ROW_TILE, CONV_ROWS):
        acc = jnp.broadcast_to(cb_ref[...], (CONV_ROWS, CONV_CH))
        for b in range(8):
            for a, w in enumerate(range(b, CONV_WIDTH, 8)):
                acc = acc + cw_ref[w:w + 1, :] * shift_ref[b, r0 + 8 * a:r0 + 8 * a + CONV_ROWS, :]
        blocks.append(acc)
    yc = jnp.concatenate(blocks, axis=0)
    mu = jnp.mean(yc, axis=-1, keepdims=True)
    var = jnp.mean(jnp.square(yc - mu), axis=-1, keepdims=True)
    yn = (yc - mu) * lax.rsqrt(var + EPS) * lng_ref[...] + lnb_ref[...]
    ys = yn * _sigmoid(yn)
    y_b = jnp.dot(ys.astype(BF16), wco_ref[...], preferred_element_type=F32)

    y_a = jnp.dot(attn_ref[...], wao_ref[...], preferred_element_type=F32)
    merged = gate_ref[:, :D_MODEL] * y_a + gate_ref[:, D_MODEL:] * y_b
    h2 = x_ref[...] + jnp.dot(merged.astype(BF16), wout_ref[...], preferred_element_type=F32)
    h2_ref[...] = h2
    ms = jnp.mean(h2 * h2, axis=-1, keepdims=True)
    hn2 = h2 * lax.rsqrt(ms + EPS) * nf_ref[...]
    hn2_ref[:, 0, :] = _pack_bf16_pairs(hn2)

    logits = lax.dot_general(wrt_ref[...], hn2, NT_DIMS, preferred_element_type=F32,
                             precision=lax.Precision.HIGHEST) + br_ref[...]
    erow = lax.broadcasted_iota(I32, (N_EXPERTS, ROW_TILE), 0)
    vals, ids = [], []
    l = logits
    for _ in range(TOP_K_EXPERTS):
        m = jnp.max(l, axis=0, keepdims=True)
        idx = jnp.min(jnp.where(l == m, erow, N_EXPERTS), axis=0, keepdims=True)
        vals.append(m)
        ids.append(idx)
        l = jnp.where(erow == idx, -jnp.inf, l)
    ex = [jnp.exp(v - vals[0]) for v in vals]
    den = ex[0] + ex[1] + ex[2] + ex[3]
    gates = [e / den for e in ex]

    onehot = [(erow == idx) for idx in ids]
    oh = jnp.concatenate([jnp.where(o, 1.0, 0.0) for o in onehot], axis=0)
    tr = lax.broadcasted_iota(I32, (ROW_TILE, ROW_TILE), 0)
    tc = lax.broadcasted_iota(I32, (ROW_TILE, ROW_TILE), 1)
    upper = jnp.where(tr <= tc, 1.0, 0.0).astype(BF16)
    pref = jnp.dot(oh.astype(BF16), upper, preferred_element_type=F32)
    offs = base_ref[:, 0:1]
    ranks = []
    for kk in range(TOP_K_EXPERTS):
        pk = pref[kk * N_EXPERTS:(kk + 1) * N_EXPERTS, :]
        r = jnp.sum(jnp.where(onehot[kk], offs + pk - 1.0, 0.0), axis=0, keepdims=True)
        ranks.append(r.astype(I32))
        offs = offs + pk[:, ROW_TILE - 1:ROW_TILE]
    base_ref[...] = jnp.broadcast_to(offs, base_ref.shape)
    cnt_ref[...] = jnp.broadcast_to(offs, cnt_ref.shape)

    zi = jnp.zeros((8 - TOP_K_EXPERTS, ROW_TILE), I32)
    eid_ref[...] = jnp.concatenate(ids + [zi], axis=0)
    rank_ref[...] = jnp.concatenate(ranks + [zi], axis=0)
    g8 = jnp.concatenate(gates + [jnp.zeros((128 - TOP_K_EXPERTS, ROW_TILE), F32)], axis=0)
    gcol_ref[...] = g8.T


def _mix(x2d, attn, y, ymeta, gate, wts, seq):
    n = x2d.shape[0]
    tiles_per_seq = seq // ROW_TILE
    halo_per_tile = ROW_TILE // HALO
    full = lambda a: pl.BlockSpec(a.shape, lambda i: (0,) * a.ndim)
    row = lambda w: pl.BlockSpec((ROW_TILE, w), lambda i: (i, 0))
    lane = lambda r: pl.BlockSpec((r, ROW_TILE), lambda i: (0, i))
    out_shape = (
        jax.ShapeDtypeStruct((n, D_MODEL), F32),
        jax.ShapeDtypeStruct((n, 1, D_MODEL // 2), I32),
        jax.ShapeDtypeStruct((8, n), I32),
        jax.ShapeDtypeStruct((8, n), I32),
        jax.ShapeDtypeStruct((n, 128), F32),
        jax.ShapeDtypeStruct((N_EXPERTS, 128), F32),
    )
    out_specs = (row(D_MODEL), pl.BlockSpec((ROW_TILE, 1, D_MODEL // 2), lambda i: (i, 0, 0)),
                 lane(8), lane(8), row(128), full(out_shape[5]))
    return pl.pallas_call(
        functools.partial(_mix_kernel, tiles_per_seq=tiles_per_seq),
        grid=(n // ROW_TILE,),
        in_specs=[row(D_MODEL), row(ATTN_WIDTH), row(CONV_CH),
                  pl.BlockSpec((HALO, CONV_CH), lambda i: (jnp.maximum(i * halo_per_tile - 1, 0), 0)),
                  full(ymeta), row(2 * D_MODEL)] + [full(w) for w in wts],
        out_specs=out_specs,
        out_shape=out_shape,
        scratch_shapes=[pltpu.VMEM((HALO + ROW_TILE, CONV_CH), F32),
                        pltpu.VMEM((8, HALO + ROW_TILE, CONV_CH), F32),
                        pltpu.VMEM((N_EXPERTS, 128), F32)],
        compiler_params=pltpu.CompilerParams(dimension_semantics=("arbitrary",),
                                             vmem_limit_bytes=VMEM_LIMIT),
        name="mix",
    )(x2d, attn, y, y, ymeta, gate, *wts)


def _slots_kernel(starts_ref, eid_ref, rank_ref, slot_ref):
    eid = eid_ref[...]
    base = jnp.zeros(eid.shape, I32)
    for e in range(N_EXPERTS):
        base = jnp.where(eid == e, starts_ref[e], base)
    slot = base + rank_ref[...]
    for t in range(slot_ref.shape[0]):
        for kk in range(TOP_K_EXPERTS):
            slot_ref[t, :, kk * ROW_TILE:(kk + 1) * ROW_TILE] = slot[kk:kk + 1, t * ROW_TILE:(t + 1) * ROW_TILE]


def _slots(starts, eid8, rank8):
    n = eid8.shape[1]
    tiles = 8
    spec = pl.BlockSpec((8, tiles * ROW_TILE), lambda i: (0, i))
    return pl.pallas_call(
        _slots_kernel,
        grid=(n // (tiles * ROW_TILE),),
        in_specs=[pl.BlockSpec(memory_space=pltpu.SMEM), spec, spec],
        out_specs=pl.BlockSpec((tiles, 1, COPIES_PER_TILE), lambda i: (i, 0, 0)),
        out_shape=jax.ShapeDtypeStruct((n // ROW_TILE, 1, COPIES_PER_TILE), I32),
        compiler_params=pltpu.CompilerParams(dimension_semantics=("arbitrary",)),
        name="slots",
    )(starts, eid8, rank8)


COPIES_PER_TILE = TOP_K_EXPERTS * ROW_TILE
SLOT_UNROLL = 16


def _slot_source_kernel(slot_ref, init_ref, src_ref, sem, *, n):
    i = pl.program_id(0)

    @pl.when(i == 0)
    def _():
        cp = pltpu.make_async_copy(init_ref, src_ref, sem)
        cp.start()
        cp.wait()

    for kk in range(TOP_K_EXPERTS):
        def body(g, carry):
            for u in range(SLOT_UNROLL):
                t = g * SLOT_UNROLL + u
                src_ref[slot_ref[0, 0, kk * ROW_TILE + t]] = kk * n + i * ROW_TILE + t
            return carry
        lax.fori_loop(0, ROW_TILE // SLOT_UNROLL, body, 0)


def _slot_sources(slot_tiles, init, n):
    return pl.pallas_call(
        functools.partial(_slot_source_kernel, n=n),
        grid=(slot_tiles.shape[0],),
        in_specs=[pl.BlockSpec((1, 1, COPIES_PER_TILE), lambda i: (i, 0, 0), memory_space=pltpu.SMEM),
                  pl.BlockSpec(memory_space=pl.ANY)],
        out_specs=pl.BlockSpec(memory_space=pltpu.SMEM),
        out_shape=jax.ShapeDtypeStruct(init.shape, I32),
        scratch_shapes=[pltpu.SemaphoreType.DMA(())],
        compiler_params=pltpu.CompilerParams(dimension_semantics=("arbitrary",)),
        name="slot_sources",
    )(slot_tiles, init)


def _expert_kernel(te_ref, nused_ref, src_ref, src_next_ref, src_prev_ref, hn_hbm,
                   wug_ref, bug_ref, wd_ref, bd_ref, yk_hbm, xbuf, ybuf, wug_bf, wd_bf, gsem, ssem, *, n):
    i = pl.program_id(0)
    nused = nused_ref[0]

    def gather_copy(src, r, b):
        tok = src[0, 0, r] & (n - 1)
        return pltpu.make_async_copy(hn_hbm.at[tok], xbuf.at[b, r], gsem.at[b])

    def scatter_copy(src, r, b):
        return pltpu.make_async_copy(ybuf.at[b, r], yk_hbm.at[src[0, 0, r]], ssem.at[b])

    @pl.when(i <= nused)
    def _():
        cur = i % 2
        oth = 1 - cur

        def wait_scatter(b):
            pltpu.make_async_copy(ybuf.at[b], yk_hbm.at[pl.ds(0, ROW_TILE)], ssem.at[b]).wait()

        @pl.when(i == 0)
        def _():
            for r in range(ROW_TILE):
                gather_copy(src_ref, r, 0).start()
            ybuf[1] = jnp.zeros(ybuf.shape[1:], F32)
            spare = pltpu.make_async_copy(ybuf.at[1], yk_hbm.at[pl.ds(TOP_K_EXPERTS * n, ROW_TILE)],
                                          ssem.at[1])
            spare.start()
            spare.wait()

        @pl.when(i < nused)
        def _():
            for r in range(ROW_TILE):
                gather_copy(src_next_ref, r, oth).start()

        @pl.when(i > 0)
        def _():
            for r in range(ROW_TILE):
                scatter_copy(src_prev_ref, r, oth).start()

        pltpu.make_async_copy(hn_hbm.at[pl.ds(0, ROW_TILE)], xbuf.at[cur], gsem.at[cur]).wait()

        prev = te_ref[jnp.maximum(i - 1, 0)]
        fresh = (i == 0) | (te_ref[jnp.minimum(i, nused - 1)] != prev)

        @pl.when(fresh)
        def _():
            wug_bf[...] = wug_ref[0].astype(BF16)
            wd_bf[...] = wd_ref[0].astype(BF16)

        xb = _unpack_bf16_pairs(xbuf[cur, :, 0, :])
        ug = jnp.dot(xb, wug_bf[...], preferred_element_type=F32) + bug_ref[0]
        gate = jnp.minimum(ug[:, :D_FF], SWIGLU_LIMIT)
        up = jnp.clip(ug[:, D_FF:], -SWIGLU_LIMIT, SWIGLU_LIMIT)
        act = (up + 1.0) * gate * _sigmoid(SWIGLU_ALPHA * gate)
        y = jnp.dot(act.astype(BF16), wd_bf[...], preferred_element_type=F32) + bd_ref[0]

        @pl.when(i > 1)
        def _():
            wait_scatter(cur)

        ybuf[cur, :, 0, :] = y

        @pl.when((i == nused) & (i > 0))
        def _():
            wait_scatter(oth)


def _experts(tile_expert, nused, src_tiles, hn2p, w_ug, b_ug, w_down, b_down, n):
    ntiles = src_tiles.shape[0]
    clamp = lambda i, nu: jnp.maximum(jnp.minimum(i, nu[0] - 1), 0)
    smem = lambda f: pl.BlockSpec((1, 1, ROW_TILE), f, memory_space=pltpu.SMEM)
    expert = lambda i, te, nu: (te[clamp(i, nu)], 0, 0)
    grid_spec = pltpu.PrefetchScalarGridSpec(
        num_scalar_prefetch=2,
        grid=(ntiles + 1,),
        in_specs=[
            smem(lambda i, te, nu: (clamp(i, nu), 0, 0)),
            smem(lambda i, te, nu: (clamp(i + 1, nu), 0, 0)),
            smem(lambda i, te, nu: (clamp(i - 1, nu), 0, 0)),
            pl.BlockSpec(memory_space=pl.ANY),
            pl.BlockSpec((1, D_MODEL, 2 * D_FF), expert),
            pl.BlockSpec((1, 1, 2 * D_FF), expert),
            pl.BlockSpec((1, D_FF, D_MODEL), expert),
            pl.BlockSpec((1, 1, D_MODEL), expert),
        ],
        out_specs=pl.BlockSpec(memory_space=pl.ANY),
        scratch_shapes=[pltpu.VMEM((2, ROW_TILE, 1, D_MODEL // 2), I32),
                        pltpu.VMEM((2, ROW_TILE, 1, D_MODEL), F32),
                        pltpu.VMEM((D_MODEL, 2 * D_FF), BF16), pltpu.VMEM((D_FF, D_MODEL), BF16),
                        pltpu.SemaphoreType.DMA((2,)), pltpu.SemaphoreType.DMA((2,))],
    )
    return pl.pallas_call(
        functools.partial(_expert_kernel, n=n),
        grid_spec=grid_spec,
        out_shape=jax.ShapeDtypeStruct((TOP_K_EXPERTS * n + ROW_TILE, 1, D_MODEL), F32),
        compiler_params=pltpu.CompilerParams(dimension_semantics=("arbitrary",),
                                             vmem_limit_bytes=VMEM_LIMIT),
        name="experts",
    )(tile_expert, nused, src_tiles, src_tiles, src_tiles, hn2p,
      w_ug, b_ug[:, None, :], w_down, b_down[:, None, :])


def _final_kernel(h2_ref, gcol_ref, nf_ref, *rest):
    y_refs, o_ref = rest[:TOP_K_EXPERTS], rest[TOP_K_EXPERTS]
    h = h2_ref[...]
    for kk in range(TOP_K_EXPERTS):
        h = h + gcol_ref[:, kk:kk + 1] * y_refs[kk][:, 0, :]
    ms = jnp.mean(h * h, axis=-1, keepdims=True)
    o_ref[...] = h * lax.rsqrt(ms + EPS) * nf_ref[...]


def _final(h2, gcol, nf, yk):
    n = h2.shape[0]
    ntiles = n // ROW_TILE
    choice = lambda kk: pl.BlockSpec((ROW_TILE, 1, D_MODEL), lambda i: (kk * ntiles + i, 0, 0))
    return pl.pallas_call(
        _final_kernel,
        grid=(ntiles,),
        in_specs=[pl.BlockSpec((ROW_TILE, D_MODEL), lambda i: (i, 0)),
                  pl.BlockSpec((ROW_TILE, 128), lambda i: (i, 0)),
                  pl.BlockSpec((1, D_MODEL), lambda i: (0, 0))] + [choice(kk) for kk in range(TOP_K_EXPERTS)],
        out_specs=pl.BlockSpec((ROW_TILE, D_MODEL), lambda i: (i, 0)),
        out_shape=jax.ShapeDtypeStruct((n, D_MODEL), F32),
        compiler_params=pltpu.CompilerParams(dimension_semantics=("arbitrary",),
                                             vmem_limit_bytes=VMEM_LIMIT),
        name="final",
    )(h2, gcol, nf, *([yk] * TOP_K_EXPERTS))


def _split_w_in(w_in, b_gate):
    c = ATTN_WIDTH
    o = 0
    wq = w_in[:, o:o + c]; o += c
    wk = w_in[:, o:o + c]; o += c
    wv = w_in[:, o:o + c]; o += c
    wiq = w_in[:, o:o + IDX_HEADS * IDX_DIM]; o += IDX_HEADS * IDX_DIM
    wik = w_in[:, o:o + IDX_DIM]; o += IDX_DIM
    wiw = w_in[:, o:o + IDX_HEADS]; o += IDX_HEADS
    wglu = w_in[:, o:o + 2 * CONV_CH]; o += 2 * CONV_CH
    wgate = w_in[:, o:]
    wiwt = jnp.concatenate([wiw.T, jnp.zeros((16 - IDX_HEADS, D_MODEL), w_in.dtype)], axis=0)
    bf = lambda a: a.astype(BF16)
    return (bf(wq), bf(wk), bf(wiq), bf(wik), bf(wv.T), bf(wiwt), bf(wglu), bf(wgate),
            b_gate[None, :].astype(F32))


def kernel(x, meta_tokens, rel_bias, norm_mix, w_in, b_gate, w_attn_out, conv_w, conv_b, conv_ln_g, conv_ln_b, w_conv_out, w_out, norm_ffn, w_router, b_router, w_up_gate, b_up_gate, w_down, b_down, norm_final):
    batch, seq, d = x.shape
    n = batch * seq
    x2d = x.reshape(n, d)

    wts = _split_w_in(w_in[0], b_gate[0])
    g_mix = norm_mix[0][None, :]
    q3, k3, iq3, ik, vt, iwt, y, gate = _project(x2d, g_mix, wts, PROJ_TILE)
    _, km3, _, ikm, vtm, _, ym, _ = _project(meta_tokens.astype(F32), g_mix, wts, N_META)

    tab, tabm = _bias_tables(rel_bias.astype(F32))
    attn = _attention(q3, iq3, iwt, k3, ik, vt, km3, ikm, vtm, tab, tabm, batch, seq)

    ymeta = jnp.concatenate([jnp.zeros((HALO - N_META, CONV_CH), F32), ym], axis=0)
    cw = jnp.concatenate([conv_w[0], jnp.zeros((32 - CONV_WIDTH, CONV_CH), F32)], axis=0)
    mix_w = (cw, conv_b[0][None, :], conv_ln_g[0][None, :], conv_ln_b[0][None, :],
             w_conv_out[0].astype(BF16), w_attn_out[0].astype(BF16), w_out[0].astype(BF16),
             norm_ffn[0][None, :], w_router[0].T, b_router[0][:, None])
    h2, hn2, eid8, rank8, gcol, cnt = _mix(x2d, attn, y, ymeta, gate, mix_w, seq)

    counts = cnt[:, 0].astype(I32)
    padded = ((counts + ROW_TILE - 1) // ROW_TILE) * ROW_TILE
    ends = jnp.cumsum(padded)
    starts = ends - padded
    slot_tiles = _slots(starts, eid8, rank8)
    nslots = n * TOP_K_EXPERTS + N_EXPERTS * ROW_TILE
    ntiles = nslots // ROW_TILE
    tile_start = jnp.arange(ntiles, dtype=I32) * ROW_TILE
    nused = (ends[-1] // ROW_TILE).astype(I32)
    last_start = jnp.maximum(ends[-1] - ROW_TILE, 0)
    tile_expert = jnp.sum((jnp.minimum(tile_start, last_start)[:, None] >= ends[None, :]).astype(I32), axis=1)

    spare = TOP_K_EXPERTS * n + jnp.arange(nslots, dtype=I32) % ROW_TILE
    src_tiles = _slot_sources(slot_tiles, spare, n).reshape(ntiles, 1, ROW_TILE)
    yk = _experts(tile_expert, nused[None], src_tiles, hn2, w_up_gate[0], b_up_gate[0], w_down[0], b_down[0], n)
    out = _final(h2, gcol, norm_final[None, :], yk)
    return out.reshape(batch, seq, d)
```

```python
import functools
import math

import jax
import jax.numpy as jnp
from jax import lax
from jax.experimental import pallas as pl
from jax.experimental.pallas import tpu as pltpu

F32 = jnp.float32
BF16 = jnp.bfloat16
I32 = jnp.int32

D_MODEL = 1024
N_META = 16
N_HEADS = 8
HEAD_DIM = 64
ATTN_WIDTH = N_HEADS * HEAD_DIM
IDX_HEADS = 8
IDX_DIM = 64
TOPK_MAX = 256
CONV_CH = 512
CONV_WIDTH = 31
N_BUCKETS = 32
MAX_DISTANCE = 128
N_EXPERTS = 32
TOP_K_EXPERTS = 4
D_FF = 1024
SWIGLU_LIMIT = 7.0
SWIGLU_ALPHA = 1.702
EPS = 1e-6
IDX_SCALE = (IDX_DIM ** -0.5) * (IDX_HEADS ** -0.5)

ROW_TILE = 256
MIX_TILE = 512
PROJ_TILE = 512
Q_TILE = 256
K_CHUNK = 256
COUNT_CHUNKS = 2
HALO = 32
CONV_ROWS = 32
NEG = -1e30
LOG2E = math.log2(math.e)
VMEM_LIMIT = 56 * 1024 * 1024

NT_DIMS = (((1,), (1,)), ((), ()))


def _sigmoid(x):
    return 1.0 / (1.0 + jnp.exp(-x))


def _pack_bf16_pairs(x):
    w = x.shape[1] // 2
    hi = lax.bitcast_convert_type(x[:, :w].astype(BF16).astype(F32), jnp.uint32)
    lo = lax.bitcast_convert_type(x[:, w:].astype(BF16).astype(F32), jnp.uint32)
    return lax.bitcast_convert_type(hi | (lo >> 16), I32)


def _unpack_bf16_pairs(p):
    u = lax.bitcast_convert_type(p, jnp.uint32)
    hi = lax.bitcast_convert_type(u & jnp.uint32(0xFFFF0000), F32)
    lo = lax.bitcast_convert_type(u << 16, F32)
    return jnp.concatenate([hi, lo], axis=1).astype(BF16)


def _proj_kernel(x_ref, g_ref, wq_ref, wk_ref, wiq_ref, wik_ref, wvt_ref, wiwt_ref, wglu_ref,
                 wgate_ref, bgate_ref,
                 q_ref, k_ref, iq_ref, ik_ref, vt_ref, iwt_ref, y_ref, gate_ref):
    x = x_ref[...]
    ms = jnp.mean(x * x, axis=-1, keepdims=True)
    xn = (x * lax.rsqrt(ms + EPS) * g_ref[...]).astype(BF16)

    q = jnp.dot(xn, wq_ref[...], preferred_element_type=F32) * (HEAD_DIM ** -0.5 * LOG2E)
    k = jnp.dot(xn, wk_ref[...], preferred_element_type=F32)
    iq = jnp.dot(xn, wiq_ref[...], preferred_element_type=F32)
    for h in range(N_HEADS):
        sl = slice(h * HEAD_DIM, (h + 1) * HEAD_DIM)
        q_ref[h] = q[:, sl].astype(BF16)
        k_ref[h] = k[:, sl].astype(BF16)
        iq_ref[h] = iq[:, sl].astype(BF16)
    ik_ref[...] = jnp.dot(xn, wik_ref[...], preferred_element_type=F32).astype(BF16)
    vt_ref[...] = lax.dot_general(wvt_ref[...], xn, NT_DIMS, preferred_element_type=F32).astype(BF16)
    iwt = lax.dot_general(wiwt_ref[...], xn, NT_DIMS, preferred_element_type=F32)
    iwt_ref[...] = iwt[:IDX_HEADS] * IDX_SCALE
    glu = jnp.dot(xn, wglu_ref[...], preferred_element_type=F32)
    y_ref[...] = glu[:, :CONV_CH] * _sigmoid(glu[:, CONV_CH:])
    gate = jnp.dot(xn, wgate_ref[...], preferred_element_type=F32) + bgate_ref[...]
    gate_ref[...] = _sigmoid(gate)


def _project(x2d, g, wts, tm):
    n = x2d.shape[0]
    wq, wk, wiq, wik, wvt, wiwt, wglu, wgate, bgate = wts
    full = lambda a: pl.BlockSpec(a.shape, lambda i: (0,) * a.ndim, pipeline_mode=pl.Buffered(1))
    out_shape = (
        jax.ShapeDtypeStruct((N_HEADS, n, HEAD_DIM), BF16),
        jax.ShapeDtypeStruct((N_HEADS, n, HEAD_DIM), BF16),
        jax.ShapeDtypeStruct((IDX_HEADS, n, IDX_DIM), BF16),
        jax.ShapeDtypeStruct((n, IDX_DIM), BF16),
        jax.ShapeDtypeStruct((ATTN_WIDTH, n), BF16),
        jax.ShapeDtypeStruct((IDX_HEADS, n), F32),
        jax.ShapeDtypeStruct((n, CONV_CH), F32),
        jax.ShapeDtypeStruct((n, 2 * D_MODEL), F32),
    )
    out_specs = (
        pl.BlockSpec((N_HEADS, tm, HEAD_DIM), lambda i: (0, i, 0)),
        pl.BlockSpec((N_HEADS, tm, HEAD_DIM), lambda i: (0, i, 0)),
        pl.BlockSpec((IDX_HEADS, tm, IDX_DIM), lambda i: (0, i, 0)),
        pl.BlockSpec((tm, IDX_DIM), lambda i: (i, 0)),
        pl.BlockSpec((ATTN_WIDTH, tm), lambda i: (0, i)),
        pl.BlockSpec((IDX_HEADS, tm), lambda i: (0, i)),
        pl.BlockSpec((tm, CONV_CH), lambda i: (i, 0)),
        pl.BlockSpec((tm, 2 * D_MODEL), lambda i: (i, 0)),
    )
    return pl.pallas_call(
        _proj_kernel,
        grid=(n // tm,),
        in_specs=[pl.BlockSpec((tm, D_MODEL), lambda i: (i, 0)), full(g), full(wq), full(wk), full(wiq),
                  full(wik), full(wvt), full(wiwt), full(wglu), full(wgate), full(bgate)],
        out_specs=out_specs,
        out_shape=out_shape,
        compiler_params=pltpu.CompilerParams(dimension_semantics=("arbitrary",),
                                             vmem_limit_bytes=VMEM_LIMIT),
        name="proj",
    )(x2d, g, wq, wk, wiq, wik, wvt, wiwt, wglu, wgate, bgate)


def _t5_bucket(n):
    max_exact = N_BUCKETS // 2
    nf = jnp.maximum(n, 1).astype(F32)
    large = max_exact + (jnp.log(nf / max_exact) / math.log(MAX_DISTANCE / max_exact)
                         * (N_BUCKETS - max_exact)).astype(I32)
    large = jnp.minimum(large, N_BUCKETS - 1)
    return jnp.where(n < max_exact, n, large)


def _bias_lookup(rb_ref, dist, h):
    bucket = _t5_bucket(jnp.maximum(dist, 0))
    out = jnp.full(dist.shape, NEG, F32)
    for b in range(N_BUCKETS):
        out = jnp.where(bucket == b, rb_ref[b, h] * LOG2E, out)
    return jnp.where(dist >= 0, out, NEG)


def _bias_kernel(rb_ref, tab_ref, tabm_ref):
    kind = pl.program_id(0)
    r = pl.program_id(1)
    rows = tab_ref.shape[2]
    s = lax.broadcasted_iota(I32, (rows, Q_TILE), 0) + r * rows
    t = lax.broadcasted_iota(I32, (rows, Q_TILE), 1)
    dist = jnp.where(kind == 2, 2 * K_CHUNK, t - s + kind * K_CHUNK)
    for h in range(N_HEADS):
        tab_ref[0, h] = _bias_lookup(rb_ref, dist, h)
    m = lax.broadcasted_iota(I32, (N_META, Q_TILE), 0)
    tm_ = lax.broadcasted_iota(I32, (N_META, Q_TILE), 1)
    distm = jnp.where(kind == 0, N_META + tm_ - m, 2 * K_CHUNK)
    for h in range(N_HEADS):
        tabm_ref[0, h] = _bias_lookup(rb_ref, distm, h)


def _bias_tables(rel_bias):
    rows = 64
    return pl.pallas_call(
        _bias_kernel,
        grid=(3, K_CHUNK // rows),
        in_specs=[pl.BlockSpec(memory_space=pltpu.SMEM)],
        out_specs=(pl.BlockSpec((1, N_HEADS, rows, Q_TILE), lambda kd, r: (kd, 0, r, 0)),
                   pl.BlockSpec((1, N_HEADS, N_META, Q_TILE), lambda kd, r: (kd, 0, 0, 0))),
        out_shape=(jax.ShapeDtypeStruct((3, N_HEADS, K_CHUNK, Q_TILE), F32),
                   jax.ShapeDtypeStruct((3, N_HEADS, N_META, Q_TILE), F32)),
        compiler_params=pltpu.CompilerParams(dimension_semantics=("arbitrary", "arbitrary")),
        name="bias_tables",
    )(rel_bias)


def _fold_rows(x, op):
    r, l = x.shape
    x3 = x.reshape(r // 8, 8, l)
    return op(x3, axis=0)


def _attn_kernel(q_ref, iq_ref, iwt_ref, k_ref, ik_ref, vt_ref, km_ref, ikm_ref, vtm_ref,
                 tab_ref, tabm_ref, o_ref, sc_ref, scm_ref, l_ref, lm_ref, ot_ref, cst_ref,
                 hi_ref, lo_ref, him_ref, lom_ref, qt_ref, iqt_ref):
    j = pl.program_id(1)
    nchunks = j + 1
    iw = iwt_ref[...]

    def chunk_rows(c):
        return pl.ds(pl.multiple_of(c * K_CHUNK, K_CHUNK), K_CHUNK)

    for h in range(N_HEADS):
        qt_ref[h] = q_ref[h].astype(F32).T.astype(BF16)
        iqt_ref[h] = iq_ref[h].astype(F32).T.astype(BF16)

    def idx_scores(ikc):
        acc = None
        for h in range(IDX_HEADS):
            s = jnp.dot(ikc, iqt_ref[h], preferred_element_type=F32)
            term = jnp.maximum(s, 0.0) * iw[h:h + 1, :]
            acc = term if acc is None else acc + term
        return acc

    scm_ref[...] = idx_scores(ikm_ref[...])

    row_minus_col = (lax.broadcasted_iota(I32, (K_CHUNK, Q_TILE), 0)
                     - lax.broadcasted_iota(I32, (K_CHUNK, Q_TILE), 1))

    def p1(c, carry):
        rows = chunk_rows(c)
        future = row_minus_col > jnp.where(c == j, 0, K_CHUNK)
        sc_ref[rows, :] = jnp.where(future, -jnp.inf, idx_scores(ik_ref[rows, :]))
        return carry

    lax.fori_loop(0, nchunks, p1, 0)

    def count(pred):
        def body(c, acc):
            blk = sc_ref[chunk_rows(c), :]
            return acc + _fold_rows(jnp.where(pred(blk, c), 1, 0).astype(I32), jnp.sum)
        acc = lax.fori_loop(0, nchunks, body, jnp.zeros((8, Q_TILE), I32))
        acc = acc + _fold_rows(jnp.where(pred(scm_ref[...], -1), 1, 0).astype(I32), jnp.sum)
        return jnp.sum(acc, axis=0, keepdims=True)

    I16 = jnp.int16
    LOW = -2 ** 15

    def split_key(s):
        bits = lax.bitcast_convert_type(s, I32)
        key = jnp.where(bits < 0, bits ^ jnp.int32(0x7FFFFFFF), bits)
        hi = lax.shift_right_arithmetic(key, 16).astype(I16)
        lo = ((key & 0xFFFF) + LOW).astype(I16)
        return hi, lo

    him_ref[...], lom_ref[...] = split_key(scm_ref[...])

    def p2(c, carry):
        rows = chunk_rows(c)
        hi_ref[rows, :], lo_ref[rows, :] = split_key(sc_ref[rows, :])
        return carry

    lax.fori_loop(0, nchunks, p2, 0)

    count_trips = lax.shift_right_logical(nchunks + COUNT_CHUNKS - 1, COUNT_CHUNKS.bit_length() - 1)

    def p2pad(c, carry):
        rows = chunk_rows(c)
        hi_ref[rows, :] = jnp.full((K_CHUNK, Q_TILE), LOW, I16)
        lo_ref[rows, :] = jnp.full((K_CHUNK, Q_TILE), LOW, I16)
        return carry

    lax.fori_loop(nchunks, count_trips * COUNT_CHUNKS, p2pad, 0)

    def tree_sum16(ind):
        parts = [ind[a * 16:(a + 1) * 16, :] for a in range(ind.shape[0] // 16)]
        while len(parts) > 1:
            parts = [parts[a] + parts[a + 1] for a in range(0, len(parts), 2)]
        return parts[0]

    def count16(ref, mref, pred):
        one, zero = jnp.ones((), BF16), jnp.zeros((), BF16)

        def body(cc, acc):
            span = COUNT_CHUNKS * K_CHUNK
            rows = pl.ds(pl.multiple_of(cc * span, span), span)
            return acc + tree_sum16(jnp.where(pred(ref[rows, :]), one, zero)).astype(F32)
        acc = lax.fori_loop(0, count_trips, body, jnp.zeros((16, Q_TILE), F32))
        acc = acc + jnp.where(pred(mref[...]), one, zero).astype(F32)
        return jnp.sum(acc, axis=0, keepdims=True).astype(I32)

    def to16(u):
        return (u + LOW).astype(I16)

    def search_hi(i, u):
        cand = u | lax.shift_left(jnp.int32(1), 15 - i)
        c16 = to16(cand)
        cnt = count16(hi_ref, him_ref, lambda blk: blk >= c16)
        return jnp.where(cnt >= TOPK_MAX, cand, u)

    u_hi = lax.fori_loop(0, 16, search_hi, jnp.zeros((1, Q_TILE), I32))
    t16 = to16(u_hi)
    need = TOPK_MAX - count16(hi_ref, him_ref, lambda blk: blk > t16)

    lom_ref[...] = jnp.where(him_ref[...] == t16, lom_ref[...], jnp.int16(LOW))

    def p2b(c, carry):
        rows = chunk_rows(c)
        lo_ref[rows, :] = jnp.where(hi_ref[rows, :] == t16, lo_ref[rows, :], jnp.int16(LOW))
        return carry

    lax.fori_loop(0, nchunks, p2b, 0)

    def search_lo(i, v):
        cand = v | lax.shift_left(jnp.int32(1), 15 - i)
        c16 = to16(cand)
        cnt = count16(lo_ref, lom_ref, lambda blk: blk >= c16)
        return jnp.where(cnt >= need, cand, v)

    u_lo = lax.fori_loop(0, 16, search_lo, jnp.zeros((1, Q_TILE), I32))
    thr_key = lax.shift_left(u_hi + LOW, 16) | u_lo
    thr_bits = jnp.where(thr_key < 0, thr_key ^ jnp.int32(0x7FFFFFFF), thr_key)
    thr = lax.bitcast_convert_type(thr_bits, F32)

    cnt_ge = count(lambda blk, c: blk >= thr)
    tie = (cnt_ge > TOPK_MAX) & (thr > -jnp.inf)
    any_tie = jnp.max(tie.astype(I32))
    cst_ref[...] = jnp.full((8, Q_TILE), 2 ** 30, I32)

    def pos_of(c, shape):
        r = lax.broadcasted_iota(I32, shape, 0)
        return jnp.where(c < 0, r, r + N_META + c * K_CHUNK)

    @pl.when(any_tie > 0)
    def _():
        cnt_gt = count(lambda blk, c: blk > thr)
        need = TOPK_MAX - cnt_gt

        def bis_pos(i, cs):
            cand = cs | lax.shift_left(jnp.int32(1), 11 - i)
            f = count(lambda blk, c: (blk == thr) & (pos_of(c, blk.shape) < cand))
            return jnp.where(f <= need, cand, cs)

        cs = lax.fori_loop(0, 12, bis_pos, jnp.zeros((1, Q_TILE), I32))
        cs = jnp.where(tie, cs, 2 ** 30)
        cst_ref[...] = jnp.broadcast_to(cs, (8, Q_TILE))

    def to_mask(blk, c, with_ties):
        if with_ties:
            cs = cst_ref[0:1, :]
            sel = (blk > thr) | ((blk == thr) & (pos_of(c, blk.shape) < cs))
        else:
            sel = blk >= thr
        return jnp.where(sel, 0.0, NEG)

    for with_ties in (False, True):
        @pl.when((any_tie > 0) == with_ties)
        def _():
            def body(c, carry):
                rows = chunk_rows(c)
                sc_ref[rows, :] = to_mask(sc_ref[rows, :], c, with_ties)
                return carry
            lax.fori_loop(0, nchunks, body, 0)
            scm_ref[...] = to_mask(scm_ref[...], -1, with_ties)

    kindm = jnp.minimum(j, 1)
    heads = [slice(h * HEAD_DIM, (h + 1) * HEAD_DIM) for h in range(N_HEADS)]

    mx0 = []
    for h in range(N_HEADS):
        lm = (jnp.dot(km_ref[h], qt_ref[h], preferred_element_type=F32)
              + tabm_ref[kindm, h] + scm_ref[...])
        lm_ref[h] = lm
        mx0.append(_fold_rows(lm, jnp.max))

    def pass_a(c, mx):
        rows = chunk_rows(c)
        kind = jnp.minimum(j - c, 2)
        mask = sc_ref[rows, :]
        out = []
        for h in range(N_HEADS):
            l = (jnp.dot(k_ref[h, rows, :], qt_ref[h], preferred_element_type=F32)
                 + tab_ref[kind, h] + mask)
            l_ref[h, rows, :] = l
            out.append(jnp.maximum(mx[h], _fold_rows(l, jnp.max)))
        return tuple(out)

    mx = lax.fori_loop(0, nchunks, pass_a, tuple(mx0))
    m = [jnp.max(mx[h], axis=0, keepdims=True) for h in range(N_HEADS)]

    den0 = []
    for h in range(N_HEADS):
        pm = jnp.exp2(lm_ref[h] - m[h])
        den0.append(_fold_rows(pm, jnp.sum))
        ot_ref[heads[h], :] = jnp.dot(vtm_ref[heads[h], :], pm.astype(BF16), preferred_element_type=F32)

    def pass_b(c, den):
        rows = chunk_rows(c)
        out = []
        for h in range(N_HEADS):
            p = jnp.exp2(l_ref[h, rows, :] - m[h])
            out.append(den[h] + _fold_rows(p, jnp.sum))
            ot_ref[heads[h], :] += jnp.dot(vt_ref[heads[h], rows], p.astype(BF16),
                                           preferred_element_type=F32)
        return tuple(out)

    den = lax.fori_loop(0, nchunks, pass_b, tuple(den0))
    for h in range(N_HEADS):
        ot_ref[heads[h], :] = ot_ref[heads[h], :] / jnp.sum(den[h], axis=0, keepdims=True)
    o_ref[...] = ot_ref[...].T.astype(BF16)


def _attention(q3, iq3, iwt, k3, ik, vt, km3, ikm, vtm, tab, tabm, batch, seq):
    n = batch * seq
    tiles = seq // Q_TILE
    full = lambda a: pl.BlockSpec(a.shape, lambda b, j: (0,) * a.ndim, pipeline_mode=pl.Buffered(1))
    return pl.pallas_call(
        _attn_kernel,
        grid=(batch, tiles),
        in_specs=[
            pl.BlockSpec((N_HEADS, Q_TILE, HEAD_DIM), lambda b, j: (0, b * tiles + j, 0)),
            pl.BlockSpec((IDX_HEADS, Q_TILE, IDX_DIM), lambda b, j: (0, b * tiles + j, 0)),
            pl.BlockSpec((IDX_HEADS, Q_TILE), lambda b, j: (0, b * tiles + j)),
            pl.BlockSpec((N_HEADS, seq, HEAD_DIM), lambda b, j: (0, b, 0)),
            pl.BlockSpec((seq, IDX_DIM), lambda b, j: (b, 0)),
            pl.BlockSpec((ATTN_WIDTH, seq), lambda b, j: (0, b)),
            full(km3), full(ikm), full(vtm), full(tab), full(tabm),
        ],
        out_specs=pl.BlockSpec((Q_TILE, ATTN_WIDTH), lambda b, j: (b * tiles + j, 0)),
        out_shape=jax.ShapeDtypeStruct((n, ATTN_WIDTH), BF16),
        scratch_shapes=[
            pltpu.VMEM((seq, Q_TILE), F32),
            pltpu.VMEM((N_META, Q_TILE), F32),
            pltpu.VMEM((N_HEADS, seq, Q_TILE), F32),
            pltpu.VMEM((N_HEADS, N_META, Q_TILE), F32),
            pltpu.VMEM((ATTN_WIDTH, Q_TILE), F32),
            pltpu.VMEM((8, Q_TILE), I32),
            pltpu.VMEM((seq, Q_TILE), jnp.int16),
            pltpu.VMEM((seq, Q_TILE), jnp.int16),
            pltpu.VMEM((N_META, Q_TILE), jnp.int16),
            pltpu.VMEM((N_META, Q_TILE), jnp.int16),
            pltpu.VMEM((N_HEADS, HEAD_DIM, Q_TILE), BF16),
            pltpu.VMEM((IDX_HEADS, IDX_DIM, Q_TILE), BF16),
        ],
        compiler_params=pltpu.CompilerParams(dimension_semantics=("arbitrary", "arbitrary"),
                                             vmem_limit_bytes=VMEM_LIMIT),
        name="attn",
    )(q3, iq3, iwt, k3, ik, vt, km3, ikm, vtm, tab, tabm)


def _mix_kernel(x_ref, attn_ref, y_ref, yprev_ref, ymeta_ref, gate_ref,
                cw_ref, cb_ref, lng_ref, lnb_ref, wco_ref, wao_ref, wout_ref, nf_ref, wrt_ref, br_ref,
                h2_ref, hn2_ref, eid_ref, rank_ref, gcol_ref, cnt_ref,
                win_ref, shift_ref, base_ref, *, tiles_per_seq):
    i = pl.program_id(0)

    @pl.when(i == 0)
    def _():
        base_ref[...] = jnp.zeros_like(base_ref)

    first = (i % tiles_per_seq) == 0
    win_ref[0:HALO, :] = jnp.where(first, ymeta_ref[...], yprev_ref[...])
    win_ref[HALO:, :] = y_ref[...]
    lead = HALO - (CONV_WIDTH - 1)
    for b in range(8):
        span = MIX_TILE + 8 * (len(range(b, CONV_WIDTH, 8)) - 1)
        shift_ref[b, 0:span, :] = win_ref[pl.ds(lead + b, span), :]
    for hh in range(MIX_TILE // ROW_TILE):
        _mix_rows(hh * ROW_TILE, x_ref, attn_ref, gate_ref, cw_ref, cb_ref, lng_ref, lnb_ref, wco_ref, wao_ref,
                  wout_ref, nf_ref, wrt_ref, br_ref, h2_ref, hn2_ref, eid_ref, rank_ref, gcol_ref, cnt_ref,
                  shift_ref, base_ref)


def _mix_rows(r_off, x_ref, attn_ref, gate_ref, cw_ref, cb_ref, lng_ref, lnb_ref, wco_ref, wao_ref,
              wout_ref, nf_ref, wrt_ref, br_ref, h2_ref, hn2_ref, eid_ref, rank_ref, gcol_ref, cnt_ref,
              shift_ref, base_ref):
    rows = slice(r_off, r_off + ROW_TILE)
    blocks = []
    for r0 in range(r_off, r_off + ROW_TILE, CONV_ROWS):
        acc = jnp.broadcast_to(cb_ref[...], (CONV_ROWS, CONV_CH))
        for b in range(8):
            for a, w in enumerate(range(b, CONV_WIDTH, 8)):
                acc = acc + cw_ref[w:w + 1, :] * shift_ref[b, r0 + 8 * a:r0 + 8 * a + CONV_ROWS, :]
        blocks.append(acc)
    yc = jnp.concatenate(blocks, axis=0)
    mu = jnp.mean(yc, axis=-1, keepdims=True)
    var = jnp.mean(jnp.square(yc - mu), axis=-1, keepdims=True)
    yn = (yc - mu) * lax.rsqrt(var + EPS) * lng_ref[...] + lnb_ref[...]
    ys = yn * _sigmoid(yn)
    y_b = jnp.dot(ys.astype(BF16), wco_ref[...], preferred_element_type=F32)

    y_a = jnp.dot(attn_ref[rows, :], wao_ref[...], preferred_element_type=F32)
    merged = gate_ref[rows, :D_MODEL] * y_a + gate_ref[rows, D_MODEL:] * y_b
    h2 = x_ref[rows, :] + jnp.dot(merged.astype(BF16), wout_ref[...], preferred_element_type=F32)
    h2_ref[rows, :] = h2
    ms = jnp.mean(h2 * h2, axis=-1, keepdims=True)
    hn2 = h2 * lax.rsqrt(ms + EPS) * nf_ref[...]
    hn2_ref[rows, 0, :] = _pack_bf16_pairs(hn2)

    logits = lax.dot_general(wrt_ref[...], hn2, NT_DIMS, preferred_element_type=F32,
                             precision=lax.Precision.HIGHEST) + br_ref[...]
    erow = lax.broadcasted_iota(I32, (N_EXPERTS, ROW_TILE), 0)
    vals, ids = [], []
    l = logits
    for _ in range(TOP_K_EXPERTS):
        m = jnp.max(l, axis=0, keepdims=True)
        idx = jnp.min(jnp.where(l == m, erow, N_EXPERTS), axis=0, keepdims=True)
        vals.append(m)
        ids.append(idx)
        l = jnp.where(erow == idx, -jnp.inf, l)
    ex = [jnp.exp(v - vals[0]) for v in vals]
    den = ex[0] + ex[1] + ex[2] + ex[3]
    gates = [e / den for e in ex]

    onehot = [(erow == idx) for idx in ids]
    oh = jnp.concatenate([jnp.where(o, 1.0, 0.0) for o in onehot], axis=0)
    tr = lax.broadcasted_iota(I32, (ROW_TILE, ROW_TILE), 0)
    tc = lax.broadcasted_iota(I32, (ROW_TILE, ROW_TILE), 1)
    upper = jnp.where(tr <= tc, 1.0, 0.0).astype(BF16)
    pref = jnp.dot(oh.astype(BF16), upper, preferred_element_type=F32)
    offs = base_ref[:, 0:1]
    ranks = []
    for kk in range(TOP_K_EXPERTS):
        pk = pref[kk * N_EXPERTS:(kk + 1) * N_EXPERTS, :]
        r = jnp.sum(jnp.where(onehot[kk], offs + pk - 1.0, 0.0), axis=0, keepdims=True)
        ranks.append(r.astype(I32))
        offs = offs + pk[:, ROW_TILE - 1:ROW_TILE]
    base_ref[...] = jnp.broadcast_to(offs, base_ref.shape)
    cnt_ref[...] = jnp.broadcast_to(offs, cnt_ref.shape)

    zi = jnp.zeros((8 - TOP_K_EXPERTS, ROW_TILE), I32)
    eid_ref[:, rows] = jnp.concatenate(ids + [zi], axis=0)
    rank_ref[:, rows] = jnp.concatenate(ranks + [zi], axis=0)
    g8 = jnp.concatenate(gates + [jnp.zeros((128 - TOP_K_EXPERTS, ROW_TILE), F32)], axis=0)
    gcol_ref[rows, :] = g8.T


def _mix(x2d, attn, y, ymeta, gate, wts, seq):
    n = x2d.shape[0]
    tiles_per_seq = seq // MIX_TILE
    halo_per_tile = MIX_TILE // HALO
    full = lambda a: pl.BlockSpec(a.shape, lambda i: (0,) * a.ndim)
    row = lambda w: pl.BlockSpec((MIX_TILE, w), lambda i: (i, 0))
    lane = lambda r: pl.BlockSpec((r, MIX_TILE), lambda i: (0, i))
    out_shape = (
        jax.ShapeDtypeStruct((n, D_MODEL), F32),
        jax.ShapeDtypeStruct((n, 1, D_MODEL // 2), I32),
        jax.ShapeDtypeStruct((8, n), I32),
        jax.ShapeDtypeStruct((8, n), I32),
        jax.ShapeDtypeStruct((n, 128), F32),
        jax.ShapeDtypeStruct((N_EXPERTS, 128), F32),
    )
    out_specs = (row(D_MODEL), pl.BlockSpec((MIX_TILE, 1, D_MODEL // 2), lambda i: (i, 0, 0)),
                 lane(8), lane(8), row(128), full(out_shape[5]))
    return pl.pallas_call(
        functools.partial(_mix_kernel, tiles_per_seq=tiles_per_seq),
        grid=(n // MIX_TILE,),
        in_specs=[row(D_MODEL), row(ATTN_WIDTH), row(CONV_CH),
                  pl.BlockSpec((HALO, CONV_CH), lambda i: (jnp.maximum(i * halo_per_tile - 1, 0), 0)),
                  full(ymeta), row(2 * D_MODEL)] + [full(w) for w in wts],
        out_specs=out_specs,
        out_shape=out_shape,
        scratch_shapes=[pltpu.VMEM((HALO + MIX_TILE, CONV_CH), F32),
                        pltpu.VMEM((8, HALO + MIX_TILE, CONV_CH), F32),
                        pltpu.VMEM((N_EXPERTS, 128), F32)],
        compiler_params=pltpu.CompilerParams(dimension_semantics=("arbitrary",),
                                             vmem_limit_bytes=VMEM_LIMIT),
        name="mix",
    )(x2d, attn, y, y, ymeta, gate, *wts)


def _slots_kernel(starts_ref, eid_ref, rank_ref, slot_ref):
    eid = eid_ref[...]
    base = jnp.zeros(eid.shape, I32)
    for e in range(N_EXPERTS):
        base = jnp.where(eid == e, starts_ref[e], base)
    slot = base + rank_ref[...]
    for t in range(slot_ref.shape[0]):
        for kk in range(TOP_K_EXPERTS):
            slot_ref[t, :, kk * ROW_TILE:(kk + 1) * ROW_TILE] = slot[kk:kk + 1, t * ROW_TILE:(t + 1) * ROW_TILE]


def _slots(starts, eid8, rank8):
    n = eid8.shape[1]
    tiles = 8
    spec = pl.BlockSpec((8, tiles * ROW_TILE), lambda i: (0, i))
    return pl.pallas_call(
        _slots_kernel,
        grid=(n // (tiles * ROW_TILE),),
        in_specs=[pl.BlockSpec(memory_space=pltpu.SMEM), spec, spec],
        out_specs=pl.BlockSpec((tiles, 1, COPIES_PER_TILE), lambda i: (i, 0, 0)),
        out_shape=jax.ShapeDtypeStruct((n // ROW_TILE, 1, COPIES_PER_TILE), I32),
        compiler_params=pltpu.CompilerParams(dimension_semantics=("arbitrary",)),
        name="slots",
    )(starts, eid8, rank8)


COPIES_PER_TILE = TOP_K_EXPERTS * ROW_TILE
SLOT_UNROLL = 16


def _slot_source_kernel(slot_ref, init_ref, src_ref, sem, *, n):
    i = pl.program_id(0)

    @pl.when(i == 0)
    def _():
        cp = pltpu.make_async_copy(init_ref, src_ref, sem)
        cp.start()
        cp.wait()

    for kk in range(TOP_K_EXPERTS):
        def body(g, carry):
            for u in range(SLOT_UNROLL):
                t = g * SLOT_UNROLL + u
                src_ref[slot_ref[0, 0, kk * ROW_TILE + t]] = kk * n + i * ROW_TILE + t
            return carry
        lax.fori_loop(0, ROW_TILE // SLOT_UNROLL, body, 0)


def _slot_sources(slot_tiles, init, n):
    return pl.pallas_call(
        functools.partial(_slot_source_kernel, n=n),
        grid=(slot_tiles.shape[0],),
        in_specs=[pl.BlockSpec((1, 1, COPIES_PER_TILE), lambda i: (i, 0, 0), memory_space=pltpu.SMEM),
                  pl.BlockSpec(memory_space=pl.ANY)],
        out_specs=pl.BlockSpec(memory_space=pltpu.SMEM),
        out_shape=jax.ShapeDtypeStruct(init.shape, I32),
        scratch_shapes=[pltpu.SemaphoreType.DMA(())],
        compiler_params=pltpu.CompilerParams(dimension_semantics=("arbitrary",)),
        name="slot_sources",
    )(slot_tiles, init)


def _expert_kernel(te_ref, nused_ref, src_ref, src_next_ref, src_prev_ref, hn_hbm,
                   wug_ref, bug_ref, wd_ref, bd_ref, yk_hbm, xbuf, ybuf, wug_bf, wd_bf, gsem, ssem, *, n):
    i = pl.program_id(0)
    nused = nused_ref[0]

    def gather_copy(src, r, b):
        tok = src[0, 0, r] & (n - 1)
        return pltpu.make_async_copy(hn_hbm.at[tok], xbuf.at[b, r], gsem.at[b])

    def scatter_copy(src, r, b):
        return pltpu.make_async_copy(ybuf.at[b, r], yk_hbm.at[src[0, 0, r]], ssem.at[b])

    @pl.when(i <= nused)
    def _():
        cur = i % 2
        oth = 1 - cur

        def wait_scatter(b):
            pltpu.make_async_copy(ybuf.at[b], yk_hbm.at[pl.ds(0, ROW_TILE)], ssem.at[b]).wait()

        @pl.when(i == 0)
        def _():
            for r in range(ROW_TILE):
                gather_copy(src_ref, r, 0).start()
            ybuf[1] = jnp.zeros(ybuf.shape[1:], F32)
            spare = pltpu.make_async_copy(ybuf.at[1], yk_hbm.at[pl.ds(TOP_K_EXPERTS * n, ROW_TILE)],
                                          ssem.at[1])
            spare.start()
            spare.wait()

        @pl.when(i < nused)
        def _():
            for r in range(ROW_TILE):
                gather_copy(src_next_ref, r, oth).start()

        @pl.when(i > 0)
        def _():
            for r in range(ROW_TILE):
                scatter_copy(src_prev_ref, r, oth).start()

        pltpu.make_async_copy(hn_hbm.at[pl.ds(0, ROW_TILE)], xbuf.at[cur], gsem.at[cur]).wait()

        prev = te_ref[jnp.maximum(i - 1, 0)]
        fresh = (i == 0) | (te_ref[jnp.minimum(i, nused - 1)] != prev)

        @pl.when(fresh)
        def _():
            wug_bf[...] = wug_ref[0].astype(BF16)
            wd_bf[...] = wd_ref[0].astype(BF16)

        xb = _unpack_bf16_pairs(xbuf[cur, :, 0, :])
        ug = jnp.dot(xb, wug_bf[...], preferred_element_type=F32) + bug_ref[0]
        gate = jnp.minimum(ug[:, :D_FF], SWIGLU_LIMIT)
        up = jnp.clip(ug[:, D_FF:], -SWIGLU_LIMIT, SWIGLU_LIMIT)
        act = (up + 1.0) * gate * _sigmoid(SWIGLU_ALPHA * gate)
        y = jnp.dot(act.astype(BF16), wd_bf[...], preferred_element_type=F32) + bd_ref[0]

        @pl.when(i > 1)
        def _():
            wait_scatter(cur)

        ybuf[cur, :, 0, :] = y

        @pl.when((i == nused) & (i > 0))
        def _():
            wait_scatter(oth)


def _experts(tile_expert, nused, src_tiles, hn2p, w_ug, b_ug, w_down, b_down, n):
    ntiles = src_tiles.shape[0]
    clamp = lambda i, nu: jnp.maximum(jnp.minimum(i, nu[0] - 1), 0)
    smem = lambda f: pl.BlockSpec((1, 1, ROW_TILE), f, memory_space=pltpu.SMEM)
    expert = lambda i, te, nu: (te[clamp(i, nu)], 0, 0)
    grid_spec = pltpu.PrefetchScalarGridSpec(
        num_scalar_prefetch=2,
        grid=(ntiles + 1,),
        in_specs=[
            smem(lambda i, te, nu: (clamp(i, nu), 0, 0)),
            smem(lambda i, te, nu: (clamp(i + 1, nu), 0, 0)),
            smem(lambda i, te, nu: (clamp(i - 1, nu), 0, 0)),
            pl.BlockSpec(memory_space=pl.ANY),
            pl.BlockSpec((1, D_MODEL, 2 * D_FF), expert),
            pl.BlockSpec((1, 1, 2 * D_FF), expert),
            pl.BlockSpec((1, D_FF, D_MODEL), expert),
            pl.BlockSpec((1, 1, D_MODEL), expert),
        ],
        out_specs=pl.BlockSpec(memory_space=pl.ANY),
        scratch_shapes=[pltpu.VMEM((2, ROW_TILE, 1, D_MODEL // 2), I32),
                        pltpu.VMEM((2, ROW_TILE, 1, D_MODEL), F32),
                        pltpu.VMEM((D_MODEL, 2 * D_FF), BF16), pltpu.VMEM((D_FF, D_MODEL), BF16),
                        pltpu.SemaphoreType.DMA((2,)), pltpu.SemaphoreType.DMA((2,))],
    )
    return pl.pallas_call(
        functools.partial(_expert_kernel, n=n),
        grid_spec=grid_spec,
        out_shape=jax.ShapeDtypeStruct((TOP_K_EXPERTS * n + ROW_TILE, 1, D_MODEL), F32),
        compiler_params=pltpu.CompilerParams(dimension_semantics=("arbitrary",),
                                             vmem_limit_bytes=VMEM_LIMIT),
        name="experts",
    )(tile_expert, nused, src_tiles, src_tiles, src_tiles, hn2p,
      w_ug, b_ug[:, None, :], w_down, b_down[:, None, :])


def _final_kernel(h2_ref, gcol_ref, nf_ref, *rest):
    y_refs, o_ref = rest[:TOP_K_EXPERTS], rest[TOP_K_EXPERTS]
    h = h2_ref[...]
    for kk in range(TOP_K_EXPERTS):
        h = h + gcol_ref[:, kk:kk + 1] * y_refs[kk][:, 0, :]
    ms = jnp.mean(h * h, axis=-1, keepdims=True)
    o_ref[...] = h * lax.rsqrt(ms + EPS) * nf_ref[...]


def _final(h2, gcol, nf, yk):
    n = h2.shape[0]
    ntiles = n // ROW_TILE
    choice = lambda kk: pl.BlockSpec((ROW_TILE, 1, D_MODEL), lambda i: (kk * ntiles + i, 0, 0))
    return pl.pallas_call(
        _final_kernel,
        grid=(ntiles,),
        in_specs=[pl.BlockSpec((ROW_TILE, D_MODEL), lambda i: (i, 0)),
                  pl.BlockSpec((ROW_TILE, 128), lambda i: (i, 0)),
                  pl.BlockSpec((1, D_MODEL), lambda i: (0, 0))] + [choice(kk) for kk in range(TOP_K_EXPERTS)],
        out_specs=pl.BlockSpec((ROW_TILE, D_MODEL), lambda i: (i, 0)),
        out_shape=jax.ShapeDtypeStruct((n, D_MODEL), F32),
        compiler_params=pltpu.CompilerParams(dimension_semantics=("arbitrary",),
                                             vmem_limit_bytes=VMEM_LIMIT),
        name="final",
    )(h2, gcol, nf, *([yk] * TOP_K_EXPERTS))


def _split_w_in(w_in, b_gate):
    c = ATTN_WIDTH
    o = 0
    wq = w_in[:, o:o + c]; o += c
    wk = w_in[:, o:o + c]; o += c
    wv = w_in[:, o:o + c]; o += c
    wiq = w_in[:, o:o + IDX_HEADS * IDX_DIM]; o += IDX_HEADS * IDX_DIM
    wik = w_in[:, o:o + IDX_DIM]; o += IDX_DIM
    wiw = w_in[:, o:o + IDX_HEADS]; o += IDX_HEADS
    wglu = w_in[:, o:o + 2 * CONV_CH]; o += 2 * CONV_CH
    wgate = w_in[:, o:]
    wiwt = jnp.concatenate([wiw.T, jnp.zeros((16 - IDX_HEADS, D_MODEL), w_in.dtype)], axis=0)
    bf = lambda a: a.astype(BF16)
    return (bf(wq), bf(wk), bf(wiq), bf(wik), bf(wv.T), bf(wiwt), bf(wglu), bf(wgate),
            b_gate[None, :].astype(F32))


def kernel(x, meta_tokens, rel_bias, norm_mix, w_in, b_gate, w_attn_out, conv_w, conv_b, conv_ln_g, conv_ln_b, w_conv_out, w_out, norm_ffn, w_router, b_router, w_up_gate, b_up_gate, w_down, b_down, norm_final):
    batch, seq, d = x.shape
    n = batch * seq
    x2d = x.reshape(n, d)

    wts = _split_w_in(w_in[0], b_gate[0])
    g_mix = norm_mix[0][None, :]
    q3, k3, iq3, ik, vt, iwt, y, gate = _project(x2d, g_mix, wts, PROJ_TILE)
    _, km3, _, ikm, vtm, _, ym, _ = _project(meta_tokens.astype(F32), g_mix, wts, N_META)

    tab, tabm = _bias_tables(rel_bias.astype(F32))
    attn = _attention(q3, iq3, iwt, k3, ik, vt, km3, ikm, vtm, tab, tabm, batch, seq)

    ymeta = jnp.concatenate([jnp.zeros((HALO - N_META, CONV_CH), F32), ym], axis=0)
    cw = jnp.concatenate([conv_w[0], jnp.zeros((32 - CONV_WIDTH, CONV_CH), F32)], axis=0)
    mix_w = (cw, conv_b[0][None, :], conv_ln_g[0][None, :], conv_ln_b[0][None, :],
             w_conv_out[0].astype(BF16), w_attn_out[0].astype(BF16), w_out[0].astype(BF16),
             norm_ffn[0][None, :], w_router[0].T, b_router[0][:, None])
    h2, hn2, eid8, rank8, gcol, cnt = _mix(x2d, attn, y, ymeta, gate, mix_w, seq)

    counts = cnt[:, 0].astype(I32)
    padded = ((counts + ROW_TILE - 1) // ROW_TILE) * ROW_TILE
    ends = jnp.cumsum(padded)
    starts = ends - padded
    slot_tiles = _slots(starts, eid8, rank8)
    nslots = n * TOP_K_EXPERTS + N_EXPERTS * ROW_TILE
    ntiles = nslots // ROW_TILE
    tile_start = jnp.arange(ntiles, dtype=I32) * ROW_TILE
    nused = (ends[-1] // ROW_TILE).astype(I32)
    last_start = jnp.maximum(ends[-1] - ROW_TILE, 0)
    tile_expert = jnp.sum((jnp.minimum(tile_start, last_start)[:, None] >= ends[None, :]).astype(I32), axis=1)

    spare = TOP_K_EXPERTS * n + jnp.arange(nslots, dtype=I32) % ROW_TILE
    src_tiles = _slot_sources(slot_tiles, spare, n).reshape(ntiles, 1, ROW_TILE)
    yk = _experts(tile_expert, nused[None], src_tiles, hn2, w_up_gate[0], b_up_gate[0], w_down[0], b_down[0], n)
    out = _final(h2, gcol, norm_final[None, :], yk)
    return out.reshape(batch, seq, d)
```

```python
import functools
import math

import jax
import jax.numpy as jnp
from jax import lax
from jax.experimental import pallas as pl
from jax.experimental.pallas import tpu as pltpu

F32 = jnp.float32
BF16 = jnp.bfloat16
I32 = jnp.int32

D_MODEL = 1024
N_META = 16
N_HEADS = 8
HEAD_DIM = 64
ATTN_WIDTH = N_HEADS * HEAD_DIM
IDX_HEADS = 8
IDX_DIM = 64
TOPK_MAX = 256
CONV_CH = 512
CONV_WIDTH = 31
N_BUCKETS = 32
MAX_DISTANCE = 128
N_EXPERTS = 32
TOP_K_EXPERTS = 4
D_FF = 1024
SWIGLU_LIMIT = 7.0
SWIGLU_ALPHA = 1.702
EPS = 1e-6
IDX_SCALE = (IDX_DIM ** -0.5) * (IDX_HEADS ** -0.5)

ROW_TILE = 256
MIX_TILE = 512
PROJ_TILE = 512
Q_TILE = 256
K_CHUNK = 256
COUNT_CHUNKS = 2
HALO = 32
CONV_ROWS = 32
NEG = -1e30
LOG2E = math.log2(math.e)
VMEM_LIMIT = 56 * 1024 * 1024

NT_DIMS = (((1,), (1,)), ((), ()))


def _sigmoid(x):
    return 1.0 / (1.0 + jnp.exp(-x))


def _pack_bf16_pairs(x):
    w = x.shape[1] // 2
    hi = lax.bitcast_convert_type(x[:, :w].astype(BF16).astype(F32), jnp.uint32)
    lo = lax.bitcast_convert_type(x[:, w:].astype(BF16).astype(F32), jnp.uint32)
    return lax.bitcast_convert_type(hi | (lo >> 16), I32)


def _unpack_bf16_pairs(p):
    u = lax.bitcast_convert_type(p, jnp.uint32)
    hi = lax.bitcast_convert_type(u & jnp.uint32(0xFFFF0000), F32)
    lo = lax.bitcast_convert_type(u << 16, F32)
    return jnp.concatenate([hi, lo], axis=1).astype(BF16)


def _proj_kernel(x_ref, g_ref, wq_ref, wk_ref, wiq_ref, wik_ref, wvt_ref, wiwt_ref, wglu_ref,
                 wgate_ref, bgate_ref,
                 q_ref, k_ref, iq_ref, ik_ref, vt_ref, iwt_ref, y_ref, gate_ref):
    x = x_ref[...]
    ms = jnp.mean(x * x, axis=-1, keepdims=True)
    xn = (x * lax.rsqrt(ms + EPS) * g_ref[...]).astype(BF16)

    q = jnp.dot(xn, wq_ref[...], preferred_element_type=F32) * (HEAD_DIM ** -0.5 * LOG2E)
    k = jnp.dot(xn, wk_ref[...], preferred_element_type=F32)
    iq = jnp.dot(xn, wiq_ref[...], preferred_element_type=F32)
    for h in range(N_HEADS):
        sl = slice(h * HEAD_DIM, (h + 1) * HEAD_DIM)
        q_ref[h] = q[:, sl].astype(BF16)
        k_ref[h] = k[:, sl].astype(BF16)
        iq_ref[h] = iq[:, sl].astype(BF16)
    ik_ref[...] = jnp.dot(xn, wik_ref[...], preferred_element_type=F32).astype(BF16)
    vt_ref[...] = lax.dot_general(wvt_ref[...], xn, NT_DIMS, preferred_element_type=F32).astype(BF16)
    iwt = lax.dot_general(wiwt_ref[...], xn, NT_DIMS, preferred_element_type=F32)
    iwt_ref[...] = iwt[:IDX_HEADS] * IDX_SCALE
    glu = jnp.dot(xn, wglu_ref[...], preferred_element_type=F32)
    y_ref[...] = glu[:, :CONV_CH] * _sigmoid(glu[:, CONV_CH:])
    gate = jnp.dot(xn, wgate_ref[...], preferred_element_type=F32) + bgate_ref[...]
    gate_ref[...] = _sigmoid(gate)


def _project(x2d, g, wts, tm):
    n = x2d.shape[0]
    wq, wk, wiq, wik, wvt, wiwt, wglu, wgate, bgate = wts
    full = lambda a: pl.BlockSpec(a.shape, lambda i: (0,) * a.ndim, pipeline_mode=pl.Buffered(1))
    out_shape = (
        jax.ShapeDtypeStruct((N_HEADS, n, HEAD_DIM), BF16),
        jax.ShapeDtypeStruct((N_HEADS, n, HEAD_DIM), BF16),
        jax.ShapeDtypeStruct((IDX_HEADS, n, IDX_DIM), BF16),
        jax.ShapeDtypeStruct((n, IDX_DIM), BF16),
        jax.ShapeDtypeStruct((ATTN_WIDTH, n), BF16),
        jax.ShapeDtypeStruct((IDX_HEADS, n), F32),
        jax.ShapeDtypeStruct((n, CONV_CH), F32),
        jax.ShapeDtypeStruct((n, 2 * D_MODEL), F32),
    )
    out_specs = (
        pl.BlockSpec((N_HEADS, tm, HEAD_DIM), lambda i: (0, i, 0)),
        pl.BlockSpec((N_HEADS, tm, HEAD_DIM), lambda i: (0, i, 0)),
        pl.BlockSpec((IDX_HEADS, tm, IDX_DIM), lambda i: (0, i, 0)),
        pl.BlockSpec((tm, IDX_DIM), lambda i: (i, 0)),
        pl.BlockSpec((ATTN_WIDTH, tm), lambda i: (0, i)),
        pl.BlockSpec((IDX_HEADS, tm), lambda i: (0, i)),
        pl.BlockSpec((tm, CONV_CH), lambda i: (i, 0)),
        pl.BlockSpec((tm, 2 * D_MODEL), lambda i: (i, 0)),
    )
    return pl.pallas_call(
        _proj_kernel,
        grid=(n // tm,),
        in_specs=[pl.BlockSpec((tm, D_MODEL), lambda i: (i, 0)), full(g), full(wq), full(wk), full(wiq),
                  full(wik), full(wvt), full(wiwt), full(wglu), full(wgate), full(bgate)],
        out_specs=out_specs,
        out_shape=out_shape,
        compiler_params=pltpu.CompilerParams(dimension_semantics=("arbitrary",),
                                             vmem_limit_bytes=VMEM_LIMIT),
        name="proj",
    )(x2d, g, wq, wk, wiq, wik, wvt, wiwt, wglu, wgate, bgate)


def _t5_bucket(n):
    max_exact = N_BUCKETS // 2
    nf = jnp.maximum(n, 1).astype(F32)
    large = max_exact + (jnp.log(nf / max_exact) / math.log(MAX_DISTANCE / max_exact)
                         * (N_BUCKETS - max_exact)).astype(I32)
    large = jnp.minimum(large, N_BUCKETS - 1)
    return jnp.where(n < max_exact, n, large)


def _bias_lookup(rb_ref, dist, h):
    bucket = _t5_bucket(jnp.maximum(dist, 0))
    out = jnp.full(dist.shape, NEG, F32)
    for b in range(N_BUCKETS):
        out = jnp.where(bucket == b, rb_ref[b, h] * LOG2E, out)
    return jnp.where(dist >= 0, out, NEG)


def _bias_kernel(rb_ref, tab_ref, tabm_ref):
    kind = pl.program_id(0)
    r = pl.program_id(1)
    rows = tab_ref.shape[2]
    s = lax.broadcasted_iota(I32, (rows, Q_TILE), 0) + r * rows
    t = lax.broadcasted_iota(I32, (rows, Q_TILE), 1)
    dist = jnp.where(kind == 2, 2 * K_CHUNK, t - s + kind * K_CHUNK)
    for h in range(N_HEADS):
        tab_ref[0, h] = _bias_lookup(rb_ref, dist, h)
    m = lax.broadcasted_iota(I32, (N_META, Q_TILE), 0)
    tm_ = lax.broadcasted_iota(I32, (N_META, Q_TILE), 1)
    distm = jnp.where(kind == 0, N_META + tm_ - m, 2 * K_CHUNK)
    for h in range(N_HEADS):
        tabm_ref[0, h] = _bias_lookup(rb_ref, distm, h)


def _bias_tables(rel_bias):
    rows = 64
    return pl.pallas_call(
        _bias_kernel,
        grid=(3, K_CHUNK // rows),
        in_specs=[pl.BlockSpec(memory_space=pltpu.SMEM)],
        out_specs=(pl.BlockSpec((1, N_HEADS, rows, Q_TILE), lambda kd, r: (kd, 0, r, 0)),
                   pl.BlockSpec((1, N_HEADS, N_META, Q_TILE), lambda kd, r: (kd, 0, 0, 0))),
        out_shape=(jax.ShapeDtypeStruct((3, N_HEADS, K_CHUNK, Q_TILE), F32),
                   jax.ShapeDtypeStruct((3, N_HEADS, N_META, Q_TILE), F32)),
        compiler_params=pltpu.CompilerParams(dimension_semantics=("arbitrary", "arbitrary")),
        name="bias_tables",
    )(rel_bias)


def _fold_rows(x, op):
    r, l = x.shape
    x3 = x.reshape(r // 8, 8, l)
    return op(x3, axis=0)


def _attn_kernel(q_ref, iq_ref, iwt_ref, k_ref, ik_ref, vt_ref, km_ref, ikm_ref, vtm_ref,
                 tab_ref, tabm_ref, o_ref, sc_ref, scm_ref, l_ref, lm_ref, ot_ref, cst_ref,
                 hi_ref, lo_ref, him_ref, lom_ref, qt_ref, iqt_ref):
    j = pl.program_id(1)
    nchunks = j + 1
    iw = iwt_ref[...]

    def chunk_rows(c):
        return pl.ds(pl.multiple_of(c * K_CHUNK, K_CHUNK), K_CHUNK)

    def paired_chunk_loop(body, init):
        carry = lax.fori_loop(0, lax.shift_right_logical(nchunks, 1),
                              lambda i, c: body(2 * i + 1, body(2 * i, c)), init)
        return lax.cond((nchunks & 1) == 1, lambda c: body(nchunks - 1, c), lambda c: c, carry)

    for h in range(N_HEADS):
        qt_ref[h] = q_ref[h].astype(F32).T.astype(BF16)
        iqt_ref[h] = iq_ref[h].astype(F32).T.astype(BF16)

    def idx_scores(ikc):
        acc = None
        for h in range(IDX_HEADS):
            s = jnp.dot(ikc, iqt_ref[h], preferred_element_type=F32)
            term = jnp.maximum(s, 0.0) * iw[h:h + 1, :]
            acc = term if acc is None else acc + term
        return acc

    scm_ref[...] = idx_scores(ikm_ref[...])

    row_minus_col = (lax.broadcasted_iota(I32, (K_CHUNK, Q_TILE), 0)
                     - lax.broadcasted_iota(I32, (K_CHUNK, Q_TILE), 1))

    def p1(c, carry):
        rows = chunk_rows(c)
        future = row_minus_col > jnp.where(c == j, 0, K_CHUNK)
        sc_ref[rows, :] = jnp.where(future, -jnp.inf, idx_scores(ik_ref[rows, :]))
        return carry

    paired_chunk_loop(p1, 0)

    def count(pred):
        def body(c, acc):
            blk = sc_ref[chunk_rows(c), :]
            return acc + _fold_rows(jnp.where(pred(blk, c), 1, 0).astype(I32), jnp.sum)
        acc = lax.fori_loop(0, nchunks, body, jnp.zeros((8, Q_TILE), I32))
        acc = acc + _fold_rows(jnp.where(pred(scm_ref[...], -1), 1, 0).astype(I32), jnp.sum)
        return jnp.sum(acc, axis=0, keepdims=True)

    I16 = jnp.int16
    LOW = -2 ** 15

    def split_key(s):
        bits = lax.bitcast_convert_type(s, I32)
        key = jnp.where(bits < 0, bits ^ jnp.int32(0x7FFFFFFF), bits)
        hi = lax.shift_right_arithmetic(key, 16).astype(I16)
        lo = ((key & 0xFFFF) + LOW).astype(I16)
        return hi, lo

    him_ref[...], lom_ref[...] = split_key(scm_ref[...])

    def p2(c, carry):
        rows = chunk_rows(c)
        hi_ref[rows, :], lo_ref[rows, :] = split_key(sc_ref[rows, :])
        return carry

    lax.fori_loop(0, nchunks, p2, 0)

    count_trips = lax.shift_right_logical(nchunks + COUNT_CHUNKS - 1, COUNT_CHUNKS.bit_length() - 1)

    def p2pad(c, carry):
        rows = chunk_rows(c)
        hi_ref[rows, :] = jnp.full((K_CHUNK, Q_TILE), LOW, I16)
        lo_ref[rows, :] = jnp.full((K_CHUNK, Q_TILE), LOW, I16)
        return carry

    lax.fori_loop(nchunks, count_trips * COUNT_CHUNKS, p2pad, 0)

    def tree_sum16(ind):
        parts = [ind[a * 16:(a + 1) * 16, :] for a in range(ind.shape[0] // 16)]
        while len(parts) > 1:
            parts = [parts[a] + parts[a + 1] for a in range(0, len(parts), 2)]
        return parts[0]

    def count16(ref, mref, pred):
        one, zero = jnp.ones((), BF16), jnp.zeros((), BF16)

        def body(cc, acc):
            span = COUNT_CHUNKS * K_CHUNK
            rows = pl.ds(pl.multiple_of(cc * span, span), span)
            return acc + tree_sum16(jnp.where(pred(ref[rows, :]), one, zero)).astype(F32)
        acc = lax.fori_loop(0, count_trips, body, jnp.zeros((16, Q_TILE), F32))
        acc = acc + jnp.where(pred(mref[...]), one, zero).astype(F32)
        return jnp.sum(acc, axis=0, keepdims=True).astype(I32)

    def to16(u):
        return (u + LOW).astype(I16)

    def search_hi(i, u):
        cand = u | lax.shift_left(jnp.int32(1), 15 - i)
        c16 = to16(cand)
        cnt = count16(hi_ref, him_ref, lambda blk: blk >= c16)
        return jnp.where(cnt >= TOPK_MAX, cand, u)

    u_hi = lax.fori_loop(0, 16, search_hi, jnp.zeros((1, Q_TILE), I32))
    t16 = to16(u_hi)
    need = TOPK_MAX - count16(hi_ref, him_ref, lambda blk: blk > t16)

    lom_ref[...] = jnp.where(him_ref[...] == t16, lom_ref[...], jnp.int16(LOW))

    def p2b(c, carry):
        rows = chunk_rows(c)
        lo_ref[rows, :] = jnp.where(hi_ref[rows, :] == t16, lo_ref[rows, :], jnp.int16(LOW))
        return carry

    lax.fori_loop(0, nchunks, p2b, 0)

    def search_lo(i, v):
        cand = v | lax.shift_left(jnp.int32(1), 15 - i)
        c16 = to16(cand)
        cnt = count16(lo_ref, lom_ref, lambda blk: blk >= c16)
        return jnp.where(cnt >= need, cand, v)

    u_lo = lax.fori_loop(0, 16, search_lo, jnp.zeros((1, Q_TILE), I32))
    thr_key = lax.shift_left(u_hi + LOW, 16) | u_lo
    thr_bits = jnp.where(thr_key < 0, thr_key ^ jnp.int32(0x7FFFFFFF), thr_key)
    thr = lax.bitcast_convert_type(thr_bits, F32)

    cnt_ge = count(lambda blk, c: blk >= thr)
    tie = (cnt_ge > TOPK_MAX) & (thr > -jnp.inf)
    any_tie = jnp.max(tie.astype(I32))
    cst_ref[...] = jnp.full((8, Q_TILE), 2 ** 30, I32)

    def pos_of(c, shape):
        r = lax.broadcasted_iota(I32, shape, 0)
        return jnp.where(c < 0, r, r + N_META + c * K_CHUNK)

    @pl.when(any_tie > 0)
    def _():
        cnt_gt = count(lambda blk, c: blk > thr)
        need = TOPK_MAX - cnt_gt

        def bis_pos(i, cs):
            cand = cs | lax.shift_left(jnp.int32(1), 11 - i)
            f = count(lambda blk, c: (blk == thr) & (pos_of(c, blk.shape) < cand))
            return jnp.where(f <= need, cand, cs)

        cs = lax.fori_loop(0, 12, bis_pos, jnp.zeros((1, Q_TILE), I32))
        cs = jnp.where(tie, cs, 2 ** 30)
        cst_ref[...] = jnp.broadcast_to(cs, (8, Q_TILE))

    def to_mask(blk, c, with_ties):
        if with_ties:
            cs = cst_ref[0:1, :]
            sel = (blk > thr) | ((blk == thr) & (pos_of(c, blk.shape) < cs))
        else:
            sel = blk >= thr
        return jnp.where(sel, 0.0, NEG)

    for with_ties in (False, True):
        @pl.when((any_tie > 0) == with_ties)
        def _():
            def body(c, carry):
                rows = chunk_rows(c)
                sc_ref[rows, :] = to_mask(sc_ref[rows, :], c, with_ties)
                return carry
            lax.fori_loop(0, nchunks, body, 0)
            scm_ref[...] = to_mask(scm_ref[...], -1, with_ties)

    kindm = jnp.minimum(j, 1)
    heads = [slice(h * HEAD_DIM, (h + 1) * HEAD_DIM) for h in range(N_HEADS)]

    mx0 = []
    for h in range(N_HEADS):
        lm = (jnp.dot(km_ref[h], qt_ref[h], preferred_element_type=F32)
              + tabm_ref[kindm, h] + scm_ref[...])
        lm_ref[h] = lm
        mx0.append(_fold_rows(lm, jnp.max))

    def pass_a(c, mx):
        rows = chunk_rows(c)
        kind = jnp.minimum(j - c, 2)
        mask = sc_ref[rows, :]
        out = []
        for h in range(N_HEADS):
            l = (jnp.dot(k_ref[h, rows, :], qt_ref[h], preferred_element_type=F32)
                 + tab_ref[kind, h] + mask)
            l_ref[h, rows, :] = l
            out.append(jnp.maximum(mx[h], _fold_rows(l, jnp.max)))
        return tuple(out)

    mx = paired_chunk_loop(pass_a, tuple(mx0))
    m = [jnp.max(mx[h], axis=0, keepdims=True) for h in range(N_HEADS)]

    den0 = []
    for h in range(N_HEADS):
        pm = jnp.exp2(lm_ref[h] - m[h])
        den0.append(_fold_rows(pm, jnp.sum))
        ot_ref[heads[h], :] = jnp.dot(vtm_ref[heads[h], :], pm.astype(BF16), preferred_element_type=F32)

    def pass_b(c, den):
        rows = chunk_rows(c)
        out = []
        for h in range(N_HEADS):
            p = jnp.exp2(l_ref[h, rows, :] - m[h])
            out.append(den[h] + _fold_rows(p, jnp.sum))
            ot_ref[heads[h], :] += jnp.dot(vt_ref[heads[h], rows], p.astype(BF16),
                                           preferred_element_type=F32)
        return tuple(out)

    den = paired_chunk_loop(pass_b, tuple(den0))
    for h in range(N_HEADS):
        ot_ref[heads[h], :] = ot_ref[heads[h], :] / jnp.sum(den[h], axis=0, keepdims=True)
    o_ref[...] = ot_ref[...].T.astype(BF16)


def _attention(q3, iq3, iwt, k3, ik, vt, km3, ikm, vtm, tab, tabm, batch, seq):
    n = batch * seq
    tiles = seq // Q_TILE
    full = lambda a: pl.BlockSpec(a.shape, lambda b, j: (0,) * a.ndim, pipeline_mode=pl.Buffered(1))
    return pl.pallas_call(
        _attn_kernel,
        grid=(batch, tiles),
        in_specs=[
            pl.BlockSpec((N_HEADS, Q_TILE, HEAD_DIM), lambda b, j: (0, b * tiles + j, 0)),
            pl.BlockSpec((IDX_HEADS, Q_TILE, IDX_DIM), lambda b, j: (0, b * tiles + j, 0)),
            pl.BlockSpec((IDX_HEADS, Q_TILE), lambda b, j: (0, b * tiles + j)),
            pl.BlockSpec((N_HEADS, seq, HEAD_DIM), lambda b, j: (0, b, 0)),
            pl.BlockSpec((seq, IDX_DIM), lambda b, j: (b, 0)),
            pl.BlockSpec((ATTN_WIDTH, seq), lambda b, j: (0, b)),
            full(km3), full(ikm), full(vtm), full(tab), full(tabm),
        ],
        out_specs=pl.BlockSpec((Q_TILE, ATTN_WIDTH), lambda b, j: (b * tiles + j, 0)),
        out_shape=jax.ShapeDtypeStruct((n, ATTN_WIDTH), BF16),
        scratch_shapes=[
            pltpu.VMEM((seq, Q_TILE), F32),
            pltpu.VMEM((N_META, Q_TILE), F32),
            pltpu.VMEM((N_HEADS, seq, Q_TILE), F32),
            pltpu.VMEM((N_HEADS, N_META, Q_TILE), F32),
            pltpu.VMEM((ATTN_WIDTH, Q_TILE), F32),
            pltpu.VMEM((8, Q_TILE), I32),
            pltpu.VMEM((seq, Q_TILE), jnp.int16),
            pltpu.VMEM((seq, Q_TILE), jnp.int16),
            pltpu.VMEM((N_META, Q_TILE), jnp.int16),
            pltpu.VMEM((N_META, Q_TILE), jnp.int16),
            pltpu.VMEM((N_HEADS, HEAD_DIM, Q_TILE), BF16),
            pltpu.VMEM((IDX_HEADS, IDX_DIM, Q_TILE), BF16),
        ],
        compiler_params=pltpu.CompilerParams(dimension_semantics=("arbitrary", "arbitrary"),
                                             vmem_limit_bytes=VMEM_LIMIT),
        name="attn",
    )(q3, iq3, iwt, k3, ik, vt, km3, ikm, vtm, tab, tabm)


def _mix_kernel(x_ref, attn_ref, y_ref, yprev_ref, ymeta_ref, gate_ref,
                cw_ref, cb_ref, lng_ref, lnb_ref, wco_ref, wao_ref, wout_ref, nf_ref, wrt_ref, br_ref,
                h2_ref, hn2_ref, eid_ref, rank_ref, gcol_ref, cnt_ref,
                win_ref, shift_ref, base_ref, *, tiles_per_seq):
    i = pl.program_id(0)

    @pl.when(i == 0)
    def _():
        base_ref[...] = jnp.zeros_like(base_ref)

    first = (i % tiles_per_seq) == 0
    win_ref[0:HALO, :] = jnp.where(first, ymeta_ref[...], yprev_ref[...])
    win_ref[HALO:, :] = y_ref[...]
    lead = HALO - (CONV_WIDTH - 1)
    for b in range(8):
        span = MIX_TILE + 8 * (len(range(b, CONV_WIDTH, 8)) - 1)
        shift_ref[b, 0:span, :] = win_ref[pl.ds(lead + b, span), :]
    for hh in range(MIX_TILE // ROW_TILE):
        _mix_rows(hh * ROW_TILE, x_ref, attn_ref, gate_ref, cw_ref, cb_ref, lng_ref, lnb_ref, wco_ref, wao_ref,
                  wout_ref, nf_ref, wrt_ref, br_ref, h2_ref, hn2_ref, eid_ref, rank_ref, gcol_ref, cnt_ref,
                  shift_ref, base_ref)


def _mix_rows(r_off, x_ref, attn_ref, gate_ref, cw_ref, cb_ref, lng_ref, lnb_ref, wco_ref, wao_ref,
              wout_ref, nf_ref, wrt_ref, br_ref, h2_ref, hn2_ref, eid_ref, rank_ref, gcol_ref, cnt_ref,
              shift_ref, base_ref):
    rows = slice(r_off, r_off + ROW_TILE)
    blocks = []
    for r0 in range(r_off, r_off + ROW_TILE, CONV_ROWS):
        acc = jnp.broadcast_to(cb_ref[...], (CONV_ROWS, CONV_CH))
        for b in range(8):
            for a, w in enumerate(range(b, CONV_WIDTH, 8)):
                acc = acc + cw_ref[w:w + 1, :] * shift_ref[b, r0 + 8 * a:r0 + 8 * a + CONV_ROWS, :]
        blocks.append(acc)
    yc = jnp.concatenate(blocks, axis=0)
    mu = jnp.mean(yc, axis=-1, keepdims=True)
    var = jnp.mean(jnp.square(yc - mu), axis=-1, keepdims=True)
    yn = (yc - mu) * lax.rsqrt(var + EPS) * lng_ref[...] + lnb_ref[...]
    ys = yn * _sigmoid(yn)
    y_b = jnp.dot(ys.astype(BF16), wco_ref[...], preferred_element_type=F32)

    y_a = jnp.dot(attn_ref[rows, :], wao_ref[...], preferred_element_type=F32)
    merged = gate_ref[rows, :D_MODEL] * y_a + gate_ref[rows, D_MODEL:] * y_b
    h2 = x_ref[rows, :] + jnp.dot(merged.astype(BF16), wout_ref[...], preferred_element_type=F32)
    h2_ref[rows, :] = h2
    ms = jnp.mean(h2 * h2, axis=-1, keepdims=True)
    hn2 = h2 * lax.rsqrt(ms + EPS) * nf_ref[...]
    hn2_ref[rows, 0, :] = _pack_bf16_pairs(hn2)

    logits = lax.dot_general(wrt_ref[...], hn2, NT_DIMS, preferred_element_type=F32,
                             precision=lax.Precision.HIGHEST) + br_ref[...]
    erow = lax.broadcasted_iota(I32, (N_EXPERTS, ROW_TILE), 0)
    vals, ids = [], []
    l = logits
    for _ in range(TOP_K_EXPERTS):
        m = jnp.max(l, axis=0, keepdims=True)
        idx = jnp.min(jnp.where(l == m, erow, N_EXPERTS), axis=0, keepdims=True)
        vals.append(m)
        ids.append(idx)
        l = jnp.where(erow == idx, -jnp.inf, l)
    ex = [jnp.exp(v - vals[0]) for v in vals]
    den = ex[0] + ex[1] + ex[2] + ex[3]
    gates = [e / den for e in ex]

    onehot = [(erow == idx) for idx in ids]
    oh = jnp.concatenate([jnp.where(o, 1.0, 0.0) for o in onehot], axis=0)
    tr = lax.broadcasted_iota(I32, (ROW_TILE, ROW_TILE), 0)
    tc = lax.broadcasted_iota(I32, (ROW_TILE, ROW_TILE), 1)
    upper = jnp.where(tr <= tc, 1.0, 0.0).astype(BF16)
    pref = jnp.dot(oh.astype(BF16), upper, preferred_element_type=F32)
    offs = base_ref[:, 0:1]
    ranks = []
    for kk in range(TOP_K_EXPERTS):
        pk = pref[kk * N_EXPERTS:(kk + 1) * N_EXPERTS, :]
        r = jnp.sum(jnp.where(onehot[kk], offs + pk - 1.0, 0.0), axis=0, keepdims=True)
        ranks.append(r.astype(I32))
        offs = offs + pk[:, ROW_TILE - 1:ROW_TILE]
    base_ref[...] = jnp.broadcast_to(offs, base_ref.shape)
    cnt_ref[...] = jnp.broadcast_to(offs, cnt_ref.shape)

    zi = jnp.zeros((8 - TOP_K_EXPERTS, ROW_TILE), I32)
    eid_ref[:, rows] = jnp.concatenate(ids + [zi], axis=0)
    rank_ref[:, rows] = jnp.concatenate(ranks + [zi], axis=0)
    g8 = jnp.concatenate(gates + [jnp.zeros((128 - TOP_K_EXPERTS, ROW_TILE), F32)], axis=0)
    gcol_ref[rows, :] = g8.T


def _mix(x2d, attn, y, ymeta, gate, wts, seq):
    n = x2d.shape[0]
    tiles_per_seq = seq // MIX_TILE
    halo_per_tile = MIX_TILE // HALO
    full = lambda a: pl.BlockSpec(a.shape, lambda i: (0,) * a.ndim)
    row = lambda w: pl.BlockSpec((MIX_TILE, w), lambda i: (i, 0))
    lane = lambda r: pl.BlockSpec((r, MIX_TILE), lambda i: (0, i))
    out_shape = (
        jax.ShapeDtypeStruct((n, D_MODEL), F32),
        jax.ShapeDtypeStruct((n, 1, D_MODEL // 2), I32),
        jax.ShapeDtypeStruct((8, n), I32),
        jax.ShapeDtypeStruct((8, n), I32),
        jax.ShapeDtypeStruct((n, 128), F32),
        jax.ShapeDtypeStruct((N_EXPERTS, 128), F32),
    )
    out_specs = (row(D_MODEL), pl.BlockSpec((MIX_TILE, 1, D_MODEL // 2), lambda i: (i, 0, 0)),
                 lane(8), lane(8), row(128), full(out_shape[5]))
    return pl.pallas_call(
        functools.partial(_mix_kernel, tiles_per_seq=tiles_per_seq),
        grid=(n // MIX_TILE,),
        in_specs=[row(D_MODEL), row(ATTN_WIDTH), row(CONV_CH),
                  pl.BlockSpec((HALO, CONV_CH), lambda i: (jnp.maximum(i * halo_per_tile - 1, 0), 0)),
                  full(ymeta), row(2 * D_MODEL)] + [full(w) for w in wts],
        out_specs=out_specs,
        out_shape=out_shape,
        scratch_shapes=[pltpu.VMEM((HALO + MIX_TILE, CONV_CH), F32),
                        pltpu.VMEM((8, HALO + MIX_TILE, CONV_CH), F32),
                        pltpu.VMEM((N_EXPERTS, 128), F32)],
        compiler_params=pltpu.CompilerParams(dimension_semantics=("arbitrary",),
                                             vmem_limit_bytes=VMEM_LIMIT),
        name="mix",
    )(x2d, attn, y, y, ymeta, gate, *wts)


def _slots_kernel(starts_ref, eid_ref, rank_ref, slot_ref):
    eid = eid_ref[...]
    base = jnp.zeros(eid.shape, I32)
    for e in range(N_EXPERTS):
        base = jnp.where(eid == e, starts_ref[e], base)
    slot = base + rank_ref[...]
    for t in range(slot_ref.shape[0]):
        for kk in range(TOP_K_EXPERTS):
            slot_ref[t, :, kk * ROW_TILE:(kk + 1) * ROW_TILE] = slot[kk:kk + 1, t * ROW_TILE:(t + 1) * ROW_TILE]


def _slots(starts, eid8, rank8):
    n = eid8.shape[1]
    tiles = 8
    spec = pl.BlockSpec((8, tiles * ROW_TILE), lambda i: (0, i))
    return pl.pallas_call(
        _slots_kernel,
        grid=(n // (tiles * ROW_TILE),),
        in_specs=[pl.BlockSpec(memory_space=pltpu.SMEM), spec, spec],
        out_specs=pl.BlockSpec((tiles, 1, COPIES_PER_TILE), lambda i: (i, 0, 0)),
        out_shape=jax.ShapeDtypeStruct((n // ROW_TILE, 1, COPIES_PER_TILE), I32),
        compiler_params=pltpu.CompilerParams(dimension_semantics=("arbitrary",)),
        name="slots",
    )(starts, eid8, rank8)


COPIES_PER_TILE = TOP_K_EXPERTS * ROW_TILE
SLOT_UNROLL = 16


def _slot_source_kernel(slot_ref, init_ref, src_ref, sem, *, n):
    i = pl.program_id(0)

    @pl.when(i == 0)
    def _():
        cp = pltpu.make_async_copy(init_ref, src_ref, sem)
        cp.start()
        cp.wait()

    for kk in range(TOP_K_EXPERTS):
        def body(g, carry):
            for u in range(SLOT_UNROLL):
                t = g * SLOT_UNROLL + u
                src_ref[slot_ref[0, 0, kk * ROW_TILE + t]] = kk * n + i * ROW_TILE + t
            return carry
        lax.fori_loop(0, ROW_TILE // SLOT_UNROLL, body, 0)


def _slot_sources(slot_tiles, init, n):
    return pl.pallas_call(
        functools.partial(_slot_source_kernel, n=n),
        grid=(slot_tiles.shape[0],),
        in_specs=[pl.BlockSpec((1, 1, COPIES_PER_TILE), lambda i: (i, 0, 0), memory_space=pltpu.SMEM),
                  pl.BlockSpec(memory_space=pl.ANY)],
        out_specs=pl.BlockSpec(memory_space=pltpu.SMEM),
        out_shape=jax.ShapeDtypeStruct(init.shape, I32),
        scratch_shapes=[pltpu.SemaphoreType.DMA(())],
        compiler_params=pltpu.CompilerParams(dimension_semantics=("arbitrary",)),
        name="slot_sources",
    )(slot_tiles, init)


def _expert_kernel(te_ref, nused_ref, src_ref, src_next_ref, src_prev_ref, hn_hbm,
                   wug_ref, bug_ref, wd_ref, bd_ref, yk_hbm, xbuf, ybuf, wug_bf, wd_bf, gsem, ssem, *, n):
    i = pl.program_id(0)
    nused = nused_ref[0]

    def gather_copy(src, r, b):
        tok = src[0, 0, r] & (n - 1)
        return pltpu.make_async_copy(hn_hbm.at[tok], xbuf.at[b, r], gsem.at[b])

    def scatter_copy(src, r, b):
        return pltpu.make_async_copy(ybuf.at[b, r], yk_hbm.at[src[0, 0, r]], ssem.at[b])

    @pl.when(i <= nused)
    def _():
        cur = i % 2
        oth = 1 - cur

        def wait_scatter(b):
            pltpu.make_async_copy(ybuf.at[b], yk_hbm.at[pl.ds(0, ROW_TILE)], ssem.at[b]).wait()

        @pl.when(i == 0)
        def _():
            for r in range(ROW_TILE):
                gather_copy(src_ref, r, 0).start()
            ybuf[1] = jnp.zeros(ybuf.shape[1:], F32)
            spare = pltpu.make_async_copy(ybuf.at[1], yk_hbm.at[pl.ds(TOP_K_EXPERTS * n, ROW_TILE)],
                                          ssem.at[1])
            spare.start()
            spare.wait()

        @pl.when(i < nused)
        def _():
            for r in range(ROW_TILE):
                gather_copy(src_next_ref, r, oth).start()

        @pl.when(i > 0)
        def _():
            for r in range(ROW_TILE):
                scatter_copy(src_prev_ref, r, oth).start()

        pltpu.make_async_copy(hn_hbm.at[pl.ds(0, ROW_TILE)], xbuf.at[cur], gsem.at[cur]).wait()

        prev = te_ref[jnp.maximum(i - 1, 0)]
        fresh = (i == 0) | (te_ref[jnp.minimum(i, nused - 1)] != prev)

        @pl.when(fresh)
        def _():
            wug_bf[...] = wug_ref[0].astype(BF16)
            wd_bf[...] = wd_ref[0].astype(BF16)

        xb = _unpack_bf16_pairs(xbuf[cur, :, 0, :])
        ug = jnp.dot(xb, wug_bf[...], preferred_element_type=F32) + bug_ref[0]
        gate = jnp.minimum(ug[:, :D_FF], SWIGLU_LIMIT)
        up = jnp.clip(ug[:, D_FF:], -SWIGLU_LIMIT, SWIGLU_LIMIT)
        act = (up + 1.0) * gate * _sigmoid(SWIGLU_ALPHA * gate)
        y = jnp.dot(act.astype(BF16), wd_bf[...], preferred_element_type=F32) + bd_ref[0]

        @pl.when(i > 1)
        def _():
            wait_scatter(cur)

        ybuf[cur, :, 0, :] = y

        @pl.when((i == nused) & (i > 0))
        def _():
            wait_scatter(oth)


def _experts(tile_expert, nused, src_tiles, hn2p, w_ug, b_ug, w_down, b_down, n):
    ntiles = src_tiles.shape[0]
    clamp = lambda i, nu: jnp.maximum(jnp.minimum(i, nu[0] - 1), 0)
    smem = lambda f: pl.BlockSpec((1, 1, ROW_TILE), f, memory_space=pltpu.SMEM)
    expert = lambda i, te, nu: (te[clamp(i, nu)], 0, 0)
    grid_spec = pltpu.PrefetchScalarGridSpec(
        num_scalar_prefetch=2,
        grid=(ntiles + 1,),
        in_specs=[
            smem(lambda i, te, nu: (clamp(i, nu), 0, 0)),
            smem(lambda i, te, nu: (clamp(i + 1, nu), 0, 0)),
            smem(lambda i, te, nu: (clamp(i - 1, nu), 0, 0)),
            pl.BlockSpec(memory_space=pl.ANY),
            pl.BlockSpec((1, D_MODEL, 2 * D_FF), expert),
            pl.BlockSpec((1, 1, 2 * D_FF), expert),
            pl.BlockSpec((1, D_FF, D_MODEL), expert),
            pl.BlockSpec((1, 1, D_MODEL), expert),
        ],
        out_specs=pl.BlockSpec(memory_space=pl.ANY),
        scratch_shapes=[pltpu.VMEM((2, ROW_TILE, 1, D_MODEL // 2), I32),
                        pltpu.VMEM((2, ROW_TILE, 1, D_MODEL), F32),
                        pltpu.VMEM((D_MODEL, 2 * D_FF), BF16), pltpu.VMEM((D_FF, D_MODEL), BF16),
                        pltpu.SemaphoreType.DMA((2,)), pltpu.SemaphoreType.DMA((2,))],
    )
    return pl.pallas_call(
        functools.partial(_expert_kernel, n=n),
        grid_spec=grid_spec,
        out_shape=jax.ShapeDtypeStruct((TOP_K_EXPERTS * n + ROW_TILE, 1, D_MODEL), F32),
        compiler_params=pltpu.CompilerParams(dimension_semantics=("arbitrary",),
                                             vmem_limit_bytes=VMEM_LIMIT),
        name="experts",
    )(tile_expert, nused, src_tiles, src_tiles, src_tiles, hn2p,
      w_ug, b_ug[:, None, :], w_down, b_down[:, None, :])


def _final_kernel(h2_ref, gcol_ref, nf_ref, *rest):
    y_refs, o_ref = rest[:TOP_K_EXPERTS], rest[TOP_K_EXPERTS]
    h = h2_ref[...]
    for kk in range(TOP_K_EXPERTS):
        h = h + gcol_ref[:, kk:kk + 1] * y_refs[kk][:, 0, :]
    ms = jnp.mean(h * h, axis=-1, keepdims=True)
    o_ref[...] = h * lax.rsqrt(ms + EPS) * nf_ref[...]


def _final(h2, gcol, nf, yk):
    n = h2.shape[0]
    ntiles = n // ROW_TILE
    choice = lambda kk: pl.BlockSpec((ROW_TILE, 1, D_MODEL), lambda i: (kk * ntiles + i, 0, 0))
    return pl.pallas_call(
        _final_kernel,
        grid=(ntiles,),
        in_specs=[pl.BlockSpec((ROW_TILE, D_MODEL), lambda i: (i, 0)),
                  pl.BlockSpec((ROW_TILE, 128), lambda i: (i, 0)),
                  pl.BlockSpec((1, D_MODEL), lambda i: (0, 0))] + [choice(kk) for kk in range(TOP_K_EXPERTS)],
        out_specs=pl.BlockSpec((ROW_TILE, D_MODEL), lambda i: (i, 0)),
        out_shape=jax.ShapeDtypeStruct((n, D_MODEL), F32),
        compiler_params=pltpu.CompilerParams(dimension_semantics=("arbitrary",),
                                             vmem_limit_bytes=VMEM_LIMIT),
        name="final",
    )(h2, gcol, nf, *([yk] * TOP_K_EXPERTS))


def _split_w_in(w_in, b_gate):
    c = ATTN_WIDTH
    o = 0
    wq = w_in[:, o:o + c]; o += c
    wk = w_in[:, o:o + c]; o += c
    wv = w_in[:, o:o + c]; o += c
    wiq = w_in[:, o:o + IDX_HEADS * IDX_DIM]; o += IDX_HEADS * IDX_DIM
    wik = w_in[:, o:o + IDX_DIM]; o += IDX_DIM
    wiw = w_in[:, o:o + IDX_HEADS]; o += IDX_HEADS
    wglu = w_in[:, o:o + 2 * CONV_CH]; o += 2 * CONV_CH
    wgate = w_in[:, o:]
    wiwt = jnp.concatenate([wiw.T, jnp.zeros((16 - IDX_HEADS, D_MODEL), w_in.dtype)], axis=0)
    bf = lambda a: a.astype(BF16)
    return (bf(wq), bf(wk), bf(wiq), bf(wik), bf(wv.T), bf(wiwt), bf(wglu), bf(wgate),
            b_gate[None, :].astype(F32))


def kernel(x, meta_tokens, rel_bias, norm_mix, w_in, b_gate, w_attn_out, conv_w, conv_b, conv_ln_g, conv_ln_b, w_conv_out, w_out, norm_ffn, w_router, b_router, w_up_gate, b_up_gate, w_down, b_down, norm_final):
    batch, seq, d = x.shape
    n = batch * seq
    x2d = x.reshape(n, d)

    wts = _split_w_in(w_in[0], b_gate[0])
    g_mix = norm_mix[0][None, :]
    q3, k3, iq3, ik, vt, iwt, y, gate = _project(x2d, g_mix, wts, PROJ_TILE)
    _, km3, _, ikm, vtm, _, ym, _ = _project(meta_tokens.astype(F32), g_mix, wts, N_META)

    tab, tabm = _bias_tables(rel_bias.astype(F32))
    attn = _attention(q3, iq3, iwt, k3, ik, vt, km3, ikm, vtm, tab, tabm, batch, seq)

    ymeta = jnp.concatenate([jnp.zeros((HALO - N_META, CONV_CH), F32), ym], axis=0)
    cw = jnp.concatenate([conv_w[0], jnp.zeros((32 - CONV_WIDTH, CONV_CH), F32)], axis=0)
    mix_w = (cw, conv_b[0][None, :], conv_ln_g[0][None, :], conv_ln_b[0][None, :],
             w_conv_out[0].astype(BF16), w_attn_out[0].astype(BF16), w_out[0].astype(BF16),
             norm_ffn[0][None, :], w_router[0].T, b_router[0][:, None])
    h2, hn2, eid8, rank8, gcol, cnt = _mix(x2d, attn, y, ymeta, gate, mix_w, seq)

    counts = cnt[:, 0].astype(I32)
    padded = ((counts + ROW_TILE - 1) // ROW_TILE) * ROW_TILE
    ends = jnp.cumsum(padded)
    starts = ends - padded
    slot_tiles = _slots(starts, eid8, rank8)
    nslots = n * TOP_K_EXPERTS + N_EXPERTS * ROW_TILE
    ntiles = nslots // ROW_TILE
    tile_start = jnp.arange(ntiles, dtype=I32) * ROW_TILE
    nused = (ends[-1] // ROW_TILE).astype(I32)
    last_start = jnp.maximum(ends[-1] - ROW_TILE, 0)
    tile_expert = jnp.sum((jnp.minimum(tile_start, last_start)[:, None] >= ends[None, :]).astype(I32), axis=1)

    spare = TOP_K_EXPERTS * n + jnp.arange(nslots, dtype=I32) % ROW_TILE
    src_tiles = _slot_sources(slot_tiles, spare, n).reshape(ntiles, 1, ROW_TILE)
    yk = _experts(tile_expert, nused[None], src_tiles, hn2, w_up_gate[0], b_up_gate[0], w_down[0], b_down[0], n)
    out = _final(h2, gcol, norm_final[None, :], yk)
    return out.reshape(batch, seq, d)
```

```python
import functools
import math

import jax
import jax.numpy as jnp
from jax import lax
from jax.experimental import pallas as pl
from jax.experimental.pallas import tpu as pltpu

F32 = jnp.float32
BF16 = jnp.bfloat16
I32 = jnp.int32

D_MODEL = 1024
N_META = 16
N_HEADS = 8
HEAD_DIM = 64
ATTN_WIDTH = N_HEADS * HEAD_DIM
IDX_HEADS = 8
IDX_DIM = 64
TOPK_MAX = 256
CONV_CH = 512
CONV_WIDTH = 31
N_BUCKETS = 32
MAX_DISTANCE = 128
N_EXPERTS = 32
TOP_K_EXPERTS = 4
D_FF = 1024
SWIGLU_LIMIT = 7.0
SWIGLU_ALPHA = 1.702
EPS = 1e-6
IDX_SCALE = (IDX_DIM ** -0.5) * (IDX_HEADS ** -0.5)

ROW_TILE = 256
MIX_TILE = 512
PROJ_TILE = 512
Q_TILE = 256
K_CHUNK = 256
COUNT_CHUNKS = 2
HALO = 32
CONV_ROWS = 32
NEG = -1e30
LOG2E = math.log2(math.e)
VMEM_LIMIT = 56 * 1024 * 1024

NT_DIMS = (((1,), (1,)), ((), ()))


def _sigmoid(x):
    return 1.0 / (1.0 + jnp.exp(-x))


def _pack_bf16_pairs(x):
    w = x.shape[1] // 2
    hi = lax.bitcast_convert_type(x[:, :w].astype(BF16).astype(F32), jnp.uint32)
    lo = lax.bitcast_convert_type(x[:, w:].astype(BF16).astype(F32), jnp.uint32)
    return lax.bitcast_convert_type(hi | (lo >> 16), I32)


def _unpack_bf16_pairs(p):
    u = lax.bitcast_convert_type(p, jnp.uint32)
    hi = lax.bitcast_convert_type(u & jnp.uint32(0xFFFF0000), F32)
    lo = lax.bitcast_convert_type(u << 16, F32)
    return jnp.concatenate([hi, lo], axis=1).astype(BF16)


def _proj_kernel(x_ref, g_ref, wq_ref, wk_ref, wiq_ref, wik_ref, wvt_ref, wiwt_ref, wglu_ref,
                 wgate_ref, bgate_ref,
                 q_ref, k_ref, iq_ref, ik_ref, vt_ref, iwt_ref, y_ref, gate_ref):
    x = x_ref[...]
    ms = jnp.mean(x * x, axis=-1, keepdims=True)
    xn = (x * lax.rsqrt(ms + EPS) * g_ref[...]).astype(BF16)

    q = jnp.dot(xn, wq_ref[...], preferred_element_type=F32) * (HEAD_DIM ** -0.5 * LOG2E)
    k = jnp.dot(xn, wk_ref[...], preferred_element_type=F32)
    iq = jnp.dot(xn, wiq_ref[...], preferred_element_type=F32)
    for h in range(N_HEADS):
        sl = slice(h * HEAD_DIM, (h + 1) * HEAD_DIM)
        q_ref[h] = q[:, sl].astype(BF16)
        k_ref[h] = k[:, sl].astype(BF16)
        iq_ref[h] = iq[:, sl].astype(BF16)
    ik_ref[...] = jnp.dot(xn, wik_ref[...], preferred_element_type=F32).astype(BF16)
    vt_ref[...] = lax.dot_general(wvt_ref[...], xn, NT_DIMS, preferred_element_type=F32).astype(BF16)
    iwt = lax.dot_general(wiwt_ref[...], xn, NT_DIMS, preferred_element_type=F32)
    iwt_ref[...] = iwt[:IDX_HEADS] * IDX_SCALE
    glu = jnp.dot(xn, wglu_ref[...], preferred_element_type=F32)
    y_ref[...] = glu[:, :CONV_CH] * _sigmoid(glu[:, CONV_CH:])
    gate = jnp.dot(xn, wgate_ref[...], preferred_element_type=F32) + bgate_ref[...]
    gate_ref[...] = _sigmoid(gate)


def _project(x2d, g, wts, tm):
    n = x2d.shape[0]
    wq, wk, wiq, wik, wvt, wiwt, wglu, wgate, bgate = wts
    full = lambda a: pl.BlockSpec(a.shape, lambda i: (0,) * a.ndim, pipeline_mode=pl.Buffered(1))
    out_shape = (
        jax.ShapeDtypeStruct((N_HEADS, n, HEAD_DIM), BF16),
        jax.ShapeDtypeStruct((N_HEADS, n, HEAD_DIM), BF16),
        jax.ShapeDtypeStruct((IDX_HEADS, n, IDX_DIM), BF16),
        jax.ShapeDtypeStruct((n, IDX_DIM), BF16),
        jax.ShapeDtypeStruct((ATTN_WIDTH, n), BF16),
        jax.ShapeDtypeStruct((IDX_HEADS, n), F32),
        jax.ShapeDtypeStruct((n, CONV_CH), F32),
        jax.ShapeDtypeStruct((n, 2 * D_MODEL), F32),
    )
    out_specs = (
        pl.BlockSpec((N_HEADS, tm, HEAD_DIM), lambda i: (0, i, 0)),
        pl.BlockSpec((N_HEADS, tm, HEAD_DIM), lambda i: (0, i, 0)),
        pl.BlockSpec((IDX_HEADS, tm, IDX_DIM), lambda i: (0, i, 0)),
        pl.BlockSpec((tm, IDX_DIM), lambda i: (i, 0)),
        pl.BlockSpec((ATTN_WIDTH, tm), lambda i: (0, i)),
        pl.BlockSpec((IDX_HEADS, tm), lambda i: (0, i)),
        pl.BlockSpec((tm, CONV_CH), lambda i: (i, 0)),
        pl.BlockSpec((tm, 2 * D_MODEL), lambda i: (i, 0)),
    )
    return pl.pallas_call(
        _proj_kernel,
        grid=(n // tm,),
        in_specs=[pl.BlockSpec((tm, D_MODEL), lambda i: (i, 0)), full(g), full(wq), full(wk), full(wiq),
                  full(wik), full(wvt), full(wiwt), full(wglu), full(wgate), full(bgate)],
        out_specs=out_specs,
        out_shape=out_shape,
        compiler_params=pltpu.CompilerParams(dimension_semantics=("arbitrary",),
                                             vmem_limit_bytes=VMEM_LIMIT),
        name="proj",
    )(x2d, g, wq, wk, wiq, wik, wvt, wiwt, wglu, wgate, bgate)


def _t5_bucket(n):
    max_exact = N_BUCKETS // 2
    nf = jnp.maximum(n, 1).astype(F32)
    large = max_exact + (jnp.log(nf / max_exact) / math.log(MAX_DISTANCE / max_exact)
                         * (N_BUCKETS - max_exact)).astype(I32)
    large = jnp.minimum(large, N_BUCKETS - 1)
    return jnp.where(n < max_exact, n, large)


def _bias_lookup(rb_ref, dist, h):
    bucket = _t5_bucket(jnp.maximum(dist, 0))
    out = jnp.full(dist.shape, NEG, F32)
    for b in range(N_BUCKETS):
        out = jnp.where(bucket == b, rb_ref[b, h] * LOG2E, out)
    return jnp.where(dist >= 0, out, NEG)


def _bias_kernel(rb_ref, tab_ref, tabm_ref):
    kind = pl.program_id(0)
    r = pl.program_id(1)
    rows = tab_ref.shape[2]
    s = lax.broadcasted_iota(I32, (rows, Q_TILE), 0) + r * rows
    t = lax.broadcasted_iota(I32, (rows, Q_TILE), 1)
    dist = jnp.where(kind == 2, 2 * K_CHUNK, t - s + kind * K_CHUNK)
    for h in range(N_HEADS):
        tab_ref[0, h] = _bias_lookup(rb_ref, dist, h)
    m = lax.broadcasted_iota(I32, (N_META, Q_TILE), 0)
    tm_ = lax.broadcasted_iota(I32, (N_META, Q_TILE), 1)
    distm = jnp.where(kind == 0, N_META + tm_ - m, 2 * K_CHUNK)
    for h in range(N_HEADS):
        tabm_ref[0, h] = _bias_lookup(rb_ref, distm, h)


def _bias_tables(rel_bias):
    rows = 64
    return pl.pallas_call(
        _bias_kernel,
        grid=(3, K_CHUNK // rows),
        in_specs=[pl.BlockSpec(memory_space=pltpu.SMEM)],
        out_specs=(pl.BlockSpec((1, N_HEADS, rows, Q_TILE), lambda kd, r: (kd, 0, r, 0)),
                   pl.BlockSpec((1, N_HEADS, N_META, Q_TILE), lambda kd, r: (kd, 0, 0, 0))),
        out_shape=(jax.ShapeDtypeStruct((3, N_HEADS, K_CHUNK, Q_TILE), F32),
                   jax.ShapeDtypeStruct((3, N_HEADS, N_META, Q_TILE), F32)),
        compiler_params=pltpu.CompilerParams(dimension_semantics=("arbitrary", "arbitrary")),
        name="bias_tables",
    )(rel_bias)


def _fold_rows(x, op):
    r, l = x.shape
    x3 = x.reshape(r // 8, 8, l)
    return op(x3, axis=0)


def _attn_kernel(q_ref, iq_ref, iwt_ref, k_ref, ik_ref, vt_ref, km_ref, ikm_ref, vtm_ref,
                 tab_ref, tabm_ref, o_ref, sc_ref, scm_ref, l_ref, lm_ref, ot_ref, cst_ref,
                 hi_ref, lo_ref, him_ref, lom_ref, qt_ref, iqt_ref):
    j = pl.program_id(1)
    nchunks = j + 1
    iw = iwt_ref[...]

    def chunk_rows(c):
        return pl.ds(pl.multiple_of(c * K_CHUNK, K_CHUNK), K_CHUNK)

    def paired_chunk_loop(body, init):
        def quad(i, c):
            for u in range(4):
                c = body(4 * i + u, c)
            return c

        carry = lax.fori_loop(0, lax.shift_right_logical(nchunks, 2), quad, init)
        base = nchunks & ~3
        carry = lax.cond((nchunks & 2) == 2, lambda c: body(base + 1, body(base, c)), lambda c: c, carry)
        return lax.cond((nchunks & 1) == 1, lambda c: body(nchunks - 1, c), lambda c: c, carry)

    for h in range(N_HEADS):
        qt_ref[h] = q_ref[h].astype(F32).T.astype(BF16)
        iqt_ref[h] = iq_ref[h].astype(F32).T.astype(BF16)

    def idx_scores(ikc):
        acc = None
        for h in range(IDX_HEADS):
            s = jnp.dot(ikc, iqt_ref[h], preferred_element_type=F32)
            term = jnp.maximum(s, 0.0) * iw[h:h + 1, :]
            acc = term if acc is None else acc + term
        return acc

    scm_ref[...] = idx_scores(ikm_ref[...])

    row_minus_col = (lax.broadcasted_iota(I32, (K_CHUNK, Q_TILE), 0)
                     - lax.broadcasted_iota(I32, (K_CHUNK, Q_TILE), 1))

    def p1(c, carry):
        rows = chunk_rows(c)
        future = row_minus_col > jnp.where(c == j, 0, K_CHUNK)
        sc_ref[rows, :] = jnp.where(future, -jnp.inf, idx_scores(ik_ref[rows, :]))
        return carry

    paired_chunk_loop(p1, 0)

    def count(pred):
        def body(c, acc):
            blk = sc_ref[chunk_rows(c), :]
            return acc + _fold_rows(jnp.where(pred(blk, c), 1, 0).astype(I32), jnp.sum)
        acc = lax.fori_loop(0, nchunks, body, jnp.zeros((8, Q_TILE), I32))
        acc = acc + _fold_rows(jnp.where(pred(scm_ref[...], -1), 1, 0).astype(I32), jnp.sum)
        return jnp.sum(acc, axis=0, keepdims=True)

    I16 = jnp.int16
    LOW = -2 ** 15

    def split_key(s):
        bits = lax.bitcast_convert_type(s, I32)
        key = jnp.where(bits < 0, bits ^ jnp.int32(0x7FFFFFFF), bits)
        hi = lax.shift_right_arithmetic(key, 16).astype(I16)
        lo = ((key & 0xFFFF) + LOW).astype(I16)
        return hi, lo

    him_ref[...], lom_ref[...] = split_key(scm_ref[...])

    def p2(c, carry):
        rows = chunk_rows(c)
        hi_ref[rows, :], lo_ref[rows, :] = split_key(sc_ref[rows, :])
        return carry

    lax.fori_loop(0, nchunks, p2, 0)

    count_trips = lax.shift_right_logical(nchunks + COUNT_CHUNKS - 1, COUNT_CHUNKS.bit_length() - 1)

    def p2pad(c, carry):
        rows = chunk_rows(c)
        hi_ref[rows, :] = jnp.full((K_CHUNK, Q_TILE), LOW, I16)
        lo_ref[rows, :] = jnp.full((K_CHUNK, Q_TILE), LOW, I16)
        return carry

    lax.fori_loop(nchunks, count_trips * COUNT_CHUNKS, p2pad, 0)

    def tree_sum16(ind):
        parts = [ind[a * 16:(a + 1) * 16, :] for a in range(ind.shape[0] // 16)]
        while len(parts) > 1:
            parts = [parts[a] + parts[a + 1] for a in range(0, len(parts), 2)]
        return parts[0]

    def count16(ref, mref, pred):
        one, zero = jnp.ones((), BF16), jnp.zeros((), BF16)

        def body(cc, acc):
            span = COUNT_CHUNKS * K_CHUNK
            rows = pl.ds(pl.multiple_of(cc * span, span), span)
            return acc + tree_sum16(jnp.where(pred(ref[rows, :]), one, zero)).astype(F32)
        acc = lax.fori_loop(0, count_trips, body, jnp.zeros((16, Q_TILE), F32))
        acc = acc + jnp.where(pred(mref[...]), one, zero).astype(F32)
        return jnp.sum(acc, axis=0, keepdims=True).astype(I32)

    def to16(u):
        return (u + LOW).astype(I16)

    def search_hi(i, u):
        cand = u | lax.shift_left(jnp.int32(1), 15 - i)
        c16 = to16(cand)
        cnt = count16(hi_ref, him_ref, lambda blk: blk >= c16)
        return jnp.where(cnt >= TOPK_MAX, cand, u)

    u_hi = lax.fori_loop(0, 16, search_hi, jnp.zeros((1, Q_TILE), I32))
    t16 = to16(u_hi)
    need = TOPK_MAX - count16(hi_ref, him_ref, lambda blk: blk > t16)

    lom_ref[...] = jnp.where(him_ref[...] == t16, lom_ref[...], jnp.int16(LOW))

    def p2b(c, carry):
        rows = chunk_rows(c)
        lo_ref[rows, :] = jnp.where(hi_ref[rows, :] == t16, lo_ref[rows, :], jnp.int16(LOW))
        return carry

    lax.fori_loop(0, nchunks, p2b, 0)

    def search_lo(i, v):
        cand = v | lax.shift_left(jnp.int32(1), 15 - i)
        c16 = to16(cand)
        cnt = count16(lo_ref, lom_ref, lambda blk: blk >= c16)
        return jnp.where(cnt >= need, cand, v)

    u_lo = lax.fori_loop(0, 16, search_lo, jnp.zeros((1, Q_TILE), I32))
    thr_key = lax.shift_left(u_hi + LOW, 16) | u_lo
    thr_bits = jnp.where(thr_key < 0, thr_key ^ jnp.int32(0x7FFFFFFF), thr_key)
    thr = lax.bitcast_convert_type(thr_bits, F32)

    cnt_ge = count(lambda blk, c: blk >= thr)
    tie = (cnt_ge > TOPK_MAX) & (thr > -jnp.inf)
    any_tie = jnp.max(tie.astype(I32))
    cst_ref[...] = jnp.full((8, Q_TILE), 2 ** 30, I32)

    def pos_of(c, shape):
        r = lax.broadcasted_iota(I32, shape, 0)
        return jnp.where(c < 0, r, r + N_META + c * K_CHUNK)

    @pl.when(any_tie > 0)
    def _():
        cnt_gt = count(lambda blk, c: blk > thr)
        need = TOPK_MAX - cnt_gt

        def bis_pos(i, cs):
            cand = cs | lax.shift_left(jnp.int32(1), 11 - i)
            f = count(lambda blk, c: (blk == thr) & (pos_of(c, blk.shape) < cand))
            return jnp.where(f <= need, cand, cs)

        cs = lax.fori_loop(0, 12, bis_pos, jnp.zeros((1, Q_TILE), I32))
        cs = jnp.where(tie, cs, 2 ** 30)
        cst_ref[...] = jnp.broadcast_to(cs, (8, Q_TILE))

    def to_mask(blk, c, with_ties):
        if with_ties:
            cs = cst_ref[0:1, :]
            sel = (blk > thr) | ((blk == thr) & (pos_of(c, blk.shape) < cs))
        else:
            sel = blk >= thr
        return jnp.where(sel, 0.0, NEG)

    for with_ties in (False, True):
        @pl.when((any_tie > 0) == with_ties)
        def _():
            def body(c, carry):
                rows = chunk_rows(c)
                sc_ref[rows, :] = to_mask(sc_ref[rows, :], c, with_ties)
                return carry
            lax.fori_loop(0, nchunks, body, 0)
            scm_ref[...] = to_mask(scm_ref[...], -1, with_ties)

    kindm = jnp.minimum(j, 1)
    heads = [slice(h * HEAD_DIM, (h + 1) * HEAD_DIM) for h in range(N_HEADS)]

    mx0 = []
    for h in range(N_HEADS):
        lm = (jnp.dot(km_ref[h], qt_ref[h], preferred_element_type=F32)
              + tabm_ref[kindm, h] + scm_ref[...])
        lm_ref[h] = lm
        mx0.append(_fold_rows(lm, jnp.max))

    def pass_a(c, mx):
        rows = chunk_rows(c)
        kind = jnp.minimum(j - c, 2)
        mask = sc_ref[rows, :]
        out = []
        for h in range(N_HEADS):
            l = (jnp.dot(k_ref[h, rows, :], qt_ref[h], preferred_element_type=F32)
                 + tab_ref[kind, h] + mask)
            l_ref[h, rows, :] = l
            out.append(jnp.maximum(mx[h], _fold_rows(l, jnp.max)))
        return tuple(out)

    mx = paired_chunk_loop(pass_a, tuple(mx0))
    m = [jnp.max(mx[h], axis=0, keepdims=True) for h in range(N_HEADS)]

    den0 = []
    for h in range(N_HEADS):
        pm = jnp.exp2(lm_ref[h] - m[h])
        den0.append(_fold_rows(pm, jnp.sum))
        ot_ref[heads[h], :] = jnp.dot(vtm_ref[heads[h], :], pm.astype(BF16), preferred_element_type=F32)

    def pass_b(c, den):
        rows = chunk_rows(c)
        out = []
        for h in range(N_HEADS):
            p = jnp.exp2(l_ref[h, rows, :] - m[h])
            out.append(den[h] + _fold_rows(p, jnp.sum))
            ot_ref[heads[h], :] += jnp.dot(vt_ref[heads[h], rows], p.astype(BF16),
                                           preferred_element_type=F32)
        return tuple(out)

    den = paired_chunk_loop(pass_b, tuple(den0))
    for h in range(N_HEADS):
        ot_ref[heads[h], :] = ot_ref[heads[h], :] / jnp.sum(den[h], axis=0, keepdims=True)
    o_ref[...] = ot_ref[...].T.astype(BF16)


def _attention(q3, iq3, iwt, k3, ik, vt, km3, ikm, vtm, tab, tabm, batch, seq):
    n = batch * seq
    tiles = seq // Q_TILE
    full = lambda a: pl.BlockSpec(a.shape, lambda b, j: (0,) * a.ndim, pipeline_mode=pl.Buffered(1))
    return pl.pallas_call(
        _attn_kernel,
        grid=(batch, tiles),
        in_specs=[
            pl.BlockSpec((N_HEADS, Q_TILE, HEAD_DIM), lambda b, j: (0, b * tiles + j, 0)),
            pl.BlockSpec((IDX_HEADS, Q_TILE, IDX_DIM), lambda b, j: (0, b * tiles + j, 0)),
            pl.BlockSpec((IDX_HEADS, Q_TILE), lambda b, j: (0, b * tiles + j)),
            pl.BlockSpec((N_HEADS, seq, HEAD_DIM), lambda b, j: (0, b, 0)),
            pl.BlockSpec((seq, IDX_DIM), lambda b, j: (b, 0)),
            pl.BlockSpec((ATTN_WIDTH, seq), lambda b, j: (0, b)),
            full(km3), full(ikm), full(vtm), full(tab), full(tabm),
        ],
        out_specs=pl.BlockSpec((Q_TILE, ATTN_WIDTH), lambda b, j: (b * tiles + j, 0)),
        out_shape=jax.ShapeDtypeStruct((n, ATTN_WIDTH), BF16),
        scratch_shapes=[
            pltpu.VMEM((seq, Q_TILE), F32),
            pltpu.VMEM((N_META, Q_TILE), F32),
            pltpu.VMEM((N_HEADS, seq, Q_TILE), F32),
            pltpu.VMEM((N_HEADS, N_META, Q_TILE), F32),
            pltpu.VMEM((ATTN_WIDTH, Q_TILE), F32),
            pltpu.VMEM((8, Q_TILE), I32),
            pltpu.VMEM((seq, Q_TILE), jnp.int16),
            pltpu.VMEM((seq, Q_TILE), jnp.int16),
            pltpu.VMEM((N_META, Q_TILE), jnp.int16),
            pltpu.VMEM((N_META, Q_TILE), jnp.int16),
            pltpu.VMEM((N_HEADS, HEAD_DIM, Q_TILE), BF16),
            pltpu.VMEM((IDX_HEADS, IDX_DIM, Q_TILE), BF16),
        ],
        compiler_params=pltpu.CompilerParams(dimension_semantics=("arbitrary", "arbitrary"),
                                             vmem_limit_bytes=VMEM_LIMIT),
        name="attn",
    )(q3, iq3, iwt, k3, ik, vt, km3, ikm, vtm, tab, tabm)


def _mix_kernel(x_ref, attn_ref, y_ref, yprev_ref, ymeta_ref, gate_ref,
                cw_ref, cb_ref, lng_ref, lnb_ref, wco_ref, wao_ref, wout_ref, nf_ref, wrt_ref, br_ref,
                h2_ref, hn2_ref, eid_ref, rank_ref, gcol_ref, cnt_ref,
                win_ref, shift_ref, base_ref, *, tiles_per_seq):
    i = pl.program_id(0)

    @pl.when(i == 0)
    def _():
        base_ref[...] = jnp.zeros_like(base_ref)

    first = (i % tiles_per_seq) == 0
    win_ref[0:HALO, :] = jnp.where(first, ymeta_ref[...], yprev_ref[...])
    win_ref[HALO:, :] = y_ref[...]
    lead = HALO - (CONV_WIDTH - 1)
    for b in range(8):
        span = MIX_TILE + 8 * (len(range(b, CONV_WIDTH, 8)) - 1)
        shift_ref[b, 0:span, :] = win_ref[pl.ds(lead + b, span), :]
    for hh in range(MIX_TILE // ROW_TILE):
        _mix_rows(hh * ROW_TILE, x_ref, attn_ref, gate_ref, cw_ref, cb_ref, lng_ref, lnb_ref, wco_ref, wao_ref,
                  wout_ref, nf_ref, wrt_ref, br_ref, h2_ref, hn2_ref, eid_ref, rank_ref, gcol_ref, cnt_ref,
                  shift_ref, base_ref)


def _mix_rows(r_off, x_ref, attn_ref, gate_ref, cw_ref, cb_ref, lng_ref, lnb_ref, wco_ref, wao_ref,
              wout_ref, nf_ref, wrt_ref, br_ref, h2_ref, hn2_ref, eid_ref, rank_ref, gcol_ref, cnt_ref,
              shift_ref, base_ref):
    rows = slice(r_off, r_off + ROW_TILE)
    blocks = []
    for r0 in range(r_off, r_off + ROW_TILE, CONV_ROWS):
        acc = jnp.broadcast_to(cb_ref[...], (CONV_ROWS, CONV_CH))
        for b in range(8):
            for a, w in enumerate(range(b, CONV_WIDTH, 8)):
                acc = acc + cw_ref[w:w + 1, :] * shift_ref[b, r0 + 8 * a:r0 + 8 * a + CONV_ROWS, :]
        blocks.append(acc)
    yc = jnp.concatenate(blocks, axis=0)
    mu = jnp.mean(yc, axis=-1, keepdims=True)
    var = jnp.mean(jnp.square(yc - mu), axis=-1, keepdims=True)
    yn = (yc - mu) * lax.rsqrt(var + EPS) * lng_ref[...] + lnb_ref[...]
    ys = yn * _sigmoid(yn)
    y_b = jnp.dot(ys.astype(BF16), wco_ref[...], preferred_element_type=F32)

    y_a = jnp.dot(attn_ref[rows, :], wao_ref[...], preferred_element_type=F32)
    merged = gate_ref[rows, :D_MODEL] * y_a + gate_ref[rows, D_MODEL:] * y_b
    h2 = x_ref[rows, :] + jnp.dot(merged.astype(BF16), wout_ref[...], preferred_element_type=F32)
    h2_ref[rows, :] = h2
    ms = jnp.mean(h2 * h2, axis=-1, keepdims=True)
    hn2 = h2 * lax.rsqrt(ms + EPS) * nf_ref[...]
    hn2_ref[rows, 0, :] = _pack_bf16_pairs(hn2)

    logits = lax.dot_general(wrt_ref[...], hn2, NT_DIMS, preferred_element_type=F32,
                             precision=lax.Precision.HIGHEST) + br_ref[...]
    erow = lax.broadcasted_iota(I32, (N_EXPERTS, ROW_TILE), 0)
    vals, ids = [], []
    l = logits
    for _ in range(TOP_K_EXPERTS):
        m = jnp.max(l, axis=0, keepdims=True)
        idx = jnp.min(jnp.where(l == m, erow, N_EXPERTS), axis=0, keepdims=True)
        vals.append(m)
        ids.append(idx)
        l = jnp.where(erow == idx, -jnp.inf, l)
    ex = [jnp.exp(v - vals[0]) for v in vals]
    den = ex[0] + ex[1] + ex[2] + ex[3]
    gates = [e / den for e in ex]

    onehot = [(erow == idx) for idx in ids]
    oh = jnp.concatenate([jnp.where(o, 1.0, 0.0) for o in onehot], axis=0)
    tr = lax.broadcasted_iota(I32, (ROW_TILE, ROW_TILE), 0)
    tc = lax.broadcasted_iota(I32, (ROW_TILE, ROW_TILE), 1)
    upper = jnp.where(tr <= tc, 1.0, 0.0).astype(BF16)
    pref = jnp.dot(oh.astype(BF16), upper, preferred_element_type=F32)
    offs = base_ref[:, 0:1]
    ranks = []
    for kk in range(TOP_K_EXPERTS):
        pk = pref[kk * N_EXPERTS:(kk + 1) * N_EXPERTS, :]
        r = jnp.sum(jnp.where(onehot[kk], offs + pk - 1.0, 0.0), axis=0, keepdims=True)
        ranks.append(r.astype(I32))
        offs = offs + pk[:, ROW_TILE - 1:ROW_TILE]
    base_ref[...] = jnp.broadcast_to(offs, base_ref.shape)
    cnt_ref[...] = jnp.broadcast_to(offs, cnt_ref.shape)

    zi = jnp.zeros((8 - TOP_K_EXPERTS, ROW_TILE), I32)
    eid_ref[:, rows] = jnp.concatenate(ids + [zi], axis=0)
    rank_ref[:, rows] = jnp.concatenate(ranks + [zi], axis=0)
    g8 = jnp.concatenate(gates + [jnp.zeros((128 - TOP_K_EXPERTS, ROW_TILE), F32)], axis=0)
    gcol_ref[rows, :] = g8.T


def _mix(x2d, attn, y, ymeta, gate, wts, seq):
    n = x2d.shape[0]
    tiles_per_seq = seq // MIX_TILE
    halo_per_tile = MIX_TILE // HALO
    full = lambda a: pl.BlockSpec(a.shape, lambda i: (0,) * a.ndim)
    row = lambda w: pl.BlockSpec((MIX_TILE, w), lambda i: (i, 0))
    lane = lambda r: pl.BlockSpec((r, MIX_TILE), lambda i: (0, i))
    out_shape = (
        jax.ShapeDtypeStruct((n, D_MODEL), F32),
        jax.ShapeDtypeStruct((n, 1, D_MODEL // 2), I32),
        jax.ShapeDtypeStruct((8, n), I32),
        jax.ShapeDtypeStruct((8, n), I32),
        jax.ShapeDtypeStruct((n, 128), F32),
        jax.ShapeDtypeStruct((N_EXPERTS, 128), F32),
    )
    out_specs = (row(D_MODEL), pl.BlockSpec((MIX_TILE, 1, D_MODEL // 2), lambda i: (i, 0, 0)),
                 lane(8), lane(8), row(128), full(out_shape[5]))
    return pl.pallas_call(
        functools.partial(_mix_kernel, tiles_per_seq=tiles_per_seq),
        grid=(n // MIX_TILE,),
        in_specs=[row(D_MODEL), row(ATTN_WIDTH), row(CONV_CH),
                  pl.BlockSpec((HALO, CONV_CH), lambda i: (jnp.maximum(i * halo_per_tile - 1, 0), 0)),
                  full(ymeta), row(2 * D_MODEL)] + [full(w) for w in wts],
        out_specs=out_specs,
        out_shape=out_shape,
        scratch_shapes=[pltpu.VMEM((HALO + MIX_TILE, CONV_CH), F32),
                        pltpu.VMEM((8, HALO + MIX_TILE, CONV_CH), F32),
                        pltpu.VMEM((N_EXPERTS, 128), F32)],
        compiler_params=pltpu.CompilerParams(dimension_semantics=("arbitrary",),
                                             vmem_limit_bytes=VMEM_LIMIT),
        name="mix",
    )(x2d, attn, y, y, ymeta, gate, *wts)


def _slots_kernel(starts_ref, eid_ref, rank_ref, slot_ref):
    eid = eid_ref[...]
    base = jnp.zeros(eid.shape, I32)
    for e in range(N_EXPERTS):
        base = jnp.where(eid == e, starts_ref[e], base)
    slot = base + rank_ref[...]
    for t in range(slot_ref.shape[0]):
        for kk in range(TOP_K_EXPERTS):
            slot_ref[t, :, kk * ROW_TILE:(kk + 1) * ROW_TILE] = slot[kk:kk + 1, t * ROW_TILE:(t + 1) * ROW_TILE]


def _slots(starts, eid8, rank8):
    n = eid8.shape[1]
    tiles = 8
    spec = pl.BlockSpec((8, tiles * ROW_TILE), lambda i: (0, i))
    return pl.pallas_call(
        _slots_kernel,
        grid=(n // (tiles * ROW_TILE),),
        in_specs=[pl.BlockSpec(memory_space=pltpu.SMEM), spec, spec],
        out_specs=pl.BlockSpec((tiles, 1, COPIES_PER_TILE), lambda i: (i, 0, 0)),
        out_shape=jax.ShapeDtypeStruct((n // ROW_TILE, 1, COPIES_PER_TILE), I32),
        compiler_params=pltpu.CompilerParams(dimension_semantics=("arbitrary",)),
        name="slots",
    )(starts, eid8, rank8)


COPIES_PER_TILE = TOP_K_EXPERTS * ROW_TILE
SLOT_UNROLL = 16


def _slot_source_kernel(slot_ref, init_ref, src_ref, sem, *, n):
    i = pl.program_id(0)

    @pl.when(i == 0)
    def _():
        cp = pltpu.make_async_copy(init_ref, src_ref, sem)
        cp.start()
        cp.wait()

    for kk in range(TOP_K_EXPERTS):
        def body(g, carry):
            for u in range(SLOT_UNROLL):
                t = g * SLOT_UNROLL + u
                src_ref[slot_ref[0, 0, kk * ROW_TILE + t]] = kk * n + i * ROW_TILE + t
            return carry
        lax.fori_loop(0, ROW_TILE // SLOT_UNROLL, body, 0)


def _slot_sources(slot_tiles, init, n):
    return pl.pallas_call(
        functools.partial(_slot_source_kernel, n=n),
        grid=(slot_tiles.shape[0],),
        in_specs=[pl.BlockSpec((1, 1, COPIES_PER_TILE), lambda i: (i, 0, 0), memory_space=pltpu.SMEM),
                  pl.BlockSpec(memory_space=pl.ANY)],
        out_specs=pl.BlockSpec(memory_space=pltpu.SMEM),
        out_shape=jax.ShapeDtypeStruct(init.shape, I32),
        scratch_shapes=[pltpu.SemaphoreType.DMA(())],
        compiler_params=pltpu.CompilerParams(dimension_semantics=("arbitrary",)),
        name="slot_sources",
    )(slot_tiles, init)


def _expert_kernel(te_ref, nused_ref, src_ref, src_next_ref, src_prev_ref, hn_hbm,
                   wug_ref, bug_ref, wd_ref, bd_ref, yk_hbm, xbuf, ybuf, wug_bf, wd_bf, gsem, ssem, *, n):
    i = pl.program_id(0)
    nused = nused_ref[0]

    def gather_copy(src, r, b):
        tok = src[0, 0, r] & (n - 1)
        return pltpu.make_async_copy(hn_hbm.at[tok], xbuf.at[b, r], gsem.at[b])

    def scatter_copy(src, r, b):
        return pltpu.make_async_copy(ybuf.at[b, r], yk_hbm.at[src[0, 0, r]], ssem.at[b])

    @pl.when(i <= nused)
    def _():
        cur = i % 2
        oth = 1 - cur

        def wait_scatter(b):
            pltpu.make_async_copy(ybuf.at[b], yk_hbm.at[pl.ds(0, ROW_TILE)], ssem.at[b]).wait()

        @pl.when(i == 0)
        def _():
            for r in range(ROW_TILE):
                gather_copy(src_ref, r, 0).start()
            ybuf[1] = jnp.zeros(ybuf.shape[1:], F32)
            spare = pltpu.make_async_copy(ybuf.at[1], yk_hbm.at[pl.ds(TOP_K_EXPERTS * n, ROW_TILE)],
                                          ssem.at[1])
            spare.start()
            spare.wait()

        @pl.when(i < nused)
        def _():
            for r in range(ROW_TILE):
                gather_copy(src_next_ref, r, oth).start()

        @pl.when(i > 0)
        def _():
            for r in range(ROW_TILE):
                scatter_copy(src_prev_ref, r, oth).start()

        pltpu.make_async_copy(hn_hbm.at[pl.ds(0, ROW_TILE)], xbuf.at[cur], gsem.at[cur]).wait()

        prev = te_ref[jnp.maximum(i - 1, 0)]
        fresh = (i == 0) | (te_ref[jnp.minimum(i, nused - 1)] != prev)

        @pl.when(fresh)
        def _():
            wug_bf[...] = wug_ref[0].astype(BF16)
            wd_bf[...] = wd_ref[0].astype(BF16)

        xb = _unpack_bf16_pairs(xbuf[cur, :, 0, :])
        ug = jnp.dot(xb, wug_bf[...], preferred_element_type=F32) + bug_ref[0]
        gate = jnp.minimum(ug[:, :D_FF], SWIGLU_LIMIT)
        up = jnp.clip(ug[:, D_FF:], -SWIGLU_LIMIT, SWIGLU_LIMIT)
        act = (up + 1.0) * gate * _sigmoid(SWIGLU_ALPHA * gate)
        y = jnp.dot(act.astype(BF16), wd_bf[...], preferred_element_type=F32) + bd_ref[0]

        @pl.when(i > 1)
        def _():
            wait_scatter(cur)

        ybuf[cur, :, 0, :] = y

        @pl.when((i == nused) & (i > 0))
        def _():
            wait_scatter(oth)


def _experts(tile_expert, nused, src_tiles, hn2p, w_ug, b_ug, w_down, b_down, n):
    ntiles = src_tiles.shape[0]
    clamp = lambda i, nu: jnp.maximum(jnp.minimum(i, nu[0] - 1), 0)
    smem = lambda f: pl.BlockSpec((1, 1, ROW_TILE), f, memory_space=pltpu.SMEM)
    expert = lambda i, te, nu: (te[clamp(i, nu)], 0, 0)
    grid_spec = pltpu.PrefetchScalarGridSpec(
        num_scalar_prefetch=2,
        grid=(ntiles + 1,),
        in_specs=[
            smem(lambda i, te, nu: (clamp(i, nu), 0, 0)),
            smem(lambda i, te, nu: (clamp(i + 1, nu), 0, 0)),
            smem(lambda i, te, nu: (clamp(i - 1, nu), 0, 0)),
            pl.BlockSpec(memory_space=pl.ANY),
            pl.BlockSpec((1, D_MODEL, 2 * D_FF), expert),
            pl.BlockSpec((1, 1, 2 * D_FF), expert),
            pl.BlockSpec((1, D_FF, D_MODEL), expert),
            pl.BlockSpec((1, 1, D_MODEL), expert),
        ],
        out_specs=pl.BlockSpec(memory_space=pl.ANY),
        scratch_shapes=[pltpu.VMEM((2, ROW_TILE, 1, D_MODEL // 2), I32),
                        pltpu.VMEM((2, ROW_TILE, 1, D_MODEL), F32),
                        pltpu.VMEM((D_MODEL, 2 * D_FF), BF16), pltpu.VMEM((D_FF, D_MODEL), BF16),
                        pltpu.SemaphoreType.DMA((2,)), pltpu.SemaphoreType.DMA((2,))],
    )
    return pl.pallas_call(
        functools.partial(_expert_kernel, n=n),
        grid_spec=grid_spec,
        out_shape=jax.ShapeDtypeStruct((TOP_K_EXPERTS * n + ROW_TILE, 1, D_MODEL), F32),
        compiler_params=pltpu.CompilerParams(dimension_semantics=("arbitrary",),
                                             vmem_limit_bytes=VMEM_LIMIT),
        name="experts",
    )(tile_expert, nused, src_tiles, src_tiles, src_tiles, hn2p,
      w_ug, b_ug[:, None, :], w_down, b_down[:, None, :])


def _final_kernel(h2_ref, gcol_ref, nf_ref, *rest):
    y_refs, o_ref = rest[:TOP_K_EXPERTS], rest[TOP_K_EXPERTS]
    h = h2_ref[...]
    for kk in range(TOP_K_EXPERTS):
        h = h + gcol_ref[:, kk:kk + 1] * y_refs[kk][:, 0, :]
    ms = jnp.mean(h * h, axis=-1, keepdims=True)
    o_ref[...] = h * lax.rsqrt(ms + EPS) * nf_ref[...]


def _final(h2, gcol, nf, yk):
    n = h2.shape[0]
    ntiles = n // ROW_TILE
    choice = lambda kk: pl.BlockSpec((ROW_TILE, 1, D_MODEL), lambda i: (kk * ntiles + i, 0, 0))
    return pl.pallas_call(
        _final_kernel,
        grid=(ntiles,),
        in_specs=[pl.BlockSpec((ROW_TILE, D_MODEL), lambda i: (i, 0)),
                  pl.BlockSpec((ROW_TILE, 128), lambda i: (i, 0)),
                  pl.BlockSpec((1, D_MODEL), lambda i: (0, 0))] + [choice(kk) for kk in range(TOP_K_EXPERTS)],
        out_specs=pl.BlockSpec((ROW_TILE, D_MODEL), lambda i: (i, 0)),
        out_shape=jax.ShapeDtypeStruct((n, D_MODEL), F32),
        compiler_params=pltpu.CompilerParams(dimension_semantics=("arbitrary",),
                                             vmem_limit_bytes=VMEM_LIMIT),
        name="final",
    )(h2, gcol, nf, *([yk] * TOP_K_EXPERTS))


def _split_w_in(w_in, b_gate):
    c = ATTN_WIDTH
    o = 0
    wq = w_in[:, o:o + c]; o += c
    wk = w_in[:, o:o + c]; o += c
    wv = w_in[:, o:o + c]; o += c
    wiq = w_in[:, o:o + IDX_HEADS * IDX_DIM]; o += IDX_HEADS * IDX_DIM
    wik = w_in[:, o:o + IDX_DIM]; o += IDX_DIM
    wiw = w_in[:, o:o + IDX_HEADS]; o += IDX_HEADS
    wglu = w_in[:, o:o + 2 * CONV_CH]; o += 2 * CONV_CH
    wgate = w_in[:, o:]
    wiwt = jnp.concatenate([wiw.T, jnp.zeros((16 - IDX_HEADS, D_MODEL), w_in.dtype)], axis=0)
    bf = lambda a: a.astype(BF16)
    return (bf(wq), bf(wk), bf(wiq), bf(wik), bf(wv.T), bf(wiwt), bf(wglu), bf(wgate),
            b_gate[None, :].astype(F32))


def kernel(x, meta_tokens, rel_bias, norm_mix, w_in, b_gate, w_attn_out, conv_w, conv_b, conv_ln_g, conv_ln_b, w_conv_out, w_out, norm_ffn, w_router, b_router, w_up_gate, b_up_gate, w_down, b_down, norm_final):
    batch, seq, d = x.shape
    n = batch * seq
    x2d = x.reshape(n, d)

    wts = _split_w_in(w_in[0], b_gate[0])
    g_mix = norm_mix[0][None, :]
    q3, k3, iq3, ik, vt, iwt, y, gate = _project(x2d, g_mix, wts, PROJ_TILE)
    _, km3, _, ikm, vtm, _, ym, _ = _project(meta_tokens.astype(F32), g_mix, wts, N_META)

    tab, tabm = _bias_tables(rel_bias.astype(F32))
    attn = _attention(q3, iq3, iwt, k3, ik, vt, km3, ikm, vtm, tab, tabm, batch, seq)

    ymeta = jnp.concatenate([jnp.zeros((HALO - N_META, CONV_CH), F32), ym], axis=0)
    cw = jnp.concatenate([conv_w[0], jnp.zeros((32 - CONV_WIDTH, CONV_CH), F32)], axis=0)
    mix_w = (cw, conv_b[0][None, :], conv_ln_g[0][None, :], conv_ln_b[0][None, :],
             w_conv_out[0].astype(BF16), w_attn_out[0].astype(BF16), w_out[0].astype(BF16),
             norm_ffn[0][None, :], w_router[0].T, b_router[0][:, None])
    h2, hn2, eid8, rank8, gcol, cnt = _mix(x2d, attn, y, ymeta, gate, mix_w, seq)

    counts = cnt[:, 0].astype(I32)
    padded = ((counts + ROW_TILE - 1) // ROW_TILE) * ROW_TILE
    ends = jnp.cumsum(padded)
    starts = ends - padded
    slot_tiles = _slots(starts, eid8, rank8)
    nslots = n * TOP_K_EXPERTS + N_EXPERTS * ROW_TILE
    ntiles = nslots // ROW_TILE
    tile_start = jnp.arange(ntiles, dtype=I32) * ROW_TILE
    nused = (ends[-1] // ROW_TILE).astype(I32)
    last_start = jnp.maximum(ends[-1] - ROW_TILE, 0)
    tile_expert = jnp.sum((jnp.minimum(tile_start, last_start)[:, None] >= ends[None, :]).astype(I32), axis=1)

    spare = TOP_K_EXPERTS * n + jnp.arange(nslots, dtype=I32) % ROW_TILE
    src_tiles = _slot_sources(slot_tiles, spare, n).reshape(ntiles, 1, ROW_TILE)
    yk = _experts(tile_expert, nused[None], src_tiles, hn2, w_up_gate[0], b_up_gate[0], w_down[0], b_down[0], n)
    out = _final(h2, gcol, norm_final[None, :], yk)
    return out.reshape(batch, seq, d)
```

```python
import functools
import math

import jax
import jax.numpy as jnp
from jax import lax
from jax.experimental import pallas as pl
from jax.experimental.pallas import tpu as pltpu

F32 = jnp.float32
BF16 = jnp.bfloat16
I32 = jnp.int32

D_MODEL = 1024
N_META = 16
N_HEADS = 8
HEAD_DIM = 64
ATTN_WIDTH = N_HEADS * HEAD_DIM
IDX_HEADS = 8
IDX_DIM = 64
TOPK_MAX = 256
CONV_CH = 512
CONV_WIDTH = 31
N_BUCKETS = 32
MAX_DISTANCE = 128
N_EXPERTS = 32
TOP_K_EXPERTS = 4
D_FF = 1024
SWIGLU_LIMIT = 7.0
SWIGLU_ALPHA = 1.702
EPS = 1e-6
IDX_SCALE = (IDX_DIM ** -0.5) * (IDX_HEADS ** -0.5)

SUBLANES = 8
LANES = 128
PACKED_ROWS = 16
ROW_TILE = 256
MIX_TILE = 512
MIX_ROWS = 256
PROJ_TILE = 512
Q_TILE = 256
K_CHUNK = 256
COUNT_CHUNKS = 2
HALO = 32
CONV_ROWS = 32
NEG = -1e30
LOG2E = math.log2(math.e)
VMEM_LIMIT = 56 * 1024 * 1024

NT_DIMS = (((1,), (1,)), ((), ()))


def _sigmoid(x):
    return 1.0 / (1.0 + jnp.exp(-x))


def _pack_bf16_pairs(x):
    w = x.shape[1] // 2
    hi = lax.bitcast_convert_type(x[:, :w].astype(BF16).astype(F32), jnp.uint32)
    lo = lax.bitcast_convert_type(x[:, w:].astype(BF16).astype(F32), jnp.uint32)
    return lax.bitcast_convert_type(hi | (lo >> 16), I32)


def _unpack_bf16_pairs(p):
    u = lax.bitcast_convert_type(p, jnp.uint32)
    hi = lax.bitcast_convert_type(u & jnp.uint32(0xFFFF0000), F32)
    lo = lax.bitcast_convert_type(u << 16, F32)
    return jnp.concatenate([hi, lo], axis=1).astype(BF16)


def _proj_kernel(x_ref, g_ref, wq_ref, wk_ref, wiq_ref, wik_ref, wvt_ref, wiwt_ref, wglu_ref,
                 wgate_ref, bgate_ref,
                 q_ref, k_ref, iq_ref, ik_ref, vt_ref, iwt_ref, y_ref, gate_ref):
    x = x_ref[...]
    ms = jnp.mean(x * x, axis=-1, keepdims=True)
    xn = (x * lax.rsqrt(ms + EPS) * g_ref[...]).astype(BF16)

    q = jnp.dot(xn, wq_ref[...], preferred_element_type=F32) * (HEAD_DIM ** -0.5 * LOG2E)
    k = jnp.dot(xn, wk_ref[...], preferred_element_type=F32)
    iq = jnp.dot(xn, wiq_ref[...], preferred_element_type=F32)
    for h in range(N_HEADS):
        sl = slice(h * HEAD_DIM, (h + 1) * HEAD_DIM)
        q_ref[h] = q[:, sl].astype(BF16)
        k_ref[h] = k[:, sl].astype(BF16)
        iq_ref[h] = iq[:, sl].astype(BF16)
    ik_ref[...] = jnp.dot(xn, wik_ref[...], preferred_element_type=F32).astype(BF16)
    vt_ref[...] = lax.dot_general(wvt_ref[...], xn, NT_DIMS, preferred_element_type=F32).astype(BF16)
    iwt = lax.dot_general(wiwt_ref[...], xn, NT_DIMS, preferred_element_type=F32)
    iwt_ref[...] = iwt[:IDX_HEADS] * IDX_SCALE
    glu = jnp.dot(xn, wglu_ref[...], preferred_element_type=F32)
    y_ref[...] = glu[:, :CONV_CH] * _sigmoid(glu[:, CONV_CH:])
    gate = jnp.dot(xn, wgate_ref[...], preferred_element_type=F32) + bgate_ref[...]
    gate_ref[...] = _sigmoid(gate)


def _project(x2d, g, wts, tm):
    n = x2d.shape[0]
    wq, wk, wiq, wik, wvt, wiwt, wglu, wgate, bgate = wts
    full = lambda a: pl.BlockSpec(a.shape, lambda i: (0,) * a.ndim, pipeline_mode=pl.Buffered(1))
    out_shape = (
        jax.ShapeDtypeStruct((N_HEADS, n, HEAD_DIM), BF16),
        jax.ShapeDtypeStruct((N_HEADS, n, HEAD_DIM), BF16),
        jax.ShapeDtypeStruct((IDX_HEADS, n, IDX_DIM), BF16),
        jax.ShapeDtypeStruct((n, IDX_DIM), BF16),
        jax.ShapeDtypeStruct((ATTN_WIDTH, n), BF16),
        jax.ShapeDtypeStruct((IDX_HEADS, n), F32),
        jax.ShapeDtypeStruct((n, CONV_CH), F32),
        jax.ShapeDtypeStruct((n, 2 * D_MODEL), F32),
    )
    out_specs = (
        pl.BlockSpec((N_HEADS, tm, HEAD_DIM), lambda i: (0, i, 0)),
        pl.BlockSpec((N_HEADS, tm, HEAD_DIM), lambda i: (0, i, 0)),
        pl.BlockSpec((IDX_HEADS, tm, IDX_DIM), lambda i: (0, i, 0)),
        pl.BlockSpec((tm, IDX_DIM), lambda i: (i, 0)),
        pl.BlockSpec((ATTN_WIDTH, tm), lambda i: (0, i)),
        pl.BlockSpec((IDX_HEADS, tm), lambda i: (0, i)),
        pl.BlockSpec((tm, CONV_CH), lambda i: (i, 0)),
        pl.BlockSpec((tm, 2 * D_MODEL), lambda i: (i, 0)),
    )
    return pl.pallas_call(
        _proj_kernel,
        grid=(n // tm,),
        in_specs=[pl.BlockSpec((tm, D_MODEL), lambda i: (i, 0)), full(g), full(wq), full(wk), full(wiq),
                  full(wik), full(wvt), full(wiwt), full(wglu), full(wgate), full(bgate)],
        out_specs=out_specs,
        out_shape=out_shape,
        compiler_params=pltpu.CompilerParams(dimension_semantics=("arbitrary",),
                                             vmem_limit_bytes=VMEM_LIMIT),
        name="proj",
    )(x2d, g, wq, wk, wiq, wik, wvt, wiwt, wglu, wgate, bgate)


def _t5_bucket(n):
    max_exact = N_BUCKETS // 2
    nf = jnp.maximum(n, 1).astype(F32)
    large = max_exact + (jnp.log(nf / max_exact) / math.log(MAX_DISTANCE / max_exact)
                         * (N_BUCKETS - max_exact)).astype(I32)
    large = jnp.minimum(large, N_BUCKETS - 1)
    return jnp.where(n < max_exact, n, large)


def _bias_lookup(rb_ref, dist, h):
    bucket = _t5_bucket(jnp.maximum(dist, 0))
    out = jnp.full(dist.shape, NEG, F32)
    for b in range(N_BUCKETS):
        out = jnp.where(bucket == b, rb_ref[b, h] * LOG2E, out)
    return jnp.where(dist >= 0, out, NEG)


def _bias_kernel(rb_ref, tab_ref, tabm_ref):
    kind = pl.program_id(0)
    r = pl.program_id(1)
    rows = tab_ref.shape[2]
    s = lax.broadcasted_iota(I32, (rows, Q_TILE), 0) + r * rows
    t = lax.broadcasted_iota(I32, (rows, Q_TILE), 1)
    dist = jnp.where(kind == 2, 2 * K_CHUNK, t - s + kind * K_CHUNK)
    for h in range(N_HEADS):
        tab_ref[0, h] = _bias_lookup(rb_ref, dist, h)
    m = lax.broadcasted_iota(I32, (N_META, Q_TILE), 0)
    tm_ = lax.broadcasted_iota(I32, (N_META, Q_TILE), 1)
    distm = jnp.where(kind == 0, N_META + tm_ - m, 2 * K_CHUNK)
    for h in range(N_HEADS):
        tabm_ref[0, h] = _bias_lookup(rb_ref, distm, h)


def _bias_tables(rel_bias):
    rows = 64
    return pl.pallas_call(
        _bias_kernel,
        grid=(3, K_CHUNK // rows),
        in_specs=[pl.BlockSpec(memory_space=pltpu.SMEM)],
        out_specs=(pl.BlockSpec((1, N_HEADS, rows, Q_TILE), lambda kd, r: (kd, 0, r, 0)),
                   pl.BlockSpec((1, N_HEADS, N_META, Q_TILE), lambda kd, r: (kd, 0, 0, 0))),
        out_shape=(jax.ShapeDtypeStruct((3, N_HEADS, K_CHUNK, Q_TILE), F32),
                   jax.ShapeDtypeStruct((3, N_HEADS, N_META, Q_TILE), F32)),
        compiler_params=pltpu.CompilerParams(dimension_semantics=("arbitrary", "arbitrary")),
        name="bias_tables",
    )(rel_bias)


def _fold_rows(x, op):
    r, l = x.shape
    x3 = x.reshape(r // SUBLANES, SUBLANES, l)
    return op(x3, axis=0)


def _attn_kernel(q_ref, iq_ref, iwt_ref, k_ref, ik_ref, vt_ref, km_ref, ikm_ref, vtm_ref,
                 tab_ref, tabm_ref, o_ref, sc_ref, scm_ref, l_ref, lm_ref, ot_ref, cst_ref,
                 hi_ref, lo_ref, him_ref, lom_ref, qt_ref, iqt_ref):
    j = pl.program_id(1)
    nchunks = j + 1
    iw = iwt_ref[...]

    def chunk_rows(c):
        return pl.ds(pl.multiple_of(c * K_CHUNK, K_CHUNK), K_CHUNK)

    def paired_chunk_loop(body, init):
        def quad(i, c):
            for u in range(4):
                c = body(4 * i + u, c)
            return c

        carry = lax.fori_loop(0, lax.shift_right_logical(nchunks, 2), quad, init)
        base = nchunks & ~3
        carry = lax.cond((nchunks & 2) == 2, lambda c: body(base + 1, body(base, c)), lambda c: c, carry)
        return lax.cond((nchunks & 1) == 1, lambda c: body(nchunks - 1, c), lambda c: c, carry)

    for h in range(N_HEADS):
        qt_ref[h] = q_ref[h].astype(F32).T.astype(BF16)
        iqt_ref[h] = iq_ref[h].astype(F32).T.astype(BF16)

    def idx_scores(ikc):
        acc = None
        for h in range(IDX_HEADS):
            s = jnp.dot(ikc, iqt_ref[h], preferred_element_type=F32)
            term = jnp.maximum(s, 0.0) * iw[h:h + 1, :]
            acc = term if acc is None else acc + term
        return acc

    scm_ref[...] = idx_scores(ikm_ref[...])

    row_minus_col = (lax.broadcasted_iota(I32, (K_CHUNK, Q_TILE), 0)
                     - lax.broadcasted_iota(I32, (K_CHUNK, Q_TILE), 1))

    def p1(c, carry):
        rows = chunk_rows(c)
        future = row_minus_col > jnp.where(c == j, 0, K_CHUNK)
        sc_ref[rows, :] = jnp.where(future, -jnp.inf, idx_scores(ik_ref[rows, :]))
        return carry

    paired_chunk_loop(p1, 0)

    def count(pred):
        def body(c, acc):
            blk = sc_ref[chunk_rows(c), :]
            return acc + _fold_rows(jnp.where(pred(blk, c), 1, 0).astype(I32), jnp.sum)
        acc = lax.fori_loop(0, nchunks, body, jnp.zeros((SUBLANES, Q_TILE), I32))
        acc = acc + _fold_rows(jnp.where(pred(scm_ref[...], -1), 1, 0).astype(I32), jnp.sum)
        return jnp.sum(acc, axis=0, keepdims=True)

    I16 = jnp.int16
    LOW = -2 ** 15

    def split_key(s):
        bits = lax.bitcast_convert_type(s, I32)
        key = jnp.where(bits < 0, bits ^ jnp.int32(0x7FFFFFFF), bits)
        hi = lax.shift_right_arithmetic(key, 16).astype(I16)
        lo = ((key & 0xFFFF) + LOW).astype(I16)
        return hi, lo

    him_ref[...], lom_ref[...] = split_key(scm_ref[...])

    def p2(c, carry):
        rows = chunk_rows(c)
        hi_ref[rows, :], lo_ref[rows, :] = split_key(sc_ref[rows, :])
        return carry

    lax.fori_loop(0, nchunks, p2, 0)

    count_trips = lax.shift_right_logical(nchunks + COUNT_CHUNKS - 1, COUNT_CHUNKS.bit_length() - 1)

    def p2pad(c, carry):
        rows = chunk_rows(c)
        hi_ref[rows, :] = jnp.full((K_CHUNK, Q_TILE), LOW, I16)
        lo_ref[rows, :] = jnp.full((K_CHUNK, Q_TILE), LOW, I16)
        return carry

    lax.fori_loop(nchunks, count_trips * COUNT_CHUNKS, p2pad, 0)

    def tree_sum16(ind):
        parts = [ind[a * PACKED_ROWS:(a + 1) * PACKED_ROWS, :] for a in range(ind.shape[0] // PACKED_ROWS)]
        while len(parts) > 1:
            parts = [parts[a] + parts[a + 1] for a in range(0, len(parts), 2)]
        return parts[0]

    def count16(ref, mref, pred):
        one, zero = jnp.ones((), BF16), jnp.zeros((), BF16)

        def body(cc, acc):
            span = COUNT_CHUNKS * K_CHUNK
            rows = pl.ds(pl.multiple_of(cc * span, span), span)
            return acc + tree_sum16(jnp.where(pred(ref[rows, :]), one, zero)).astype(F32)
        acc = lax.fori_loop(0, count_trips, body, jnp.zeros((PACKED_ROWS, Q_TILE), F32))
        acc = acc + jnp.where(pred(mref[...]), one, zero).astype(F32)
        return jnp.sum(acc, axis=0, keepdims=True).astype(I32)

    def to16(u):
        return (u + LOW).astype(I16)

    def search_hi(i, u):
        cand = u | lax.shift_left(jnp.int32(1), 15 - i)
        c16 = to16(cand)
        cnt = count16(hi_ref, him_ref, lambda blk: blk >= c16)
        return jnp.where(cnt >= TOPK_MAX, cand, u)

    u_hi = lax.fori_loop(0, 16, search_hi, jnp.zeros((1, Q_TILE), I32))
    t16 = to16(u_hi)
    need = TOPK_MAX - count16(hi_ref, him_ref, lambda blk: blk > t16)

    lom_ref[...] = jnp.where(him_ref[...] == t16, lom_ref[...], jnp.int16(LOW))

    def p2b(c, carry):
        rows = chunk_rows(c)
        lo_ref[rows, :] = jnp.where(hi_ref[rows, :] == t16, lo_ref[rows, :], jnp.int16(LOW))
        return carry

    lax.fori_loop(0, nchunks, p2b, 0)

    def search_lo(i, v):
        cand = v | lax.shift_left(jnp.int32(1), 15 - i)
        c16 = to16(cand)
        cnt = count16(lo_ref, lom_ref, lambda blk: blk >= c16)
        return jnp.where(cnt >= need, cand, v)

    u_lo = lax.fori_loop(0, 16, search_lo, jnp.zeros((1, Q_TILE), I32))
    thr_key = lax.shift_left(u_hi + LOW, 16) | u_lo
    thr_bits = jnp.where(thr_key < 0, thr_key ^ jnp.int32(0x7FFFFFFF), thr_key)
    thr = lax.bitcast_convert_type(thr_bits, F32)

    cnt_ge = count(lambda blk, c: blk >= thr)
    tie = (cnt_ge > TOPK_MAX) & (thr > -jnp.inf)
    any_tie = jnp.max(tie.astype(I32))
    cst_ref[...] = jnp.full((SUBLANES, Q_TILE), 2 ** 30, I32)

    def pos_of(c, shape):
        r = lax.broadcasted_iota(I32, shape, 0)
        return jnp.where(c < 0, r, r + N_META + c * K_CHUNK)

    @pl.when(any_tie > 0)
    def _():
        cnt_gt = count(lambda blk, c: blk > thr)
        need = TOPK_MAX - cnt_gt

        def bis_pos(i, cs):
            cand = cs | lax.shift_left(jnp.int32(1), 11 - i)
            f = count(lambda blk, c: (blk == thr) & (pos_of(c, blk.shape) < cand))
            return jnp.where(f <= need, cand, cs)

        cs = lax.fori_loop(0, 12, bis_pos, jnp.zeros((1, Q_TILE), I32))
        cs = jnp.where(tie, cs, 2 ** 30)
        cst_ref[...] = jnp.broadcast_to(cs, (SUBLANES, Q_TILE))

    def to_mask(blk, c, with_ties):
        if with_ties:
            cs = cst_ref[0:1, :]
            sel = (blk > thr) | ((blk == thr) & (pos_of(c, blk.shape) < cs))
        else:
            sel = blk >= thr
        return jnp.where(sel, 0.0, NEG)

    for with_ties in (False, True):
        @pl.when((any_tie > 0) == with_ties)
        def _():
            def body(c, carry):
                rows = chunk_rows(c)
                sc_ref[rows, :] = to_mask(sc_ref[rows, :], c, with_ties)
                return carry
            lax.fori_loop(0, nchunks, body, 0)
            scm_ref[...] = to_mask(scm_ref[...], -1, with_ties)

    kindm = jnp.minimum(j, 1)
    heads = [slice(h * HEAD_DIM, (h + 1) * HEAD_DIM) for h in range(N_HEADS)]

    mx0 = []
    for h in range(N_HEADS):
        lm = (jnp.dot(km_ref[h], qt_ref[h], preferred_element_type=F32)
              + tabm_ref[kindm, h] + scm_ref[...])
        lm_ref[h] = lm
        mx0.append(_fold_rows(lm, jnp.max))

    def pass_a(c, mx):
        rows = chunk_rows(c)
        kind = jnp.minimum(j - c, 2)
        mask = sc_ref[rows, :]
        out = []
        for h in range(N_HEADS):
            l = (jnp.dot(k_ref[h, rows, :], qt_ref[h], preferred_element_type=F32)
                 + tab_ref[kind, h] + mask)
            l_ref[h, rows, :] = l
            out.append(jnp.maximum(mx[h], _fold_rows(l, jnp.max)))
        return tuple(out)

    mx = paired_chunk_loop(pass_a, tuple(mx0))
    m = [jnp.max(mx[h], axis=0, keepdims=True) for h in range(N_HEADS)]

    den0 = []
    for h in range(N_HEADS):
        pm = jnp.exp2(lm_ref[h] - m[h])
        den0.append(_fold_rows(pm, jnp.sum))
        ot_ref[heads[h], :] = jnp.dot(vtm_ref[heads[h], :], pm.astype(BF16), preferred_element_type=F32)

    def pass_b(c, den):
        rows = chunk_rows(c)
        out = []
        for h in range(N_HEADS):
            p = jnp.exp2(l_ref[h, rows, :] - m[h])
            out.append(den[h] + _fold_rows(p, jnp.sum))
            ot_ref[heads[h], :] += jnp.dot(vt_ref[heads[h], rows], p.astype(BF16),
                                           preferred_element_type=F32)
        return tuple(out)

    den = paired_chunk_loop(pass_b, tuple(den0))
    for h in range(N_HEADS):
        ot_ref[heads[h], :] = ot_ref[heads[h], :] / jnp.sum(den[h], axis=0, keepdims=True)
    o_ref[...] = ot_ref[...].T.astype(BF16)


def _attention(q3, iq3, iwt, k3, ik, vt, km3, ikm, vtm, tab, tabm, batch, seq):
    n = batch * seq
    tiles = seq // Q_TILE
    full = lambda a: pl.BlockSpec(a.shape, lambda b, j: (0,) * a.ndim, pipeline_mode=pl.Buffered(1))
    return pl.pallas_call(
        _attn_kernel,
        grid=(batch, tiles),
        in_specs=[
            pl.BlockSpec((N_HEADS, Q_TILE, HEAD_DIM), lambda b, j: (0, b * tiles + j, 0)),
            pl.BlockSpec((IDX_HEADS, Q_TILE, IDX_DIM), lambda b, j: (0, b * tiles + j, 0)),
            pl.BlockSpec((IDX_HEADS, Q_TILE), lambda b, j: (0, b * tiles + j)),
            pl.BlockSpec((N_HEADS, seq, HEAD_DIM), lambda b, j: (0, b, 0)),
            pl.BlockSpec((seq, IDX_DIM), lambda b, j: (b, 0)),
            pl.BlockSpec((ATTN_WIDTH, seq), lambda b, j: (0, b)),
            full(km3), full(ikm), full(vtm), full(tab), full(tabm),
        ],
        out_specs=pl.BlockSpec((Q_TILE, ATTN_WIDTH), lambda b, j: (b * tiles + j, 0)),
        out_shape=jax.ShapeDtypeStruct((n, ATTN_WIDTH), BF16),
        scratch_shapes=[
            pltpu.VMEM((seq, Q_TILE), F32),
            pltpu.VMEM((N_META, Q_TILE), F32),
            pltpu.VMEM((N_HEADS, seq, Q_TILE), F32),
            pltpu.VMEM((N_HEADS, N_META, Q_TILE), F32),
            pltpu.VMEM((ATTN_WIDTH, Q_TILE), F32),
            pltpu.VMEM((SUBLANES, Q_TILE), I32),
            pltpu.VMEM((seq, Q_TILE), jnp.int16),
            pltpu.VMEM((seq, Q_TILE), jnp.int16),
            pltpu.VMEM((N_META, Q_TILE), jnp.int16),
            pltpu.VMEM((N_META, Q_TILE), jnp.int16),
            pltpu.VMEM((N_HEADS, HEAD_DIM, Q_TILE), BF16),
            pltpu.VMEM((IDX_HEADS, IDX_DIM, Q_TILE), BF16),
        ],
        compiler_params=pltpu.CompilerParams(dimension_semantics=("arbitrary", "arbitrary"),
                                             vmem_limit_bytes=VMEM_LIMIT),
        name="attn",
    )(q3, iq3, iwt, k3, ik, vt, km3, ikm, vtm, tab, tabm)


def _mix_kernel(x_ref, attn_ref, y_ref, yprev_ref, ymeta_ref, gate_ref,
                cw_ref, cb_ref, lng_ref, lnb_ref, wco_ref, wao_ref, wout_ref, nf_ref, wrt_ref, br_ref,
                h2_ref, hn2_ref, eid_ref, rank_ref, gcol_ref, cnt_ref,
                win_ref, shift_ref, base_ref, *, tiles_per_seq):
    i = pl.program_id(0)

    @pl.when(i == 0)
    def _():
        base_ref[...] = jnp.zeros_like(base_ref)

    first = (i % tiles_per_seq) == 0
    win_ref[0:HALO, :] = jnp.where(first, ymeta_ref[...], yprev_ref[...])
    win_ref[HALO:, :] = y_ref[...]
    lead = HALO - (CONV_WIDTH - 1)
    for b in range(SUBLANES):
        span = MIX_TILE + SUBLANES * (len(range(b, CONV_WIDTH, SUBLANES)) - 1)
        shift_ref[b, 0:span, :] = win_ref[pl.ds(lead + b, span), :]
    for hh in range(MIX_TILE // MIX_ROWS):
        _mix_rows(hh * MIX_ROWS, x_ref, attn_ref, gate_ref, cw_ref, cb_ref, lng_ref, lnb_ref, wco_ref, wao_ref,
                  wout_ref, nf_ref, wrt_ref, br_ref, h2_ref, hn2_ref, eid_ref, rank_ref, gcol_ref, cnt_ref,
                  shift_ref, base_ref)


def _mix_rows(r_off, x_ref, attn_ref, gate_ref, cw_ref, cb_ref, lng_ref, lnb_ref, wco_ref, wao_ref,
              wout_ref, nf_ref, wrt_ref, br_ref, h2_ref, hn2_ref, eid_ref, rank_ref, gcol_ref, cnt_ref,
              shift_ref, base_ref):
    rows = slice(r_off, r_off + MIX_ROWS)
    blocks = []
    for r0 in range(r_off, r_off + MIX_ROWS, CONV_ROWS):
        acc = jnp.broadcast_to(cb_ref[...], (CONV_ROWS, CONV_CH))
        for b in range(SUBLANES):
            for a, w in enumerate(range(b, CONV_WIDTH, SUBLANES)):
                r1 = r0 + SUBLANES * a
                acc = acc + cw_ref[w:w + 1, :] * shift_ref[b, r1:r1 + CONV_ROWS, :]
        blocks.append(acc)
    yc = jnp.concatenate(blocks, axis=0)
    mu = jnp.mean(yc, axis=-1, keepdims=True)
    var = jnp.mean(jnp.square(yc - mu), axis=-1, keepdims=True)
    yn = (yc - mu) * lax.rsqrt(var + EPS) * lng_ref[...] + lnb_ref[...]
    ys = yn * _sigmoid(yn)
    y_b = jnp.dot(ys.astype(BF16), wco_ref[...], preferred_element_type=F32)

    y_a = jnp.dot(attn_ref[rows, :], wao_ref[...], preferred_element_type=F32)
    merged = gate_ref[rows, :D_MODEL] * y_a + gate_ref[rows, D_MODEL:] * y_b
    h2 = x_ref[rows, :] + jnp.dot(merged.astype(BF16), wout_ref[...], preferred_element_type=F32)
    h2_ref[rows, :] = h2
    ms = jnp.mean(h2 * h2, axis=-1, keepdims=True)
    hn2 = h2 * lax.rsqrt(ms + EPS) * nf_ref[...]
    hn2_ref[rows, 0, :] = _pack_bf16_pairs(hn2)

    logits = lax.dot_general(wrt_ref[...], hn2, NT_DIMS, preferred_element_type=F32,
                             precision=lax.Precision.HIGHEST) + br_ref[...]
    erow = lax.broadcasted_iota(I32, (N_EXPERTS, MIX_ROWS), 0)
    vals, ids = [], []
    l = logits
    for _ in range(TOP_K_EXPERTS):
        m = jnp.max(l, axis=0, keepdims=True)
        idx = jnp.min(jnp.where(l == m, erow, N_EXPERTS), axis=0, keepdims=True)
        vals.append(m)
        ids.append(idx)
        l = jnp.where(erow == idx, -jnp.inf, l)
    ex = [jnp.exp(v - vals[0]) for v in vals]
    den = ex[0] + ex[1] + ex[2] + ex[3]
    gates = [e / den for e in ex]

    onehot = [(erow == idx) for idx in ids]
    oh = jnp.concatenate([jnp.where(o, 1.0, 0.0) for o in onehot], axis=0)
    tr = lax.broadcasted_iota(I32, (MIX_ROWS, MIX_ROWS), 0)
    tc = lax.broadcasted_iota(I32, (MIX_ROWS, MIX_ROWS), 1)
    upper = jnp.where(tr <= tc, 1.0, 0.0).astype(BF16)
    pref = jnp.dot(oh.astype(BF16), upper, preferred_element_type=F32)
    offs = base_ref[:, 0:1]
    ranks = []
    for kk in range(TOP_K_EXPERTS):
        pk = pref[kk * N_EXPERTS:(kk + 1) * N_EXPERTS, :]
        r = jnp.sum(jnp.where(onehot[kk], offs + pk - 1.0, 0.0), axis=0, keepdims=True)
        ranks.append(r.astype(I32))
        offs = offs + pk[:, MIX_ROWS - 1:MIX_ROWS]
    base_ref[...] = jnp.broadcast_to(offs, base_ref.shape)
    cnt_ref[...] = jnp.broadcast_to(offs, cnt_ref.shape)

    zi = jnp.zeros((SUBLANES - TOP_K_EXPERTS, MIX_ROWS), I32)
    eid_ref[:, rows] = jnp.concatenate(ids + [zi], axis=0)
    rank_ref[:, rows] = jnp.concatenate(ranks + [zi], axis=0)
    g8 = jnp.concatenate(gates + [jnp.zeros((LANES - TOP_K_EXPERTS, MIX_ROWS), F32)], axis=0)
    gcol_ref[rows, :] = g8.T


def _mix(x2d, attn, y, ymeta, gate, wts, seq):
    n = x2d.shape[0]
    tiles_per_seq = seq // MIX_TILE
    halo_per_tile = MIX_TILE // HALO
    full = lambda a: pl.BlockSpec(a.shape, lambda i: (0,) * a.ndim)
    row = lambda w: pl.BlockSpec((MIX_TILE, w), lambda i: (i, 0))
    lane = lambda r: pl.BlockSpec((r, MIX_TILE), lambda i: (0, i))
    out_shape = (
        jax.ShapeDtypeStruct((n, D_MODEL), F32),
        jax.ShapeDtypeStruct((n, 1, D_MODEL // 2), I32),
        jax.ShapeDtypeStruct((SUBLANES, n), I32),
        jax.ShapeDtypeStruct((SUBLANES, n), I32),
        jax.ShapeDtypeStruct((n, LANES), F32),
        jax.ShapeDtypeStruct((N_EXPERTS, LANES), F32),
    )
    out_specs = (row(D_MODEL), pl.BlockSpec((MIX_TILE, 1, D_MODEL // 2), lambda i: (i, 0, 0)),
                 lane(SUBLANES), lane(SUBLANES), row(LANES), full(out_shape[5]))
    return pl.pallas_call(
        functools.partial(_mix_kernel, tiles_per_seq=tiles_per_seq),
        grid=(n // MIX_TILE,),
        in_specs=[row(D_MODEL), row(ATTN_WIDTH), row(CONV_CH),
                  pl.BlockSpec((HALO, CONV_CH), lambda i: (jnp.maximum(i * halo_per_tile - 1, 0), 0)),
                  full(ymeta), row(2 * D_MODEL)] + [full(w) for w in wts],
        out_specs=out_specs,
        out_shape=out_shape,
        scratch_shapes=[pltpu.VMEM((HALO + MIX_TILE, CONV_CH), F32),
                        pltpu.VMEM((SUBLANES, HALO + MIX_TILE, CONV_CH), F32),
                        pltpu.VMEM((N_EXPERTS, LANES), F32)],
        compiler_params=pltpu.CompilerParams(dimension_semantics=("arbitrary",),
                                             vmem_limit_bytes=VMEM_LIMIT),
        name="mix",
    )(x2d, attn, y, y, ymeta, gate, *wts)


def _slots_kernel(starts_ref, eid_ref, rank_ref, slot_ref):
    eid = eid_ref[...]
    base = jnp.zeros(eid.shape, I32)
    for e in range(N_EXPERTS):
        base = jnp.where(eid == e, starts_ref[e], base)
    slot = base + rank_ref[...]
    for t in range(slot_ref.shape[0]):
        for kk in range(TOP_K_EXPERTS):
            slot_ref[t, :, kk * ROW_TILE:(kk + 1) * ROW_TILE] = slot[kk:kk + 1, t * ROW_TILE:(t + 1) * ROW_TILE]


def _slots(starts, eid8, rank8):
    n = eid8.shape[1]
    tiles = 8
    spec = pl.BlockSpec((SUBLANES, tiles * ROW_TILE), lambda i: (0, i))
    return pl.pallas_call(
        _slots_kernel,
        grid=(n // (tiles * ROW_TILE),),
        in_specs=[pl.BlockSpec(memory_space=pltpu.SMEM), spec, spec],
        out_specs=pl.BlockSpec((tiles, 1, COPIES_PER_TILE), lambda i: (i, 0, 0)),
        out_shape=jax.ShapeDtypeStruct((n // ROW_TILE, 1, COPIES_PER_TILE), I32),
        compiler_params=pltpu.CompilerParams(dimension_semantics=("arbitrary",)),
        name="slots",
    )(starts, eid8, rank8)


COPIES_PER_TILE = TOP_K_EXPERTS * ROW_TILE
SLOT_UNROLL = 16


def _slot_source_kernel(slot_ref, init_ref, src_ref, sem, *, n):
    i = pl.program_id(0)

    @pl.when(i == 0)
    def _():
        cp = pltpu.make_async_copy(init_ref, src_ref, sem)
        cp.start()
        cp.wait()

    for kk in range(TOP_K_EXPERTS):
        def body(g, carry):
            for u in range(SLOT_UNROLL):
                t = g * SLOT_UNROLL + u
                src_ref[slot_ref[0, 0, kk * ROW_TILE + t]] = kk * n + i * ROW_TILE + t
            return carry
        lax.fori_loop(0, ROW_TILE // SLOT_UNROLL, body, 0)


def _slot_sources(slot_tiles, init, n):
    return pl.pallas_call(
        functools.partial(_slot_source_kernel, n=n),
        grid=(slot_tiles.shape[0],),
        in_specs=[pl.BlockSpec((1, 1, COPIES_PER_TILE), lambda i: (i, 0, 0), memory_space=pltpu.SMEM),
                  pl.BlockSpec(memory_space=pl.ANY)],
        out_specs=pl.BlockSpec(memory_space=pltpu.SMEM),
        out_shape=jax.ShapeDtypeStruct(init.shape, I32),
        scratch_shapes=[pltpu.SemaphoreType.DMA(())],
        compiler_params=pltpu.CompilerParams(dimension_semantics=("arbitrary",)),
        name="slot_sources",
    )(slot_tiles, init)


def _expert_kernel(te_ref, nused_ref, src_ref, src_next_ref, src_prev_ref, hn_hbm,
                   wug_ref, bug_ref, wd_ref, bd_ref, yk_hbm, xbuf, ybuf, wug_bf, wd_bf, gsem, ssem, *, n):
    i = pl.program_id(0)
    nused = nused_ref[0]

    def gather_copy(src, r, b):
        tok = src[0, 0, r] & (n - 1)
        return pltpu.make_async_copy(hn_hbm.at[tok], xbuf.at[b, r], gsem.at[b])

    def scatter_copy(src, r, b):
        return pltpu.make_async_copy(ybuf.at[b, r], yk_hbm.at[src[0, 0, r]], ssem.at[b])

    @pl.when(i <= nused)
    def _():
        cur = i % 2
        oth = 1 - cur

        def wait_scatter(b):
            pltpu.make_async_copy(ybuf.at[b], yk_hbm.at[pl.ds(0, ROW_TILE)], ssem.at[b]).wait()

        @pl.when(i == 0)
        def _():
            for r in range(ROW_TILE):
                gather_copy(src_ref, r, 0).start()
            ybuf[1] = jnp.zeros(ybuf.shape[1:], F32)
            spare = pltpu.make_async_copy(ybuf.at[1], yk_hbm.at[pl.ds(TOP_K_EXPERTS * n, ROW_TILE)],
                                          ssem.at[1])
            spare.start()
            spare.wait()

        @pl.when(i < nused)
        def _():
            for r in range(ROW_TILE):
                gather_copy(src_next_ref, r, oth).start()

        @pl.when(i > 0)
        def _():
            for r in range(ROW_TILE):
                scatter_copy(src_prev_ref, r, oth).start()

        pltpu.make_async_copy(hn_hbm.at[pl.ds(0, ROW_TILE)], xbuf.at[cur], gsem.at[cur]).wait()

        prev = te_ref[jnp.maximum(i - 1, 0)]
        fresh = (i == 0) | (te_ref[jnp.minimum(i, nused - 1)] != prev)

        @pl.when(fresh)
        def _():
            wug_bf[...] = wug_ref[0].astype(BF16)
            wd_bf[...] = wd_ref[0].astype(BF16)

        xb = _unpack_bf16_pairs(xbuf[cur, :, 0, :])
        ug = jnp.dot(xb, wug_bf[...], preferred_element_type=F32) + bug_ref[0]
        gate = jnp.minimum(ug[:, :D_FF], SWIGLU_LIMIT)
        up = jnp.clip(ug[:, D_FF:], -SWIGLU_LIMIT, SWIGLU_LIMIT)
        act = (up + 1.0) * gate * _sigmoid(SWIGLU_ALPHA * gate)
        y = jnp.dot(act.astype(BF16), wd_bf[...], preferred_element_type=F32) + bd_ref[0]

        @pl.when(i > 1)
        def _():
            wait_scatter(cur)

        ybuf[cur, :, 0, :] = y

        @pl.when((i == nused) & (i > 0))
        def _():
            wait_scatter(oth)


def _experts(tile_expert, nused, src_tiles, hn2p, w_ug, b_ug, w_down, b_down, n):
    ntiles = src_tiles.shape[0]
    clamp = lambda i, nu: jnp.maximum(jnp.minimum(i, nu[0] - 1), 0)
    smem = lambda f: pl.BlockSpec((1, 1, ROW_TILE), f, memory_space=pltpu.SMEM)
    expert = lambda i, te, nu: (te[clamp(i, nu)], 0, 0)
    grid_spec = pltpu.PrefetchScalarGridSpec(
        num_scalar_prefetch=2,
        grid=(ntiles + 1,),
        in_specs=[
            smem(lambda i, te, nu: (clamp(i, nu), 0, 0)),
            smem(lambda i, te, nu: (clamp(i + 1, nu), 0, 0)),
            smem(lambda i, te, nu: (clamp(i - 1, nu), 0, 0)),
            pl.BlockSpec(memory_space=pl.ANY),
            pl.BlockSpec((1, D_MODEL, 2 * D_FF), expert),
            pl.BlockSpec((1, 1, 2 * D_FF), expert),
            pl.BlockSpec((1, D_FF, D_MODEL), expert),
            pl.BlockSpec((1, 1, D_MODEL), expert),
        ],
        out_specs=pl.BlockSpec(memory_space=pl.ANY),
        scratch_shapes=[pltpu.VMEM((2, ROW_TILE, 1, D_MODEL // 2), I32),
                        pltpu.VMEM((2, ROW_TILE, 1, D_MODEL), F32),
                        pltpu.VMEM((D_MODEL, 2 * D_FF), BF16), pltpu.VMEM((D_FF, D_MODEL), BF16),
                        pltpu.SemaphoreType.DMA((2,)), pltpu.SemaphoreType.DMA((2,))],
    )
    return pl.pallas_call(
        functools.partial(_expert_kernel, n=n),
        grid_spec=grid_spec,
        out_shape=jax.ShapeDtypeStruct((TOP_K_EXPERTS * n + ROW_TILE, 1, D_MODEL), F32),
        compiler_params=pltpu.CompilerParams(dimension_semantics=("arbitrary",),
                                             vmem_limit_bytes=VMEM_LIMIT),
        name="experts",
    )(tile_expert, nused, src_tiles, src_tiles, src_tiles, hn2p,
      w_ug, b_ug[:, None, :], w_down, b_down[:, None, :])


def _final_kernel(h2_ref, gcol_ref, nf_ref, *rest):
    y_refs, o_ref = rest[:TOP_K_EXPERTS], rest[TOP_K_EXPERTS]
    h = h2_ref[...]
    for kk in range(TOP_K_EXPERTS):
        h = h + gcol_ref[:, kk:kk + 1] * y_refs[kk][:, 0, :]
    ms = jnp.mean(h * h, axis=-1, keepdims=True)
    o_ref[...] = h * lax.rsqrt(ms + EPS) * nf_ref[...]


def _final(h2, gcol, nf, yk):
    n = h2.shape[0]
    ntiles = n // ROW_TILE
    choice = lambda kk: pl.BlockSpec((ROW_TILE, 1, D_MODEL), lambda i: (kk * ntiles + i, 0, 0))
    return pl.pallas_call(
        _final_kernel,
        grid=(ntiles,),
        in_specs=[pl.BlockSpec((ROW_TILE, D_MODEL), lambda i: (i, 0)),
                  pl.BlockSpec((ROW_TILE, LANES), lambda i: (i, 0)),
                  pl.BlockSpec((1, D_MODEL), lambda i: (0, 0))] + [choice(kk) for kk in range(TOP_K_EXPERTS)],
        out_specs=pl.BlockSpec((ROW_TILE, D_MODEL), lambda i: (i, 0)),
        out_shape=jax.ShapeDtypeStruct((n, D_MODEL), F32),
        compiler_params=pltpu.CompilerParams(dimension_semantics=("arbitrary",),
                                             vmem_limit_bytes=VMEM_LIMIT),
        name="final",
    )(h2, gcol, nf, *([yk] * TOP_K_EXPERTS))


def _split_w_in(w_in, b_gate):
    c = ATTN_WIDTH
    o = 0
    wq = w_in[:, o:o + c]; o += c
    wk = w_in[:, o:o + c]; o += c
    wv = w_in[:, o:o + c]; o += c
    wiq = w_in[:, o:o + IDX_HEADS * IDX_DIM]; o += IDX_HEADS * IDX_DIM
    wik = w_in[:, o:o + IDX_DIM]; o += IDX_DIM
    wiw = w_in[:, o:o + IDX_HEADS]; o += IDX_HEADS
    wglu = w_in[:, o:o + 2 * CONV_CH]; o += 2 * CONV_CH
    wgate = w_in[:, o:]
    wiwt = jnp.concatenate([wiw.T, jnp.zeros((PACKED_ROWS - IDX_HEADS, D_MODEL), w_in.dtype)], axis=0)
    bf = lambda a: a.astype(BF16)
    return (bf(wq), bf(wk), bf(wiq), bf(wik), bf(wv.T), bf(wiwt), bf(wglu), bf(wgate),
            b_gate[None, :].astype(F32))


def kernel(x, meta_tokens, rel_bias, norm_mix, w_in, b_gate, w_attn_out, conv_w, conv_b, conv_ln_g, conv_ln_b, w_conv_out, w_out, norm_ffn, w_router, b_router, w_up_gate, b_up_gate, w_down, b_down, norm_final):
    batch, seq, d = x.shape
    n = batch * seq
    assert d == D_MODEL and seq % MIX_TILE == 0 and seq % (COUNT_CHUNKS * K_CHUNK) == 0 and n % PROJ_TILE == 0
    assert n & (n - 1) == 0, "slot sources keep the token index in the low bits of choice * n + token"
    assert min(TOPK_MAX, seq // 4) == TOPK_MAX
    x2d = x.reshape(n, d)

    wts = _split_w_in(w_in[0], b_gate[0])
    g_mix = norm_mix[0][None, :]
    q3, k3, iq3, ik, vt, iwt, y, gate = _project(x2d, g_mix, wts, PROJ_TILE)
    _, km3, _, ikm, vtm, _, ym, _ = _project(meta_tokens.astype(F32), g_mix, wts, N_META)

    tab, tabm = _bias_tables(rel_bias.astype(F32))
    attn = _attention(q3, iq3, iwt, k3, ik, vt, km3, ikm, vtm, tab, tabm, batch, seq)

    ymeta = jnp.concatenate([jnp.zeros((HALO - N_META, CONV_CH), F32), ym], axis=0)
    cw = jnp.concatenate([conv_w[0], jnp.zeros((-CONV_WIDTH % SUBLANES, CONV_CH), F32)], axis=0)
    mix_w = (cw, conv_b[0][None, :], conv_ln_g[0][None, :], conv_ln_b[0][None, :],
             w_conv_out[0].astype(BF16), w_attn_out[0].astype(BF16), w_out[0].astype(BF16),
             norm_ffn[0][None, :], w_router[0].T, b_router[0][:, None])
    h2, hn2, eid8, rank8, gcol, cnt = _mix(x2d, attn, y, ymeta, gate, mix_w, seq)

    counts = cnt[:, 0].astype(I32)
    padded = ((counts + ROW_TILE - 1) // ROW_TILE) * ROW_TILE
    ends = jnp.cumsum(padded)
    starts = ends - padded
    slot_tiles = _slots(starts, eid8, rank8)
    nslots = n * TOP_K_EXPERTS + N_EXPERTS * ROW_TILE
    ntiles = nslots // ROW_TILE
    tile_start = jnp.arange(ntiles, dtype=I32) * ROW_TILE
    nused = (ends[-1] // ROW_TILE).astype(I32)
    last_start = jnp.maximum(ends[-1] - ROW_TILE, 0)
    tile_expert = jnp.sum((jnp.minimum(tile_start, last_start)[:, None] >= ends[None, :]).astype(I32), axis=1)

    spare = TOP_K_EXPERTS * n + jnp.arange(nslots, dtype=I32) % ROW_TILE
    src_tiles = _slot_sources(slot_tiles, spare, n).reshape(ntiles, 1, ROW_TILE)
    yk = _experts(tile_expert, nused[None], src_tiles, hn2, w_up_gate[0], b_up_gate[0], w_down[0], b_down[0], n)
    out = _final(h2, gcol, norm_final[None, :], yk)
    return out.reshape(batch, seq, d)
```

```python
import functools
import math

import jax
import jax.numpy as jnp
from jax import lax
from jax.experimental import pallas as pl
from jax.experimental.pallas import tpu as pltpu

F32 = jnp.float32
BF16 = jnp.bfloat16
I32 = jnp.int32

D_MODEL = 1024
N_META = 16
N_HEADS = 8
HEAD_DIM = 64
ATTN_WIDTH = N_HEADS * HEAD_DIM
IDX_HEADS = 8
IDX_DIM = 64
TOPK_MAX = 256
CONV_CH = 512
CONV_WIDTH = 31
N_BUCKETS = 32
MAX_DISTANCE = 128
N_EXPERTS = 32
TOP_K_EXPERTS = 4
D_FF = 1024
SWIGLU_LIMIT = 7.0
SWIGLU_ALPHA = 1.702
EPS = 1e-6
IDX_SCALE = (IDX_DIM ** -0.5) * (IDX_HEADS ** -0.5)

SUBLANES = 8
LANES = 128
PACKED_ROWS = 16
ROW_TILE = 256
MIX_TILE = 512
MIX_ROWS = 256
PROJ_TILE = 512
Q_TILE = 256
K_CHUNK = 256
COUNT_CHUNKS = 2
POS_BITS = 12
HALO = 32
CONV_ROWS = 32
NEG = -1e30
LOG2E = math.log2(math.e)
VMEM_LIMIT = 56 * 1024 * 1024

NT_DIMS = (((1,), (1,)), ((), ()))


def _sigmoid(x):
    return 1.0 / (1.0 + jnp.exp(-x))


def _pack_bf16_pairs(x):
    w = x.shape[1] // 2
    hi = lax.bitcast_convert_type(x[:, :w].astype(BF16).astype(F32), jnp.uint32)
    lo = lax.bitcast_convert_type(x[:, w:].astype(BF16).astype(F32), jnp.uint32)
    return lax.bitcast_convert_type(hi | (lo >> 16), I32)


def _unpack_bf16_pairs(p):
    u = lax.bitcast_convert_type(p, jnp.uint32)
    hi = lax.bitcast_convert_type(u & jnp.uint32(0xFFFF0000), F32)
    lo = lax.bitcast_convert_type(u << 16, F32)
    return jnp.concatenate([hi, lo], axis=1).astype(BF16)


def _proj_kernel(x_ref, g_ref, wq_ref, wk_ref, wiq_ref, wik_ref, wvt_ref, wiwt_ref, wglu_ref,
                 wgate_ref, bgate_ref,
                 q_ref, k_ref, iq_ref, ik_ref, vt_ref, iwt_ref, y_ref, gate_ref):
    x = x_ref[...]
    ms = jnp.mean(x * x, axis=-1, keepdims=True)
    xn = (x * lax.rsqrt(ms + EPS) * g_ref[...]).astype(BF16)

    q = jnp.dot(xn, wq_ref[...], preferred_element_type=F32) * (HEAD_DIM ** -0.5 * LOG2E)
    k = jnp.dot(xn, wk_ref[...], preferred_element_type=F32)
    iq = jnp.dot(xn, wiq_ref[...], preferred_element_type=F32)
    for h in range(N_HEADS):
        sl = slice(h * HEAD_DIM, (h + 1) * HEAD_DIM)
        q_ref[h] = q[:, sl].astype(BF16)
        k_ref[h] = k[:, sl].astype(BF16)
        iq_ref[h] = iq[:, sl].astype(BF16)
    ik_ref[...] = jnp.dot(xn, wik_ref[...], preferred_element_type=F32).astype(BF16)
    vt_ref[...] = lax.dot_general(wvt_ref[...], xn, NT_DIMS, preferred_element_type=F32).astype(BF16)
    iwt = lax.dot_general(wiwt_ref[...], xn, NT_DIMS, preferred_element_type=F32)
    iwt_ref[...] = iwt[:IDX_HEADS] * IDX_SCALE
    glu = jnp.dot(xn, wglu_ref[...], preferred_element_type=F32)
    y_ref[...] = glu[:, :CONV_CH] * _sigmoid(glu[:, CONV_CH:])
    gate = jnp.dot(xn, wgate_ref[...], preferred_element_type=F32) + bgate_ref[...]
    gate_ref[...] = _sigmoid(gate)


def _project(x2d, g, wts, tm):
    n = x2d.shape[0]
    wq, wk, wiq, wik, wvt, wiwt, wglu, wgate, bgate = wts
    full = lambda a: pl.BlockSpec(a.shape, lambda i: (0,) * a.ndim, pipeline_mode=pl.Buffered(1))
    out_shape = (
        jax.ShapeDtypeStruct((N_HEADS, n, HEAD_DIM), BF16),
        jax.ShapeDtypeStruct((N_HEADS, n, HEAD_DIM), BF16),
        jax.ShapeDtypeStruct((IDX_HEADS, n, IDX_DIM), BF16),
        jax.ShapeDtypeStruct((n, IDX_DIM), BF16),
        jax.ShapeDtypeStruct((ATTN_WIDTH, n), BF16),
        jax.ShapeDtypeStruct((IDX_HEADS, n), F32),
        jax.ShapeDtypeStruct((n, CONV_CH), F32),
        jax.ShapeDtypeStruct((n, 2 * D_MODEL), F32),
    )
    out_specs = (
        pl.BlockSpec((N_HEADS, tm, HEAD_DIM), lambda i: (0, i, 0)),
        pl.BlockSpec((N_HEADS, tm, HEAD_DIM), lambda i: (0, i, 0)),
        pl.BlockSpec((IDX_HEADS, tm, IDX_DIM), lambda i: (0, i, 0)),
        pl.BlockSpec((tm, IDX_DIM), lambda i: (i, 0)),
        pl.BlockSpec((ATTN_WIDTH, tm), lambda i: (0, i)),
        pl.BlockSpec((IDX_HEADS, tm), lambda i: (0, i)),
        pl.BlockSpec((tm, CONV_CH), lambda i: (i, 0)),
        pl.BlockSpec((tm, 2 * D_MODEL), lambda i: (i, 0)),
    )
    return pl.pallas_call(
        _proj_kernel,
        grid=(n // tm,),
        in_specs=[pl.BlockSpec((tm, D_MODEL), lambda i: (i, 0)), full(g), full(wq), full(wk), full(wiq),
                  full(wik), full(wvt), full(wiwt), full(wglu), full(wgate), full(bgate)],
        out_specs=out_specs,
        out_shape=out_shape,
        compiler_params=pltpu.CompilerParams(dimension_semantics=("arbitrary",),
                                             vmem_limit_bytes=VMEM_LIMIT),
        name="proj",
    )(x2d, g, wq, wk, wiq, wik, wvt, wiwt, wglu, wgate, bgate)


def _t5_bucket(n):
    max_exact = N_BUCKETS // 2
    nf = jnp.maximum(n, 1).astype(F32)
    large = max_exact + (jnp.log(nf / max_exact) / math.log(MAX_DISTANCE / max_exact)
                         * (N_BUCKETS - max_exact)).astype(I32)
    large = jnp.minimum(large, N_BUCKETS - 1)
    return jnp.where(n < max_exact, n, large)


def _bias_lookup(rb_ref, dist, h):
    bucket = _t5_bucket(jnp.maximum(dist, 0))
    out = jnp.full(dist.shape, NEG, F32)
    for b in range(N_BUCKETS):
        out = jnp.where(bucket == b, rb_ref[b, h] * LOG2E, out)
    return jnp.where(dist >= 0, out, NEG)


def _bias_kernel(rb_ref, tab_ref, tabm_ref):
    kind = pl.program_id(0)
    r = pl.program_id(1)
    rows = tab_ref.shape[2]
    s = lax.broadcasted_iota(I32, (rows, Q_TILE), 0) + r * rows
    t = lax.broadcasted_iota(I32, (rows, Q_TILE), 1)
    dist = jnp.where(kind == 2, 2 * K_CHUNK, t - s + kind * K_CHUNK)
    for h in range(N_HEADS):
        tab_ref[0, h] = _bias_lookup(rb_ref, dist, h)
    m = lax.broadcasted_iota(I32, (N_META, Q_TILE), 0)
    tm_ = lax.broadcasted_iota(I32, (N_META, Q_TILE), 1)
    distm = jnp.where(kind == 0, N_META + tm_ - m, 2 * K_CHUNK)
    for h in range(N_HEADS):
        tabm_ref[0, h] = _bias_lookup(rb_ref, distm, h)


def _bias_tables(rel_bias):
    rows = 64
    return pl.pallas_call(
        _bias_kernel,
        grid=(3, K_CHUNK // rows),
        in_specs=[pl.BlockSpec(memory_space=pltpu.SMEM)],
        out_specs=(pl.BlockSpec((1, N_HEADS, rows, Q_TILE), lambda kd, r: (kd, 0, r, 0)),
                   pl.BlockSpec((1, N_HEADS, N_META, Q_TILE), lambda kd, r: (kd, 0, 0, 0))),
        out_shape=(jax.ShapeDtypeStruct((3, N_HEADS, K_CHUNK, Q_TILE), F32),
                   jax.ShapeDtypeStruct((3, N_HEADS, N_META, Q_TILE), F32)),
        compiler_params=pltpu.CompilerParams(dimension_semantics=("arbitrary", "arbitrary")),
        name="bias_tables",
    )(rel_bias)


def _fold_rows(x, op):
    r, l = x.shape
    x3 = x.reshape(r // SUBLANES, SUBLANES, l)
    return op(x3, axis=0)


def _attn_kernel(q_ref, iq_ref, iwt_ref, k_ref, ik_ref, vt_ref, km_ref, ikm_ref, vtm_ref,
                 tab_ref, tabm_ref, o_ref, sc_ref, scm_ref, l_ref, lm_ref, ot_ref, cst_ref,
                 hi_ref, lo_ref, him_ref, lom_ref, qt_ref, iqt_ref):
    j = pl.program_id(1)
    nchunks = j + 1
    iw = iwt_ref[...]

    def chunk_rows(c):
        return pl.ds(pl.multiple_of(c * K_CHUNK, K_CHUNK), K_CHUNK)

    def paired_chunk_loop(body, init):
        def quad(i, c):
            for u in range(4):
                c = body(4 * i + u, c)
            return c

        carry = lax.fori_loop(0, lax.shift_right_logical(nchunks, 2), quad, init)
        base = nchunks & ~3
        carry = lax.cond((nchunks & 2) == 2, lambda c: body(base + 1, body(base, c)), lambda c: c, carry)
        return lax.cond((nchunks & 1) == 1, lambda c: body(nchunks - 1, c), lambda c: c, carry)

    for h in range(N_HEADS):
        qt_ref[h] = q_ref[h].astype(F32).T.astype(BF16)
        iqt_ref[h] = iq_ref[h].astype(F32).T.astype(BF16)

    def idx_scores(ikc):
        acc = None
        for h in range(IDX_HEADS):
            s = jnp.dot(ikc, iqt_ref[h], preferred_element_type=F32)
            term = jnp.maximum(s, 0.0) * iw[h:h + 1, :]
            acc = term if acc is None else acc + term
        return acc

    scm_ref[...] = idx_scores(ikm_ref[...])

    row_minus_col = (lax.broadcasted_iota(I32, (K_CHUNK, Q_TILE), 0)
                     - lax.broadcasted_iota(I32, (K_CHUNK, Q_TILE), 1))

    def p1(c, carry):
        rows = chunk_rows(c)
        future = row_minus_col > jnp.where(c == j, 0, K_CHUNK)
        sc_ref[rows, :] = jnp.where(future, -jnp.inf, idx_scores(ik_ref[rows, :]))
        return carry

    paired_chunk_loop(p1, 0)

    def count(pred):
        def body(c, acc):
            blk = sc_ref[chunk_rows(c), :]
            return acc + _fold_rows(jnp.where(pred(blk, c), 1, 0).astype(I32), jnp.sum)
        acc = lax.fori_loop(0, nchunks, body, jnp.zeros((SUBLANES, Q_TILE), I32))
        acc = acc + _fold_rows(jnp.where(pred(scm_ref[...], -1), 1, 0).astype(I32), jnp.sum)
        return jnp.sum(acc, axis=0, keepdims=True)

    I16 = jnp.int16
    HALF = 16
    LOW = -2 ** (HALF - 1)
    NO_CUT = 2 ** 30

    def split_key(s):
        bits = lax.bitcast_convert_type(s, I32)
        key = jnp.where(bits < 0, bits ^ jnp.int32(0x7FFFFFFF), bits)
        hi = lax.shift_right_arithmetic(key, HALF).astype(I16)
        lo = ((key & (2 ** HALF - 1)) + LOW).astype(I16)
        return hi, lo

    him_ref[...], lom_ref[...] = split_key(scm_ref[...])

    def p2(c, carry):
        rows = chunk_rows(c)
        hi_ref[rows, :], lo_ref[rows, :] = split_key(sc_ref[rows, :])
        return carry

    lax.fori_loop(0, nchunks, p2, 0)

    count_trips = lax.shift_right_logical(nchunks + COUNT_CHUNKS - 1, COUNT_CHUNKS.bit_length() - 1)

    def p2pad(c, carry):
        rows = chunk_rows(c)
        hi_ref[rows, :] = jnp.full((K_CHUNK, Q_TILE), LOW, I16)
        lo_ref[rows, :] = jnp.full((K_CHUNK, Q_TILE), LOW, I16)
        return carry

    lax.fori_loop(nchunks, count_trips * COUNT_CHUNKS, p2pad, 0)

    def tree_sum16(ind):
        parts = [ind[a * PACKED_ROWS:(a + 1) * PACKED_ROWS, :] for a in range(ind.shape[0] // PACKED_ROWS)]
        while len(parts) > 1:
            parts = [parts[a] + parts[a + 1] for a in range(0, len(parts), 2)]
        return parts[0]

    def count16(ref, mref, pred):
        one, zero = jnp.ones((), BF16), jnp.zeros((), BF16)

        def body(cc, acc):
            span = COUNT_CHUNKS * K_CHUNK
            rows = pl.ds(pl.multiple_of(cc * span, span), span)
            return acc + tree_sum16(jnp.where(pred(ref[rows, :]), one, zero)).astype(F32)
        acc = lax.fori_loop(0, count_trips, body, jnp.zeros((PACKED_ROWS, Q_TILE), F32))
        acc = acc + jnp.where(pred(mref[...]), one, zero).astype(F32)
        return jnp.sum(acc, axis=0, keepdims=True).astype(I32)

    def to16(u):
        return (u + LOW).astype(I16)

    def search_hi(i, u):
        cand = u | lax.shift_left(jnp.int32(1), HALF - 1 - i)
        c16 = to16(cand)
        cnt = count16(hi_ref, him_ref, lambda blk: blk >= c16)
        return jnp.where(cnt >= TOPK_MAX, cand, u)

    u_hi = lax.fori_loop(0, HALF, search_hi, jnp.zeros((1, Q_TILE), I32))
    t16 = to16(u_hi)
    need = TOPK_MAX - count16(hi_ref, him_ref, lambda blk: blk > t16)

    lom_ref[...] = jnp.where(him_ref[...] == t16, lom_ref[...], jnp.int16(LOW))

    def p2b(c, carry):
        rows = chunk_rows(c)
        lo_ref[rows, :] = jnp.where(hi_ref[rows, :] == t16, lo_ref[rows, :], jnp.int16(LOW))
        return carry

    lax.fori_loop(0, nchunks, p2b, 0)

    def search_lo(i, v):
        cand = v | lax.shift_left(jnp.int32(1), HALF - 1 - i)
        c16 = to16(cand)
        cnt = count16(lo_ref, lom_ref, lambda blk: blk >= c16)
        return jnp.where(cnt >= need, cand, v)

    u_lo = lax.fori_loop(0, HALF, search_lo, jnp.zeros((1, Q_TILE), I32))
    thr_key = lax.shift_left(u_hi + LOW, HALF) | u_lo
    thr_bits = jnp.where(thr_key < 0, thr_key ^ jnp.int32(0x7FFFFFFF), thr_key)
    thr = lax.bitcast_convert_type(thr_bits, F32)

    cnt_ge = count(lambda blk, c: blk >= thr)
    tie = (cnt_ge > TOPK_MAX) & (thr > -jnp.inf)
    any_tie = jnp.max(tie.astype(I32))
    cst_ref[...] = jnp.full((SUBLANES, Q_TILE), NO_CUT, I32)

    def pos_of(c, shape):
        r = lax.broadcasted_iota(I32, shape, 0)
        return jnp.where(c < 0, r, r + N_META + c * K_CHUNK)

    @pl.when(any_tie > 0)
    def _():
        cnt_gt = count(lambda blk, c: blk > thr)
        need = TOPK_MAX - cnt_gt

        def bis_pos(i, cs):
            cand = cs | lax.shift_left(jnp.int32(1), POS_BITS - 1 - i)
            f = count(lambda blk, c: (blk == thr) & (pos_of(c, blk.shape) < cand))
            return jnp.where(f <= need, cand, cs)

        cs = lax.fori_loop(0, POS_BITS, bis_pos, jnp.zeros((1, Q_TILE), I32))
        cs = jnp.where(tie, cs, NO_CUT)
        cst_ref[...] = jnp.broadcast_to(cs, (SUBLANES, Q_TILE))

    def to_mask(blk, c, with_ties):
        if with_ties:
            cs = cst_ref[0:1, :]
            sel = (blk > thr) | ((blk == thr) & (pos_of(c, blk.shape) < cs))
        else:
            sel = blk >= thr
        return jnp.where(sel, 0.0, NEG)

    for with_ties in (False, True):
        @pl.when((any_tie > 0) == with_ties)
        def _():
            def body(c, carry):
                rows = chunk_rows(c)
                sc_ref[rows, :] = to_mask(sc_ref[rows, :], c, with_ties)
                return carry
            lax.fori_loop(0, nchunks, body, 0)
            scm_ref[...] = to_mask(scm_ref[...], -1, with_ties)

    kindm = jnp.minimum(j, 1)
    heads = [slice(h * HEAD_DIM, (h + 1) * HEAD_DIM) for h in range(N_HEADS)]

    mx0 = []
    for h in range(N_HEADS):
        lm = (jnp.dot(km_ref[h], qt_ref[h], preferred_element_type=F32)
              + tabm_ref[kindm, h] + scm_ref[...])
        lm_ref[h] = lm
        mx0.append(_fold_rows(lm, jnp.max))

    def pass_a(c, mx):
        rows = chunk_rows(c)
        kind = jnp.minimum(j - c, 2)
        mask = sc_ref[rows, :]
        out = []
        for h in range(N_HEADS):
            l = (jnp.dot(k_ref[h, rows, :], qt_ref[h], preferred_element_type=F32)
                 + tab_ref[kind, h] + mask)
            l_ref[h, rows, :] = l
            out.append(jnp.maximum(mx[h], _fold_rows(l, jnp.max)))
        return tuple(out)

    mx = paired_chunk_loop(pass_a, tuple(mx0))
    m = [jnp.max(mx[h], axis=0, keepdims=True) for h in range(N_HEADS)]

    den0 = []
    for h in range(N_HEADS):
        pm = jnp.exp2(lm_ref[h] - m[h])
        den0.append(_fold_rows(pm, jnp.sum))
        ot_ref[heads[h], :] = jnp.dot(vtm_ref[heads[h], :], pm.astype(BF16), preferred_element_type=F32)

    def pass_b(c, den):
        rows = chunk_rows(c)
        out = []
        for h in range(N_HEADS):
            p = jnp.exp2(l_ref[h, rows, :] - m[h])
            out.append(den[h] + _fold_rows(p, jnp.sum))
            ot_ref[heads[h], :] += jnp.dot(vt_ref[heads[h], rows], p.astype(BF16),
                                           preferred_element_type=F32)
        return tuple(out)

    den = paired_chunk_loop(pass_b, tuple(den0))
    for h in range(N_HEADS):
        ot_ref[heads[h], :] = ot_ref[heads[h], :] / jnp.sum(den[h], axis=0, keepdims=True)
    o_ref[...] = ot_ref[...].T.astype(BF16)


def _attention(q3, iq3, iwt, k3, ik, vt, km3, ikm, vtm, tab, tabm, batch, seq):
    n = batch * seq
    tiles = seq // Q_TILE
    full = lambda a: pl.BlockSpec(a.shape, lambda b, j: (0,) * a.ndim, pipeline_mode=pl.Buffered(1))
    return pl.pallas_call(
        _attn_kernel,
        grid=(batch, tiles),
        in_specs=[
            pl.BlockSpec((N_HEADS, Q_TILE, HEAD_DIM), lambda b, j: (0, b * tiles + j, 0)),
            pl.BlockSpec((IDX_HEADS, Q_TILE, IDX_DIM), lambda b, j: (0, b * tiles + j, 0)),
            pl.BlockSpec((IDX_HEADS, Q_TILE), lambda b, j: (0, b * tiles + j)),
            pl.BlockSpec((N_HEADS, seq, HEAD_DIM), lambda b, j: (0, b, 0)),
            pl.BlockSpec((seq, IDX_DIM), lambda b, j: (b, 0)),
            pl.BlockSpec((ATTN_WIDTH, seq), lambda b, j: (0, b)),
            full(km3), full(ikm), full(vtm), full(tab), full(tabm),
        ],
        out_specs=pl.BlockSpec((Q_TILE, ATTN_WIDTH), lambda b, j: (b * tiles + j, 0)),
        out_shape=jax.ShapeDtypeStruct((n, ATTN_WIDTH), BF16),
        scratch_shapes=[
            pltpu.VMEM((seq, Q_TILE), F32),
            pltpu.VMEM((N_META, Q_TILE), F32),
            pltpu.VMEM((N_HEADS, seq, Q_TILE), F32),
            pltpu.VMEM((N_HEADS, N_META, Q_TILE), F32),
            pltpu.VMEM((ATTN_WIDTH, Q_TILE), F32),
            pltpu.VMEM((SUBLANES, Q_TILE), I32),
            pltpu.VMEM((seq, Q_TILE), jnp.int16),
            pltpu.VMEM((seq, Q_TILE), jnp.int16),
            pltpu.VMEM((N_META, Q_TILE), jnp.int16),
            pltpu.VMEM((N_META, Q_TILE), jnp.int16),
            pltpu.VMEM((N_HEADS, HEAD_DIM, Q_TILE), BF16),
            pltpu.VMEM((IDX_HEADS, IDX_DIM, Q_TILE), BF16),
        ],
        compiler_params=pltpu.CompilerParams(dimension_semantics=("arbitrary", "arbitrary"),
                                             vmem_limit_bytes=VMEM_LIMIT),
        name="attn",
    )(q3, iq3, iwt, k3, ik, vt, km3, ikm, vtm, tab, tabm)


def _mix_kernel(x_ref, attn_ref, y_ref, yprev_ref, ymeta_ref, gate_ref,
                cw_ref, cb_ref, lng_ref, lnb_ref, wco_ref, wao_ref, wout_ref, nf_ref, wrt_ref, br_ref,
                h2_ref, hn2_ref, eid_ref, rank_ref, gcol_ref, cnt_ref,
                win_ref, shift_ref, base_ref, *, tiles_per_seq):
    i = pl.program_id(0)

    @pl.when(i == 0)
    def _():
        base_ref[...] = jnp.zeros_like(base_ref)

    first = (i % tiles_per_seq) == 0
    win_ref[0:HALO, :] = jnp.where(first, ymeta_ref[...], yprev_ref[...])
    win_ref[HALO:, :] = y_ref[...]
    lead = HALO - (CONV_WIDTH - 1)
    for b in range(SUBLANES):
        span = MIX_TILE + SUBLANES * (len(range(b, CONV_WIDTH, SUBLANES)) - 1)
        shift_ref[b, 0:span, :] = win_ref[pl.ds(lead + b, span), :]
    for hh in range(MIX_TILE // MIX_ROWS):
        _mix_rows(hh * MIX_ROWS, x_ref, attn_ref, gate_ref, cw_ref, cb_ref, lng_ref, lnb_ref, wco_ref, wao_ref,
                  wout_ref, nf_ref, wrt_ref, br_ref, h2_ref, hn2_ref, eid_ref, rank_ref, gcol_ref, cnt_ref,
                  shift_ref, base_ref)


def _mix_rows(r_off, x_ref, attn_ref, gate_ref, cw_ref, cb_ref, lng_ref, lnb_ref, wco_ref, wao_ref,
              wout_ref, nf_ref, wrt_ref, br_ref, h2_ref, hn2_ref, eid_ref, rank_ref, gcol_ref, cnt_ref,
              shift_ref, base_ref):
    rows = slice(r_off, r_off + MIX_ROWS)
    blocks = []
    for r0 in range(r_off, r_off + MIX_ROWS, CONV_ROWS):
        acc = jnp.broadcast_to(cb_ref[...], (CONV_ROWS, CONV_CH))
        for b in range(SUBLANES):
            for a, w in enumerate(range(b, CONV_WIDTH, SUBLANES)):
                r1 = r0 + SUBLANES * a
                acc = acc + cw_ref[w:w + 1, :] * shift_ref[b, r1:r1 + CONV_ROWS, :]
        blocks.append(acc)
    yc = jnp.concatenate(blocks, axis=0)
    mu = jnp.mean(yc, axis=-1, keepdims=True)
    var = jnp.mean(jnp.square(yc - mu), axis=-1, keepdims=True)
    yn = (yc - mu) * lax.rsqrt(var + EPS) * lng_ref[...] + lnb_ref[...]
    ys = yn * _sigmoid(yn)
    y_b = jnp.dot(ys.astype(BF16), wco_ref[...], preferred_element_type=F32)

    y_a = jnp.dot(attn_ref[rows, :], wao_ref[...], preferred_element_type=F32)
    merged = gate_ref[rows, :D_MODEL] * y_a + gate_ref[rows, D_MODEL:] * y_b
    h2 = x_ref[rows, :] + jnp.dot(merged.astype(BF16), wout_ref[...], preferred_element_type=F32)
    h2_ref[rows, :] = h2
    ms = jnp.mean(h2 * h2, axis=-1, keepdims=True)
    hn2 = h2 * lax.rsqrt(ms + EPS) * nf_ref[...]
    hn2_ref[rows, 0, :] = _pack_bf16_pairs(hn2)

    logits = lax.dot_general(wrt_ref[...], hn2, NT_DIMS, preferred_element_type=F32,
                             precision=lax.Precision.HIGHEST) + br_ref[...]
    erow = lax.broadcasted_iota(I32, (N_EXPERTS, MIX_ROWS), 0)
    vals, ids = [], []
    l = logits
    for _ in range(TOP_K_EXPERTS):
        m = jnp.max(l, axis=0, keepdims=True)
        idx = jnp.min(jnp.where(l == m, erow, N_EXPERTS), axis=0, keepdims=True)
        vals.append(m)
        ids.append(idx)
        l = jnp.where(erow == idx, -jnp.inf, l)
    ex = [jnp.exp(v - vals[0]) for v in vals]
    den = ex[0] + ex[1] + ex[2] + ex[3]
    gates = [e / den for e in ex]

    onehot = [(erow == idx) for idx in ids]
    oh = jnp.concatenate([jnp.where(o, 1.0, 0.0) for o in onehot], axis=0)
    tr = lax.broadcasted_iota(I32, (MIX_ROWS, MIX_ROWS), 0)
    tc = lax.broadcasted_iota(I32, (MIX_ROWS, MIX_ROWS), 1)
    upper = jnp.where(tr <= tc, 1.0, 0.0).astype(BF16)
    pref = jnp.dot(oh.astype(BF16), upper, preferred_element_type=F32)
    offs = base_ref[:, 0:1]
    ranks = []
    for kk in range(TOP_K_EXPERTS):
        pk = pref[kk * N_EXPERTS:(kk + 1) * N_EXPERTS, :]
        r = jnp.sum(jnp.where(onehot[kk], offs + pk - 1.0, 0.0), axis=0, keepdims=True)
        ranks.append(r.astype(I32))
        offs = offs + pk[:, MIX_ROWS - 1:MIX_ROWS]
    base_ref[...] = jnp.broadcast_to(offs, base_ref.shape)
    cnt_ref[...] = jnp.broadcast_to(offs, cnt_ref.shape)

    zi = jnp.zeros((SUBLANES - TOP_K_EXPERTS, MIX_ROWS), I32)
    eid_ref[:, rows] = jnp.concatenate(ids + [zi], axis=0)
    rank_ref[:, rows] = jnp.concatenate(ranks + [zi], axis=0)
    g8 = jnp.concatenate(gates + [jnp.zeros((LANES - TOP_K_EXPERTS, MIX_ROWS), F32)], axis=0)
    gcol_ref[rows, :] = g8.T


def _mix(x2d, attn, y, ymeta, gate, wts, seq):
    n = x2d.shape[0]
    tiles_per_seq = seq // MIX_TILE
    halo_per_tile = MIX_TILE // HALO
    full = lambda a: pl.BlockSpec(a.shape, lambda i: (0,) * a.ndim)
    row = lambda w: pl.BlockSpec((MIX_TILE, w), lambda i: (i, 0))
    lane = lambda r: pl.BlockSpec((r, MIX_TILE), lambda i: (0, i))
    out_shape = (
        jax.ShapeDtypeStruct((n, D_MODEL), F32),
        jax.ShapeDtypeStruct((n, 1, D_MODEL // 2), I32),
        jax.ShapeDtypeStruct((SUBLANES, n), I32),
        jax.ShapeDtypeStruct((SUBLANES, n), I32),
        jax.ShapeDtypeStruct((n, LANES), F32),
        jax.ShapeDtypeStruct((N_EXPERTS, LANES), F32),
    )
    out_specs = (row(D_MODEL), pl.BlockSpec((MIX_TILE, 1, D_MODEL // 2), lambda i: (i, 0, 0)),
                 lane(SUBLANES), lane(SUBLANES), row(LANES), full(out_shape[5]))
    return pl.pallas_call(
        functools.partial(_mix_kernel, tiles_per_seq=tiles_per_seq),
        grid=(n // MIX_TILE,),
        in_specs=[row(D_MODEL), row(ATTN_WIDTH), row(CONV_CH),
                  pl.BlockSpec((HALO, CONV_CH), lambda i: (jnp.maximum(i * halo_per_tile - 1, 0), 0)),
                  full(ymeta), row(2 * D_MODEL)] + [full(w) for w in wts],
        out_specs=out_specs,
        out_shape=out_shape,
        scratch_shapes=[pltpu.VMEM((HALO + MIX_TILE, CONV_CH), F32),
                        pltpu.VMEM((SUBLANES, HALO + MIX_TILE, CONV_CH), F32),
                        pltpu.VMEM((N_EXPERTS, LANES), F32)],
        compiler_params=pltpu.CompilerParams(dimension_semantics=("arbitrary",),
                                             vmem_limit_bytes=VMEM_LIMIT),
        name="mix",
    )(x2d, attn, y, y, ymeta, gate, *wts)


def _slots_kernel(starts_ref, eid_ref, rank_ref, slot_ref):
    eid = eid_ref[...]
    base = jnp.zeros(eid.shape, I32)
    for e in range(N_EXPERTS):
        base = jnp.where(eid == e, starts_ref[e], base)
    slot = base + rank_ref[...]
    for t in range(slot_ref.shape[0]):
        for kk in range(TOP_K_EXPERTS):
            slot_ref[t, :, kk * ROW_TILE:(kk + 1) * ROW_TILE] = slot[kk:kk + 1, t * ROW_TILE:(t + 1) * ROW_TILE]


def _slots(starts, eid8, rank8):
    n = eid8.shape[1]
    tiles = 8
    spec = pl.BlockSpec((SUBLANES, tiles * ROW_TILE), lambda i: (0, i))
    return pl.pallas_call(
        _slots_kernel,
        grid=(n // (tiles * ROW_TILE),),
        in_specs=[pl.BlockSpec(memory_space=pltpu.SMEM), spec, spec],
        out_specs=pl.BlockSpec((tiles, 1, COPIES_PER_TILE), lambda i: (i, 0, 0)),
        out_shape=jax.ShapeDtypeStruct((n // ROW_TILE, 1, COPIES_PER_TILE), I32),
        compiler_params=pltpu.CompilerParams(dimension_semantics=("arbitrary",)),
        name="slots",
    )(starts, eid8, rank8)


COPIES_PER_TILE = TOP_K_EXPERTS * ROW_TILE
SLOT_UNROLL = 16


def _slot_source_kernel(slot_ref, init_ref, src_ref, sem, *, n):
    i = pl.program_id(0)

    @pl.when(i == 0)
    def _():
        cp = pltpu.make_async_copy(init_ref, src_ref, sem)
        cp.start()
        cp.wait()

    for kk in range(TOP_K_EXPERTS):
        def body(g, carry):
            for u in range(SLOT_UNROLL):
                t = g * SLOT_UNROLL + u
                src_ref[slot_ref[0, 0, kk * ROW_TILE + t]] = kk * n + i * ROW_TILE + t
            return carry
        lax.fori_loop(0, ROW_TILE // SLOT_UNROLL, body, 0)


def _slot_sources(slot_tiles, init, n):
    return pl.pallas_call(
        functools.partial(_slot_source_kernel, n=n),
        grid=(slot_tiles.shape[0],),
        in_specs=[pl.BlockSpec((1, 1, COPIES_PER_TILE), lambda i: (i, 0, 0), memory_space=pltpu.SMEM),
                  pl.BlockSpec(memory_space=pl.ANY)],
        out_specs=pl.BlockSpec(memory_space=pltpu.SMEM),
        out_shape=jax.ShapeDtypeStruct(init.shape, I32),
        scratch_shapes=[pltpu.SemaphoreType.DMA(())],
        compiler_params=pltpu.CompilerParams(dimension_semantics=("arbitrary",)),
        name="slot_sources",
    )(slot_tiles, init)


def _expert_kernel(te_ref, nused_ref, src_ref, src_next_ref, src_prev_ref, hn_hbm,
                   wug_ref, bug_ref, wd_ref, bd_ref, yk_hbm, xbuf, ybuf, wug_bf, wd_bf, gsem, ssem, *, n):
    i = pl.program_id(0)
    nused = nused_ref[0]

    def gather_copy(src, r, b):
        tok = src[0, 0, r] & (n - 1)
        return pltpu.make_async_copy(hn_hbm.at[tok], xbuf.at[b, r], gsem.at[b])

    def scatter_copy(src, r, b):
        return pltpu.make_async_copy(ybuf.at[b, r], yk_hbm.at[src[0, 0, r]], ssem.at[b])

    @pl.when(i <= nused)
    def _():
        cur = i % 2
        oth = 1 - cur

        def wait_scatter(b):
            pltpu.make_async_copy(ybuf.at[b], yk_hbm.at[pl.ds(0, ROW_TILE)], ssem.at[b]).wait()

        @pl.when(i == 0)
        def _():
            for r in range(ROW_TILE):
                gather_copy(src_ref, r, 0).start()
            ybuf[1] = jnp.zeros(ybuf.shape[1:], F32)
            spare = pltpu.make_async_copy(ybuf.at[1], yk_hbm.at[pl.ds(TOP_K_EXPERTS * n, ROW_TILE)],
                                          ssem.at[1])
            spare.start()
            spare.wait()

        @pl.when(i < nused)
        def _():
            for r in range(ROW_TILE):
                gather_copy(src_next_ref, r, oth).start()

        @pl.when(i > 0)
        def _():
            for r in range(ROW_TILE):
                scatter_copy(src_prev_ref, r, oth).start()

        pltpu.make_async_copy(hn_hbm.at[pl.ds(0, ROW_TILE)], xbuf.at[cur], gsem.at[cur]).wait()

        prev = te_ref[jnp.maximum(i - 1, 0)]
        fresh = (i == 0) | (te_ref[jnp.minimum(i, nused - 1)] != prev)

        @pl.when(fresh)
        def _():
            wug_bf[...] = wug_ref[0].astype(BF16)
            wd_bf[...] = wd_ref[0].astype(BF16)

        xb = _unpack_bf16_pairs(xbuf[cur, :, 0, :])
        ug = jnp.dot(xb, wug_bf[...], preferred_element_type=F32) + bug_ref[0]
        gate = jnp.minimum(ug[:, :D_FF], SWIGLU_LIMIT)
        up = jnp.clip(ug[:, D_FF:], -SWIGLU_LIMIT, SWIGLU_LIMIT)
        act = (up + 1.0) * gate * _sigmoid(SWIGLU_ALPHA * gate)
        y = jnp.dot(act.astype(BF16), wd_bf[...], preferred_element_type=F32) + bd_ref[0]

        @pl.when(i > 1)
        def _():
            wait_scatter(cur)

        ybuf[cur, :, 0, :] = y

        @pl.when((i == nused) & (i > 0))
        def _():
            wait_scatter(oth)


def _experts(tile_expert, nused, src_tiles, hn2p, w_ug, b_ug, w_down, b_down, n):
    ntiles = src_tiles.shape[0]
    clamp = lambda i, nu: jnp.maximum(jnp.minimum(i, nu[0] - 1), 0)
    smem = lambda f: pl.BlockSpec((1, 1, ROW_TILE), f, memory_space=pltpu.SMEM)
    expert = lambda i, te, nu: (te[clamp(i, nu)], 0, 0)
    grid_spec = pltpu.PrefetchScalarGridSpec(
        num_scalar_prefetch=2,
        grid=(ntiles + 1,),
        in_specs=[
            smem(lambda i, te, nu: (clamp(i, nu), 0, 0)),
            smem(lambda i, te, nu: (clamp(i + 1, nu), 0, 0)),
            smem(lambda i, te, nu: (clamp(i - 1, nu), 0, 0)),
            pl.BlockSpec(memory_space=pl.ANY),
            pl.BlockSpec((1, D_MODEL, 2 * D_FF), expert),
            pl.BlockSpec((1, 1, 2 * D_FF), expert),
            pl.BlockSpec((1, D_FF, D_MODEL), expert),
            pl.BlockSpec((1, 1, D_MODEL), expert),
        ],
        out_specs=pl.BlockSpec(memory_space=pl.ANY),
        scratch_shapes=[pltpu.VMEM((2, ROW_TILE, 1, D_MODEL // 2), I32),
                        pltpu.VMEM((2, ROW_TILE, 1, D_MODEL), F32),
                        pltpu.VMEM((D_MODEL, 2 * D_FF), BF16), pltpu.VMEM((D_FF, D_MODEL), BF16),
                        pltpu.SemaphoreType.DMA((2,)), pltpu.SemaphoreType.DMA((2,))],
    )
    return pl.pallas_call(
        functools.partial(_expert_kernel, n=n),
        grid_spec=grid_spec,
        out_shape=jax.ShapeDtypeStruct((TOP_K_EXPERTS * n + ROW_TILE, 1, D_MODEL), F32),
        compiler_params=pltpu.CompilerParams(dimension_semantics=("arbitrary",),
                                             vmem_limit_bytes=VMEM_LIMIT),
        name="experts",
    )(tile_expert, nused, src_tiles, src_tiles, src_tiles, hn2p,
      w_ug, b_ug[:, None, :], w_down, b_down[:, None, :])


def _final_kernel(h2_ref, gcol_ref, nf_ref, *rest):
    y_refs, o_ref = rest[:TOP_K_EXPERTS], rest[TOP_K_EXPERTS]
    h = h2_ref[...]
    for kk in range(TOP_K_EXPERTS):
        h = h + gcol_ref[:, kk:kk + 1] * y_refs[kk][:, 0, :]
    ms = jnp.mean(h * h, axis=-1, keepdims=True)
    o_ref[...] = h * lax.rsqrt(ms + EPS) * nf_ref[...]


def _final(h2, gcol, nf, yk):
    n = h2.shape[0]
    ntiles = n // ROW_TILE
    choice = lambda kk: pl.BlockSpec((ROW_TILE, 1, D_MODEL), lambda i: (kk * ntiles + i, 0, 0))
    return pl.pallas_call(
        _final_kernel,
        grid=(ntiles,),
        in_specs=[pl.BlockSpec((ROW_TILE, D_MODEL), lambda i: (i, 0)),
                  pl.BlockSpec((ROW_TILE, LANES), lambda i: (i, 0)),
                  pl.BlockSpec((1, D_MODEL), lambda i: (0, 0))] + [choice(kk) for kk in range(TOP_K_EXPERTS)],
        out_specs=pl.BlockSpec((ROW_TILE, D_MODEL), lambda i: (i, 0)),
        out_shape=jax.ShapeDtypeStruct((n, D_MODEL), F32),
        compiler_params=pltpu.CompilerParams(dimension_semantics=("arbitrary",),
                                             vmem_limit_bytes=VMEM_LIMIT),
        name="final",
    )(h2, gcol, nf, *([yk] * TOP_K_EXPERTS))


def _split_w_in(w_in, b_gate):
    c = ATTN_WIDTH
    o = 0
    wq = w_in[:, o:o + c]; o += c
    wk = w_in[:, o:o + c]; o += c
    wv = w_in[:, o:o + c]; o += c
    wiq = w_in[:, o:o + IDX_HEADS * IDX_DIM]; o += IDX_HEADS * IDX_DIM
    wik = w_in[:, o:o + IDX_DIM]; o += IDX_DIM
    wiw = w_in[:, o:o + IDX_HEADS]; o += IDX_HEADS
    wglu = w_in[:, o:o + 2 * CONV_CH]; o += 2 * CONV_CH
    wgate = w_in[:, o:]
    wiwt = jnp.concatenate([wiw.T, jnp.zeros((PACKED_ROWS - IDX_HEADS, D_MODEL), w_in.dtype)], axis=0)
    bf = lambda a: a.astype(BF16)
    return (bf(wq), bf(wk), bf(wiq), bf(wik), bf(wv.T), bf(wiwt), bf(wglu), bf(wgate),
            b_gate[None, :].astype(F32))


def kernel(x, meta_tokens, rel_bias, norm_mix, w_in, b_gate, w_attn_out, conv_w, conv_b, conv_ln_g, conv_ln_b, w_conv_out, w_out, norm_ffn, w_router, b_router, w_up_gate, b_up_gate, w_down, b_down, norm_final):
    batch, seq, d = x.shape
    n = batch * seq
    assert d == D_MODEL and seq % MIX_TILE == 0 and seq % (COUNT_CHUNKS * K_CHUNK) == 0 and n % PROJ_TILE == 0
    assert n & (n - 1) == 0, "slot sources keep the token index in the low bits of choice * n + token"
    assert min(TOPK_MAX, seq // 4) == TOPK_MAX and N_META + seq < 2 ** POS_BITS
    x2d = x.reshape(n, d)

    wts = _split_w_in(w_in[0], b_gate[0])
    g_mix = norm_mix[0][None, :]
    q3, k3, iq3, ik, vt, iwt, y, gate = _project(x2d, g_mix, wts, PROJ_TILE)
    _, km3, _, ikm, vtm, _, ym, _ = _project(meta_tokens.astype(F32), g_mix, wts, N_META)

    tab, tabm = _bias_tables(rel_bias.astype(F32))
    attn = _attention(q3, iq3, iwt, k3, ik, vt, km3, ikm, vtm, tab, tabm, batch, seq)

    ymeta = jnp.concatenate([jnp.zeros((HALO - N_META, CONV_CH), F32), ym], axis=0)
    cw = jnp.concatenate([conv_w[0], jnp.zeros((-CONV_WIDTH % SUBLANES, CONV_CH), F32)], axis=0)
    mix_w = (cw, conv_b[0][None, :], conv_ln_g[0][None, :], conv_ln_b[0][None, :],
             w_conv_out[0].astype(BF16), w_attn_out[0].astype(BF16), w_out[0].astype(BF16),
             norm_ffn[0][None, :], w_router[0].T, b_router[0][:, None])
    h2, hn2, eid8, rank8, gcol, cnt = _mix(x2d, attn, y, ymeta, gate, mix_w, seq)

    counts = cnt[:, 0].astype(I32)
    padded = ((counts + ROW_TILE - 1) // ROW_TILE) * ROW_TILE
    ends = jnp.cumsum(padded)
    starts = ends - padded
    slot_tiles = _slots(starts, eid8, rank8)
    nslots = n * TOP_K_EXPERTS + N_EXPERTS * ROW_TILE
    ntiles = nslots // ROW_TILE
    tile_start = jnp.arange(ntiles, dtype=I32) * ROW_TILE
    nused = (ends[-1] // ROW_TILE).astype(I32)
    last_start = jnp.maximum(ends[-1] - ROW_TILE, 0)
    tile_expert = jnp.sum((jnp.minimum(tile_start, last_start)[:, None] >= ends[None, :]).astype(I32), axis=1)

    spare = TOP_K_EXPERTS * n + jnp.arange(nslots, dtype=I32) % ROW_TILE
    src_tiles = _slot_sources(slot_tiles, spare, n).reshape(ntiles, 1, ROW_TILE)
    yk = _experts(tile_expert, nused[None], src_tiles, hn2, w_up_gate[0], b_up_gate[0], w_down[0], b_down[0], n)
    out = _final(h2, gcol, norm_final[None, :], yk)
    return out.reshape(batch, seq, d)
```

```python
import functools
import math

import jax
import jax.numpy as jnp
from jax import lax
from jax.experimental import pallas as pl
from jax.experimental.pallas import tpu as pltpu

F32 = jnp.float32
BF16 = jnp.bfloat16
I32 = jnp.int32

D_MODEL = 1024
N_META = 16
N_HEADS = 8
HEAD_DIM = 64
ATTN_WIDTH = N_HEADS * HEAD_DIM
IDX_HEADS = 8
IDX_DIM = 64
TOPK_MAX = 256
CONV_CH = 512
CONV_WIDTH = 31
N_BUCKETS = 32
MAX_DISTANCE = 128
N_EXPERTS = 32
TOP_K_EXPERTS = 4
D_FF = 1024
SWIGLU_LIMIT = 7.0
SWIGLU_ALPHA = 1.702
EPS = 1e-6
IDX_SCALE = (IDX_DIM ** -0.5) * (IDX_HEADS ** -0.5)

SUBLANES = 8
LANES = 128
PACKED_ROWS = 16
ROW_TILE = 256
MIX_TILE = 512
MIX_ROWS = 256
PROJ_TILE = 512
Q_TILE = 256
K_CHUNK = 256
COUNT_CHUNKS = 2
POS_BITS = 12
HALO = 32
CONV_ROWS = 32
NEG = -1e30
LOG2E = math.log2(math.e)
VMEM_LIMIT = 56 * 1024 * 1024

NT_DIMS = (((1,), (1,)), ((), ()))


def _sigmoid(x):
    return 1.0 / (1.0 + jnp.exp(-x))


def _pack_bf16_pairs(x):
    w = x.shape[1] // 2
    hi = lax.bitcast_convert_type(x[:, :w].astype(BF16).astype(F32), jnp.uint32)
    lo = lax.bitcast_convert_type(x[:, w:].astype(BF16).astype(F32), jnp.uint32)
    return lax.bitcast_convert_type(hi | (lo >> 16), I32)


def _unpack_bf16_pairs(p):
    u = lax.bitcast_convert_type(p, jnp.uint32)
    hi = lax.bitcast_convert_type(u & jnp.uint32(0xFFFF0000), F32)
    lo = lax.bitcast_convert_type(u << 16, F32)
    return jnp.concatenate([hi, lo], axis=1).astype(BF16)


def _proj_kernel(x_ref, g_ref, wq_ref, wk_ref, wiq_ref, wik_ref, wvt_ref, wiwt_ref, wglu_ref,
                 wgate_ref, bgate_ref,
                 q_ref, k_ref, iq_ref, ik_ref, vt_ref, iwt_ref, y_ref, gate_ref):
    x = x_ref[...]
    ms = jnp.mean(x * x, axis=-1, keepdims=True)
    xn = (x * lax.rsqrt(ms + EPS) * g_ref[...]).astype(BF16)

    q = jnp.dot(xn, wq_ref[...], preferred_element_type=F32) * (HEAD_DIM ** -0.5 * LOG2E)
    k = jnp.dot(xn, wk_ref[...], preferred_element_type=F32)
    iq = jnp.dot(xn, wiq_ref[...], preferred_element_type=F32)
    for h in range(N_HEADS):
        sl = slice(h * HEAD_DIM, (h + 1) * HEAD_DIM)
        q_ref[h] = q[:, sl].astype(BF16)
        k_ref[h] = k[:, sl].astype(BF16)
        iq_ref[h] = iq[:, sl].astype(BF16)
    ik_ref[...] = jnp.dot(xn, wik_ref[...], preferred_element_type=F32).astype(BF16)
    vt_ref[...] = lax.dot_general(wvt_ref[...], xn, NT_DIMS, preferred_element_type=F32).astype(BF16)
    iwt = lax.dot_general(wiwt_ref[...], xn, NT_DIMS, preferred_element_type=F32)
    iwt_ref[...] = iwt[:IDX_HEADS] * IDX_SCALE
    glu = jnp.dot(xn, wglu_ref[...], preferred_element_type=F32)
    y_ref[...] = glu[:, :CONV_CH] * _sigmoid(glu[:, CONV_CH:])
    gate = jnp.dot(xn, wgate_ref[...], preferred_element_type=F32) + bgate_ref[...]
    gate_ref[...] = _sigmoid(gate)


def _project(x2d, g, wts, tm):
    n = x2d.shape[0]
    wq, wk, wiq, wik, wvt, wiwt, wglu, wgate, bgate = wts
    full = lambda a: pl.BlockSpec(a.shape, lambda i: (0,) * a.ndim, pipeline_mode=pl.Buffered(1))
    out_shape = (
        jax.ShapeDtypeStruct((N_HEADS, n, HEAD_DIM), BF16),
        jax.ShapeDtypeStruct((N_HEADS, n, HEAD_DIM), BF16),
        jax.ShapeDtypeStruct((IDX_HEADS, n, IDX_DIM), BF16),
        jax.ShapeDtypeStruct((n, IDX_DIM), BF16),
        jax.ShapeDtypeStruct((ATTN_WIDTH, n), BF16),
        jax.ShapeDtypeStruct((IDX_HEADS, n), F32),
        jax.ShapeDtypeStruct((n, CONV_CH), F32),
        jax.ShapeDtypeStruct((n, 2 * D_MODEL), F32),
    )
    out_specs = (
        pl.BlockSpec((N_HEADS, tm, HEAD_DIM), lambda i: (0, i, 0)),
        pl.BlockSpec((N_HEADS, tm, HEAD_DIM), lambda i: (0, i, 0)),
        pl.BlockSpec((IDX_HEADS, tm, IDX_DIM), lambda i: (0, i, 0)),
        pl.BlockSpec((tm, IDX_DIM), lambda i: (i, 0)),
        pl.BlockSpec((ATTN_WIDTH, tm), lambda i: (0, i)),
        pl.BlockSpec((IDX_HEADS, tm), lambda i: (0, i)),
        pl.BlockSpec((tm, CONV_CH), lambda i: (i, 0)),
        pl.BlockSpec((tm, 2 * D_MODEL), lambda i: (i, 0)),
    )
    return pl.pallas_call(
        _proj_kernel,
        grid=(n // tm,),
        in_specs=[pl.BlockSpec((tm, D_MODEL), lambda i: (i, 0)), full(g), full(wq), full(wk), full(wiq),
                  full(wik), full(wvt), full(wiwt), full(wglu), full(wgate), full(bgate)],
        out_specs=out_specs,
        out_shape=out_shape,
        compiler_params=pltpu.CompilerParams(dimension_semantics=("arbitrary",),
                                             vmem_limit_bytes=VMEM_LIMIT),
        name="proj",
    )(x2d, g, wq, wk, wiq, wik, wvt, wiwt, wglu, wgate, bgate)


def _t5_bucket(n):
    max_exact = N_BUCKETS // 2
    nf = jnp.maximum(n, 1).astype(F32)
    large = max_exact + (jnp.log(nf / max_exact) / math.log(MAX_DISTANCE / max_exact)
                         * (N_BUCKETS - max_exact)).astype(I32)
    large = jnp.minimum(large, N_BUCKETS - 1)
    return jnp.where(n < max_exact, n, large)


def _bias_lookup(rb_ref, dist, h):
    bucket = _t5_bucket(jnp.maximum(dist, 0))
    out = jnp.full(dist.shape, NEG, F32)
    for b in range(N_BUCKETS):
        out = jnp.where(bucket == b, rb_ref[b, h] * LOG2E, out)
    return jnp.where(dist >= 0, out, NEG)


def _bias_kernel(rb_ref, tab_ref, tabm_ref):
    kind = pl.program_id(0)
    r = pl.program_id(1)
    rows = tab_ref.shape[2]
    s = lax.broadcasted_iota(I32, (rows, Q_TILE), 0) + r * rows
    t = lax.broadcasted_iota(I32, (rows, Q_TILE), 1)
    dist = jnp.where(kind == 2, 2 * K_CHUNK, t - s + kind * K_CHUNK)
    for h in range(N_HEADS):
        tab_ref[0, h] = _bias_lookup(rb_ref, dist, h)
    m = lax.broadcasted_iota(I32, (N_META, Q_TILE), 0)
    tm_ = lax.broadcasted_iota(I32, (N_META, Q_TILE), 1)
    distm = jnp.where(kind == 0, N_META + tm_ - m, 2 * K_CHUNK)
    for h in range(N_HEADS):
        tabm_ref[0, h] = _bias_lookup(rb_ref, distm, h)


def _bias_tables(rel_bias):
    rows = 64
    return pl.pallas_call(
        _bias_kernel,
        grid=(3, K_CHUNK // rows),
        in_specs=[pl.BlockSpec(memory_space=pltpu.SMEM)],
        out_specs=(pl.BlockSpec((1, N_HEADS, rows, Q_TILE), lambda kd, r: (kd, 0, r, 0)),
                   pl.BlockSpec((1, N_HEADS, N_META, Q_TILE), lambda kd, r: (kd, 0, 0, 0))),
        out_shape=(jax.ShapeDtypeStruct((3, N_HEADS, K_CHUNK, Q_TILE), F32),
                   jax.ShapeDtypeStruct((3, N_HEADS, N_META, Q_TILE), F32)),
        compiler_params=pltpu.CompilerParams(dimension_semantics=("arbitrary", "arbitrary")),
        name="bias_tables",
    )(rel_bias)


def _fold_rows(x, op):
    r, l = x.shape
    x3 = x.reshape(r // SUBLANES, SUBLANES, l)
    return op(x3, axis=0)


def _attn_kernel(q_ref, iq_ref, iwt_ref, k_ref, ik_ref, vt_ref, km_ref, ikm_ref, vtm_ref,
                 tab_ref, tabm_ref, o_ref, sc_ref, scm_ref, l_ref, lm_ref, ot_ref, cst_ref,
                 hi_ref, lo_ref, him_ref, lom_ref, qt_ref, iqt_ref):
    j = pl.program_id(1)
    nchunks = j + 1
    iw = iwt_ref[...]

    def chunk_rows(c):
        return pl.ds(pl.multiple_of(c * K_CHUNK, K_CHUNK), K_CHUNK)

    def paired_chunk_loop(body, init):
        def quad(i, c):
            for u in range(4):
                c = body(4 * i + u, c)
            return c

        carry = lax.fori_loop(0, lax.shift_right_logical(nchunks, 2), quad, init)
        base = nchunks & ~3
        carry = lax.cond((nchunks & 2) == 2, lambda c: body(base + 1, body(base, c)), lambda c: c, carry)
        return lax.cond((nchunks & 1) == 1, lambda c: body(nchunks - 1, c), lambda c: c, carry)

    for h in range(N_HEADS):
        qt_ref[h] = q_ref[h].astype(F32).T.astype(BF16)
        iqt_ref[h] = iq_ref[h].astype(F32).T.astype(BF16)

    I16 = jnp.int16
    HALF = 16
    LOW = -2 ** (HALF - 1)
    NO_CUT = 2 ** 30

    def split_key(s):
        bits = lax.bitcast_convert_type(s, I32)
        key = jnp.where(bits < 0, bits ^ jnp.int32(0x7FFFFFFF), bits)
        hi = lax.shift_right_arithmetic(key, HALF).astype(I16)
        lo = ((key & (2 ** HALF - 1)) + LOW).astype(I16)
        return hi, lo

    def idx_scores(ikc):
        acc = None
        for h in range(IDX_HEADS):
            s = jnp.dot(ikc, iqt_ref[h], preferred_element_type=F32)
            term = jnp.maximum(s, 0.0) * iw[h:h + 1, :]
            acc = term if acc is None else acc + term
        return acc

    scm_ref[...] = idx_scores(ikm_ref[...])

    row_minus_col = (lax.broadcasted_iota(I32, (K_CHUNK, Q_TILE), 0)
                     - lax.broadcasted_iota(I32, (K_CHUNK, Q_TILE), 1))

    def p1(c, carry):
        rows = chunk_rows(c)
        future = row_minus_col > jnp.where(c == j, 0, K_CHUNK)
        s = jnp.where(future, -jnp.inf, idx_scores(ik_ref[rows, :]))
        sc_ref[rows, :] = s
        hi_ref[rows, :], lo_ref[rows, :] = split_key(s)
        return carry

    paired_chunk_loop(p1, 0)

    def count(pred):
        def body(c, acc):
            blk = sc_ref[chunk_rows(c), :]
            return acc + _fold_rows(jnp.where(pred(blk, c), 1, 0).astype(I32), jnp.sum)
        acc = lax.fori_loop(0, nchunks, body, jnp.zeros((SUBLANES, Q_TILE), I32))
        acc = acc + _fold_rows(jnp.where(pred(scm_ref[...], -1), 1, 0).astype(I32), jnp.sum)
        return jnp.sum(acc, axis=0, keepdims=True)

    him_ref[...], lom_ref[...] = split_key(scm_ref[...])

    count_trips = lax.shift_right_logical(nchunks + COUNT_CHUNKS - 1, COUNT_CHUNKS.bit_length() - 1)

    def p2pad(c, carry):
        rows = chunk_rows(c)
        hi_ref[rows, :] = jnp.full((K_CHUNK, Q_TILE), LOW, I16)
        lo_ref[rows, :] = jnp.full((K_CHUNK, Q_TILE), LOW, I16)
        return carry

    lax.fori_loop(nchunks, count_trips * COUNT_CHUNKS, p2pad, 0)

    def tree_sum16(ind):
        parts = [ind[a * PACKED_ROWS:(a + 1) * PACKED_ROWS, :] for a in range(ind.shape[0] // PACKED_ROWS)]
        while len(parts) > 1:
            parts = [parts[a] + parts[a + 1] for a in range(0, len(parts), 2)]
        return parts[0]

    def count16(ref, mref, pred):
        one, zero = jnp.ones((), BF16), jnp.zeros((), BF16)

        def body(cc, acc):
            span = COUNT_CHUNKS * K_CHUNK
            rows = pl.ds(pl.multiple_of(cc * span, span), span)
            return acc + tree_sum16(jnp.where(pred(ref[rows, :]), one, zero)).astype(F32)
        acc = lax.fori_loop(0, count_trips, body, jnp.zeros((PACKED_ROWS, Q_TILE), F32))
        acc = acc + jnp.where(pred(mref[...]), one, zero).astype(F32)
        return jnp.sum(acc, axis=0, keepdims=True).astype(I32)

    def to16(u):
        return (u + LOW).astype(I16)

    def search_hi(i, u):
        cand = u | lax.shift_left(jnp.int32(1), HALF - 1 - i)
        c16 = to16(cand)
        cnt = count16(hi_ref, him_ref, lambda blk: blk >= c16)
        return jnp.where(cnt >= TOPK_MAX, cand, u)

    u_hi = lax.fori_loop(0, HALF, search_hi, jnp.zeros((1, Q_TILE), I32))
    t16 = to16(u_hi)
    need = TOPK_MAX - count16(hi_ref, him_ref, lambda blk: blk > t16)

    lom_ref[...] = jnp.where(him_ref[...] == t16, lom_ref[...], jnp.int16(LOW))

    def p2b(c, carry):
        rows = chunk_rows(c)
        lo_ref[rows, :] = jnp.where(hi_ref[rows, :] == t16, lo_ref[rows, :], jnp.int16(LOW))
        return carry

    lax.fori_loop(0, nchunks, p2b, 0)

    def search_lo(i, v):
        cand = v | lax.shift_left(jnp.int32(1), HALF - 1 - i)
        c16 = to16(cand)
        cnt = count16(lo_ref, lom_ref, lambda blk: blk >= c16)
        return jnp.where(cnt >= need, cand, v)

    u_lo = lax.fori_loop(0, HALF, search_lo, jnp.zeros((1, Q_TILE), I32))
    thr_key = lax.shift_left(u_hi + LOW, HALF) | u_lo
    thr_bits = jnp.where(thr_key < 0, thr_key ^ jnp.int32(0x7FFFFFFF), thr_key)
    thr = lax.bitcast_convert_type(thr_bits, F32)

    cnt_ge = count(lambda blk, c: blk >= thr)
    tie = (cnt_ge > TOPK_MAX) & (thr > -jnp.inf)
    any_tie = jnp.max(tie.astype(I32))
    cst_ref[...] = jnp.full((SUBLANES, Q_TILE), NO_CUT, I32)

    def pos_of(c, shape):
        r = lax.broadcasted_iota(I32, shape, 0)
        return jnp.where(c < 0, r, r + N_META + c * K_CHUNK)

    @pl.when(any_tie > 0)
    def _():
        cnt_gt = count(lambda blk, c: blk > thr)
        need = TOPK_MAX - cnt_gt

        def bis_pos(i, cs):
            cand = cs | lax.shift_left(jnp.int32(1), POS_BITS - 1 - i)
            f = count(lambda blk, c: (blk == thr) & (pos_of(c, blk.shape) < cand))
            return jnp.where(f <= need, cand, cs)

        cs = lax.fori_loop(0, POS_BITS, bis_pos, jnp.zeros((1, Q_TILE), I32))
        cs = jnp.where(tie, cs, NO_CUT)
        cst_ref[...] = jnp.broadcast_to(cs, (SUBLANES, Q_TILE))

    def to_mask(blk, c, with_ties):
        if with_ties:
            cs = cst_ref[0:1, :]
            sel = (blk > thr) | ((blk == thr) & (pos_of(c, blk.shape) < cs))
        else:
            sel = blk >= thr
        return jnp.where(sel, 0.0, NEG)

    for with_ties in (False, True):
        @pl.when((any_tie > 0) == with_ties)
        def _():
            def body(c, carry):
                rows = chunk_rows(c)
                sc_ref[rows, :] = to_mask(sc_ref[rows, :], c, with_ties)
                return carry
            lax.fori_loop(0, nchunks, body, 0)
            scm_ref[...] = to_mask(scm_ref[...], -1, with_ties)

    kindm = jnp.minimum(j, 1)
    heads = [slice(h * HEAD_DIM, (h + 1) * HEAD_DIM) for h in range(N_HEADS)]

    mx0 = []
    for h in range(N_HEADS):
        lm = (jnp.dot(km_ref[h], qt_ref[h], preferred_element_type=F32)
              + tabm_ref[kindm, h] + scm_ref[...])
        lm_ref[h] = lm
        mx0.append(_fold_rows(lm, jnp.max))

    def pass_a(c, mx):
        rows = chunk_rows(c)
        kind = jnp.minimum(j - c, 2)
        mask = sc_ref[rows, :]
        out = []
        for h in range(N_HEADS):
            l = (jnp.dot(k_ref[h, rows, :], qt_ref[h], preferred_element_type=F32)
                 + tab_ref[kind, h] + mask)
            l_ref[h, rows, :] = l
            out.append(jnp.maximum(mx[h], _fold_rows(l, jnp.max)))
        return tuple(out)

    mx = paired_chunk_loop(pass_a, tuple(mx0))
    m = [jnp.max(mx[h], axis=0, keepdims=True) for h in range(N_HEADS)]

    den0 = []
    for h in range(N_HEADS):
        pm = jnp.exp2(lm_ref[h] - m[h])
        den0.append(_fold_rows(pm, jnp.sum))
        ot_ref[heads[h], :] = jnp.dot(vtm_ref[heads[h], :], pm.astype(BF16), preferred_element_type=F32)

    def pass_b(c, den):
        rows = chunk_rows(c)
        out = []
        for h in range(N_HEADS):
            p = jnp.exp2(l_ref[h, rows, :] - m[h])
            out.append(den[h] + _fold_rows(p, jnp.sum))
            ot_ref[heads[h], :] += jnp.dot(vt_ref[heads[h], rows], p.astype(BF16),
                                           preferred_element_type=F32)
        return tuple(out)

    den = paired_chunk_loop(pass_b, tuple(den0))
    for h in range(N_HEADS):
        ot_ref[heads[h], :] = ot_ref[heads[h], :] / jnp.sum(den[h], axis=0, keepdims=True)
    o_ref[...] = ot_ref[...].T.astype(BF16)


def _attention(q3, iq3, iwt, k3, ik, vt, km3, ikm, vtm, tab, tabm, batch, seq):
    n = batch * seq
    tiles = seq // Q_TILE
    full = lambda a: pl.BlockSpec(a.shape, lambda b, j: (0,) * a.ndim, pipeline_mode=pl.Buffered(1))
    return pl.pallas_call(
        _attn_kernel,
        grid=(batch, tiles),
        in_specs=[
            pl.BlockSpec((N_HEADS, Q_TILE, HEAD_DIM), lambda b, j: (0, b * tiles + j, 0)),
            pl.BlockSpec((IDX_HEADS, Q_TILE, IDX_DIM), lambda b, j: (0, b * tiles + j, 0)),
            pl.BlockSpec((IDX_HEADS, Q_TILE), lambda b, j: (0, b * tiles + j)),
            pl.BlockSpec((N_HEADS, seq, HEAD_DIM), lambda b, j: (0, b, 0)),
            pl.BlockSpec((seq, IDX_DIM), lambda b, j: (b, 0)),
            pl.BlockSpec((ATTN_WIDTH, seq), lambda b, j: (0, b)),
            full(km3), full(ikm), full(vtm), full(tab), full(tabm),
        ],
        out_specs=pl.BlockSpec((Q_TILE, ATTN_WIDTH), lambda b, j: (b * tiles + j, 0)),
        out_shape=jax.ShapeDtypeStruct((n, ATTN_WIDTH), BF16),
        scratch_shapes=[
            pltpu.VMEM((seq, Q_TILE), F32),
            pltpu.VMEM((N_META, Q_TILE), F32),
            pltpu.VMEM((N_HEADS, seq, Q_TILE), F32),
            pltpu.VMEM((N_HEADS, N_META, Q_TILE), F32),
            pltpu.VMEM((ATTN_WIDTH, Q_TILE), F32),
            pltpu.VMEM((SUBLANES, Q_TILE), I32),
            pltpu.VMEM((seq, Q_TILE), jnp.int16),
            pltpu.VMEM((seq, Q_TILE), jnp.int16),
            pltpu.VMEM((N_META, Q_TILE), jnp.int16),
            pltpu.VMEM((N_META, Q_TILE), jnp.int16),
            pltpu.VMEM((N_HEADS, HEAD_DIM, Q_TILE), BF16),
            pltpu.VMEM((IDX_HEADS, IDX_DIM, Q_TILE), BF16),
        ],
        compiler_params=pltpu.CompilerParams(dimension_semantics=("arbitrary", "arbitrary"),
                                             vmem_limit_bytes=VMEM_LIMIT),
        name="attn",
    )(q3, iq3, iwt, k3, ik, vt, km3, ikm, vtm, tab, tabm)


def _mix_kernel(x_ref, attn_ref, y_ref, yprev_ref, ymeta_ref, gate_ref,
                cw_ref, cb_ref, lng_ref, lnb_ref, wco_ref, wao_ref, wout_ref, nf_ref, wrt_ref, br_ref,
                h2_ref, hn2_ref, eid_ref, rank_ref, gcol_ref, cnt_ref,
                win_ref, shift_ref, base_ref, *, tiles_per_seq):
    i = pl.program_id(0)

    @pl.when(i == 0)
    def _():
        base_ref[...] = jnp.zeros_like(base_ref)

    first = (i % tiles_per_seq) == 0
    win_ref[0:HALO, :] = jnp.where(first, ymeta_ref[...], yprev_ref[...])
    win_ref[HALO:, :] = y_ref[...]
    lead = HALO - (CONV_WIDTH - 1)
    for b in range(SUBLANES):
        span = MIX_TILE + SUBLANES * (len(range(b, CONV_WIDTH, SUBLANES)) - 1)
        shift_ref[b, 0:span, :] = win_ref[pl.ds(lead + b, span), :]
    for hh in range(MIX_TILE // MIX_ROWS):
        _mix_rows(hh * MIX_ROWS, x_ref, attn_ref, gate_ref, cw_ref, cb_ref, lng_ref, lnb_ref, wco_ref, wao_ref,
                  wout_ref, nf_ref, wrt_ref, br_ref, h2_ref, hn2_ref, eid_ref, rank_ref, gcol_ref, cnt_ref,
                  shift_ref, base_ref)


def _mix_rows(r_off, x_ref, attn_ref, gate_ref, cw_ref, cb_ref, lng_ref, lnb_ref, wco_ref, wao_ref,
              wout_ref, nf_ref, wrt_ref, br_ref, h2_ref, hn2_ref, eid_ref, rank_ref, gcol_ref, cnt_ref,
              shift_ref, base_ref):
    rows = slice(r_off, r_off + MIX_ROWS)
    blocks = []
    for r0 in range(r_off, r_off + MIX_ROWS, CONV_ROWS):
        acc = jnp.broadcast_to(cb_ref[...], (CONV_ROWS, CONV_CH))
        for b in range(SUBLANES):
            for a, w in enumerate(range(b, CONV_WIDTH, SUBLANES)):
                r1 = r0 + SUBLANES * a
                acc = acc + cw_ref[w:w + 1, :] * shift_ref[b, r1:r1 + CONV_ROWS, :]
        blocks.append(acc)
    yc = jnp.concatenate(blocks, axis=0)
    mu = jnp.mean(yc, axis=-1, keepdims=True)
    var = jnp.mean(jnp.square(yc - mu), axis=-1, keepdims=True)
    yn = (yc - mu) * lax.rsqrt(var + EPS) * lng_ref[...] + lnb_ref[...]
    ys = yn * _sigmoid(yn)
    y_b = jnp.dot(ys.astype(BF16), wco_ref[...], preferred_element_type=F32)

    y_a = jnp.dot(attn_ref[rows, :], wao_ref[...], preferred_element_type=F32)
    merged = gate_ref[rows, :D_MODEL] * y_a + gate_ref[rows, D_MODEL:] * y_b
    h2 = x_ref[rows, :] + jnp.dot(merged.astype(BF16), wout_ref[...], preferred_element_type=F32)
    h2_ref[rows, :] = h2
    ms = jnp.mean(h2 * h2, axis=-1, keepdims=True)
    hn2 = h2 * lax.rsqrt(ms + EPS) * nf_ref[...]
    hn2_ref[rows, 0, :] = _pack_bf16_pairs(hn2)

    logits = lax.dot_general(wrt_ref[...], hn2, NT_DIMS, preferred_element_type=F32,
                             precision=lax.Precision.HIGHEST) + br_ref[...]
    erow = lax.broadcasted_iota(I32, (N_EXPERTS, MIX_ROWS), 0)
    vals, ids = [], []
    l = logits
    for _ in range(TOP_K_EXPERTS):
        m = jnp.max(l, axis=0, keepdims=True)
        idx = jnp.min(jnp.where(l == m, erow, N_EXPERTS), axis=0, keepdims=True)
        vals.append(m)
        ids.append(idx)
        l = jnp.where(erow == idx, -jnp.inf, l)
    ex = [jnp.exp(v - vals[0]) for v in vals]
    den = ex[0] + ex[1] + ex[2] + ex[3]
    gates = [e / den for e in ex]

    onehot = [(erow == idx) for idx in ids]
    oh = jnp.concatenate([jnp.where(o, 1.0, 0.0) for o in onehot], axis=0)
    tr = lax.broadcasted_iota(I32, (MIX_ROWS, MIX_ROWS), 0)
    tc = lax.broadcasted_iota(I32, (MIX_ROWS, MIX_ROWS), 1)
    upper = jnp.where(tr <= tc, 1.0, 0.0).astype(BF16)
    pref = jnp.dot(oh.astype(BF16), upper, preferred_element_type=F32)
    offs = base_ref[:, 0:1]
    ranks = []
    for kk in range(TOP_K_EXPERTS):
        pk = pref[kk * N_EXPERTS:(kk + 1) * N_EXPERTS, :]
        r = jnp.sum(jnp.where(onehot[kk], offs + pk - 1.0, 0.0), axis=0, keepdims=True)
        ranks.append(r.astype(I32))
        offs = offs + pk[:, MIX_ROWS - 1:MIX_ROWS]
    base_ref[...] = jnp.broadcast_to(offs, base_ref.shape)
    cnt_ref[...] = jnp.broadcast_to(offs, cnt_ref.shape)

    zi = jnp.zeros((SUBLANES - TOP_K_EXPERTS, MIX_ROWS), I32)
    eid_ref[:, rows] = jnp.concatenate(ids + [zi], axis=0)
    rank_ref[:, rows] = jnp.concatenate(ranks + [zi], axis=0)
    g8 = jnp.concatenate(gates + [jnp.zeros((LANES - TOP_K_EXPERTS, MIX_ROWS), F32)], axis=0)
    gcol_ref[rows, :] = g8.T


def _mix(x2d, attn, y, ymeta, gate, wts, seq):
    n = x2d.shape[0]
    tiles_per_seq = seq // MIX_TILE
    halo_per_tile = MIX_TILE // HALO
    full = lambda a: pl.BlockSpec(a.shape, lambda i: (0,) * a.ndim)
    row = lambda w: pl.BlockSpec((MIX_TILE, w), lambda i: (i, 0))
    lane = lambda r: pl.BlockSpec((r, MIX_TILE), lambda i: (0, i))
    out_shape = (
        jax.ShapeDtypeStruct((n, D_MODEL), F32),
        jax.ShapeDtypeStruct((n, 1, D_MODEL // 2), I32),
        jax.ShapeDtypeStruct((SUBLANES, n), I32),
        jax.ShapeDtypeStruct((SUBLANES, n), I32),
        jax.ShapeDtypeStruct((n, LANES), F32),
        jax.ShapeDtypeStruct((N_EXPERTS, LANES), F32),
    )
    out_specs = (row(D_MODEL), pl.BlockSpec((MIX_TILE, 1, D_MODEL // 2), lambda i: (i, 0, 0)),
                 lane(SUBLANES), lane(SUBLANES), row(LANES), full(out_shape[5]))
    return pl.pallas_call(
        functools.partial(_mix_kernel, tiles_per_seq=tiles_per_seq),
        grid=(n // MIX_TILE,),
        in_specs=[row(D_MODEL), row(ATTN_WIDTH), row(CONV_CH),
                  pl.BlockSpec((HALO, CONV_CH), lambda i: (jnp.maximum(i * halo_per_tile - 1, 0), 0)),
                  full(ymeta), row(2 * D_MODEL)] + [full(w) for w in wts],
        out_specs=out_specs,
        out_shape=out_shape,
        scratch_shapes=[pltpu.VMEM((HALO + MIX_TILE, CONV_CH), F32),
                        pltpu.VMEM((SUBLANES, HALO + MIX_TILE, CONV_CH), F32),
                        pltpu.VMEM((N_EXPERTS, LANES), F32)],
        compiler_params=pltpu.CompilerParams(dimension_semantics=("arbitrary",),
                                             vmem_limit_bytes=VMEM_LIMIT),
        name="mix",
    )(x2d, attn, y, y, ymeta, gate, *wts)


def _slots_kernel(starts_ref, eid_ref, rank_ref, slot_ref):
    eid = eid_ref[...]
    base = jnp.zeros(eid.shape, I32)
    for e in range(N_EXPERTS):
        base = jnp.where(eid == e, starts_ref[e], base)
    slot = base + rank_ref[...]
    for t in range(slot_ref.shape[0]):
        for kk in range(TOP_K_EXPERTS):
            slot_ref[t, :, kk * ROW_TILE:(kk + 1) * ROW_TILE] = slot[kk:kk + 1, t * ROW_TILE:(t + 1) * ROW_TILE]


def _slots(starts, eid8, rank8):
    n = eid8.shape[1]
    tiles = 8
    spec = pl.BlockSpec((SUBLANES, tiles * ROW_TILE), lambda i: (0, i))
    return pl.pallas_call(
        _slots_kernel,
        grid=(n // (tiles * ROW_TILE),),
        in_specs=[pl.BlockSpec(memory_space=pltpu.SMEM), spec, spec],
        out_specs=pl.BlockSpec((tiles, 1, COPIES_PER_TILE), lambda i: (i, 0, 0)),
        out_shape=jax.ShapeDtypeStruct((n // ROW_TILE, 1, COPIES_PER_TILE), I32),
        compiler_params=pltpu.CompilerParams(dimension_semantics=("arbitrary",)),
        name="slots",
    )(starts, eid8, rank8)


COPIES_PER_TILE = TOP_K_EXPERTS * ROW_TILE
SLOT_UNROLL = 16


def _slot_source_kernel(slot_ref, init_ref, src_ref, sem, *, n):
    i = pl.program_id(0)

    @pl.when(i == 0)
    def _():
        cp = pltpu.make_async_copy(init_ref, src_ref, sem)
        cp.start()
        cp.wait()

    for kk in range(TOP_K_EXPERTS):
        def body(g, carry):
            for u in range(SLOT_UNROLL):
                t = g * SLOT_UNROLL + u
                src_ref[slot_ref[0, 0, kk * ROW_TILE + t]] = kk * n + i * ROW_TILE + t
            return carry
        lax.fori_loop(0, ROW_TILE // SLOT_UNROLL, body, 0)


def _slot_sources(slot_tiles, init, n):
    return pl.pallas_call(
        functools.partial(_slot_source_kernel, n=n),
        grid=(slot_tiles.shape[0],),
        in_specs=[pl.BlockSpec((1, 1, COPIES_PER_TILE), lambda i: (i, 0, 0), memory_space=pltpu.SMEM),
                  pl.BlockSpec(memory_space=pl.ANY)],
        out_specs=pl.BlockSpec(memory_space=pltpu.SMEM),
        out_shape=jax.ShapeDtypeStruct(init.shape, I32),
        scratch_shapes=[pltpu.SemaphoreType.DMA(())],
        compiler_params=pltpu.CompilerParams(dimension_semantics=("arbitrary",)),
        name="slot_sources",
    )(slot_tiles, init)


def _expert_kernel(te_ref, nused_ref, src_ref, src_next_ref, src_prev_ref, hn_hbm,
                   wug_ref, bug_ref, wd_ref, bd_ref, yk_hbm, xbuf, ybuf, wug_bf, wd_bf, gsem, ssem, *, n):
    i = pl.program_id(0)
    nused = nused_ref[0]

    def gather_copy(src, r, b):
        tok = src[0, 0, r] & (n - 1)
        return pltpu.make_async_copy(hn_hbm.at[tok], xbuf.at[b, r], gsem.at[b])

    def scatter_copy(src, r, b):
        return pltpu.make_async_copy(ybuf.at[b, r], yk_hbm.at[src[0, 0, r]], ssem.at[b])

    @pl.when(i <= nused)
    def _():
        cur = i % 2
        oth = 1 - cur

        def wait_scatter(b):
            pltpu.make_async_copy(ybuf.at[b], yk_hbm.at[pl.ds(0, ROW_TILE)], ssem.at[b]).wait()

        @pl.when(i == 0)
        def _():
            for r in range(ROW_TILE):
                gather_copy(src_ref, r, 0).start()
            ybuf[1] = jnp.zeros(ybuf.shape[1:], F32)
            spare = pltpu.make_async_copy(ybuf.at[1], yk_hbm.at[pl.ds(TOP_K_EXPERTS * n, ROW_TILE)],
                                          ssem.at[1])
            spare.start()
            spare.wait()

        @pl.when(i < nused)
        def _():
            for r in range(ROW_TILE):
                gather_copy(src_next_ref, r, oth).start()

        @pl.when(i > 0)
        def _():
            for r in range(ROW_TILE):
                scatter_copy(src_prev_ref, r, oth).start()

        pltpu.make_async_copy(hn_hbm.at[pl.ds(0, ROW_TILE)], xbuf.at[cur], gsem.at[cur]).wait()

        prev = te_ref[jnp.maximum(i - 1, 0)]
        fresh = (i == 0) | (te_ref[jnp.minimum(i, nused - 1)] != prev)

        @pl.when(fresh)
        def _():
            wug_bf[...] = wug_ref[0].astype(BF16)
            wd_bf[...] = wd_ref[0].astype(BF16)

        xb = _unpack_bf16_pairs(xbuf[cur, :, 0, :])
        ug = jnp.dot(xb, wug_bf[...], preferred_element_type=F32) + bug_ref[0]
        gate = jnp.minimum(ug[:, :D_FF], SWIGLU_LIMIT)
        up = jnp.clip(ug[:, D_FF:], -SWIGLU_LIMIT, SWIGLU_LIMIT)
        act = (up + 1.0) * gate * _sigmoid(SWIGLU_ALPHA * gate)
        y = jnp.dot(act.astype(BF16), wd_bf[...], preferred_element_type=F32) + bd_ref[0]

        @pl.when(i > 1)
        def _():
            wait_scatter(cur)

        ybuf[cur, :, 0, :] = y

        @pl.when((i == nused) & (i > 0))
        def _():
            wait_scatter(oth)


def _experts(tile_expert, nused, src_tiles, hn2p, w_ug, b_ug, w_down, b_down, n):
    ntiles = src_tiles.shape[0]
    clamp = lambda i, nu: jnp.maximum(jnp.minimum(i, nu[0] - 1), 0)
    smem = lambda f: pl.BlockSpec((1, 1, ROW_TILE), f, memory_space=pltpu.SMEM)
    expert = lambda i, te, nu: (te[clamp(i, nu)], 0, 0)
    grid_spec = pltpu.PrefetchScalarGridSpec(
        num_scalar_prefetch=2,
        grid=(ntiles + 1,),
        in_specs=[
            smem(lambda i, te, nu: (clamp(i, nu), 0, 0)),
            smem(lambda i, te, nu: (clamp(i + 1, nu), 0, 0)),
            smem(lambda i, te, nu: (clamp(i - 1, nu), 0, 0)),
            pl.BlockSpec(memory_space=pl.ANY),
            pl.BlockSpec((1, D_MODEL, 2 * D_FF), expert),
            pl.BlockSpec((1, 1, 2 * D_FF), expert),
            pl.BlockSpec((1, D_FF, D_MODEL), expert),
            pl.BlockSpec((1, 1, D_MODEL), expert),
        ],
        out_specs=pl.BlockSpec(memory_space=pl.ANY),
        scratch_shapes=[pltpu.VMEM((2, ROW_TILE, 1, D_MODEL // 2), I32),
                        pltpu.VMEM((2, ROW_TILE, 1, D_MODEL), F32),
                        pltpu.VMEM((D_MODEL, 2 * D_FF), BF16), pltpu.VMEM((D_FF, D_MODEL), BF16),
                        pltpu.SemaphoreType.DMA((2,)), pltpu.SemaphoreType.DMA((2,))],
    )
    return pl.pallas_call(
        functools.partial(_expert_kernel, n=n),
        grid_spec=grid_spec,
        out_shape=jax.ShapeDtypeStruct((TOP_K_EXPERTS * n + ROW_TILE, 1, D_MODEL), F32),
        compiler_params=pltpu.CompilerParams(dimension_semantics=("arbitrary",),
                                             vmem_limit_bytes=VMEM_LIMIT),
        name="experts",
    )(tile_expert, nused, src_tiles, src_tiles, src_tiles, hn2p,
      w_ug, b_ug[:, None, :], w_down, b_down[:, None, :])


def _final_kernel(h2_ref, gcol_ref, nf_ref, *rest):
    y_refs, o_ref = rest[:TOP_K_EXPERTS], rest[TOP_K_EXPERTS]
    h = h2_ref[...]
    for kk in range(TOP_K_EXPERTS):
        h = h + gcol_ref[:, kk:kk + 1] * y_refs[kk][:, 0, :]
    ms = jnp.mean(h * h, axis=-1, keepdims=True)
    o_ref[...] = h * lax.rsqrt(ms + EPS) * nf_ref[...]


def _final(h2, gcol, nf, yk):
    n = h2.shape[0]
    ntiles = n // ROW_TILE
    choice = lambda kk: pl.BlockSpec((ROW_TILE, 1, D_MODEL), lambda i: (kk * ntiles + i, 0, 0))
    return pl.pallas_call(
        _final_kernel,
        grid=(ntiles,),
        in_specs=[pl.BlockSpec((ROW_TILE, D_MODEL), lambda i: (i, 0)),
                  pl.BlockSpec((ROW_TILE, LANES), lambda i: (i, 0)),
                  pl.BlockSpec((1, D_MODEL), lambda i: (0, 0))] + [choice(kk) for kk in range(TOP_K_EXPERTS)],
        out_specs=pl.BlockSpec((ROW_TILE, D_MODEL), lambda i: (i, 0)),
        out_shape=jax.ShapeDtypeStruct((n, D_MODEL), F32),
        compiler_params=pltpu.CompilerParams(dimension_semantics=("arbitrary",),
                                             vmem_limit_bytes=VMEM_LIMIT),
        name="final",
    )(h2, gcol, nf, *([yk] * TOP_K_EXPERTS))


def _split_w_in(w_in, b_gate):
    c = ATTN_WIDTH
    o = 0
    wq = w_in[:, o:o + c]; o += c
    wk = w_in[:, o:o + c]; o += c
    wv = w_in[:, o:o + c]; o += c
    wiq = w_in[:, o:o + IDX_HEADS * IDX_DIM]; o += IDX_HEADS * IDX_DIM
    wik = w_in[:, o:o + IDX_DIM]; o += IDX_DIM
    wiw = w_in[:, o:o + IDX_HEADS]; o += IDX_HEADS
    wglu = w_in[:, o:o + 2 * CONV_CH]; o += 2 * CONV_CH
    wgate = w_in[:, o:]
    wiwt = jnp.concatenate([wiw.T, jnp.zeros((PACKED_ROWS - IDX_HEADS, D_MODEL), w_in.dtype)], axis=0)
    bf = lambda a: a.astype(BF16)
    return (bf(wq), bf(wk), bf(wiq), bf(wik), bf(wv.T), bf(wiwt), bf(wglu), bf(wgate),
            b_gate[None, :].astype(F32))


def kernel(x, meta_tokens, rel_bias, norm_mix, w_in, b_gate, w_attn_out, conv_w, conv_b, conv_ln_g, conv_ln_b, w_conv_out, w_out, norm_ffn, w_router, b_router, w_up_gate, b_up_gate, w_down, b_down, norm_final):
    batch, seq, d = x.shape
    n = batch * seq
    assert d == D_MODEL and seq % MIX_TILE == 0 and seq % (COUNT_CHUNKS * K_CHUNK) == 0 and n % PROJ_TILE == 0
    assert n & (n - 1) == 0, "slot sources keep the token index in the low bits of choice * n + token"
    assert min(TOPK_MAX, seq // 4) == TOPK_MAX and N_META + seq < 2 ** POS_BITS
    x2d = x.reshape(n, d)

    wts = _split_w_in(w_in[0], b_gate[0])
    g_mix = norm_mix[0][None, :]
    q3, k3, iq3, ik, vt, iwt, y, gate = _project(x2d, g_mix, wts, PROJ_TILE)
    _, km3, _, ikm, vtm, _, ym, _ = _project(meta_tokens.astype(F32), g_mix, wts, N_META)

    tab, tabm = _bias_tables(rel_bias.astype(F32))
    attn = _attention(q3, iq3, iwt, k3, ik, vt, km3, ikm, vtm, tab, tabm, batch, seq)

    ymeta = jnp.concatenate([jnp.zeros((HALO - N_META, CONV_CH), F32), ym], axis=0)
    cw = jnp.concatenate([conv_w[0], jnp.zeros((-CONV_WIDTH % SUBLANES, CONV_CH), F32)], axis=0)
    mix_w = (cw, conv_b[0][None, :], conv_ln_g[0][None, :], conv_ln_b[0][None, :],
             w_conv_out[0].astype(BF16), w_attn_out[0].astype(BF16), w_out[0].astype(BF16),
             norm_ffn[0][None, :], w_router[0].T, b_router[0][:, None])
    h2, hn2, eid8, rank8, gcol, cnt = _mix(x2d, attn, y, ymeta, gate, mix_w, seq)

    counts = cnt[:, 0].astype(I32)
    padded = ((counts + ROW_TILE - 1) // ROW_TILE) * ROW_TILE
    ends = jnp.cumsum(padded)
    starts = ends - padded
    slot_tiles = _slots(starts, eid8, rank8)
    nslots = n * TOP_K_EXPERTS + N_EXPERTS * ROW_TILE
    ntiles = nslots // ROW_TILE
    tile_start = jnp.arange(ntiles, dtype=I32) * ROW_TILE
    nused = (ends[-1] // ROW_TILE).astype(I32)
    last_start = jnp.maximum(ends[-1] - ROW_TILE, 0)
    tile_expert = jnp.sum((jnp.minimum(tile_start, last_start)[:, None] >= ends[None, :]).astype(I32), axis=1)

    spare = TOP_K_EXPERTS * n + jnp.arange(nslots, dtype=I32) % ROW_TILE
    src_tiles = _slot_sources(slot_tiles, spare, n).reshape(ntiles, 1, ROW_TILE)
    yk = _experts(tile_expert, nused[None], src_tiles, hn2, w_up_gate[0], b_up_gate[0], w_down[0], b_down[0], n)
    out = _final(h2, gcol, norm_final[None, :], yk)
    return out.reshape(batch, seq, d)
```

```python
import functools
import math

import jax
import jax.numpy as jnp
from jax import lax
from jax.experimental import pallas as pl
from jax.experimental.pallas import tpu as pltpu

F32 = jnp.float32
BF16 = jnp.bfloat16
I32 = jnp.int32

D_MODEL = 1024
N_META = 16
N_HEADS = 8
HEAD_DIM = 64
ATTN_WIDTH = N_HEADS * HEAD_DIM
IDX_HEADS = 8
IDX_DIM = 64
TOPK_MAX = 256
CONV_CH = 512
CONV_WIDTH = 31
N_BUCKETS = 32
MAX_DISTANCE = 128
N_EXPERTS = 32
TOP_K_EXPERTS = 4
D_FF = 1024
SWIGLU_LIMIT = 7.0
SWIGLU_ALPHA = 1.702
EPS = 1e-6
IDX_SCALE = (IDX_DIM ** -0.5) * (IDX_HEADS ** -0.5)

SUBLANES = 8
LANES = 128
PACKED_ROWS = 16
ROW_TILE = 256
MIX_TILE = 512
MIX_ROWS = 256
PROJ_TILE = 512
Q_TILE = 256
K_CHUNK = 256
COUNT_CHUNKS = 2
POS_BITS = 12
HALO = 32
CONV_ROWS = 32
NEG = -1e30
LOG2E = math.log2(math.e)
VMEM_LIMIT = 56 * 1024 * 1024

NT_DIMS = (((1,), (1,)), ((), ()))


def _sigmoid(x):
    return 1.0 / (1.0 + jnp.exp(-x))


def _pack_bf16_pairs(x):
    w = x.shape[1] // 2
    hi = lax.bitcast_convert_type(x[:, :w].astype(BF16).astype(F32), jnp.uint32)
    lo = lax.bitcast_convert_type(x[:, w:].astype(BF16).astype(F32), jnp.uint32)
    return lax.bitcast_convert_type(hi | (lo >> 16), I32)


def _unpack_bf16_pairs(p):
    u = lax.bitcast_convert_type(p, jnp.uint32)
    hi = lax.bitcast_convert_type(u & jnp.uint32(0xFFFF0000), F32)
    lo = lax.bitcast_convert_type(u << 16, F32)
    return jnp.concatenate([hi, lo], axis=1).astype(BF16)


def _proj_kernel(x_ref, g_ref, wq_ref, wk_ref, wiq_ref, wik_ref, wvt_ref, wiwt_ref, wglu_ref,
                 wgate_ref, bgate_ref,
                 q_ref, k_ref, iq_ref, ik_ref, vt_ref, iwt_ref, y_ref, gate_ref):
    x = x_ref[...]
    ms = jnp.mean(x * x, axis=-1, keepdims=True)
    xn = (x * lax.rsqrt(ms + EPS) * g_ref[...]).astype(BF16)

    q = jnp.dot(xn, wq_ref[...], preferred_element_type=F32) * (HEAD_DIM ** -0.5 * LOG2E)
    k = jnp.dot(xn, wk_ref[...], preferred_element_type=F32)
    iq = jnp.dot(xn, wiq_ref[...], preferred_element_type=F32)
    for h in range(N_HEADS):
        sl = slice(h * HEAD_DIM, (h + 1) * HEAD_DIM)
        q_ref[h] = q[:, sl].astype(BF16)
        k_ref[h] = k[:, sl].astype(BF16)
        iq_ref[h] = iq[:, sl].astype(BF16)
    ik_ref[...] = jnp.dot(xn, wik_ref[...], preferred_element_type=F32).astype(BF16)
    vt_ref[...] = lax.dot_general(wvt_ref[...], xn, NT_DIMS, preferred_element_type=F32).astype(BF16)
    iwt = lax.dot_general(wiwt_ref[...], xn, NT_DIMS, preferred_element_type=F32)
    iwt_ref[...] = iwt[:IDX_HEADS] * IDX_SCALE
    glu = jnp.dot(xn, wglu_ref[...], preferred_element_type=F32)
    y_ref[...] = glu[:, :CONV_CH] * _sigmoid(glu[:, CONV_CH:])
    gate = jnp.dot(xn, wgate_ref[...], preferred_element_type=F32) + bgate_ref[...]
    gate_ref[...] = _sigmoid(gate)


def _project(x2d, g, wts, tm):
    n = x2d.shape[0]
    wq, wk, wiq, wik, wvt, wiwt, wglu, wgate, bgate = wts
    full = lambda a: pl.BlockSpec(a.shape, lambda i: (0,) * a.ndim, pipeline_mode=pl.Buffered(1))
    out_shape = (
        jax.ShapeDtypeStruct((N_HEADS, n, HEAD_DIM), BF16),
        jax.ShapeDtypeStruct((N_HEADS, n, HEAD_DIM), BF16),
        jax.ShapeDtypeStruct((IDX_HEADS, n, IDX_DIM), BF16),
        jax.ShapeDtypeStruct((n, IDX_DIM), BF16),
        jax.ShapeDtypeStruct((ATTN_WIDTH, n), BF16),
        jax.ShapeDtypeStruct((IDX_HEADS, n), F32),
        jax.ShapeDtypeStruct((n, CONV_CH), F32),
        jax.ShapeDtypeStruct((n, 2 * D_MODEL), F32),
    )
    out_specs = (
        pl.BlockSpec((N_HEADS, tm, HEAD_DIM), lambda i: (0, i, 0)),
        pl.BlockSpec((N_HEADS, tm, HEAD_DIM), lambda i: (0, i, 0)),
        pl.BlockSpec((IDX_HEADS, tm, IDX_DIM), lambda i: (0, i, 0)),
        pl.BlockSpec((tm, IDX_DIM), lambda i: (i, 0)),
        pl.BlockSpec((ATTN_WIDTH, tm), lambda i: (0, i)),
        pl.BlockSpec((IDX_HEADS, tm), lambda i: (0, i)),
        pl.BlockSpec((tm, CONV_CH), lambda i: (i, 0)),
        pl.BlockSpec((tm, 2 * D_MODEL), lambda i: (i, 0)),
    )
    return pl.pallas_call(
        _proj_kernel,
        grid=(n // tm,),
        in_specs=[pl.BlockSpec((tm, D_MODEL), lambda i: (i, 0)), full(g), full(wq), full(wk), full(wiq),
                  full(wik), full(wvt), full(wiwt), full(wglu), full(wgate), full(bgate)],
        out_specs=out_specs,
        out_shape=out_shape,
        compiler_params=pltpu.CompilerParams(dimension_semantics=("arbitrary",),
                                             vmem_limit_bytes=VMEM_LIMIT),
        name="proj",
    )(x2d, g, wq, wk, wiq, wik, wvt, wiwt, wglu, wgate, bgate)


def _t5_bucket(n):
    max_exact = N_BUCKETS // 2
    nf = jnp.maximum(n, 1).astype(F32)
    large = max_exact + (jnp.log(nf / max_exact) / math.log(MAX_DISTANCE / max_exact)
                         * (N_BUCKETS - max_exact)).astype(I32)
    large = jnp.minimum(large, N_BUCKETS - 1)
    return jnp.where(n < max_exact, n, large)


def _bias_lookup(rb_ref, dist, h):
    bucket = _t5_bucket(jnp.maximum(dist, 0))
    out = jnp.full(dist.shape, NEG, F32)
    for b in range(N_BUCKETS):
        out = jnp.where(bucket == b, rb_ref[b, h] * LOG2E, out)
    return jnp.where(dist >= 0, out, NEG)


def _bias_kernel(rb_ref, tab_ref, tabm_ref):
    kind = pl.program_id(0)
    r = pl.program_id(1)
    rows = tab_ref.shape[2]
    s = lax.broadcasted_iota(I32, (rows, Q_TILE), 0) + r * rows
    t = lax.broadcasted_iota(I32, (rows, Q_TILE), 1)
    dist = jnp.where(kind == 2, 2 * K_CHUNK, t - s + kind * K_CHUNK)
    for h in range(N_HEADS):
        tab_ref[0, h] = _bias_lookup(rb_ref, dist, h)
    m = lax.broadcasted_iota(I32, (N_META, Q_TILE), 0)
    tm_ = lax.broadcasted_iota(I32, (N_META, Q_TILE), 1)
    distm = jnp.where(kind == 0, N_META + tm_ - m, 2 * K_CHUNK)
    for h in range(N_HEADS):
        tabm_ref[0, h] = _bias_lookup(rb_ref, distm, h)


def _bias_tables(rel_bias):
    rows = 64
    return pl.pallas_call(
        _bias_kernel,
        grid=(3, K_CHUNK // rows),
        in_specs=[pl.BlockSpec(memory_space=pltpu.SMEM)],
        out_specs=(pl.BlockSpec((1, N_HEADS, rows, Q_TILE), lambda kd, r: (kd, 0, r, 0)),
                   pl.BlockSpec((1, N_HEADS, N_META, Q_TILE), lambda kd, r: (kd, 0, 0, 0))),
        out_shape=(jax.ShapeDtypeStruct((3, N_HEADS, K_CHUNK, Q_TILE), F32),
                   jax.ShapeDtypeStruct((3, N_HEADS, N_META, Q_TILE), F32)),
        compiler_params=pltpu.CompilerParams(dimension_semantics=("arbitrary", "arbitrary")),
        name="bias_tables",
    )(rel_bias)


def _fold_rows(x, op):
    r, l = x.shape
    x3 = x.reshape(r // SUBLANES, SUBLANES, l)
    return op(x3, axis=0)


def _attn_kernel(q_ref, iq_ref, iwt_ref, k_ref, ik_ref, vt_ref, km_ref, ikm_ref, vtm_ref,
                 tab_ref, tabm_ref, o_ref, sc_ref, scm_ref, l_ref, lm_ref, ot_ref, cst_ref,
                 hi_ref, lo_ref, him_ref, lom_ref, qt_ref, iqt_ref):
    j = pl.program_id(1)
    nchunks = j + 1
    iw = iwt_ref[...]

    def chunk_rows(c):
        return pl.ds(pl.multiple_of(c * K_CHUNK, K_CHUNK), K_CHUNK)

    def paired_chunk_loop(body, init):
        def quad(i, c):
            for u in range(4):
                c = body(4 * i + u, c)
            return c

        carry = lax.fori_loop(0, lax.shift_right_logical(nchunks, 2), quad, init)
        base = nchunks & ~3
        carry = lax.cond((nchunks & 2) == 2, lambda c: body(base + 1, body(base, c)), lambda c: c, carry)
        return lax.cond((nchunks & 1) == 1, lambda c: body(nchunks - 1, c), lambda c: c, carry)

    for h in range(N_HEADS):
        qt_ref[h] = q_ref[h].astype(F32).T.astype(BF16)
        iqt_ref[h] = iq_ref[h].astype(F32).T.astype(BF16)

    I16 = jnp.int16
    HALF = 16
    LOW = -2 ** (HALF - 1)
    NO_CUT = 2 ** 30

    def split_key(s):
        bits = lax.bitcast_convert_type(s, I32)
        key = jnp.where(bits < 0, bits ^ jnp.int32(0x7FFFFFFF), bits)
        hi = lax.shift_right_arithmetic(key, HALF).astype(I16)
        lo = ((key & (2 ** HALF - 1)) + LOW).astype(I16)
        return hi, lo

    def idx_scores(ikc):
        acc = None
        for h in range(IDX_HEADS):
            s = jnp.dot(ikc, iqt_ref[h], preferred_element_type=F32)
            term = jnp.maximum(s, 0.0) * iw[h:h + 1, :]
            acc = term if acc is None else acc + term
        return acc

    scm_ref[...] = idx_scores(ikm_ref[...])

    row_minus_col = (lax.broadcasted_iota(I32, (K_CHUNK, Q_TILE), 0)
                     - lax.broadcasted_iota(I32, (K_CHUNK, Q_TILE), 1))

    def p1(c, carry):
        rows = chunk_rows(c)
        future = row_minus_col > jnp.where(c == j, 0, K_CHUNK)
        s = jnp.where(future, -jnp.inf, idx_scores(ik_ref[rows, :]))
        sc_ref[rows, :] = s
        hi_ref[rows, :], lo_ref[rows, :] = split_key(s)
        return carry

    paired_chunk_loop(p1, 0)

    def count(pred):
        def body(c, acc):
            blk = sc_ref[chunk_rows(c), :]
            return acc + _fold_rows(jnp.where(pred(blk, c), 1, 0).astype(I32), jnp.sum)
        acc = lax.fori_loop(0, nchunks, body, jnp.zeros((SUBLANES, Q_TILE), I32))
        acc = acc + _fold_rows(jnp.where(pred(scm_ref[...], -1), 1, 0).astype(I32), jnp.sum)
        return jnp.sum(acc, axis=0, keepdims=True)

    him_ref[...], lom_ref[...] = split_key(scm_ref[...])

    count_trips = lax.shift_right_logical(nchunks + COUNT_CHUNKS - 1, COUNT_CHUNKS.bit_length() - 1)

    def p2pad(c, carry):
        rows = chunk_rows(c)
        hi_ref[rows, :] = jnp.full((K_CHUNK, Q_TILE), LOW, I16)
        lo_ref[rows, :] = jnp.full((K_CHUNK, Q_TILE), LOW, I16)
        return carry

    lax.fori_loop(nchunks, count_trips * COUNT_CHUNKS, p2pad, 0)

    def tree_sum16(ind):
        parts = [ind[a * PACKED_ROWS:(a + 1) * PACKED_ROWS, :] for a in range(ind.shape[0] // PACKED_ROWS)]
        while len(parts) > 1:
            parts = [parts[a] + parts[a + 1] for a in range(0, len(parts), 2)]
        return parts[0]

    def count16(ref, mref, pred):
        one, zero = jnp.ones((), BF16), jnp.zeros((), BF16)

        def body(cc, acc):
            span = COUNT_CHUNKS * K_CHUNK
            rows = pl.ds(pl.multiple_of(cc * span, span), span)
            return acc + tree_sum16(jnp.where(pred(ref[rows, :]), one, zero)).astype(F32)
        acc = lax.fori_loop(0, count_trips, body, jnp.zeros((PACKED_ROWS, Q_TILE), F32))
        acc = acc + jnp.where(pred(mref[...]), one, zero).astype(F32)
        return jnp.sum(acc, axis=0, keepdims=True).astype(I32)

    def to16(u):
        return (u + LOW).astype(I16)

    def search_hi(i, u):
        cand = u | lax.shift_left(jnp.int32(1), HALF - 1 - i)
        c16 = to16(cand)
        cnt = count16(hi_ref, him_ref, lambda blk: blk >= c16)
        return jnp.where(cnt >= TOPK_MAX, cand, u)

    u_hi = lax.fori_loop(0, HALF, search_hi, jnp.zeros((1, Q_TILE), I32))
    t16 = to16(u_hi)
    need = TOPK_MAX - count16(hi_ref, him_ref, lambda blk: blk > t16)

    lom_ref[...] = jnp.where(him_ref[...] == t16, lom_ref[...], jnp.int16(LOW))

    def p2b(c, carry):
        rows = chunk_rows(c)
        lo_ref[rows, :] = jnp.where(hi_ref[rows, :] == t16, lo_ref[rows, :], jnp.int16(LOW))
        return carry

    lax.fori_loop(0, nchunks, p2b, 0)

    def search_lo(i, v):
        cand = v | lax.shift_left(jnp.int32(1), HALF - 1 - i)
        c16 = to16(cand)
        cnt = count16(lo_ref, lom_ref, lambda blk: blk >= c16)
        return jnp.where(cnt >= need, cand, v)

    u_lo = lax.fori_loop(0, HALF, search_lo, jnp.zeros((1, Q_TILE), I32))
    thr_key = lax.shift_left(u_hi + LOW, HALF) | u_lo
    thr_bits = jnp.where(thr_key < 0, thr_key ^ jnp.int32(0x7FFFFFFF), thr_key)
    thr = lax.bitcast_convert_type(thr_bits, F32)

    cnt_ge = count(lambda blk, c: blk >= thr)
    tie = (cnt_ge > TOPK_MAX) & (thr > -jnp.inf)
    any_tie = jnp.max(tie.astype(I32))
    cst_ref[...] = jnp.full((SUBLANES, Q_TILE), NO_CUT, I32)

    def pos_of(c, shape):
        r = lax.broadcasted_iota(I32, shape, 0)
        return jnp.where(c < 0, r, r + N_META + c * K_CHUNK)

    @pl.when(any_tie > 0)
    def _():
        cnt_gt = count(lambda blk, c: blk > thr)
        need = TOPK_MAX - cnt_gt

        def bis_pos(i, cs):
            cand = cs | lax.shift_left(jnp.int32(1), POS_BITS - 1 - i)
            f = count(lambda blk, c: (blk == thr) & (pos_of(c, blk.shape) < cand))
            return jnp.where(f <= need, cand, cs)

        cs = lax.fori_loop(0, POS_BITS, bis_pos, jnp.zeros((1, Q_TILE), I32))
        cs = jnp.where(tie, cs, NO_CUT)
        cst_ref[...] = jnp.broadcast_to(cs, (SUBLANES, Q_TILE))

    def to_mask(blk, c, with_ties):
        if with_ties:
            cs = cst_ref[0:1, :]
            sel = (blk > thr) | ((blk == thr) & (pos_of(c, blk.shape) < cs))
        else:
            sel = blk >= thr
        return jnp.where(sel, 0.0, NEG)

    for with_ties in (False, True):
        @pl.when((any_tie > 0) == with_ties)
        def _():
            def body(c, carry):
                rows = chunk_rows(c)
                sc_ref[rows, :] = to_mask(sc_ref[rows, :], c, with_ties)
                return carry
            lax.fori_loop(0, nchunks, body, 0)
            scm_ref[...] = to_mask(scm_ref[...], -1, with_ties)

    kindm = jnp.minimum(j, 1)
    heads = [slice(h * HEAD_DIM, (h + 1) * HEAD_DIM) for h in range(N_HEADS)]

    mx0 = []
    for h in range(N_HEADS):
        lm = (jnp.dot(km_ref[h], qt_ref[h], preferred_element_type=F32)
              + tabm_ref[kindm, h] + scm_ref[...])
        lm_ref[h] = lm
        mx0.append(_fold_rows(lm, jnp.max))

    def pass_a(c, mx):
        rows = chunk_rows(c)
        kind = jnp.minimum(j - c, 2)
        mask = sc_ref[rows, :]
        out = []
        for h in range(N_HEADS):
            l = (jnp.dot(k_ref[h, rows, :], qt_ref[h], preferred_element_type=F32)
                 + tab_ref[kind, h] + mask)
            l_ref[h, rows, :] = l
            out.append(jnp.maximum(mx[h], _fold_rows(l, jnp.max)))
        return tuple(out)

    mx = paired_chunk_loop(pass_a, tuple(mx0))
    m = [jnp.max(mx[h], axis=0, keepdims=True) for h in range(N_HEADS)]

    den0 = []
    for h in range(N_HEADS):
        pm = jnp.exp2(lm_ref[h] - m[h])
        den0.append(_fold_rows(pm, jnp.sum))
        ot_ref[heads[h], :] = jnp.dot(vtm_ref[heads[h], :], pm.astype(BF16), preferred_element_type=F32)

    def pass_b(c, den):
        rows = chunk_rows(c)
        out = []
        for h in range(N_HEADS):
            p = jnp.exp2(l_ref[h, rows, :] - m[h])
            out.append(den[h] + _fold_rows(p, jnp.sum))
            ot_ref[heads[h], :] += jnp.dot(vt_ref[heads[h], rows], p.astype(BF16),
                                           preferred_element_type=F32)
        return tuple(out)

    den = paired_chunk_loop(pass_b, tuple(den0))
    for h in range(N_HEADS):
        ot_ref[heads[h], :] = ot_ref[heads[h], :] / jnp.sum(den[h], axis=0, keepdims=True)
    o_ref[...] = ot_ref[...].T.astype(BF16)


def _attention(q3, iq3, iwt, k3, ik, vt, km3, ikm, vtm, tab, tabm, batch, seq):
    n = batch * seq
    tiles = seq // Q_TILE
    full = lambda a: pl.BlockSpec(a.shape, lambda b, j: (0,) * a.ndim, pipeline_mode=pl.Buffered(1))
    return pl.pallas_call(
        _attn_kernel,
        grid=(batch, tiles),
        in_specs=[
            pl.BlockSpec((N_HEADS, Q_TILE, HEAD_DIM), lambda b, j: (0, b * tiles + j, 0)),
            pl.BlockSpec((IDX_HEADS, Q_TILE, IDX_DIM), lambda b, j: (0, b * tiles + j, 0)),
            pl.BlockSpec((IDX_HEADS, Q_TILE), lambda b, j: (0, b * tiles + j)),
            pl.BlockSpec((N_HEADS, seq, HEAD_DIM), lambda b, j: (0, b, 0)),
            pl.BlockSpec((seq, IDX_DIM), lambda b, j: (b, 0)),
            pl.BlockSpec((ATTN_WIDTH, seq), lambda b, j: (0, b)),
            full(km3), full(ikm), full(vtm), full(tab), full(tabm),
        ],
        out_specs=pl.BlockSpec((Q_TILE, ATTN_WIDTH), lambda b, j: (b * tiles + j, 0)),
        out_shape=jax.ShapeDtypeStruct((n, ATTN_WIDTH), BF16),
        scratch_shapes=[
            pltpu.VMEM((seq, Q_TILE), F32),
            pltpu.VMEM((N_META, Q_TILE), F32),
            pltpu.VMEM((N_HEADS, seq, Q_TILE), F32),
            pltpu.VMEM((N_HEADS, N_META, Q_TILE), F32),
            pltpu.VMEM((ATTN_WIDTH, Q_TILE), F32),
            pltpu.VMEM((SUBLANES, Q_TILE), I32),
            pltpu.VMEM((seq, Q_TILE), jnp.int16),
            pltpu.VMEM((seq, Q_TILE), jnp.int16),
            pltpu.VMEM((N_META, Q_TILE), jnp.int16),
            pltpu.VMEM((N_META, Q_TILE), jnp.int16),
            pltpu.VMEM((N_HEADS, HEAD_DIM, Q_TILE), BF16),
            pltpu.VMEM((IDX_HEADS, IDX_DIM, Q_TILE), BF16),
        ],
        compiler_params=pltpu.CompilerParams(dimension_semantics=("arbitrary", "arbitrary"),
                                             vmem_limit_bytes=VMEM_LIMIT),
        name="attn",
    )(q3, iq3, iwt, k3, ik, vt, km3, ikm, vtm, tab, tabm)


def _mix_kernel(x_ref, attn_ref, y_ref, yprev_ref, ymeta_ref, gate_ref,
                cw_ref, cb_ref, lng_ref, lnb_ref, wco_ref, wao_ref, wout_ref, nf_ref, wrt_ref, br_ref,
                h2_ref, hn2_ref, eid_ref, rank_ref, gcol_ref, cnt_ref,
                win_ref, shift_ref, base_ref, *, tiles_per_seq):
    i = pl.program_id(0)

    @pl.when(i == 0)
    def _():
        base_ref[...] = jnp.zeros_like(base_ref)

    first = (i % tiles_per_seq) == 0
    win_ref[0:HALO, :] = jnp.where(first, ymeta_ref[...], yprev_ref[...])
    win_ref[HALO:, :] = y_ref[...]
    lead = HALO - (CONV_WIDTH - 1)
    for b in range(SUBLANES):
        span = MIX_TILE + SUBLANES * (len(range(b, CONV_WIDTH, SUBLANES)) - 1)
        shift_ref[b, 0:span, :] = win_ref[pl.ds(lead + b, span), :]
    for hh in range(MIX_TILE // MIX_ROWS):
        _mix_rows(hh * MIX_ROWS, x_ref, attn_ref, gate_ref, cw_ref, cb_ref, lng_ref, lnb_ref, wco_ref, wao_ref,
                  wout_ref, nf_ref, wrt_ref, br_ref, h2_ref, hn2_ref, eid_ref, rank_ref, gcol_ref, cnt_ref,
                  shift_ref, base_ref)


def _mix_rows(r_off, x_ref, attn_ref, gate_ref, cw_ref, cb_ref, lng_ref, lnb_ref, wco_ref, wao_ref,
              wout_ref, nf_ref, wrt_ref, br_ref, h2_ref, hn2_ref, eid_ref, rank_ref, gcol_ref, cnt_ref,
              shift_ref, base_ref):
    rows = slice(r_off, r_off + MIX_ROWS)
    blocks = []
    for r0 in range(r_off, r_off + MIX_ROWS, CONV_ROWS):
        acc = jnp.broadcast_to(cb_ref[...], (CONV_ROWS, CONV_CH))
        for b in range(SUBLANES):
            for a, w in enumerate(range(b, CONV_WIDTH, SUBLANES)):
                r1 = r0 + SUBLANES * a
                acc = acc + cw_ref[w:w + 1, :] * shift_ref[b, r1:r1 + CONV_ROWS, :]
        blocks.append(acc)
    yc = jnp.concatenate(blocks, axis=0)
    mu = jnp.mean(yc, axis=-1, keepdims=True)
    var = jnp.mean(jnp.square(yc - mu), axis=-1, keepdims=True)
    yn = (yc - mu) * lax.rsqrt(var + EPS) * lng_ref[...] + lnb_ref[...]
    ys = yn * _sigmoid(yn)
    y_b = jnp.dot(ys.astype(BF16), wco_ref[...], preferred_element_type=F32)

    y_a = jnp.dot(attn_ref[rows, :], wao_ref[...], preferred_element_type=F32)
    merged = gate_ref[rows, :D_MODEL] * y_a + gate_ref[rows, D_MODEL:] * y_b
    h2 = x_ref[rows, :] + jnp.dot(merged.astype(BF16), wout_ref[...], preferred_element_type=F32)
    h2_ref[rows, :] = h2
    ms = jnp.mean(h2 * h2, axis=-1, keepdims=True)
    hn2 = h2 * lax.rsqrt(ms + EPS) * nf_ref[...]
    hn2_ref[rows, 0, :] = _pack_bf16_pairs(hn2)

    logits = lax.dot_general(wrt_ref[...], hn2, NT_DIMS, preferred_element_type=F32,
                             precision=lax.Precision.HIGHEST) + br_ref[...]
    erow = lax.broadcasted_iota(I32, (N_EXPERTS, MIX_ROWS), 0)
    vals, ids = [], []
    l = logits
    for _ in range(TOP_K_EXPERTS):
        m = jnp.max(l, axis=0, keepdims=True)
        idx = jnp.min(jnp.where(l == m, erow, N_EXPERTS), axis=0, keepdims=True)
        vals.append(m)
        ids.append(idx)
        l = jnp.where(erow == idx, -jnp.inf, l)
    ex = [jnp.exp(v - vals[0]) for v in vals]
    den = ex[0] + ex[1] + ex[2] + ex[3]
    gates = [e / den for e in ex]

    onehot = [(erow == idx) for idx in ids]
    oh = jnp.concatenate([jnp.where(o, 1.0, 0.0) for o in onehot], axis=0)
    tr = lax.broadcasted_iota(I32, (MIX_ROWS, MIX_ROWS), 0)
    tc = lax.broadcasted_iota(I32, (MIX_ROWS, MIX_ROWS), 1)
    upper = jnp.where(tr <= tc, 1.0, 0.0).astype(BF16)
    pref = jnp.dot(oh.astype(BF16), upper, preferred_element_type=F32)
    offs = base_ref[:, 0:1]
    ranks = []
    for kk in range(TOP_K_EXPERTS):
        pk = pref[kk * N_EXPERTS:(kk + 1) * N_EXPERTS, :]
        r = jnp.sum(jnp.where(onehot[kk], offs + pk - 1.0, 0.0), axis=0, keepdims=True)
        ranks.append(r.astype(I32))
        offs = offs + pk[:, MIX_ROWS - 1:MIX_ROWS]
    base_ref[...] = jnp.broadcast_to(offs, base_ref.shape)
    cnt_ref[...] = jnp.broadcast_to(offs, cnt_ref.shape)

    zi = jnp.zeros((SUBLANES - TOP_K_EXPERTS, MIX_ROWS), I32)
    eid_ref[:, rows] = jnp.concatenate(ids + [zi], axis=0)
    rank_ref[:, rows] = jnp.concatenate(ranks + [zi], axis=0)
    g8 = jnp.concatenate(gates + [jnp.zeros((LANES - TOP_K_EXPERTS, MIX_ROWS), F32)], axis=0)
    gcol_ref[rows, :] = g8.T


def _mix(x2d, attn, y, ymeta, gate, wts, seq):
    n = x2d.shape[0]
    tiles_per_seq = seq // MIX_TILE
    halo_per_tile = MIX_TILE // HALO
    full = lambda a: pl.BlockSpec(a.shape, lambda i: (0,) * a.ndim)
    row = lambda w: pl.BlockSpec((MIX_TILE, w), lambda i: (i, 0))
    lane = lambda r: pl.BlockSpec((r, MIX_TILE), lambda i: (0, i))
    out_shape = (
        jax.ShapeDtypeStruct((n, D_MODEL), F32),
        jax.ShapeDtypeStruct((n, 1, D_MODEL // 2), I32),
        jax.ShapeDtypeStruct((SUBLANES, n), I32),
        jax.ShapeDtypeStruct((SUBLANES, n), I32),
        jax.ShapeDtypeStruct((n, LANES), F32),
        jax.ShapeDtypeStruct((N_EXPERTS, LANES), F32),
    )
    out_specs = (row(D_MODEL), pl.BlockSpec((MIX_TILE, 1, D_MODEL // 2), lambda i: (i, 0, 0)),
                 lane(SUBLANES), lane(SUBLANES), row(LANES), full(out_shape[5]))
    return pl.pallas_call(
        functools.partial(_mix_kernel, tiles_per_seq=tiles_per_seq),
        grid=(n // MIX_TILE,),
        in_specs=[row(D_MODEL), row(ATTN_WIDTH), row(CONV_CH),
                  pl.BlockSpec((HALO, CONV_CH), lambda i: (jnp.maximum(i * halo_per_tile - 1, 0), 0)),
                  full(ymeta), row(2 * D_MODEL)] + [full(w) for w in wts],
        out_specs=out_specs,
        out_shape=out_shape,
        scratch_shapes=[pltpu.VMEM((HALO + MIX_TILE, CONV_CH), F32),
                        pltpu.VMEM((SUBLANES, HALO + MIX_TILE, CONV_CH), F32),
                        pltpu.VMEM((N_EXPERTS, LANES), F32)],
        compiler_params=pltpu.CompilerParams(dimension_semantics=("arbitrary",),
                                             vmem_limit_bytes=VMEM_LIMIT),
        name="mix",
    )(x2d, attn, y, y, ymeta, gate, *wts)


def _slots_kernel(starts_ref, eid_ref, rank_ref, slot_ref):
    eid = eid_ref[...]
    base = jnp.zeros(eid.shape, I32)
    for e in range(N_EXPERTS):
        base = jnp.where(eid == e, starts_ref[e], base)
    slot = base + rank_ref[...]
    for t in range(slot_ref.shape[0]):
        for kk in range(TOP_K_EXPERTS):
            slot_ref[t, :, kk * ROW_TILE:(kk + 1) * ROW_TILE] = slot[kk:kk + 1, t * ROW_TILE:(t + 1) * ROW_TILE]


def _slots(starts, eid8, rank8):
    n = eid8.shape[1]
    tiles = 8
    spec = pl.BlockSpec((SUBLANES, tiles * ROW_TILE), lambda i: (0, i))
    return pl.pallas_call(
        _slots_kernel,
        grid=(n // (tiles * ROW_TILE),),
        in_specs=[pl.BlockSpec(memory_space=pltpu.SMEM), spec, spec],
        out_specs=pl.BlockSpec((tiles, 1, COPIES_PER_TILE), lambda i: (i, 0, 0)),
        out_shape=jax.ShapeDtypeStruct((n // ROW_TILE, 1, COPIES_PER_TILE), I32),
        compiler_params=pltpu.CompilerParams(dimension_semantics=("arbitrary",)),
        name="slots",
    )(starts, eid8, rank8)


COPIES_PER_TILE = TOP_K_EXPERTS * ROW_TILE
SLOT_UNROLL = 16


def _slot_source_kernel(slot_ref, init_ref, src_ref, sem, *, n):
    i = pl.program_id(0)

    @pl.when(i == 0)
    def _():
        cp = pltpu.make_async_copy(init_ref, src_ref, sem)
        cp.start()
        cp.wait()

    for kk in range(TOP_K_EXPERTS):
        def body(g, carry):
            first = kk * ROW_TILE + g * SLOT_UNROLL
            value = kk * n + i * ROW_TILE + g * SLOT_UNROLL
            for u in range(SLOT_UNROLL):
                src_ref[slot_ref[0, 0, first + u]] = value + u
            return carry
        lax.fori_loop(0, ROW_TILE // SLOT_UNROLL, body, 0)


def _slot_sources(slot_tiles, init, n):
    return pl.pallas_call(
        functools.partial(_slot_source_kernel, n=n),
        grid=(slot_tiles.shape[0],),
        in_specs=[pl.BlockSpec((1, 1, COPIES_PER_TILE), lambda i: (i, 0, 0), memory_space=pltpu.SMEM),
                  pl.BlockSpec(memory_space=pl.ANY)],
        out_specs=pl.BlockSpec(memory_space=pltpu.SMEM),
        out_shape=jax.ShapeDtypeStruct(init.shape, I32),
        scratch_shapes=[pltpu.SemaphoreType.DMA(())],
        compiler_params=pltpu.CompilerParams(dimension_semantics=("arbitrary",)),
        name="slot_sources",
    )(slot_tiles, init)


def _expert_kernel(te_ref, nused_ref, src_ref, src_next_ref, src_prev_ref, hn_hbm,
                   wug_ref, bug_ref, wd_ref, bd_ref, yk_hbm, xbuf, ybuf, wug_bf, wd_bf, gsem, ssem, *, n):
    i = pl.program_id(0)
    nused = nused_ref[0]

    def gather_copy(src, r, b):
        tok = src[0, 0, r] & (n - 1)
        return pltpu.make_async_copy(hn_hbm.at[tok], xbuf.at[b, r], gsem.at[b])

    def scatter_copy(src, r, b):
        return pltpu.make_async_copy(ybuf.at[b, r], yk_hbm.at[src[0, 0, r]], ssem.at[b])

    @pl.when(i <= nused)
    def _():
        cur = i % 2
        oth = 1 - cur

        def wait_scatter(b):
            pltpu.make_async_copy(ybuf.at[b], yk_hbm.at[pl.ds(0, ROW_TILE)], ssem.at[b]).wait()

        @pl.when(i == 0)
        def _():
            for r in range(ROW_TILE):
                gather_copy(src_ref, r, 0).start()
            ybuf[1] = jnp.zeros(ybuf.shape[1:], F32)
            spare = pltpu.make_async_copy(ybuf.at[1], yk_hbm.at[pl.ds(TOP_K_EXPERTS * n, ROW_TILE)],
                                          ssem.at[1])
            spare.start()
            spare.wait()

        @pl.when(i < nused)
        def _():
            for r in range(ROW_TILE):
                gather_copy(src_next_ref, r, oth).start()

        @pl.when(i > 0)
        def _():
            for r in range(ROW_TILE):
                scatter_copy(src_prev_ref, r, oth).start()

        pltpu.make_async_copy(hn_hbm.at[pl.ds(0, ROW_TILE)], xbuf.at[cur], gsem.at[cur]).wait()

        prev = te_ref[jnp.maximum(i - 1, 0)]
        fresh = (i == 0) | (te_ref[jnp.minimum(i, nused - 1)] != prev)

        @pl.when(fresh)
        def _():
            wug_bf[...] = wug_ref[0].astype(BF16)
            wd_bf[...] = wd_ref[0].astype(BF16)

        xb = _unpack_bf16_pairs(xbuf[cur, :, 0, :])
        ug = jnp.dot(xb, wug_bf[...], preferred_element_type=F32) + bug_ref[0]
        gate = jnp.minimum(ug[:, :D_FF], SWIGLU_LIMIT)
        up = jnp.clip(ug[:, D_FF:], -SWIGLU_LIMIT, SWIGLU_LIMIT)
        act = (up + 1.0) * gate * _sigmoid(SWIGLU_ALPHA * gate)
        y = jnp.dot(act.astype(BF16), wd_bf[...], preferred_element_type=F32) + bd_ref[0]

        @pl.when(i > 1)
        def _():
            wait_scatter(cur)

        ybuf[cur, :, 0, :] = y

        @pl.when((i == nused) & (i > 0))
        def _():
            wait_scatter(oth)


def _experts(tile_expert, nused, src_tiles, hn2p, w_ug, b_ug, w_down, b_down, n):
    ntiles = src_tiles.shape[0]
    clamp = lambda i, nu: jnp.maximum(jnp.minimum(i, nu[0] - 1), 0)
    smem = lambda f: pl.BlockSpec((1, 1, ROW_TILE), f, memory_space=pltpu.SMEM)
    expert = lambda i, te, nu: (te[clamp(i, nu)], 0, 0)
    grid_spec = pltpu.PrefetchScalarGridSpec(
        num_scalar_prefetch=2,
        grid=(ntiles + 1,),
        in_specs=[
            smem(lambda i, te, nu: (clamp(i, nu), 0, 0)),
            smem(lambda i, te, nu: (clamp(i + 1, nu), 0, 0)),
            smem(lambda i, te, nu: (clamp(i - 1, nu), 0, 0)),
            pl.BlockSpec(memory_space=pl.ANY),
            pl.BlockSpec((1, D_MODEL, 2 * D_FF), expert),
            pl.BlockSpec((1, 1, 2 * D_FF), expert),
            pl.BlockSpec((1, D_FF, D_MODEL), expert),
            pl.BlockSpec((1, 1, D_MODEL), expert),
        ],
        out_specs=pl.BlockSpec(memory_space=pl.ANY),
        scratch_shapes=[pltpu.VMEM((2, ROW_TILE, 1, D_MODEL // 2), I32),
                        pltpu.VMEM((2, ROW_TILE, 1, D_MODEL), F32),
                        pltpu.VMEM((D_MODEL, 2 * D_FF), BF16), pltpu.VMEM((D_FF, D_MODEL), BF16),
                        pltpu.SemaphoreType.DMA((2,)), pltpu.SemaphoreType.DMA((2,))],
    )
    return pl.pallas_call(
        functools.partial(_expert_kernel, n=n),
        grid_spec=grid_spec,
        out_shape=jax.ShapeDtypeStruct((TOP_K_EXPERTS * n + ROW_TILE, 1, D_MODEL), F32),
        compiler_params=pltpu.CompilerParams(dimension_semantics=("arbitrary",),
                                             vmem_limit_bytes=VMEM_LIMIT),
        name="experts",
    )(tile_expert, nused, src_tiles, src_tiles, src_tiles, hn2p,
      w_ug, b_ug[:, None, :], w_down, b_down[:, None, :])


def _final_kernel(h2_ref, gcol_ref, nf_ref, *rest):
    y_refs, o_ref = rest[:TOP_K_EXPERTS], rest[TOP_K_EXPERTS]
    h = h2_ref[...]
    for kk in range(TOP_K_EXPERTS):
        h = h + gcol_ref[:, kk:kk + 1] * y_refs[kk][:, 0, :]
    ms = jnp.mean(h * h, axis=-1, keepdims=True)
    o_ref[...] = h * lax.rsqrt(ms + EPS) * nf_ref[...]


def _final(h2, gcol, nf, yk):
    n = h2.shape[0]
    ntiles = n // ROW_TILE
    choice = lambda kk: pl.BlockSpec((ROW_TILE, 1, D_MODEL), lambda i: (kk * ntiles + i, 0, 0))
    return pl.pallas_call(
        _final_kernel,
        grid=(ntiles,),
        in_specs=[pl.BlockSpec((ROW_TILE, D_MODEL), lambda i: (i, 0)),
                  pl.BlockSpec((ROW_TILE, LANES), lambda i: (i, 0)),
                  pl.BlockSpec((1, D_MODEL), lambda i: (0, 0))] + [choice(kk) for kk in range(TOP_K_EXPERTS)],
        out_specs=pl.BlockSpec((ROW_TILE, D_MODEL), lambda i: (i, 0)),
        out_shape=jax.ShapeDtypeStruct((n, D_MODEL), F32),
        compiler_params=pltpu.CompilerParams(dimension_semantics=("arbitrary",),
                                             vmem_limit_bytes=VMEM_LIMIT),
        name="final",
    )(h2, gcol, nf, *([yk] * TOP_K_EXPERTS))


def _split_w_in(w_in, b_gate):
    c = ATTN_WIDTH
    o = 0
    wq = w_in[:, o:o + c]; o += c
    wk = w_in[:, o:o + c]; o += c
    wv = w_in[:, o:o + c]; o += c
    wiq = w_in[:, o:o + IDX_HEADS * IDX_DIM]; o += IDX_HEADS * IDX_DIM
    wik = w_in[:, o:o + IDX_DIM]; o += IDX_DIM
    wiw = w_in[:, o:o + IDX_HEADS]; o += IDX_HEADS
    wglu = w_in[:, o:o + 2 * CONV_CH]; o += 2 * CONV_CH
    wgate = w_in[:, o:]
    wiwt = jnp.concatenate([wiw.T, jnp.zeros((PACKED_ROWS - IDX_HEADS, D_MODEL), w_in.dtype)], axis=0)
    bf = lambda a: a.astype(BF16)
    return (bf(wq), bf(wk), bf(wiq), bf(wik), bf(wv.T), bf(wiwt), bf(wglu), bf(wgate),
            b_gate[None, :].astype(F32))


def kernel(x, meta_tokens, rel_bias, norm_mix, w_in, b_gate, w_attn_out, conv_w, conv_b, conv_ln_g, conv_ln_b, w_conv_out, w_out, norm_ffn, w_router, b_router, w_up_gate, b_up_gate, w_down, b_down, norm_final):
    batch, seq, d = x.shape
    n = batch * seq
    assert d == D_MODEL and seq % MIX_TILE == 0 and seq % (COUNT_CHUNKS * K_CHUNK) == 0 and n % PROJ_TILE == 0
    assert n & (n - 1) == 0, "slot sources keep the token index in the low bits of choice * n + token"
    assert min(TOPK_MAX, seq // 4) == TOPK_MAX and N_META + seq < 2 ** POS_BITS
    x2d = x.reshape(n, d)

    wts = _split_w_in(w_in[0], b_gate[0])
    g_mix = norm_mix[0][None, :]
    q3, k3, iq3, ik, vt, iwt, y, gate = _project(x2d, g_mix, wts, PROJ_TILE)
    _, km3, _, ikm, vtm, _, ym, _ = _project(meta_tokens.astype(F32), g_mix, wts, N_META)

    tab, tabm = _bias_tables(rel_bias.astype(F32))
    attn = _attention(q3, iq3, iwt, k3, ik, vt, km3, ikm, vtm, tab, tabm, batch, seq)

    ymeta = jnp.concatenate([jnp.zeros((HALO - N_META, CONV_CH), F32), ym], axis=0)
    cw = jnp.concatenate([conv_w[0], jnp.zeros((-CONV_WIDTH % SUBLANES, CONV_CH), F32)], axis=0)
    mix_w = (cw, conv_b[0][None, :], conv_ln_g[0][None, :], conv_ln_b[0][None, :],
             w_conv_out[0].astype(BF16), w_attn_out[0].astype(BF16), w_out[0].astype(BF16),
             norm_ffn[0][None, :], w_router[0].T, b_router[0][:, None])
    h2, hn2, eid8, rank8, gcol, cnt = _mix(x2d, attn, y, ymeta, gate, mix_w, seq)

    counts = cnt[:, 0].astype(I32)
    padded = ((counts + ROW_TILE - 1) // ROW_TILE) * ROW_TILE
    ends = jnp.cumsum(padded)
    starts = ends - padded
    slot_tiles = _slots(starts, eid8, rank8)
    nslots = n * TOP_K_EXPERTS + N_EXPERTS * ROW_TILE
    ntiles = nslots // ROW_TILE
    tile_start = jnp.arange(ntiles, dtype=I32) * ROW_TILE
    nused = (ends[-1] // ROW_TILE).astype(I32)
    last_start = jnp.maximum(ends[-1] - ROW_TILE, 0)
    tile_expert = jnp.sum((jnp.minimum(tile_start, last_start)[:, None] >= ends[None, :]).astype(I32), axis=1)

    spare = TOP_K_EXPERTS * n + jnp.arange(nslots, dtype=I32) % ROW_TILE
    src_tiles = _slot_sources(slot_tiles, spare, n).reshape(ntiles, 1, ROW_TILE)
    yk = _experts(tile_expert, nused[None], src_tiles, hn2, w_up_gate[0], b_up_gate[0], w_down[0], b_down[0], n)
    out = _final(h2, gcol, norm_final[None, :], yk)
    return out.reshape(batch, seq, d)
```
